```python
import math
import jax
import jax.numpy as jnp
from jax import lax
import numpy as np

D_MODEL = 1024
BATCH = 16
SEQ = 256
DEPTH = 2
DEC_BATCH = 2
DEC_SEQ = 4096
PAST_LEN = 512

GRID_W = 64
EPS = 1e-6

DA_HEADS = 4
DA_QK = 64
DA_V = 2 * DA_QK
ROPE_BASE = 10000.0
Q_BLOCK = 128

RW_HEADS = 8
RW_HD = 64
RW_W = RW_HEADS * RW_HD
RW_DECAY_LORA = 64
RW_A_LORA = 64
RW_G_LORA = 128
RW_GN_EPS = 64e-5

RET_HEADS = 4
RET_DK = 64
RET_DV = 128
RET_QK_W = RET_HEADS * RET_DK
RET_W = RET_HEADS * RET_DV
RET_CHUNK = 128

CV_W = 512
CONV_K = 31

N_BRANCH = 4
BR_W = 512

N_EXPERTS = 32
TOP_K = 4
D_FF = 1024
SWIGLU_LIMIT = 7.0
SWIGLU_ALPHA = 1.702
MOE_BLOCK = 128

DA_COLS = DA_HEADS * (4 * DA_QK + DA_V)
RW_COLS = 3 * RW_W + 2 * RW_DECAY_LORA + 2 * RW_A_LORA + RW_G_LORA
RET_COLS = 2 * RET_QK_W + 2 * RET_W
CV_COLS = 2 * CV_W
GATE_COLS = N_BRANCH * D_MODEL
IN_COLS = DA_COLS + RW_COLS + RET_COLS + CV_COLS + GATE_COLS

kernel_name = 'hybrid_diffusion_prefix_step'


def _split(x, sizes):
    return jnp.split(x, np.cumsum(sizes)[:-1].tolist(), axis=-1)


def rmsnorm(x, w):
    x32 = x.astype(jnp.float32)
    y = x32 * lax.rsqrt(jnp.mean(x32 * x32, axis=-1, keepdims=True) + EPS)
    return (y * w.astype(jnp.float32)).astype(x.dtype)


def standardize(x, eps):
    x32 = x.astype(jnp.float32)
    mu = jnp.mean(x32, axis=-1, keepdims=True)
    var = jnp.mean(jnp.square(x32 - mu), axis=-1, keepdims=True)
    return (x32 - mu) * lax.rsqrt(var + eps)


def axial_rope_tables(n_tok):
    rows = n_tok // GRID_W
    row = jnp.repeat(jnp.arange(rows, dtype=jnp.float32), GRID_W)
    col = jnp.tile(jnp.arange(GRID_W, dtype=jnp.float32), rows)
    n_pairs = DA_QK // 4
    inv = ROPE_BASE ** (-jnp.arange(n_pairs, dtype=jnp.float32) / n_pairs)
    ang = jnp.concatenate([row[:, None] * inv, col[:, None] * inv], axis=-1)
    return jnp.cos(ang), jnp.sin(ang)


def apply_axial_rope(x, cos, sin):
    xp = x.reshape(x.shape[:-1] + (DA_QK // 2, 2))
    xe, xo = xp[..., 0], xp[..., 1]
    c = cos[None, :, None, None, :].astype(x.dtype)
    s = sin[None, :, None, None, :].astype(x.dtype)
    return jnp.stack([xe * c - xo * s, xe * s + xo * c], axis=-1).reshape(x.shape)


def diff_attention(q, k, v, lam):
    b, tq = q.shape[:2]
    nb = tq // Q_BLOCK
    qb = jnp.moveaxis(q.reshape((b, nb, Q_BLOCK) + q.shape[2:]), 1, 0)
    scale = DA_QK ** -0.5

    def block(qblk):
        s = jnp.einsum('bqhmd,bkhmd->bhmqk', qblk, k).astype(jnp.float32) * scale
        pr = jax.nn.softmax(s, axis=-1)
        a = (pr[:, :, 0] - lam * pr[:, :, 1]).astype(v.dtype)
        return jnp.einsum('bhqk,bkhd->bqhd', a, v)

    o = lax.map(block, qb)
    return jnp.moveaxis(o, 0, 1).reshape(b, tq, q.shape[2], v.shape[-1])


def centred_shift(x, mu):
    prev = jnp.pad(x[:, :-1], ((0, 0), (1, 0), (0, 0)))
    nxt = jnp.pad(x[:, 1:], ((0, 0), (0, 1), (0, 0)))
    return x + mu[0] * (prev - x) + mu[1] * (nxt - x)


def rwkv_scan(r, w, k, v, kk, a, s0):
    def step(S, inp):
        r_t, w_t, k_t, v_t, kk_t, a_t = inp
        sa = jnp.einsum('bhvk,bhk->bhv', S, -kk_t)
        S = (S * w_t[:, :, None, :] + sa[..., None] * (kk_t * a_t)[:, :, None, :]
             + v_t[..., None] * k_t[:, :, None, :])
        return S, jnp.einsum('bhvk,bhk->bhv', S, r_t)

    xs = tuple(jnp.moveaxis(z, 1, 0) for z in (r, w, k, v, kk, a))
    s_final, ys = lax.scan(step, s0.astype(r.dtype), xs)
    return jnp.moveaxis(ys, 0, 1), s_final


def _flip(z):
    return jnp.flip(z, axis=1)


def _rw_heads(z):
    return z.reshape(z.shape[:-1] + (RW_HEADS, RW_HD))


def rwkv_branch(u, p, s0):
    b, t, _ = u.shape
    r, k, v, wl, al, gl = _split(u, (RW_W, RW_W, RW_W, 2 * RW_DECAY_LORA, 2 * RW_A_LORA, RW_G_LORA))
    wl = wl.reshape(b, t, 2, RW_DECAY_LORA)
    al = al.reshape(b, t, 2, RW_A_LORA)
    w = p['rw_w0'] + jnp.einsum('btnl,nlc->btnc', jnp.tanh(wl), p['rw_w_up'])
    decay = jnp.exp(-jnp.exp(-jax.nn.softplus(-w.astype(jnp.float32)) - 0.5)).astype(u.dtype)
    a = jax.nn.sigmoid(p['rw_a0'] + jnp.einsum('btnl,nlc->btnc', al, p['rw_a_up']))
    g = jax.nn.sigmoid(gl) @ p['rw_g_up']
    kk = _rw_heads(k * p['rw_k_k']).astype(jnp.float32)
    kk = (kk * lax.rsqrt(jnp.maximum(jnp.sum(kk * kk, axis=-1, keepdims=True), 1e-12))).astype(u.dtype)
    k_dir = k[:, :, None, :] * (1.0 + (a - 1.0) * p['rw_k_a'])
    r_h, v_h = _rw_heads(r), _rw_heads(v)
    y_f, s_f = rwkv_scan(r_h, _rw_heads(decay[:, :, 0]), _rw_heads(k_dir[:, :, 0]), v_h, kk,
                         _rw_heads(a[:, :, 0]), s0[:, 0])
    y_b, s_b = rwkv_scan(_flip(r_h), _flip(_rw_heads(decay[:, :, 1])), _flip(_rw_heads(k_dir[:, :, 1])),
                         _flip(v_h), _flip(kk), _flip(_rw_heads(a[:, :, 1])), s0[:, 1])
    y = standardize(y_f + _flip(y_b), RW_GN_EPS).reshape(b, t, RW_W)
    y = y * p['rw_ln_w'].astype(jnp.float32) + p['rw_ln_b'].astype(jnp.float32)
    bonus = jnp.sum(r_h * _rw_heads(k_dir[:, :, 0] + k_dir[:, :, 1]) * p['rw_r_k'], axis=-1, keepdims=True) * v_h
    out = (y.astype(u.dtype) + bonus.reshape(b, t, RW_W)) * g
    return out, jnp.stack([s_f, s_b], axis=1)


def retention_chunkwise(q, k, v, log_gamma, s0):
    b, t, h, _ = q.shape
    nc = t // RET_CHUNK

    def chunks(z):
        return z.reshape(b, nc, RET_CHUNK, h, z.shape[-1]).transpose(1, 0, 3, 2, 4)

    pos = jnp.arange(RET_CHUNK, dtype=jnp.float32)
    diff = pos[:, None] - pos[None, :]
    lg = log_gamma[:, None, None]
    decay_mat = jnp.where(diff >= 0, jnp.exp(lg * jnp.maximum(diff, 0.0)), 0.0).astype(q.dtype)
    q_decay = jnp.exp(log_gamma[:, None] * (pos + 1.0))[..., None].astype(q.dtype)
    k_decay = jnp.exp(log_gamma[:, None] * (RET_CHUNK - 1.0 - pos))[..., None].astype(q.dtype)
    chunk_decay = jnp.exp(log_gamma * RET_CHUNK)[:, None, None].astype(q.dtype)

    def step(S, inp):
        qc, kc, vc = inp
        att = jnp.einsum('bhqd,bhkd->bhqk', qc, kc) * decay_mat
        o = jnp.einsum('bhqk,bhkv->bhqv', att, vc) + jnp.einsum('bhqd,bhdv->bhqv', qc * q_decay, S)
        S = S * chunk_decay + jnp.einsum('bhkd,bhkv->bhdv', kc * k_decay, vc)
        return S, o

    s_final, o = lax.scan(step, s0.astype(q.dtype), (chunks(q), chunks(k), chunks(v)))
    o = o.transpose(1, 0, 3, 2, 4).reshape(b, t, h, v.shape[-1])
    return o, s_final


def retention_branch(u, p, s0):
    b, t, _ = u.shape
    q, k, v, gate = _split(u, (RET_QK_W, RET_QK_W, RET_W, RET_W))
    q = q.reshape(b, t, RET_HEADS, RET_DK)
    k = k.reshape(b, t, RET_HEADS, RET_DK) * (RET_DK ** -0.5)
    v = v.reshape(b, t, RET_HEADS, RET_DV)
    log_gamma = jax.nn.log_sigmoid(p['ret_decay_logit'].astype(jnp.float32))
    o_f, s_f = retention_chunkwise(q, k, v, log_gamma[0], s0[:, 0])
    o_b, s_b = retention_chunkwise(_flip(q), _flip(k), _flip(v), log_gamma[1], s0[:, 1])
    o = standardize(o_f + _flip(o_b), EPS).reshape(b, t, RET_W) * p['ret_norm_w'].astype(jnp.float32)
    out = jax.nn.silu(gate) * o.astype(u.dtype)
    return out, jnp.stack([s_f, s_b], axis=1)


def conformer_conv(u, p):
    a, gt = jnp.split(u, 2, axis=-1)
    z = a * jax.nn.sigmoid(gt)
    z = lax.conv_general_dilated(z, p['cv_dw_w'][:, None, :], window_strides=(1,),
                                 padding=[(CONV_K // 2, CONV_K // 2)],
                                 dimension_numbers=('NWC', 'WIO', 'NWC'),
                                 feature_group_count=CV_W) + p['cv_dw_b']
    z = standardize(z, EPS) * p['cv_ln_w'].astype(jnp.float32) + p['cv_ln_b'].astype(jnp.float32)
    return jax.nn.silu(z).astype(u.dtype)


def token_mixing(h, p, lam_init, rope, ctx):
    b, t, _ = h.shape
    proj = h @ p['w_in']
    da, rw, ret, cv, gate_logits = _split(proj, (DA_COLS, RW_COLS, RET_COLS, CV_COLS, GATE_COLS))

    q, k, v = _split(da, (2 * DA_HEADS * DA_QK, 2 * DA_HEADS * DA_QK, DA_HEADS * DA_V))
    q = q.reshape(b, t, DA_HEADS, 2, DA_QK)
    k = k.reshape(b, t, DA_HEADS, 2, DA_QK)
    v = v.reshape(b, t, DA_HEADS, DA_V)
    lq1, lk1, lq2, lk2 = p['da_lambda'].astype(jnp.float32)
    lam = jnp.exp(jnp.sum(lq1 * lk1)) - jnp.exp(jnp.sum(lq2 * lk2)) + lam_init
    if ctx is None:
        k_all, v_all = k, v
        rw_s0 = jnp.zeros((b, 2, RW_HEADS, RW_HD, RW_HD), h.dtype)
        ret_s0 = jnp.zeros((b, 2, RET_HEADS, RET_DK, RET_DV), h.dtype)
    else:
        ctx_k, ctx_v, rw_s0, ret_s0 = ctx
        cos, sin = rope
        q = apply_axial_rope(q, cos, sin)
        k_all = jnp.concatenate([ctx_k.astype(h.dtype), apply_axial_rope(k, cos, sin)], axis=1)
        v_all = jnp.concatenate([ctx_v.astype(h.dtype), v], axis=1)
    o_da = diff_attention(q, k_all, v_all, lam)
    o_da = (rmsnorm(o_da, p['da_norm_w']) * (1.0 - lam_init)).reshape(b, t, DA_HEADS * DA_V)

    o_rw, rw_state = rwkv_branch(centred_shift(rw, p['rw_shift']), p, rw_s0)
    o_ret, ret_state = retention_branch(ret, p, ret_s0)
    o_cv = conformer_conv(cv, p)

    branches = jnp.stack([o_da, o_rw, o_ret, o_cv], axis=2)
    br = jnp.einsum('btnc,ncd->btnd', branches, p['w_branch'])
    gates = jax.nn.sigmoid(gate_logits.reshape(b, t, N_BRANCH, D_MODEL))
    out = jnp.sum(gates * br, axis=2) @ p['w_out']
    if ctx is None:
        return out, (k, v, rw_state, ret_state)
    return out, None


def routed_ffn(x, p):
    n, d = x.shape
    logits = (x @ p['router_w'] + p['router_b']).astype(jnp.float32)
    top_val, top_idx = lax.top_k(logits, TOP_K)
    gate = jax.nn.softmax(top_val, axis=-1).astype(x.dtype).reshape(-1)
    flat_e = top_idx.reshape(-1)
    flat_tok = jnp.arange(n * TOP_K) // TOP_K
    order = jnp.argsort(flat_e)
    se, stok, sgate = flat_e[order], flat_tok[order], gate[order]
    counts = jnp.bincount(flat_e, length=N_EXPERTS)
    padded = (counts + MOE_BLOCK - 1) // MOE_BLOCK * MOE_BLOCK
    pad_end = jnp.cumsum(padded)
    pad_start = pad_end - padded
    start = jnp.cumsum(counts) - counts
    dest = pad_start[se] + jnp.arange(n * TOP_K) - start[se]
    n_rows = -(-(n * TOP_K) // MOE_BLOCK) * MOE_BLOCK + N_EXPERTS * MOE_BLOCK
    n_blocks = n_rows // MOE_BLOCK
    row_tok = jnp.full((n_rows,), n, jnp.int32).at[dest].set(stok.astype(jnp.int32))
    blk_e = jnp.minimum(jnp.searchsorted(pad_end, jnp.arange(n_blocks) * MOE_BLOCK, side='right'),
                        N_EXPERTS - 1)
    x_rows = jnp.concatenate([x, jnp.zeros((1, d), x.dtype)], axis=0)[row_tok]
    x_rows = x_rows.reshape(n_blocks, MOE_BLOCK, d)
    w1, b1, w2, b2 = p['moe_w1'], p['moe_b1'], p['moe_w2'], p['moe_b2']

    def expert_block(args):
        xb, e = args
        hb = xb @ w1[e] + b1[e]
        hg, hu = jnp.split(hb, 2, axis=-1)
        hg = jnp.minimum(hg, SWIGLU_LIMIT)
        hu = jnp.clip(hu, -SWIGLU_LIMIT, SWIGLU_LIMIT)
        act = hg * jax.nn.sigmoid(SWIGLU_ALPHA * hg) * (hu + 1.0)
        return act @ w2[e] + b2[e]

    y_rows = lax.map(expert_block, (x_rows, blk_e)).reshape(n_rows, d)
    return jax.ops.segment_sum(y_rows[dest] * sgate[:, None], stok, num_segments=n)


def trunk_layer(x, mod, p, lam_init, rope, ctx):
    sh1, sc1, g1, sh2, sc2, g2 = jnp.split(mod, 6, axis=-1)
    h = rmsnorm(x, p['norm_mix_w']) * (1.0 + sc1) + sh1
    mix, ctx_out = token_mixing(h, p, lam_init, rope, ctx)
    x = x + g1 * mix
    h = rmsnorm(x, p['norm_ffn_w']) * (1.0 + sc2) + sh2
    b, t, d = h.shape
    x = x + g2 * routed_ffn(h.reshape(b * t, d), p).reshape(b, t, d)
    return x, ctx_out


def setup_inputs(seed: int = 0) -> dict:
    key = jax.random.key(seed)
    ks = list(jax.random.split(key, 48))
    f32 = jnp.float32

    def nrm(shape, scale):
        return jax.random.normal(ks.pop(), shape, f32) * scale

    ret_base = jnp.log(2.0 ** (5.0 + jnp.arange(RET_HEADS, dtype=f32)) - 1.0)
    return dict(
        x_prompt=nrm((BATCH, SEQ, D_MODEL), 1.0),
        x_sample=nrm((DEC_BATCH, DEC_SEQ, D_MODEL), 1.0),
        c=nrm((DEC_BATCH, D_MODEL), 1.0),
        cache_da_k=nrm((DEC_BATCH, DEPTH, PAST_LEN, DA_HEADS, 2, DA_QK), 1.0),
        cache_da_v=nrm((DEC_BATCH, DEPTH, PAST_LEN, DA_HEADS, DA_V), 1.0),
        state_rwkv=nrm((DEC_BATCH, DEPTH, 2, RW_HEADS, RW_HD, RW_HD), 0.5),
        state_ret=nrm((DEC_BATCH, DEPTH, 2, RET_HEADS, RET_DK, RET_DV), 1.0),
        c_ctx=nrm((D_MODEL,), 1.0),
        ada_w=nrm((DEPTH, D_MODEL, 6 * D_MODEL), 0.5 * D_MODEL ** -0.5),
        ada_b=nrm((DEPTH, 6 * D_MODEL), 0.01),
        norm_mix_w=1.0 + nrm((DEPTH, D_MODEL), 0.01),
        norm_ffn_w=1.0 + nrm((DEPTH, D_MODEL), 0.01),
        w_in=nrm((DEPTH, D_MODEL, IN_COLS), D_MODEL ** -0.5),
        da_lambda=nrm((DEPTH, 4, DA_QK), 0.1),
        da_norm_w=1.0 + nrm((DEPTH, DA_V), 0.01),
        rw_shift=jax.random.uniform(ks.pop(), (DEPTH, 2, RW_COLS), f32, 0.1, 0.5),
        rw_w0=-1.0 + nrm((DEPTH, 2, RW_W), 0.5),
        rw_w_up=nrm((DEPTH, 2, RW_DECAY_LORA, RW_W), 0.5 * RW_DECAY_LORA ** -0.5),
        rw_a0=nrm((DEPTH, 2, RW_W), 0.1),
        rw_a_up=nrm((DEPTH, 2, RW_A_LORA, RW_W), 0.5 * RW_A_LORA ** -0.5),
        rw_g_up=nrm((DEPTH, RW_G_LORA, RW_W), RW_G_LORA ** -0.5),
        rw_k_k=0.85 + nrm((DEPTH, RW_W), 0.05),
        rw_k_a=1.0 + nrm((DEPTH, RW_W), 0.05),
        rw_r_k=nrm((DEPTH, RW_HEADS, RW_HD), 0.1),
        rw_ln_w=1.0 + nrm((DEPTH, RW_W), 0.01),
        rw_ln_b=nrm((DEPTH, RW_W), 0.01),
        ret_decay_logit=ret_base + nrm((DEPTH, 2, RET_HEADS), 0.01),
        ret_norm_w=1.0 + nrm((DEPTH, RET_W), 0.01),
        cv_dw_w=nrm((DEPTH, CONV_K, CV_W), CONV_K ** -0.5),
        cv_dw_b=nrm((DEPTH, CV_W), 0.01),
        cv_ln_w=1.0 + nrm((DEPTH, CV_W), 0.01),
        cv_ln_b=nrm((DEPTH, CV_W), 0.01),
        w_branch=nrm((DEPTH, N_BRANCH, BR_W, D_MODEL), BR_W ** -0.5),
        w_out=nrm((DEPTH, D_MODEL, D_MODEL), D_MODEL ** -0.5),
        router_w=nrm((DEPTH, D_MODEL, N_EXPERTS), D_MODEL ** -0.5),
        router_b=nrm((DEPTH, N_EXPERTS), 0.01),
        moe_w1=nrm((DEPTH, N_EXPERTS, D_MODEL, 2 * D_FF), D_MODEL ** -0.5),
        moe_b1=nrm((DEPTH, N_EXPERTS, 2 * D_FF), 0.01),
        moe_w2=nrm((DEPTH, N_EXPERTS, D_FF, D_MODEL), D_FF ** -0.5),
        moe_b2=nrm((DEPTH, N_EXPERTS, D_MODEL), 0.01),
        final_norm_w=1.0 + nrm((D_MODEL,), 0.01),
    )


def reference(x_prompt, x_sample, c, cache_da_k, cache_da_v, state_rwkv, state_ret, c_ctx,
              ada_w, ada_b, norm_mix_w, norm_ffn_w, w_in, da_lambda, da_norm_w,
              rw_shift, rw_w0, rw_w_up, rw_a0, rw_a_up, rw_g_up, rw_k_k, rw_k_a, rw_r_k,
              rw_ln_w, rw_ln_b, ret_decay_logit, ret_norm_w, cv_dw_w, cv_dw_b, cv_ln_w, cv_ln_b,
              w_branch, w_out, router_w, router_b, moe_w1, moe_b1, moe_w2, moe_b2, final_norm_w):
    rope = axial_rope_tables(x_sample.shape[1])
    xp, xs = x_prompt, x_sample
    ks, vs, rws, rets = [], [], [], []
    for i in range(DEPTH):
        p = dict(norm_mix_w=norm_mix_w[i], norm_ffn_w=norm_ffn_w[i], w_in=w_in[i],
                 da_lambda=da_lambda[i], da_norm_w=da_norm_w[i], rw_shift=rw_shift[i],
                 rw_w0=rw_w0[i], rw_w_up=rw_w_up[i], rw_a0=rw_a0[i], rw_a_up=rw_a_up[i],
                 rw_g_up=rw_g_up[i], rw_k_k=rw_k_k[i], rw_k_a=rw_k_a[i], rw_r_k=rw_r_k[i],
                 rw_ln_w=rw_ln_w[i], rw_ln_b=rw_ln_b[i], ret_decay_logit=ret_decay_logit[i],
                 ret_norm_w=ret_norm_w[i], cv_dw_w=cv_dw_w[i], cv_dw_b=cv_dw_b[i],
                 cv_ln_w=cv_ln_w[i], cv_ln_b=cv_ln_b[i], w_branch=w_branch[i], w_out=w_out[i],
                 router_w=router_w[i], router_b=router_b[i], moe_w1=moe_w1[i], moe_b1=moe_b1[i],
                 moe_w2=moe_w2[i], moe_b2=moe_b2[i])
        lam_init = 0.8 - 0.6 * math.exp(-0.3 * i)
        mod_ctx = (jax.nn.silu(c_ctx) @ ada_w[i] + ada_b[i])[None, None, :]
        mod_lat = (jax.nn.silu(c) @ ada_w[i] + ada_b[i])[:, None, :]
        xp, (k_i, v_i, rw_i, ret_i) = trunk_layer(xp, mod_ctx, p, lam_init, None, None)
        ks.append(k_i)
        vs.append(v_i)
        rws.append(rw_i)
        rets.append(ret_i)
        xs, _ = trunk_layer(xs, mod_lat, p, lam_init, rope,
                            (cache_da_k[:, i], cache_da_v[:, i], state_rwkv[:, i], state_ret[:, i]))
    y_prompt = rmsnorm(xp, final_norm_w)
    y_sample = rmsnorm(xs, final_norm_w)
    new_da_k = jnp.stack(ks, axis=1)
    new_da_v = jnp.stack(vs, axis=1)
    new_rwkv_state = jnp.stack(rws, axis=1)
    new_ret_state = jnp.stack(rets, axis=1)
    return (y_prompt, y_sample, new_da_k, new_da_v, new_rwkv_state, new_ret_state)
```

```python
import functools
import math

import jax
import jax.numpy as jnp
from jax import lax
from jax.experimental import pallas as pl
from jax.experimental.pallas import tpu as pltpu

F32 = jnp.float32
BF16 = jnp.bfloat16

D_MODEL = 1024
BATCH = 16
SEQ = 256
DEPTH = 2
DEC_BATCH = 2
DEC_SEQ = 4096
PAST_LEN = 512
GRID_W = 64
EPS = 1e-6

DA_HEADS = 4
DA_QK = 64
DA_V = 128
ROPE_BASE = 10000.0

RW_HEADS = 8
RW_HD = 64
RW_W = 512
RW_LORA = 64
RW_G_LORA = 128
RW_GN_EPS = 64e-5

RET_HEADS = 4
RET_DK = 64
RET_DV = 128
RET_CHUNK = 128

CV_W = 512
CONV_K = 31
BR_W = 512
N_BRANCH = 4

N_EXPERTS = 32
TOP_K = 4
D_FF = 1024
SWIGLU_LIMIT = 7.0
SWIGLU_ALPHA = 1.702

N_CTX = BATCH * SEQ
N_LAT = DEC_BATCH * DEC_SEQ
N_TOK = N_CTX + N_LAT
N_SEQS = BATCH + DEC_BATCH
MOD_ROWS = 8
MOD_GROUP = 4096

RW_COLS = 1920
P_RW = 0
P_DA = 2048
P_RET = 3584
P_CV = 5120
P_GATE = 6144
P_COLS = 10240

VMEM_LIMIT = 48 * 1024 * 1024


def _cparams(sem):
    return pltpu.CompilerParams(dimension_semantics=sem, vmem_limit_bytes=VMEM_LIMIT)


def _dg(a, b, dims):
    return lax.dot_general(a, b, (dims, ((), ())), preferred_element_type=F32)


NN = ((1,), (0,))
NT = ((1,), (1,))
TN = ((0,), (0,))


def _dot(a, b, dims=NN):
    return _dg(a.astype(BF16), b.astype(BF16), dims)


def _split(x):
    hi = x.astype(BF16)
    lo = (x - hi.astype(F32)).astype(BF16)
    return hi, lo


def _dot3(a, b, dims=NN):
    ah, al = _split(a)
    bh, bl = _split(b)
    return _dg(ah, bh, dims) + (_dg(ah, bl, dims) + _dg(al, bh, dims))


def _dot2x(a, e, dims=NN):
    ah, al = _split(a)
    am = (a - ah.astype(F32) - al.astype(F32)).astype(BF16)
    eb = e.astype(BF16)
    return _dg(ah, eb, dims) + (_dg(al, eb, dims) + _dg(am, eb, dims))


def _sigmoid(x):
    return 1.0 / (1.0 + jnp.exp(-x))


def _softplus(x):
    return jnp.maximum(x, 0.0) + jnp.log(1.0 + jnp.exp(-jnp.abs(x)))


def _seq_pos(row):
    in_ctx = row < N_CTX
    pos = jnp.where(in_ctx, row & (SEQ - 1), (row - N_CTX) & (DEC_SEQ - 1))
    length = jnp.where(in_ctx, SEQ, DEC_SEQ)
    return pos, length


def _seq_index(row):
    return jnp.where(row < N_CTX, row // SEQ, BATCH + (row - N_CTX) // DEC_SEQ)


def _dotx(e, b, dims=NN):
    bh, bl = _split(b)
    bm = (b - bh.astype(F32) - bl.astype(F32)).astype(BF16)
    eb = e.astype(BF16)
    return _dg(eb, bh, dims) + (_dg(eb, bl, dims) + _dg(eb, bm, dims))


def _mod_kernel(c_ref, w_ref, b_ref, o_ref):
    c = c_ref[...]
    s = c * _sigmoid(c)
    o_ref[0] = _dot3(s, w_ref[0]) + b_ref[0]


def _modulation(cvec, ada_w, ada_b):
    tn = 1536
    return pl.pallas_call(
        _mod_kernel,
        grid=(DEPTH, 6 * D_MODEL // tn),
        in_specs=[
            pl.BlockSpec((MOD_ROWS, D_MODEL), lambda l, j: (0, 0)),
            pl.BlockSpec((1, D_MODEL, tn), lambda l, j: (l, 0, j)),
            pl.BlockSpec((1, 1, tn), lambda l, j: (l, 0, j)),
        ],
        out_specs=pl.BlockSpec((1, MOD_ROWS, tn), lambda l, j: (l, 0, j)),
        out_shape=jax.ShapeDtypeStruct((DEPTH, MOD_ROWS, 6 * D_MODEL), F32),
        compiler_params=_cparams(("parallel", "parallel")),
        name="modulation",
    )(cvec, ada_w, ada_b.reshape(DEPTH, 1, 6 * D_MODEL))


def _mod_row(mod_ref, first_row):
    g = first_row // MOD_GROUP
    return mod_ref[pl.ds(g, 1), :]


def _rms(x, w):
    return x * lax.rsqrt(jnp.mean(x * x, axis=-1, keepdims=True) + EPS) * w


IN_TM = 1024
IN_TN = 1024


def _inproj_kernel(x_ref, mod_ref, nw_ref, w_ref, o_ref, h_ref):
    i = pl.program_id(0)

    @pl.when(pl.program_id(1) == 0)
    def _():
        m = _mod_row(mod_ref, i * IN_TM)
        sh = m[:, 0:D_MODEL]
        sc = m[:, D_MODEL:2 * D_MODEL]
        h_ref[...] = (_rms(x_ref[...], nw_ref[...]) * (1.0 + sc) + sh).astype(BF16)

    o_ref[...] = _dg(h_ref[...], w_ref[...].astype(BF16), NN)


def _input_projection(x, mod, norm_w, w_p):
    n = x.shape[0]
    return pl.pallas_call(
        _inproj_kernel,
        grid=(n // IN_TM, P_COLS // IN_TN),
        in_specs=[
            pl.BlockSpec((IN_TM, D_MODEL), lambda i, j: (i, 0)),
            pl.BlockSpec((MOD_ROWS, 6 * D_MODEL), lambda i, j: (0, 0)),
            pl.BlockSpec((1, D_MODEL), lambda i, j: (0, 0)),
            pl.BlockSpec((D_MODEL, IN_TN), lambda i, j: (0, j)),
        ],
        out_specs=pl.BlockSpec((IN_TM, IN_TN), lambda i, j: (i, j)),
        out_shape=jax.ShapeDtypeStruct((n, P_COLS), F32),
        scratch_shapes=[pltpu.VMEM((IN_TM, D_MODEL), BF16)],
        compiler_params=_cparams(("parallel", "arbitrary")),
        name="input_projection",
    )(x, mod, norm_w.reshape(1, D_MODEL), w_p)


QK_TM = 512
DA_W = DA_HEADS * 2 * DA_QK


def _qkprep_kernel(q_ref, k_ref, v_ref, c_ref, se_ref, so_ref, qo_ref, ko_ref, vo_ref):
    i = pl.program_id(0)
    vo_ref[...] = v_ref[...].astype(BF16)
    scale = DA_QK ** -0.5

    @pl.when(i * QK_TM < N_CTX)
    def _():
        qo_ref[...] = (q_ref[...] * scale).astype(BF16)
        ko_ref[...] = k_ref[...].astype(BF16)

    @pl.when(i * QK_TM >= N_CTX)
    def _():
        c = c_ref[...]
        se = se_ref[...]
        so = so_ref[...]

        def rope(x):
            nxt = pltpu.roll(x, DA_W - 1, axis=1)
            prv = pltpu.roll(x, 1, axis=1)
            return x * c + nxt * se + prv * so

        qo_ref[...] = (rope(q_ref[...]) * scale).astype(BF16)
        ko_ref[...] = rope(k_ref[...]).astype(BF16)


def _rope_tables():
    rows = DEC_SEQ // GRID_W
    row = jnp.repeat(jnp.arange(rows, dtype=F32), GRID_W)
    col = jnp.tile(jnp.arange(GRID_W, dtype=F32), rows)
    n_pairs = DA_QK // 4
    inv = ROPE_BASE ** (-jnp.arange(n_pairs, dtype=F32) / n_pairs)
    ang = jnp.concatenate([row[:, None] * inv, col[:, None] * inv], axis=-1)
    cos = jnp.repeat(jnp.cos(ang), 2, axis=-1)
    sin = jnp.repeat(jnp.sin(ang), 2, axis=-1)
    even = (jnp.arange(DA_QK) % 2 == 0)[None, :]
    s_even = jnp.where(even, -sin, 0.0)
    s_odd = jnp.where(even, 0.0, sin)
    rep = lambda t: jnp.tile(t, (1, DA_W // DA_QK))
    return rep(cos), rep(s_even), rep(s_odd)


def _qk_prepare(proj, tables):
    n = proj.shape[0]
    lat0 = N_CTX // QK_TM
    nlat = DEC_SEQ // QK_TM
    tab = pl.BlockSpec((QK_TM, DA_W), lambda i: (jnp.maximum(i - lat0, 0) % nlat, 0))
    c0 = P_DA // DA_W
    out = jax.ShapeDtypeStruct((n, DA_W), BF16)
    return pl.pallas_call(
        _qkprep_kernel,
        grid=(n // QK_TM,),
        in_specs=[
            pl.BlockSpec((QK_TM, DA_W), lambda i: (i, c0)),
            pl.BlockSpec((QK_TM, DA_W), lambda i: (i, c0 + 1)),
            pl.BlockSpec((QK_TM, DA_W), lambda i: (i, c0 + 2)),
            tab, tab, tab,
        ],
        out_specs=[pl.BlockSpec((QK_TM, DA_W), lambda i: (i, 0))] * 3,
        out_shape=[out, out, out],
        compiler_params=_cparams(("parallel",)),
        name="qk_prepare",
    )(proj, proj, proj, *tables)


DA_TQ = 128


def _da_kernel(q_ref, k_ref, v_ref, dl_ref, nw_ref, o_ref, *, lam_init):
    dl = dl_ref[...]
    lam = (jnp.exp(jnp.sum(dl[0:1] * dl[1:2], axis=1, keepdims=True))
           - jnp.exp(jnp.sum(dl[2:3] * dl[3:4], axis=1, keepdims=True)) + lam_init)
    nw = nw_ref[...] * (1.0 - lam_init)
    q = q_ref[0]
    for h in range(DA_HEADS):
        es, rs = [], []
        for m in range(2):
            c0 = (2 * h + m) * DA_QK
            s = _dg(q[:, c0:c0 + DA_QK], k_ref[0, :, c0:c0 + DA_QK], NT)
            e = jnp.exp(s - jnp.max(s, axis=-1, keepdims=True))
            es.append(e)
            rs.append(1.0 / jnp.sum(e, axis=-1, keepdims=True))
        a = es[0] * rs[0] - es[1] * (lam * rs[1])
        o = _dg(a.astype(BF16), v_ref[0, :, h * DA_V:(h + 1) * DA_V], NN)
        o_ref[0, :, h * DA_V:(h + 1) * DA_V] = _rms(o, nw)


def _diff_attention(q, k, v, da_lambda, da_norm_w, lam_init):
    b, tq, _ = q.shape
    tk = k.shape[1]
    return pl.pallas_call(
        functools.partial(_da_kernel, lam_init=lam_init),
        grid=(b, tq // DA_TQ),
        in_specs=[
            pl.BlockSpec((1, DA_TQ, DA_W), lambda i, j: (i, j, 0)),
            pl.BlockSpec((1, tk, DA_W), lambda i, j: (i, 0, 0)),
            pl.BlockSpec((1, tk, DA_W), lambda i, j: (i, 0, 0)),
            pl.BlockSpec((4, DA_QK), lambda i, j: (0, 0)),
            pl.BlockSpec((1, DA_V), lambda i, j: (0, 0)),
        ],
        out_specs=pl.BlockSpec((1, DA_TQ, DA_W), lambda i, j: (i, j, 0)),
        out_shape=jax.ShapeDtypeStruct((b, tq, DA_W), F32),
        compiler_params=_cparams(("parallel", "arbitrary")),
        name="diff_attention",
    )(q, k, v, da_lambda, da_norm_w.reshape(1, DA_V))


def _da_branch(proj, cache_k, cache_v, da_lambda, da_norm_w, lam_init, tables):
    qb, kb, vb = _qk_prepare(proj, tables)
    ctx = lambda t: t[:N_CTX].reshape(BATCH, SEQ, DA_W)
    lat = lambda t: t[N_CTX:].reshape(DEC_BATCH, DEC_SEQ, DA_W)
    o_ctx = _diff_attention(ctx(qb), ctx(kb), ctx(vb), da_lambda, da_norm_w, lam_init)
    k_all = jnp.concatenate([cache_k.reshape(DEC_BATCH, PAST_LEN, DA_W).astype(BF16), lat(kb)], axis=1)
    v_all = jnp.concatenate([cache_v.reshape(DEC_BATCH, PAST_LEN, DA_W).astype(BF16), lat(vb)], axis=1)
    o_lat = _diff_attention(lat(qb), k_all, v_all, da_lambda, da_norm_w, lam_init)
    return jnp.concatenate([o_ctx.reshape(N_CTX, DA_W), o_lat.reshape(N_LAT, DA_W)], axis=0)


RW_TM = 256
RW_C = 64
RW_PAIRS = RW_HEADS // 2
HALO = 8


def _head_ones():
    idx = jnp.arange(RW_W) // RW_HD
    return (idx[:, None] == idx[None, :]).astype(BF16)


def _rwprep_kernel(x_ref, xp_ref, xn_ref, mu_ref, w0_ref, wup_ref, a0_ref, aup_ref, gup_ref, kk_ref, ka_ref,
                   ones_ref, r_o, v_o, kk_o, g_o, kd_o, lw_o, a_o, buf):
    row0 = pl.program_id(0) * RW_TM
    x = x_ref[:, 0:RW_COLS]
    buf[HALO:HALO + RW_TM, :] = x
    buf[HALO - 1:HALO, :] = xp_ref[HALO - 1:HALO, 0:RW_COLS]
    buf[HALO + RW_TM:HALO + RW_TM + 1, :] = xn_ref[0:1, 0:RW_COLS]
    rows = row0 + lax.broadcasted_iota(jnp.int32, (RW_TM, 1), 0)
    pos, length = _seq_pos(rows)
    prev = jnp.where(pos == 0, 0.0, buf[HALO - 1:HALO - 1 + RW_TM, :])
    nxt = jnp.where(pos == length - 1, 0.0, buf[HALO + 1:HALO + 1 + RW_TM, :])
    mu = mu_ref[...]
    u = x + mu[0:1] * (prev - x) + mu[1:2] * (nxt - x)

    r = u[:, 0:RW_W]
    k = u[:, RW_W:2 * RW_W]
    v = u[:, 2 * RW_W:3 * RW_W]
    wl = u[:, 3 * RW_W:3 * RW_W + 128]
    al = u[:, 3 * RW_W + 128:3 * RW_W + 256]
    gl = u[:, 3 * RW_W + 256:3 * RW_W + 384]
    w_raw = w0_ref[...] + _dot3(jnp.tanh(wl), wup_ref[...])
    lw = -jnp.exp(-_softplus(-w_raw) - 0.5)
    a = _sigmoid(a0_ref[...] + _dot3(al, aup_ref[...]))
    g = _dot3(_sigmoid(gl), gup_ref[...])
    kk = k * kk_ref[...]
    kk = kk * lax.rsqrt(jnp.maximum(_dot2x(kk * kk, ones_ref[...]), 1e-12))
    kd = jnp.concatenate([k, k], axis=1) * (1.0 + (a - 1.0) * ka_ref[...])

    r_o[...] = r
    v_o[...] = v
    kk_o[...] = kk
    g_o[...] = g
    for d in range(2):
        kd_o[d] = kd[:, d * RW_W:(d + 1) * RW_W]
        lw_o[d] = lw[:, d * RW_W:(d + 1) * RW_W]
        a_o[d] = a[:, d * RW_W:(d + 1) * RW_W]


def _rw_prepare(proj, p):
    n = proj.shape[0]
    nh = n // HALO
    steps = RW_TM // HALO
    wide = P_DA - P_RW
    cat2 = lambda t: t.reshape(1, 2 * RW_W)
    blockdiag = lambda t: jnp.concatenate(
        [jnp.concatenate([t[0], jnp.zeros_like(t[0])], axis=1),
         jnp.concatenate([jnp.zeros_like(t[1]), t[1]], axis=1)], axis=0)
    const = lambda shape: pl.BlockSpec(shape, lambda i: (0,) * len(shape))
    row = pl.BlockSpec((RW_TM, RW_W), lambda i: (i, 0))
    row2 = pl.BlockSpec((2, RW_TM, RW_W), lambda i: (0, i, 0))
    o1 = jax.ShapeDtypeStruct((n, RW_W), F32)
    o2 = jax.ShapeDtypeStruct((2, n, RW_W), F32)
    return pl.pallas_call(
        _rwprep_kernel,
        grid=(n // RW_TM,),
        in_specs=[
            pl.BlockSpec((RW_TM, wide), lambda i: (i, 0)),
            pl.BlockSpec((HALO, wide), lambda i: (jnp.maximum(i * steps - 1, 0), 0)),
            pl.BlockSpec((HALO, wide), lambda i: (jnp.minimum((i + 1) * steps, nh - 1), 0)),
            const((2, RW_COLS)), const((1, 2 * RW_W)), const((128, 2 * RW_W)), const((1, 2 * RW_W)),
            const((128, 2 * RW_W)), const((RW_G_LORA, RW_W)), const((1, RW_W)), const((1, 2 * RW_W)),
            const((RW_W, RW_W)),
        ],
        out_specs=[row, row, row, row, row2, row2, row2],
        out_shape=[o1, o1, o1, o1, o2, o2, o2],
        scratch_shapes=[pltpu.VMEM((RW_TM + 2 * HALO, RW_COLS), F32)],
        compiler_params=_cparams(("parallel",)),
        name="rwkv_prepare",
    )(proj, proj, proj, p['rw_shift'], cat2(p['rw_w0']), blockdiag(p['rw_w_up']), cat2(p['rw_a0']),
      blockdiag(p['rw_a_up']), p['rw_g_up'], p['rw_k_k'].reshape(1, RW_W),
      jnp.tile(p['rw_k_a'].reshape(1, RW_W), (1, 2)), _head_ones())


def _rwscan_kernel(r_ref, v_ref, kk_ref, kd_ref, lw_ref, a_ref, s0_ref, y_ref, st_ref, s_scr):
    d = pl.program_id(0)
    step = pl.program_id(1)
    nch = pl.num_programs(1)
    ci = jnp.where(d == 0, step, nch - 1 - step)
    pos, length = _seq_pos(ci * RW_C)
    first = jnp.where(d == 0, pos == 0, pos + RW_C == length)

    @pl.when(first)
    def _():
        s_scr[...] = s0_ref[0, 0]

    c = RW_C
    sgn = jnp.where(d == 0, 1, -1)
    ri = lax.broadcasted_iota(jnp.int32, (c, c), 0)
    cj = lax.broadcasted_iota(jnp.int32, (c, c), 1)
    before = ((ri - cj) * sgn >= 0).astype(F32)
    r2 = lax.broadcasted_iota(jnp.int32, (2 * c, 2 * c), 0)
    c2 = lax.broadcasted_iota(jnp.int32, (2 * c, 2 * c), 1)
    same = (r2 // c) == (c2 // c)
    dif2 = ((r2 % c) - (c2 % c)) * sgn
    incl2 = same & (dif2 >= 0)
    strict2 = same & (dif2 > 0)
    eye2 = r2 == c2
    m_e = lax.broadcasted_iota(jnp.int32, (1, 2 * c), 1) < c

    lw = lw_ref[0]
    cum = _dotx(before, lw)
    tot = jnp.sum(lw, axis=0, keepdims=True)
    kk = kk_ref[...]
    kd = kd_ref[0]
    bp = kk * a_ref[0]
    g_inv = jnp.exp(-cum)
    g_rem = jnp.exp(tot - cum)
    ag = -kk * jnp.exp(cum - lw)
    rg = r_ref[...] * jnp.exp(cum)
    bdn = bp * g_inv
    kdn = kd * g_inv
    bc = bp * g_rem
    kc = kd * g_rem
    gt = jnp.exp(tot)
    v = v_ref[...]

    def stack(x, p):
        xs = x[:, p * 2 * c:(p + 1) * 2 * c]
        return jnp.concatenate([jnp.where(m_e, xs, 0.0), jnp.where(m_e, 0.0, xs)], axis=0)

    for p in range(RW_PAIRS):
        agb, rgb, bdb, kdb, bcb, kcb, vb = (stack(t, p) for t in (ag, rg, bdn, kdn, bc, kc, v))
        gm = _dot3(jnp.concatenate([agb, rgb], axis=0), jnp.concatenate([bdb, kdb], axis=0), NT)
        lbb = jnp.where(strict2, gm[:2 * c, :2 * c], 0.0)
        lkb = jnp.where(strict2, gm[:2 * c, 2 * c:], 0.0)
        lbr = jnp.where(incl2, gm[2 * c:, :2 * c], 0.0)
        lkr = jnp.where(incl2, gm[2 * c:, 2 * c:], 0.0)
        x = jnp.where(eye2, 1.0, lbb)
        pw = lbb
        for _ in range(int(math.log2(c)) - 1):
            pw = _dot3(pw, pw)
            x = x + _dot3(x, pw)
        tw = _dot3(x, jnp.concatenate([_dot3(lkb, vb), agb], axis=1))
        w1 = tw[:, :2 * c]
        tag = tw[:, 2 * c:]
        s0 = s_scr[p]
        u = _dot3(tag, s0) + w1
        y = _dot3(jnp.concatenate([rgb, lbr, lkr], axis=1), jnp.concatenate([s0, u, vb], axis=0))
        y_ref[0, :, p * 2 * c:(p + 1) * 2 * c] = y[:c] + y[c:]
        dg = jnp.where(eye2, jnp.broadcast_to(gt[:, p * 2 * c:(p + 1) * 2 * c], (2 * c, 2 * c)), 0.0)
        sn = _dot3(jnp.concatenate([bcb, kcb, dg], axis=0), jnp.concatenate([u, vb, s0], axis=0), TN)
        s_scr[p] = sn
        st_ref[0, 0, p] = sn


def _rw_scan(r, v, kk, kd, lw, a, s0):
    n = r.shape[0]
    nch = n // RW_C
    ci = lambda d, s: jnp.where(d == 0, s, nch - 1 - s)
    row = pl.BlockSpec((RW_C, RW_W), lambda d, s: (ci(d, s), 0))
    row2 = pl.BlockSpec((1, RW_C, RW_W), lambda d, s: (d, ci(d, s), 0))
    st = pl.BlockSpec((1, 1, RW_PAIRS, 2 * RW_C, 2 * RW_C),
                      lambda d, s: (_seq_index(ci(d, s) * RW_C), d, 0, 0, 0))
    return pl.pallas_call(
        _rwscan_kernel,
        grid=(2, nch),
        in_specs=[row, row, row, row2, row2, row2, st],
        out_specs=[row2, st],
        out_shape=[jax.ShapeDtypeStruct((2, n, RW_W), F32),
                   jax.ShapeDtypeStruct((N_SEQS, 2, RW_PAIRS, 2 * RW_C, 2 * RW_C), F32)],
        scratch_shapes=[pltpu.VMEM((RW_PAIRS, 2 * RW_C, 2 * RW_C), F32)],
        compiler_params=_cparams(("arbitrary", "arbitrary")),
        name="rwkv_scan",
    )(r, v, kk, kd, lw, a, s0)


def _rwpost_kernel(y_ref, r_ref, v_ref, g_ref, kd_ref, rk_ref, lnw_ref, lnb_ref, ones_ref, o_ref):
    ones = ones_ref[...]
    inv = 1.0 / RW_HD
    y = y_ref[0] + y_ref[1]
    xc = y - _dot2x(y, ones) * inv
    var = _dot2x(xc * xc, ones) * inv
    yn = xc * lax.rsqrt(var + RW_GN_EPS) * lnw_ref[...] + lnb_ref[...]
    bonus = _dot2x(r_ref[...] * (kd_ref[0] + kd_ref[1]) * rk_ref[...], ones) * v_ref[...]
    o_ref[...] = (yn + bonus) * g_ref[...]


def _rw_post(y, r, v, g, kd, p):
    n = r.shape[0]
    row = pl.BlockSpec((RW_TM, RW_W), lambda i: (i, 0))
    row2 = pl.BlockSpec((2, RW_TM, RW_W), lambda i: (0, i, 0))
    vec = pl.BlockSpec((1, RW_W), lambda i: (0, 0))
    return pl.pallas_call(
        _rwpost_kernel,
        grid=(n // RW_TM,),
        in_specs=[row2, row, row, row, row2, vec, vec, vec, pl.BlockSpec((RW_W, RW_W), lambda i: (0, 0))],
        out_specs=row,
        out_shape=jax.ShapeDtypeStruct((n, RW_W), F32),
        compiler_params=_cparams(("parallel",)),
        name="rwkv_post",
    )(y, r, v, g, kd, p['rw_r_k'].reshape(1, RW_W), p['rw_ln_w'].reshape(1, RW_W),
      p['rw_ln_b'].reshape(1, RW_W), _head_ones())


def _pairs_from_heads(s):
    lead = s.shape[:-3]
    st = jnp.swapaxes(s, -1, -2).reshape(lead + (RW_PAIRS, 2, RW_HD, RW_HD))
    eye = jnp.eye(2, dtype=s.dtype)
    out = st[..., :, :, :, None, :] * eye[:, None, :, None]
    return out.reshape(lead + (RW_PAIRS, 2 * RW_HD, 2 * RW_HD))


def _heads_from_pairs(s):
    lead = s.shape[:-3]
    s6 = s.reshape(lead + (RW_PAIRS, 2, RW_HD, 2, RW_HD))
    diag = jnp.stack([s6[..., 0, :, 0, :], s6[..., 1, :, 1, :]], axis=-3)
    return jnp.swapaxes(diag.reshape(lead + (RW_HEADS, RW_HD, RW_HD)), -1, -2)


def _rwkv_branch(proj, state0, p):
    r, v, kk, g, kd, lw, a = _rw_prepare(proj, p)
    s0 = jnp.concatenate([jnp.zeros((BATCH,) + state0.shape[1:], F32), state0], axis=0)
    y, st = _rw_scan(r, v, kk, kd, lw, a, _pairs_from_heads(s0))
    return _rw_post(y, r, v, g, kd, p), _heads_from_pairs(st)


RET_W = RET_HEADS * RET_DV
RET_PAIRS = RET_HEADS // 2


def _ret_kernel(q_ref, k_ref, v_ref, lg_ref, s0_ref, o_ref, st_ref, s_scr):
    d = pl.program_id(0)
    step = pl.program_id(1)
    nch = pl.num_programs(1)
    c = RET_CHUNK
    ci = jnp.where(d == 0, step, nch - 1 - step)
    pos, length = _seq_pos(ci * c)
    first = jnp.where(d == 0, pos == 0, pos + c == length)

    @pl.when(first)
    def _():
        s_scr[...] = s0_ref[0, 0]

    sgn = jnp.where(d == 0, 1, -1)
    lgs = -_softplus(-lg_ref[pl.ds(d, 1), :])
    ri = lax.broadcasted_iota(jnp.int32, (c, c), 0)
    cj = lax.broadcasted_iota(jnp.int32, (c, c), 1)
    dif = (ri - cj) * sgn
    valid = dif >= 0
    dist = jnp.maximum(dif, 0).astype(F32)
    pr = lax.broadcasted_iota(jnp.int32, (c, 2 * RET_DK), 0)
    tau = jnp.where(d == 0, pr, c - 1 - pr).astype(F32)
    low = lax.broadcasted_iota(jnp.int32, (1, 2 * RET_DK), 1) < RET_DK
    rlow = lax.broadcasted_iota(jnp.int32, (2 * RET_DK, RET_DV), 0) < RET_DK
    q = q_ref[...]
    k = k_ref[...] * (RET_DK ** -0.5)
    for p in range(RET_PAIRS):
        lg_e = lgs[:, 2 * p:2 * p + 1]
        lg_o = lgs[:, 2 * p + 1:2 * p + 2]
        lg_row = jnp.where(low, lg_e, lg_o)
        qp = q[:, p * 128:(p + 1) * 128]
        kp = k[:, p * 128:(p + 1) * 128]
        qd = qp * jnp.exp(lg_row * (tau + 1.0))
        kdec = kp * jnp.exp(lg_row * (c - 1.0 - tau))
        sp = s_scr[p]
        sn = sp * jnp.where(rlow, jnp.exp(lg_e * c), jnp.exp(lg_o * c))
        for j, lg in ((0, lg_e), (1, lg_o)):
            h = 2 * p + j
            mj = low if j == 0 else jnp.logical_not(low)
            vj = v_ref[:, h * RET_DV:(h + 1) * RET_DV]
            att = _dot3(jnp.where(mj, qp, 0.0), kp, NT) * jnp.where(valid, jnp.exp(lg * dist), 0.0)
            o_ref[0, :, h * RET_DV:(h + 1) * RET_DV] = _dot3(att, vj) + _dot3(jnp.where(mj, qd, 0.0), sp)
            sn = sn + _dot3(jnp.where(mj, kdec, 0.0), vj, TN)
        s_scr[p] = sn
        st_ref[0, 0, p] = sn


def _ret_scan(proj, logit, s0):
    n = proj.shape[0]
    c = RET_CHUNK
    nch = n // c
    ci = lambda d, s: jnp.where(d == 0, s, nch - 1 - s)
    qk_w = RET_HEADS * RET_DK
    st = pl.BlockSpec((1, 1, RET_PAIRS, 2 * RET_DK, RET_DV),
                      lambda d, s: (_seq_index(ci(d, s) * c), d, 0, 0, 0))
    return pl.pallas_call(
        _ret_kernel,
        grid=(2, nch),
        in_specs=[
            pl.BlockSpec((c, qk_w), lambda d, s: (ci(d, s), P_RET // qk_w)),
            pl.BlockSpec((c, qk_w), lambda d, s: (ci(d, s), P_RET // qk_w + 1)),
            pl.BlockSpec((c, RET_W), lambda d, s: (ci(d, s), (P_RET + 2 * qk_w) // RET_W)),
            pl.BlockSpec((2, RET_HEADS), lambda d, s: (0, 0)),
            st,
        ],
        out_specs=[pl.BlockSpec((1, c, RET_W), lambda d, s: (d, ci(d, s), 0)), st],
        out_shape=[jax.ShapeDtypeStruct((2, n, RET_W), F32),
                   jax.ShapeDtypeStruct((N_SEQS, 2, RET_PAIRS, 2 * RET_DK, RET_DV), F32)],
        scratch_shapes=[pltpu.VMEM((RET_PAIRS, 2 * RET_DK, RET_DV), F32)],
        compiler_params=_cparams(("arbitrary", "arbitrary")),
        name="retention_scan",
    )(proj, proj, proj, logit, s0)


def _standardize(x, eps):
    mu = jnp.mean(x, axis=-1, keepdims=True)
    xc = x - mu
    return xc * lax.rsqrt(jnp.mean(xc * xc, axis=-1, keepdims=True) + eps)


def _retpost_kernel(o_ref, g_ref, nw_ref, out_ref):
    o = o_ref[0] + o_ref[1]
    g = g_ref[...]
    for h in range(RET_HEADS):
        sl = slice(h * RET_DV, (h + 1) * RET_DV)
        gh = g[:, sl]
        out_ref[:, sl] = gh * _sigmoid(gh) * (_standardize(o[:, sl], EPS) * nw_ref[:, sl])


def _ret_post(o, proj, norm_w):
    n = proj.shape[0]
    tm = 512
    return pl.pallas_call(
        _retpost_kernel,
        grid=(n // tm,),
        in_specs=[
            pl.BlockSpec((2, tm, RET_W), lambda i: (0, i, 0)),
            pl.BlockSpec((tm, RET_W), lambda i: (i, (P_RET + 1024) // RET_W)),
            pl.BlockSpec((1, RET_W), lambda i: (0, 0)),
        ],
        out_specs=pl.BlockSpec((tm, RET_W), lambda i: (i, 0)),
        out_shape=jax.ShapeDtypeStruct((n, RET_W), F32),
        compiler_params=_cparams(("parallel",)),
        name="retention_post",
    )(o, proj, norm_w.reshape(1, RET_W))


def _retention_branch(proj, state0, p):
    s0 = jnp.concatenate([jnp.zeros((BATCH,) + state0.shape[1:], F32), state0], axis=0)
    s0 = s0.reshape(N_SEQS, 2, RET_PAIRS, 2 * RET_DK, RET_DV)
    o, st = _ret_scan(proj, p['ret_decay_logit'], s0)
    return _ret_post(o, proj, p['ret_norm_w']), st.reshape(N_SEQS, 2, RET_HEADS, RET_DK, RET_DV)


CV_TM = 256
CV_HALO = 16


def _conv_kernel(a_ref, g_ref, ap_ref, gp_ref, an_ref, gn_ref, w_ref, b_ref, lnw_ref, lnb_ref, o_ref, buf):
    row0 = pl.program_id(0) * CV_TM
    pos0, len0 = _seq_pos(row0)
    buf[CV_HALO:CV_HALO + CV_TM, :] = a_ref[...] * _sigmoid(g_ref[...])
    buf[0:CV_HALO, :] = jnp.where(pos0 == 0, 0.0, ap_ref[...] * _sigmoid(gp_ref[...]))
    buf[CV_HALO + CV_TM:, :] = jnp.where(pos0 + CV_TM == len0, 0.0, an_ref[...] * _sigmoid(gn_ref[...]))
    base = CV_HALO - CONV_K // 2
    acc = jnp.zeros((CV_TM, CV_W), F32)
    for j in range(CONV_K):
        acc = acc + w_ref[j:j + 1, :] * buf[base + j:base + j + CV_TM, :]
    z = _standardize(acc + b_ref[...], EPS) * lnw_ref[...] + lnb_ref[...]
    o_ref[...] = z * _sigmoid(z)


def _conv_branch(proj, p):
    n = proj.shape[0]
    nh = n // CV_HALO
    steps = CV_TM // CV_HALO
    ca = P_CV // CV_W
    prev = lambda i: jnp.maximum(i * steps - 1, 0)
    nxt = lambda i: jnp.minimum((i + 1) * steps, nh - 1)
    vec = pl.BlockSpec((1, CV_W), lambda i: (0, 0))
    return pl.pallas_call(
        _conv_kernel,
        grid=(n // CV_TM,),
        in_specs=[
            pl.BlockSpec((CV_TM, CV_W), lambda i: (i, ca)),
            pl.BlockSpec((CV_TM, CV_W), lambda i: (i, ca + 1)),
            pl.BlockSpec((CV_HALO, CV_W), lambda i: (prev(i), ca)),
            pl.BlockSpec((CV_HALO, CV_W), lambda i: (prev(i), ca + 1)),
            pl.BlockSpec((CV_HALO, CV_W), lambda i: (nxt(i), ca)),
            pl.BlockSpec((CV_HALO, CV_W), lambda i: (nxt(i), ca + 1)),
            pl.BlockSpec((CONV_K, CV_W), lambda i: (0, 0)),
            vec, vec, vec,
        ],
        out_specs=pl.BlockSpec((CV_TM, CV_W), lambda i: (i, 0)),
        out_shape=jax.ShapeDtypeStruct((n, CV_W), F32),
        scratch_shapes=[pltpu.VMEM((CV_TM + 2 * CV_HALO, CV_W), F32)],
        compiler_params=_cparams(("parallel",)),
        name="conformer_conv",
    )(proj, proj, proj, proj, proj, proj, p['cv_dw_w'], p['cv_dw_b'].reshape(1, CV_W),
      p['cv_ln_w'].reshape(1, CV_W), p['cv_ln_b'].reshape(1, CV_W))


MG_TM = 512


def _merge_kernel(da_ref, rw_ref, ret_ref, cv_ref, g0_ref, g1_ref, g2_ref, g3_ref, x_ref, mod_ref, wb_ref, wo_ref,
                  nw_ref, x_o, h_o):
    m = None
    for n, (br, gt) in enumerate(((da_ref, g0_ref), (rw_ref, g1_ref), (ret_ref, g2_ref), (cv_ref, g3_ref))):
        t = _sigmoid(gt[...]) * _dg(br[...].astype(BF16), wb_ref[n], NN)
        m = t if m is None else m + t
    out = _dg(m.astype(BF16), wo_ref[...], NN)
    mrow = _mod_row(mod_ref, pl.program_id(0) * MG_TM)
    gate1 = mrow[:, 2 * D_MODEL:3 * D_MODEL]
    sh2 = mrow[:, 3 * D_MODEL:4 * D_MODEL]
    sc2 = mrow[:, 4 * D_MODEL:5 * D_MODEL]
    x1 = x_ref[...] + gate1 * out
    x_o[...] = x1
    h_o[...] = _rms(x1, nw_ref[...]) * (1.0 + sc2) + sh2


def _merge(branches, proj, x, mod, w_branch, w_out, norm_w):
    n = x.shape[0]
    br = pl.BlockSpec((MG_TM, BR_W), lambda i: (i, 0))
    gspec = lambda j: pl.BlockSpec((MG_TM, D_MODEL), lambda i: (i, P_GATE // D_MODEL + j))
    full = pl.BlockSpec((MG_TM, D_MODEL), lambda i: (i, 0))
    out = jax.ShapeDtypeStruct((n, D_MODEL), F32)
    return pl.pallas_call(
        _merge_kernel,
        grid=(n // MG_TM,),
        in_specs=[br, br, br, br, gspec(0), gspec(1), gspec(2), gspec(3), full,
                  pl.BlockSpec((MOD_ROWS, 6 * D_MODEL), lambda i: (0, 0)),
                  pl.BlockSpec((N_BRANCH, BR_W, D_MODEL), lambda i: (0, 0, 0)),
                  pl.BlockSpec((D_MODEL, D_MODEL), lambda i: (0, 0)),
                  pl.BlockSpec((1, D_MODEL), lambda i: (0, 0))],
        out_specs=[full, full],
        out_shape=[out, out],
        compiler_params=_cparams(("parallel",)),
        name="gated_merge",
    )(*branches, proj, proj, proj, proj, x, mod, w_branch.astype(BF16), w_out.astype(BF16),
      norm_w.reshape(1, D_MODEL))


RT_TM = 256
MOE_BM = 256
MOE_ROWS = N_TOK * TOP_K + N_EXPERTS * MOE_BM
DP_TM = 512
DP_GROUP = 64
CB_TM = 256


def _router_kernel(h_ref, w_ref, b_ref, tri_ref, idx_o, gate_o, rank_o, cnt_o, carry):
    @pl.when(pl.program_id(0) == 0)
    def _():
        carry[...] = jnp.zeros_like(carry)

    logits = _dot3(w_ref[...], h_ref[...], NT) + b_ref[...]
    e_iota = lax.broadcasted_iota(jnp.int32, logits.shape, 0)
    work = logits
    vals, idxs, hots = [], [], []
    for _ in range(TOP_K):
        mx = jnp.max(work, axis=0, keepdims=True)
        ix = jnp.min(jnp.where(work == mx, e_iota, N_EXPERTS), axis=0, keepdims=True)
        hot = e_iota == ix
        vals.append(mx)
        idxs.append(ix)
        hots.append(hot.astype(F32))
        work = jnp.where(hot, -jnp.inf, work)
    es = [jnp.exp(v - vals[0]) for v in vals]
    inv = 1.0 / (es[0] + es[1] + es[2] + es[3])
    chosen = hots[0] + hots[1] + hots[2] + hots[3]
    ahead = carry[...][:, 0:1] + _dg(chosen.astype(BF16), tri_ref[...], NN)
    idx_o[...] = jnp.concatenate(idxs, axis=0)
    gate_o[...] = jnp.concatenate([e * inv for e in es], axis=0)
    rank_o[...] = jnp.concatenate(
        [jnp.sum(hot * ahead, axis=0, keepdims=True) for hot in hots], axis=0).astype(jnp.int32)
    carry[...] = carry[...] + jnp.sum(chosen, axis=1, keepdims=True)
    cnt_o[...] = carry[...]


def _router(h, router_w, router_b):
    n = h.shape[0]
    tri = (jnp.arange(RT_TM)[:, None] < jnp.arange(RT_TM)[None, :]).astype(BF16)
    col = pl.BlockSpec((TOP_K, RT_TM), lambda i: (0, i))
    return pl.pallas_call(
        _router_kernel,
        grid=(n // RT_TM,),
        in_specs=[
            pl.BlockSpec((RT_TM, D_MODEL), lambda i: (i, 0)),
            pl.BlockSpec((N_EXPERTS, D_MODEL), lambda i: (0, 0)),
            pl.BlockSpec((N_EXPERTS, 1), lambda i: (0, 0)),
            pl.BlockSpec((RT_TM, RT_TM), lambda i: (0, 0)),
        ],
        out_specs=[col, col, col, pl.BlockSpec((N_EXPERTS, 128), lambda i: (0, 0))],
        out_shape=[jax.ShapeDtypeStruct((TOP_K, n), jnp.int32), jax.ShapeDtypeStruct((TOP_K, n), F32),
                   jax.ShapeDtypeStruct((TOP_K, n), jnp.int32), jax.ShapeDtypeStruct((N_EXPERTS, 128), F32)],
        scratch_shapes=[pltpu.VMEM((N_EXPERTS, 128), F32)],
        compiler_params=_cparams(("arbitrary",)),
        name="router",
    )(h, router_w.T, router_b.reshape(N_EXPERTS, 1), tri)


def _tile_major(t, tm):
    k, n = t.shape
    return t.reshape(k, n // tm, tm).transpose(1, 0, 2).reshape(n // tm, 1, k * tm)


def _dispatch_kernel(dest_ref, h_hbm, z_hbm, o_hbm, sem):
    del z_hbm
    row0 = pl.program_id(0) * DP_TM
    n_groups = DP_TM // DP_GROUP

    def wait_group(slot):
        pltpu.make_async_copy(h_hbm.at[pl.ds(0, TOP_K * DP_GROUP)], o_hbm.at[pl.ds(0, TOP_K * DP_GROUP)],
                              sem.at[slot]).wait()

    def group(gi, carry):
        slot = gi % 2

        def issue(t, c):
            tok = gi * DP_GROUP + t
            for k in range(TOP_K):
                dst = dest_ref[0, 0, k * DP_TM + tok]
                pltpu.make_async_copy(h_hbm.at[pl.ds(row0 + tok, 1)], o_hbm.at[pl.ds(dst, 1)],
                                      sem.at[slot]).start()
            return c

        lax.fori_loop(0, DP_GROUP, issue, 0)

        @pl.when(gi > 0)
        def _():
            wait_group(1 - slot)

        return carry

    lax.fori_loop(0, n_groups, group, 0)
    wait_group((n_groups - 1) % 2)


def _dispatch(h, dest):
    n = h.shape[0]
    zeros = jnp.zeros((MOE_ROWS, D_MODEL), F32)
    return pl.pallas_call(
        _dispatch_kernel,
        grid=(n // DP_TM,),
        in_specs=[
            pl.BlockSpec((1, 1, TOP_K * DP_TM), lambda i: (i, 0, 0), memory_space=pltpu.SMEM),
            pl.BlockSpec(memory_space=pl.ANY),
            pl.BlockSpec(memory_space=pl.ANY),
        ],
        out_specs=pl.BlockSpec(memory_space=pl.ANY),
        out_shape=jax.ShapeDtypeStruct((MOE_ROWS, D_MODEL), F32),
        scratch_shapes=[pltpu.SemaphoreType.DMA((2,))],
        input_output_aliases={2: 0},
        compiler_params=_cparams(("arbitrary",)),
        name="moe_dispatch",
    )(_tile_major(dest, DP_TM), h, zeros)


def _expert_kernel(be_ref, nu_ref, x_ref, w1_ref, b1_ref, w2_ref, b2_ref, o_ref, w1b, w2b):
    i = pl.program_id(0)
    changed = jnp.logical_or(i == 0, be_ref[i] != be_ref[jnp.maximum(i - 1, 0)])

    @pl.when(changed)
    def _():
        w1b[...] = w1_ref[0].astype(BF16)
        w2b[...] = w2_ref[0].astype(BF16)

    @pl.when(i < nu_ref[0])
    def _():
        hb = _dg(x_ref[...].astype(BF16), w1b[...], NN) + b1_ref[0]
        hg = jnp.minimum(hb[:, :D_FF], SWIGLU_LIMIT)
        hu = jnp.clip(hb[:, D_FF:], -SWIGLU_LIMIT, SWIGLU_LIMIT)
        act = hg * _sigmoid(SWIGLU_ALPHA * hg) * (hu + 1.0)
        o_ref[...] = _dg(act.astype(BF16), w2b[...], NN) + b2_ref[0]

    @pl.when(i >= nu_ref[0])
    def _():
        o_ref[...] = jnp.zeros_like(o_ref)


def _experts(x_rows, blk_e, n_used, w1, b1, w2, b2):
    nb = MOE_ROWS // MOE_BM
    grid_spec = pltpu.PrefetchScalarGridSpec(
        num_scalar_prefetch=2,
        grid=(nb,),
        in_specs=[
            pl.BlockSpec((MOE_BM, D_MODEL), lambda i, be, nu: (i, 0)),
            pl.BlockSpec((1, D_MODEL, 2 * D_FF), lambda i, be, nu: (be[i], 0, 0)),
            pl.BlockSpec((1, 1, 2 * D_FF), lambda i, be, nu: (be[i], 0, 0)),
            pl.BlockSpec((1, D_FF, D_MODEL), lambda i, be, nu: (be[i], 0, 0)),
            pl.BlockSpec((1, 1, D_MODEL), lambda i, be, nu: (be[i], 0, 0)),
        ],
        out_specs=pl.BlockSpec((MOE_BM, D_MODEL), lambda i, be, nu: (i, 0)),
        scratch_shapes=[pltpu.VMEM((D_MODEL, 2 * D_FF), BF16), pltpu.VMEM((D_FF, D_MODEL), BF16)],
    )
    return pl.pallas_call(
        _expert_kernel,
        grid_spec=grid_spec,
        out_shape=jax.ShapeDtypeStruct((MOE_ROWS, D_MODEL), F32),
        compiler_params=pltpu.CompilerParams(dimension_semantics=("arbitrary",),
                                             vmem_limit_bytes=56 * 1024 * 1024),
        name="moe_experts",
    )(blk_e, n_used, x_rows, w1, b1.reshape(N_EXPERTS, 1, 2 * D_FF), w2, b2.reshape(N_EXPERTS, 1, D_MODEL))


def _combine_kernel(dest_ref, gate_ref, x_ref, mod_ref, fw_ref, y_hbm, o_ref, buf, sem, *, final):
    def issue(t, c):
        for k in range(TOP_K):
            dst = dest_ref[0, 0, k * CB_TM + t]
            pltpu.make_async_copy(y_hbm.at[pl.ds(dst, 1)], buf.at[k, pl.ds(t, 1)], sem.at[0]).start()
        return c

    lax.fori_loop(0, CB_TM, issue, 0)
    for k in range(TOP_K):
        pltpu.make_async_copy(y_hbm.at[pl.ds(0, CB_TM)], buf.at[k], sem.at[0]).wait()
    g = gate_ref[...]
    acc = g[:, 0:1] * buf[0]
    for k in range(1, TOP_K):
        acc = acc + g[:, k:k + 1] * buf[k]
    gate2 = _mod_row(mod_ref, pl.program_id(0) * CB_TM)[:, 5 * D_MODEL:6 * D_MODEL]
    x2 = x_ref[...] + gate2 * acc
    o_ref[...] = _rms(x2, fw_ref[...]) if final else x2


def _combine(y_rows, dest, gates, x, mod, final_w, final):
    n = x.shape[0]
    full = pl.BlockSpec((CB_TM, D_MODEL), lambda i: (i, 0))
    return pl.pallas_call(
        functools.partial(_combine_kernel, final=final),
        grid=(n // CB_TM,),
        in_specs=[
            pl.BlockSpec((1, 1, TOP_K * CB_TM), lambda i: (i, 0, 0), memory_space=pltpu.SMEM),
            pl.BlockSpec((CB_TM, TOP_K), lambda i: (i, 0)),
            full,
            pl.BlockSpec((MOD_ROWS, 6 * D_MODEL), lambda i: (0, 0)),
            pl.BlockSpec((1, D_MODEL), lambda i: (0, 0)),
            pl.BlockSpec(memory_space=pl.ANY),
        ],
        out_specs=full,
        out_shape=jax.ShapeDtypeStruct((n, D_MODEL), F32),
        scratch_shapes=[pltpu.VMEM((TOP_K, CB_TM, D_MODEL), F32), pltpu.SemaphoreType.DMA((1,))],
        compiler_params=_cparams(("arbitrary",)),
        name="moe_combine",
    )(_tile_major(dest, CB_TM), gates.T, x, mod, final_w.reshape(1, D_MODEL), y_rows)


def _routed_ffn(h, x, mod, p, final_w, final):
    idx, gates, rank, counts = _router(h, p['router_w'], p['router_b'])
    counts = counts[:, 0].astype(jnp.int32)
    padded = (counts + MOE_BM - 1) // MOE_BM * MOE_BM
    pad_end = jnp.cumsum(padded)
    pad_start = pad_end - padded
    dest = pad_start[idx] + rank
    nb = MOE_ROWS // MOE_BM
    blk_e = jnp.minimum(jnp.searchsorted(pad_end, jnp.arange(nb) * MOE_BM, side='right'),
                        N_EXPERTS - 1).astype(jnp.int32)
    n_used = (pad_end[-1:] // MOE_BM).astype(jnp.int32)
    x_rows = _dispatch(h, dest)
    y_rows = _experts(x_rows, blk_e, n_used, p['moe_w1'], p['moe_b1'], p['moe_w2'], p['moe_b2'])
    return _combine(y_rows, dest, gates, x, mod, final_w, final)


_LAYER_PARAMS = ('norm_mix_w', 'norm_ffn_w', 'da_lambda', 'da_norm_w', 'rw_shift', 'rw_w0', 'rw_w_up', 'rw_a0',
                 'rw_a_up', 'rw_g_up', 'rw_k_k', 'rw_k_a', 'rw_r_k', 'rw_ln_w', 'rw_ln_b', 'ret_decay_logit',
                 'ret_norm_w', 'cv_dw_w', 'cv_dw_b', 'cv_ln_w', 'cv_ln_b', 'w_branch', 'w_out', 'router_w',
                 'router_b', 'moe_w1', 'moe_b1', 'moe_w2', 'moe_b2')


def _layer(x, mod, w_p, p, lam_init, caches, tables, final_w, final):
    cache_k, cache_v, state_rw, state_ret = caches
    proj = _input_projection(x, mod, p['norm_mix_w'], w_p)
    o_da = _da_branch(proj, cache_k, cache_v, p['da_lambda'], p['da_norm_w'], lam_init, tables)
    o_rw, rw_state = _rwkv_branch(proj, state_rw, p)
    o_ret, ret_state = _retention_branch(proj, state_ret, p)
    o_cv = _conv_branch(proj, p)
    x1, h2 = _merge((o_da, o_rw, o_ret, o_cv), proj, x, mod, p['w_branch'], p['w_out'], p['norm_ffn_w'])
    x2 = _routed_ffn(h2, x1, mod, p, final_w, final)
    new_k = proj[:N_CTX, P_DA + DA_W:P_DA + 2 * DA_W].reshape(BATCH, SEQ, DA_HEADS, 2, DA_QK)
    new_v = proj[:N_CTX, P_DA + 2 * DA_W:P_DA + 3 * DA_W].reshape(BATCH, SEQ, DA_HEADS, DA_V)
    return x2, (new_k, new_v, rw_state[:BATCH], ret_state[:BATCH])


def kernel(x_prompt, x_sample, c, cache_da_k, cache_da_v, state_rwkv, state_ret, c_ctx, ada_w, ada_b, norm_mix_w,
           norm_ffn_w, w_in, da_lambda, da_norm_w, rw_shift, rw_w0, rw_w_up, rw_a0, rw_a_up, rw_g_up, rw_k_k,
           rw_k_a, rw_r_k, rw_ln_w, rw_ln_b, ret_decay_logit, ret_norm_w, cv_dw_w, cv_dw_b, cv_ln_w, cv_ln_b,
           w_branch, w_out, router_w, router_b, moe_w1, moe_b1, moe_w2, moe_b2, final_norm_w):
    weights = dict(norm_mix_w=norm_mix_w, norm_ffn_w=norm_ffn_w, da_lambda=da_lambda, da_norm_w=da_norm_w,
                   rw_shift=rw_shift, rw_w0=rw_w0, rw_w_up=rw_w_up, rw_a0=rw_a0, rw_a_up=rw_a_up, rw_g_up=rw_g_up,
                   rw_k_k=rw_k_k, rw_k_a=rw_k_a, rw_r_k=rw_r_k, rw_ln_w=rw_ln_w, rw_ln_b=rw_ln_b,
                   ret_decay_logit=ret_decay_logit, ret_norm_w=ret_norm_w, cv_dw_w=cv_dw_w, cv_dw_b=cv_dw_b,
                   cv_ln_w=cv_ln_w, cv_ln_b=cv_ln_b, w_branch=w_branch, w_out=w_out, router_w=router_w,
                   router_b=router_b, moe_w1=moe_w1, moe_b1=moe_b1, moe_w2=moe_w2, moe_b2=moe_b2)
    x = jnp.concatenate([x_prompt.reshape(N_CTX, D_MODEL), x_sample.reshape(N_LAT, D_MODEL)], axis=0)
    cvec = jnp.concatenate([c_ctx[None, :], c, jnp.zeros((MOD_ROWS - 1 - DEC_BATCH, D_MODEL), F32)], axis=0)
    mod = _modulation(cvec, ada_w, ada_b)
    w_p = _pad_w_in(w_in).astype(BF16)
    tables = _rope_tables()
    outs = []
    for i in range(DEPTH):
        p = {name: weights[name][i] for name in _LAYER_PARAMS}
        lam_init = 0.8 - 0.6 * math.exp(-0.3 * i)
        caches = (cache_da_k[:, i], cache_da_v[:, i], state_rwkv[:, i], state_ret[:, i])
        x, ctx_out = _layer(x, mod[i], w_p[i], p, lam_init, caches, tables, final_norm_w, i == DEPTH - 1)
        outs.append(ctx_out)
    y_prompt = x[:N_CTX].reshape(BATCH, SEQ, D_MODEL)
    y_sample = x[N_CTX:].reshape(DEC_BATCH, DEC_SEQ, D_MODEL)
    stack = lambda j: jnp.stack([o[j] for o in outs], axis=1)
    return (y_prompt, y_sample, stack(0), stack(1), stack(2), stack(3))


def _pad_w_in(w_in):
    da, rw, ret, cv, gate = jnp.split(w_in, [1536, 3456, 4992, 6016], axis=-1)
    pad = jnp.zeros(w_in.shape[:-1] + (P_DA - RW_COLS,), w_in.dtype)
    return jnp.concatenate([rw, pad, da, ret, cv, gate], axis=-1)
```

```python
import functools
import math

import jax
import jax.numpy as jnp
from jax import lax
from jax.experimental import pallas as pl
from jax.experimental.pallas import tpu as pltpu

F32 = jnp.float32
BF16 = jnp.bfloat16

D_MODEL = 1024
BATCH = 16
SEQ = 256
DEPTH = 2
DEC_BATCH = 2
DEC_SEQ = 4096
PAST_LEN = 512
GRID_W = 64
EPS = 1e-6

DA_HEADS = 4
DA_QK = 64
DA_V = 128
ROPE_BASE = 10000.0

RW_HEADS = 8
RW_HD = 64
RW_W = 512
RW_LORA = 64
RW_G_LORA = 128
RW_GN_EPS = 64e-5

RET_HEADS = 4
RET_DK = 64
RET_DV = 128
RET_CHUNK = 128

CV_W = 512
CONV_K = 31
BR_W = 512
N_BRANCH = 4

N_EXPERTS = 32
TOP_K = 4
D_FF = 1024
SWIGLU_LIMIT = 7.0
SWIGLU_ALPHA = 1.702

N_CTX = BATCH * SEQ
N_LAT = DEC_BATCH * DEC_SEQ
N_TOK = N_CTX + N_LAT
N_SEQS = BATCH + DEC_BATCH
MOD_ROWS = 8
MOD_GROUP = 4096

RW_COLS = 1920
P_RW = 0
P_DA = 2048
P_RET = 3584
P_CV = 5120
P_GATE = 6144
P_COLS = 10240

VMEM_LIMIT = 48 * 1024 * 1024


def _cparams(sem):
    return pltpu.CompilerParams(dimension_semantics=sem, vmem_limit_bytes=VMEM_LIMIT)


def _dg(a, b, dims):
    return lax.dot_general(a, b, (dims, ((), ())), preferred_element_type=F32)


NN = ((1,), (0,))
NT = ((1,), (1,))
TN = ((0,), (0,))


def _dot(a, b, dims=NN):
    return _dg(a.astype(BF16), b.astype(BF16), dims)


def _split(x):
    hi = x.astype(BF16)
    lo = (x - hi.astype(F32)).astype(BF16)
    return hi, lo


def _dot3(a, b, dims=NN):
    ah, al = _split(a)
    bh, bl = _split(b)
    return _dg(ah, bh, dims) + (_dg(ah, bl, dims) + _dg(al, bh, dims))


def _dot2x(a, e, dims=NN):
    ah, al = _split(a)
    am = (a - ah.astype(F32) - al.astype(F32)).astype(BF16)
    eb = e.astype(BF16)
    return _dg(ah, eb, dims) + (_dg(al, eb, dims) + _dg(am, eb, dims))


def _sigmoid(x):
    return 1.0 / (1.0 + jnp.exp(-x))


def _softplus(x):
    return jnp.maximum(x, 0.0) + jnp.log(1.0 + jnp.exp(-jnp.abs(x)))


def _seq_pos(row):
    in_ctx = row < N_CTX
    pos = jnp.where(in_ctx, row & (SEQ - 1), (row - N_CTX) & (DEC_SEQ - 1))
    length = jnp.where(in_ctx, SEQ, DEC_SEQ)
    return pos, length


def _seq_index(row):
    return jnp.where(row < N_CTX, row // SEQ, BATCH + (row - N_CTX) // DEC_SEQ)


def _dotx(e, b, dims=NN):
    bh, bl = _split(b)
    bm = (b - bh.astype(F32) - bl.astype(F32)).astype(BF16)
    eb = e.astype(BF16)
    return _dg(eb, bh, dims) + (_dg(eb, bl, dims) + _dg(eb, bm, dims))


def _mod_kernel(c_ref, w_ref, b_ref, o_ref):
    c = c_ref[...]
    s = c * _sigmoid(c)
    o_ref[0] = _dot3(s, w_ref[0]) + b_ref[0]


def _modulation(cvec, ada_w, ada_b):
    tn = 1536
    return pl.pallas_call(
        _mod_kernel,
        grid=(DEPTH, 6 * D_MODEL // tn),
        in_specs=[
            pl.BlockSpec((MOD_ROWS, D_MODEL), lambda l, j: (0, 0)),
            pl.BlockSpec((1, D_MODEL, tn), lambda l, j: (l, 0, j)),
            pl.BlockSpec((1, 1, tn), lambda l, j: (l, 0, j)),
        ],
        out_specs=pl.BlockSpec((1, MOD_ROWS, tn), lambda l, j: (l, 0, j)),
        out_shape=jax.ShapeDtypeStruct((DEPTH, MOD_ROWS, 6 * D_MODEL), F32),
        compiler_params=_cparams(("parallel", "parallel")),
        name="modulation",
    )(cvec, ada_w, ada_b.reshape(DEPTH, 1, 6 * D_MODEL))


def _mod_row(mod_ref, first_row):
    g = first_row // MOD_GROUP
    return mod_ref[pl.ds(g, 1), :]


def _rms(x, w):
    return x * lax.rsqrt(jnp.mean(x * x, axis=-1, keepdims=True) + EPS) * w


IN_TM = 1024
IN_TN = 1024


def _inproj_kernel(x_ref, mod_ref, nw_ref, w_ref, o_ref, h_ref):
    i = pl.program_id(0)

    @pl.when(pl.program_id(1) == 0)
    def _():
        m = _mod_row(mod_ref, i * IN_TM)
        sh = m[:, 0:D_MODEL]
        sc = m[:, D_MODEL:2 * D_MODEL]
        h_ref[...] = (_rms(x_ref[...], nw_ref[...]) * (1.0 + sc) + sh).astype(BF16)

    o_ref[...] = _dg(h_ref[...], w_ref[...].astype(BF16), NN)


def _input_projection(x, mod, norm_w, w_p):
    n = x.shape[0]
    return pl.pallas_call(
        _inproj_kernel,
        grid=(n // IN_TM, P_COLS // IN_TN),
        in_specs=[
            pl.BlockSpec((IN_TM, D_MODEL), lambda i, j: (i, 0)),
            pl.BlockSpec((MOD_ROWS, 6 * D_MODEL), lambda i, j: (0, 0)),
            pl.BlockSpec((1, D_MODEL), lambda i, j: (0, 0)),
            pl.BlockSpec((D_MODEL, IN_TN), lambda i, j: (0, j)),
        ],
        out_specs=pl.BlockSpec((IN_TM, IN_TN), lambda i, j: (i, j)),
        out_shape=jax.ShapeDtypeStruct((n, P_COLS), F32),
        scratch_shapes=[pltpu.VMEM((IN_TM, D_MODEL), BF16)],
        compiler_params=_cparams(("parallel", "arbitrary")),
        name="input_projection",
    )(x, mod, norm_w.reshape(1, D_MODEL), w_p)


QK_TM = 512
DA_W = DA_HEADS * 2 * DA_QK


def _qkprep_kernel(q_ref, k_ref, v_ref, c_ref, se_ref, so_ref, qo_ref, ko_ref, vo_ref):
    i = pl.program_id(0)
    vo_ref[...] = v_ref[...].astype(BF16)
    scale = DA_QK ** -0.5

    @pl.when(i * QK_TM < N_CTX)
    def _():
        qo_ref[...] = (q_ref[...] * scale).astype(BF16)
        ko_ref[...] = k_ref[...].astype(BF16)

    @pl.when(i * QK_TM >= N_CTX)
    def _():
        c = c_ref[...]
        se = se_ref[...]
        so = so_ref[...]

        def rope(x):
            nxt = pltpu.roll(x, DA_W - 1, axis=1)
            prv = pltpu.roll(x, 1, axis=1)
            return x * c + nxt * se + prv * so

        qo_ref[...] = (rope(q_ref[...]) * scale).astype(BF16)
        ko_ref[...] = rope(k_ref[...]).astype(BF16)


def _rope_tables():
    rows = DEC_SEQ // GRID_W
    row = jnp.repeat(jnp.arange(rows, dtype=F32), GRID_W)
    col = jnp.tile(jnp.arange(GRID_W, dtype=F32), rows)
    n_pairs = DA_QK // 4
    inv = ROPE_BASE ** (-jnp.arange(n_pairs, dtype=F32) / n_pairs)
    ang = jnp.concatenate([row[:, None] * inv, col[:, None] * inv], axis=-1)
    cos = jnp.repeat(jnp.cos(ang), 2, axis=-1)
    sin = jnp.repeat(jnp.sin(ang), 2, axis=-1)
    even = (jnp.arange(DA_QK) % 2 == 0)[None, :]
    s_even = jnp.where(even, -sin, 0.0)
    s_odd = jnp.where(even, 0.0, sin)
    rep = lambda t: jnp.tile(t, (1, DA_W // DA_QK))
    return rep(cos), rep(s_even), rep(s_odd)


def _qk_prepare(proj, tables):
    n = proj.shape[0]
    lat0 = N_CTX // QK_TM
    nlat = DEC_SEQ // QK_TM
    tab = pl.BlockSpec((QK_TM, DA_W), lambda i: (jnp.maximum(i - lat0, 0) % nlat, 0))
    c0 = P_DA // DA_W
    out = jax.ShapeDtypeStruct((n, DA_W), BF16)
    return pl.pallas_call(
        _qkprep_kernel,
        grid=(n // QK_TM,),
        in_specs=[
            pl.BlockSpec((QK_TM, DA_W), lambda i: (i, c0)),
            pl.BlockSpec((QK_TM, DA_W), lambda i: (i, c0 + 1)),
            pl.BlockSpec((QK_TM, DA_W), lambda i: (i, c0 + 2)),
            tab, tab, tab,
        ],
        out_specs=[pl.BlockSpec((QK_TM, DA_W), lambda i: (i, 0))] * 3,
        out_shape=[out, out, out],
        compiler_params=_cparams(("parallel",)),
        name="qk_prepare",
    )(proj, proj, proj, *tables)


DA_TQ = 128


def _da_kernel(q_ref, k_ref, v_ref, dl_ref, nw_ref, o_ref, *, lam_init):
    dl = dl_ref[...]
    lam = (jnp.exp(jnp.sum(dl[0:1] * dl[1:2], axis=1, keepdims=True))
           - jnp.exp(jnp.sum(dl[2:3] * dl[3:4], axis=1, keepdims=True)) + lam_init)
    nw = nw_ref[...] * (1.0 - lam_init)
    q = q_ref[0]
    for h in range(DA_HEADS):
        es, rs = [], []
        for m in range(2):
            c0 = (2 * h + m) * DA_QK
            s = _dg(q[:, c0:c0 + DA_QK], k_ref[0, :, c0:c0 + DA_QK], NT)
            e = jnp.exp(s - jnp.max(s, axis=-1, keepdims=True))
            es.append(e)
            rs.append(1.0 / jnp.sum(e, axis=-1, keepdims=True))
        a = es[0] * rs[0] - es[1] * (lam * rs[1])
        o = _dg(a.astype(BF16), v_ref[0, :, h * DA_V:(h + 1) * DA_V], NN)
        o_ref[0, :, h * DA_V:(h + 1) * DA_V] = _rms(o, nw)


def _diff_attention(q, k, v, da_lambda, da_norm_w, lam_init):
    b, tq, _ = q.shape
    tk = k.shape[1]
    return pl.pallas_call(
        functools.partial(_da_kernel, lam_init=lam_init),
        grid=(b, tq // DA_TQ),
        in_specs=[
            pl.BlockSpec((1, DA_TQ, DA_W), lambda i, j: (i, j, 0)),
            pl.BlockSpec((1, tk, DA_W), lambda i, j: (i, 0, 0)),
            pl.BlockSpec((1, tk, DA_W), lambda i, j: (i, 0, 0)),
            pl.BlockSpec((4, DA_QK), lambda i, j: (0, 0)),
            pl.BlockSpec((1, DA_V), lambda i, j: (0, 0)),
        ],
        out_specs=pl.BlockSpec((1, DA_TQ, DA_W), lambda i, j: (i, j, 0)),
        out_shape=jax.ShapeDtypeStruct((b, tq, DA_W), F32),
        compiler_params=_cparams(("parallel", "arbitrary")),
        name="diff_attention",
    )(q, k, v, da_lambda, da_norm_w.reshape(1, DA_V))


def _da_branch(proj, cache_k, cache_v, da_lambda, da_norm_w, lam_init, tables):
    qb, kb, vb = _qk_prepare(proj, tables)
    ctx = lambda t: t[:N_CTX].reshape(BATCH, SEQ, DA_W)
    lat = lambda t: t[N_CTX:].reshape(DEC_BATCH, DEC_SEQ, DA_W)
    o_ctx = _diff_attention(ctx(qb), ctx(kb), ctx(vb), da_lambda, da_norm_w, lam_init)
    k_all = jnp.concatenate([cache_k.reshape(DEC_BATCH, PAST_LEN, DA_W).astype(BF16), lat(kb)], axis=1)
    v_all = jnp.concatenate([cache_v.reshape(DEC_BATCH, PAST_LEN, DA_W).astype(BF16), lat(vb)], axis=1)
    o_lat = _diff_attention(lat(qb), k_all, v_all, da_lambda, da_norm_w, lam_init)
    return jnp.concatenate([o_ctx.reshape(N_CTX, DA_W), o_lat.reshape(N_LAT, DA_W)], axis=0)


RW_TM = 256
RW_C = 64
RW_PAIRS = RW_HEADS // 2
HALO = 8


def _head_ones():
    idx = jnp.arange(RW_W) // RW_HD
    return (idx[:, None] == idx[None, :]).astype(BF16)


def _rwprep_kernel(x_ref, xp_ref, xn_ref, mu_ref, w0_ref, wup_ref, a0_ref, aup_ref, gup_ref, kk_ref, ka_ref,
                   ones_ref, r_o, v_o, kk_o, g_o, kd_o, lw_o, a_o, buf):
    row0 = pl.program_id(0) * RW_TM
    x = x_ref[:, 0:RW_COLS]
    buf[HALO:HALO + RW_TM, :] = x
    buf[HALO - 1:HALO, :] = xp_ref[HALO - 1:HALO, 0:RW_COLS]
    buf[HALO + RW_TM:HALO + RW_TM + 1, :] = xn_ref[0:1, 0:RW_COLS]
    rows = row0 + lax.broadcasted_iota(jnp.int32, (RW_TM, 1), 0)
    pos, length = _seq_pos(rows)
    prev = jnp.where(pos == 0, 0.0, buf[HALO - 1:HALO - 1 + RW_TM, :])
    nxt = jnp.where(pos == length - 1, 0.0, buf[HALO + 1:HALO + 1 + RW_TM, :])
    mu = mu_ref[...]
    u = x + mu[0:1] * (prev - x) + mu[1:2] * (nxt - x)

    r = u[:, 0:RW_W]
    k = u[:, RW_W:2 * RW_W]
    v = u[:, 2 * RW_W:3 * RW_W]
    wl = u[:, 3 * RW_W:3 * RW_W + 128]
    al = u[:, 3 * RW_W + 128:3 * RW_W + 256]
    gl = u[:, 3 * RW_W + 256:3 * RW_W + 384]
    w_raw = w0_ref[...] + _dot3(jnp.tanh(wl), wup_ref[...])
    lw = -jnp.exp(-_softplus(-w_raw) - 0.5)
    a = _sigmoid(a0_ref[...] + _dot3(al, aup_ref[...]))
    g = _dot3(_sigmoid(gl), gup_ref[...])
    kk = k * kk_ref[...]
    kk = kk * lax.rsqrt(jnp.maximum(_dot2x(kk * kk, ones_ref[...]), 1e-12))
    kd = jnp.concatenate([k, k], axis=1) * (1.0 + (a - 1.0) * ka_ref[...])

    r_o[...] = r
    v_o[...] = v
    kk_o[...] = kk
    g_o[...] = g
    for d in range(2):
        kd_o[d] = kd[:, d * RW_W:(d + 1) * RW_W]
        lw_o[d] = lw[:, d * RW_W:(d + 1) * RW_W]
        a_o[d] = a[:, d * RW_W:(d + 1) * RW_W]


def _rw_prepare(proj, p):
    n = proj.shape[0]
    nh = n // HALO
    steps = RW_TM // HALO
    wide = P_DA - P_RW
    cat2 = lambda t: t.reshape(1, 2 * RW_W)
    blockdiag = lambda t: jnp.concatenate(
        [jnp.concatenate([t[0], jnp.zeros_like(t[0])], axis=1),
         jnp.concatenate([jnp.zeros_like(t[1]), t[1]], axis=1)], axis=0)
    const = lambda shape: pl.BlockSpec(shape, lambda i: (0,) * len(shape))
    row = pl.BlockSpec((RW_TM, RW_W), lambda i: (i, 0))
    row2 = pl.BlockSpec((2, RW_TM, RW_W), lambda i: (0, i, 0))
    o1 = jax.ShapeDtypeStruct((n, RW_W), F32)
    o2 = jax.ShapeDtypeStruct((2, n, RW_W), F32)
    return pl.pallas_call(
        _rwprep_kernel,
        grid=(n // RW_TM,),
        in_specs=[
            pl.BlockSpec((RW_TM, wide), lambda i: (i, 0)),
            pl.BlockSpec((HALO, wide), lambda i: (jnp.maximum(i * steps - 1, 0), 0)),
            pl.BlockSpec((HALO, wide), lambda i: (jnp.minimum((i + 1) * steps, nh - 1), 0)),
            const((2, RW_COLS)), const((1, 2 * RW_W)), const((128, 2 * RW_W)), const((1, 2 * RW_W)),
            const((128, 2 * RW_W)), const((RW_G_LORA, RW_W)), const((1, RW_W)), const((1, 2 * RW_W)),
            const((RW_W, RW_W)),
        ],
        out_specs=[row, row, row, row, row2, row2, row2],
        out_shape=[o1, o1, o1, o1, o2, o2, o2],
        scratch_shapes=[pltpu.VMEM((RW_TM + 2 * HALO, RW_COLS), F32)],
        compiler_params=_cparams(("parallel",)),
        name="rwkv_prepare",
    )(proj, proj, proj, p['rw_shift'], cat2(p['rw_w0']), blockdiag(p['rw_w_up']), cat2(p['rw_a0']),
      blockdiag(p['rw_a_up']), p['rw_g_up'], p['rw_k_k'].reshape(1, RW_W),
      jnp.tile(p['rw_k_a'].reshape(1, RW_W), (1, 2)), _head_ones())


RW_TB = 256
RW_NC = RW_TB // RW_C
RW_SIDE = 8


def _rwscan_kernel(r_ref, v_ref, kk_ref, kd_ref, lw_ref, a_ref, tri_ref, inc_ref, str_ref, s0_ref, y_ref, st_ref,
                   s_scr, *, backward):
    step = pl.program_id(0)
    nb = pl.num_programs(0)
    bi = (nb - 1 - step) if backward else step
    pos, length = _seq_pos(bi * RW_TB)
    first = (pos + RW_TB == length) if backward else (pos == 0)

    @pl.when(first)
    def _():
        s_scr[...] = s0_ref[0, 0]

    c = RW_C
    incl2 = inc_ref[...] > 0.5
    strict2 = str_ref[...] > 0.5
    eye2 = (lax.broadcasted_iota(jnp.int32, (2 * c, 2 * c), 0)
            == lax.broadcasted_iota(jnp.int32, (2 * c, 2 * c), 1))
    m_e = lax.broadcasted_iota(jnp.int32, (1, 2 * c), 1) < c

    lw = lw_ref[0]
    cum = _dotx(tri_ref[...], lw)
    tots = [cum[(j * c if backward else j * c + c - 1):(j * c + 1 if backward else j * c + c), :]
            for j in range(RW_NC)]
    tot_b = jnp.concatenate([jnp.broadcast_to(t, (c, RW_W)) for t in tots], axis=0)
    kk = kk_ref[...]
    kd = kd_ref[0]
    bp = kk * a_ref[0]
    g_inv = jnp.exp(-cum)
    g_rem = jnp.exp(tot_b - cum)
    ag = -kk * jnp.exp(cum - lw)
    rg = r_ref[...] * jnp.exp(cum)
    bdn = bp * g_inv
    kdn = kd * g_inv
    bc = bp * g_rem
    kc = kd * g_rem
    v = v_ref[...]

    def stack(x, j, p):
        xs = x[j * c:(j + 1) * c, p * 2 * c:(p + 1) * 2 * c]
        return jnp.concatenate([jnp.where(m_e, xs, 0.0), jnp.where(m_e, 0.0, xs)], axis=0).astype(BF16)

    pre = {}
    keys = [(j, p) for j in range(RW_NC) for p in range(RW_PAIRS)]
    for g0 in range(0, len(keys), RW_SIDE):
        grp = keys[g0:g0 + RW_SIDE]
        ops = {k: tuple(stack(t, *k) for t in (ag, rg, bdn, kdn, bc, kc, v)) for k in grp}
        gm = {k: _dg(jnp.concatenate([ops[k][0], ops[k][1]], axis=0),
                     jnp.concatenate([ops[k][2], ops[k][3]], axis=0), NT) for k in grp}
        lbb = {k: jnp.where(strict2, gm[k][:2 * c, :2 * c], 0.0) for k in grp}
        lkb = {k: jnp.where(strict2, gm[k][:2 * c, 2 * c:], 0.0).astype(BF16) for k in grp}
        lrk = {k: jnp.concatenate([jnp.where(incl2, gm[k][2 * c:, :2 * c], 0.0),
                                   jnp.where(incl2, gm[k][2 * c:, 2 * c:], 0.0)], axis=1).astype(BF16)
               for k in grp}
        lv = {k: _dg(lkb[k], ops[k][6], NN) for k in grp}
        x = {k: jnp.where(eye2, 1.0, lbb[k]) for k in grp}
        pw = lbb
        for _ in range(int(math.log2(c)) - 1):
            pwb = {k: pw[k].astype(BF16) for k in grp}
            pw = {k: _dg(pwb[k], pwb[k], NN) for k in grp}
            x = {k: x[k] + _dg(x[k].astype(BF16), pw[k].astype(BF16), NN) for k in grp}
        tw = {k: _dg(x[k].astype(BF16), jnp.concatenate([lv[k].astype(BF16), ops[k][0]], axis=1), NN)
              for k in grp}
        for k in grp:
            pre[k] = (tw[k][:, :2 * c], tw[k][:, 2 * c:].astype(BF16), ops[k][1], lrk[k], ops[k][6],
                      jnp.concatenate([ops[k][4], ops[k][5]], axis=0))

    order = range(RW_NC - 1, -1, -1) if backward else range(RW_NC)
    pairs = range(RW_PAIRS)
    s = [s_scr[p] for p in pairs]
    for j in order:
        sb = [s[p].astype(BF16) for p in pairs]
        u = [_dg(pre[j, p][1], sb[p], NT) + pre[j, p][0] for p in pairs]
        uv = [jnp.concatenate([u[p].astype(BF16), pre[j, p][4]], axis=0) for p in pairs]
        y = [_dg(pre[j, p][2], sb[p], NT) + _dg(pre[j, p][3], uv[p], NN) for p in pairs]
        s = [s[p] * jnp.exp(tots[j][:, p * 2 * c:(p + 1) * 2 * c]) + _dg(uv[p], pre[j, p][5], TN) for p in pairs]
        for p in pairs:
            y_ref[j * c:(j + 1) * c, p * 2 * c:(p + 1) * 2 * c] = y[p][:c] + y[p][c:]
    for p in pairs:
        s_scr[p] = s[p]
        st_ref[0, p] = s[p]


def _scan_masks(backward):
    t = jnp.arange(RW_TB)
    sgn = -1 if backward else 1
    same_chunk = (t[:, None] // RW_C) == (t[None, :] // RW_C)
    tri = (same_chunk & ((t[:, None] - t[None, :]) * sgn >= 0)).astype(BF16)
    q = jnp.arange(2 * RW_C)
    same_head = (q[:, None] // RW_C) == (q[None, :] // RW_C)
    dif = ((q[:, None] % RW_C) - (q[None, :] % RW_C)) * sgn
    return tri, (same_head & (dif >= 0)).astype(F32), (same_head & (dif > 0)).astype(F32)


def _rw_scan(r, v, kk, kd, lw, a, s0, backward):
    n = r.shape[0]
    nb = n // RW_TB
    d = 1 if backward else 0
    blk = (lambda i: nb - 1 - i) if backward else (lambda i: i)
    row = pl.BlockSpec((RW_TB, RW_W), lambda i: (blk(i), 0))
    row2 = pl.BlockSpec((1, RW_TB, RW_W), lambda i: (d, blk(i), 0))
    const = lambda shape: pl.BlockSpec(shape, lambda i: (0,) * len(shape))
    pair = 2 * RW_C
    return pl.pallas_call(
        functools.partial(_rwscan_kernel, backward=backward),
        grid=(nb,),
        in_specs=[row, row, row, row2, row2, row2, const((RW_TB, RW_TB)), const((pair, pair)), const((pair, pair)),
                  pl.BlockSpec((1, 1, RW_PAIRS, pair, pair), lambda i: (_seq_index(blk(i) * RW_TB), d, 0, 0, 0))],
        out_specs=[row, pl.BlockSpec((1, RW_PAIRS, pair, pair), lambda i: (_seq_index(blk(i) * RW_TB), 0, 0, 0))],
        out_shape=[jax.ShapeDtypeStruct((n, RW_W), F32),
                   jax.ShapeDtypeStruct((N_SEQS, RW_PAIRS, pair, pair), F32)],
        scratch_shapes=[pltpu.VMEM((RW_PAIRS, pair, pair), F32)],
        compiler_params=_cparams(("arbitrary",)),
        name="rwkv_scan_bwd" if backward else "rwkv_scan_fwd",
    )(r, v, kk, kd, lw, a, *_scan_masks(backward), s0)


def _rwpost_kernel(yf_ref, yb_ref, r_ref, v_ref, g_ref, kd_ref, rk_ref, lnw_ref, lnb_ref, ones_ref, o_ref):
    ones = ones_ref[...]
    inv = 1.0 / RW_HD
    y = yf_ref[...] + yb_ref[...]
    xc = y - _dot2x(y, ones) * inv
    var = _dot2x(xc * xc, ones) * inv
    yn = xc * lax.rsqrt(var + RW_GN_EPS) * lnw_ref[...] + lnb_ref[...]
    bonus = _dot2x(r_ref[...] * (kd_ref[0] + kd_ref[1]) * rk_ref[...], ones) * v_ref[...]
    o_ref[...] = (yn + bonus) * g_ref[...]


def _rw_post(yf, yb, r, v, g, kd, p):
    n = r.shape[0]
    row = pl.BlockSpec((RW_TM, RW_W), lambda i: (i, 0))
    row2 = pl.BlockSpec((2, RW_TM, RW_W), lambda i: (0, i, 0))
    vec = pl.BlockSpec((1, RW_W), lambda i: (0, 0))
    return pl.pallas_call(
        _rwpost_kernel,
        grid=(n // RW_TM,),
        in_specs=[row, row, row, row, row, row2, vec, vec, vec, pl.BlockSpec((RW_W, RW_W), lambda i: (0, 0))],
        out_specs=row,
        out_shape=jax.ShapeDtypeStruct((n, RW_W), F32),
        compiler_params=_cparams(("parallel",)),
        name="rwkv_post",
    )(yf, yb, r, v, g, kd, p['rw_r_k'].reshape(1, RW_W), p['rw_ln_w'].reshape(1, RW_W),
      p['rw_ln_b'].reshape(1, RW_W), _head_ones())


def _pairs_from_heads(s):
    lead = s.shape[:-3]
    st = s.reshape(lead + (RW_PAIRS, 2, RW_HD, RW_HD))
    eye = jnp.eye(2, dtype=s.dtype)
    out = st[..., :, :, :, None, :] * eye[:, None, :, None]
    return out.reshape(lead + (RW_PAIRS, 2 * RW_HD, 2 * RW_HD))


def _heads_from_pairs(s):
    lead = s.shape[:-3]
    s6 = s.reshape(lead + (RW_PAIRS, 2, RW_HD, 2, RW_HD))
    diag = jnp.stack([s6[..., 0, :, 0, :], s6[..., 1, :, 1, :]], axis=-3)
    return diag.reshape(lead + (RW_HEADS, RW_HD, RW_HD))


def _rwkv_branch(proj, state0, p):
    r, v, kk, g, kd, lw, a = _rw_prepare(proj, p)
    s0 = jnp.concatenate([jnp.zeros((BATCH,) + state0.shape[1:], F32), state0], axis=0)
    s0 = _pairs_from_heads(s0)
    yf, sf = _rw_scan(r, v, kk, kd, lw, a, s0, backward=False)
    yb, sb = _rw_scan(r, v, kk, kd, lw, a, s0, backward=True)
    st = _heads_from_pairs(jnp.stack([sf, sb], axis=1))
    return _rw_post(yf, yb, r, v, g, kd, p), st


RET_W = RET_HEADS * RET_DV
RET_PAIRS = RET_HEADS // 2


def _ret_kernel(q_ref, k_ref, v_ref, lg_ref, s0_ref, o_ref, st_ref, s_scr):
    d = pl.program_id(0)
    step = pl.program_id(1)
    nch = pl.num_programs(1)
    c = RET_CHUNK
    ci = jnp.where(d == 0, step, nch - 1 - step)
    pos, length = _seq_pos(ci * c)
    first = jnp.where(d == 0, pos == 0, pos + c == length)

    @pl.when(first)
    def _():
        s_scr[...] = s0_ref[0, 0]

    sgn = jnp.where(d == 0, 1, -1)
    lgs = -_softplus(-lg_ref[pl.ds(d, 1), :])
    ri = lax.broadcasted_iota(jnp.int32, (c, c), 0)
    cj = lax.broadcasted_iota(jnp.int32, (c, c), 1)
    dif = (ri - cj) * sgn
    valid = dif >= 0
    dist = jnp.maximum(dif, 0).astype(F32)
    pr = lax.broadcasted_iota(jnp.int32, (c, 2 * RET_DK), 0)
    tau = jnp.where(d == 0, pr, c - 1 - pr).astype(F32)
    low = lax.broadcasted_iota(jnp.int32, (1, 2 * RET_DK), 1) < RET_DK
    rlow = lax.broadcasted_iota(jnp.int32, (2 * RET_DK, RET_DV), 0) < RET_DK
    q = q_ref[...]
    k = k_ref[...] * (RET_DK ** -0.5)
    for p in range(RET_PAIRS):
        lg_e = lgs[:, 2 * p:2 * p + 1]
        lg_o = lgs[:, 2 * p + 1:2 * p + 2]
        lg_row = jnp.where(low, lg_e, lg_o)
        qp = q[:, p * 128:(p + 1) * 128]
        kp = k[:, p * 128:(p + 1) * 128]
        qd = qp * jnp.exp(lg_row * (tau + 1.0))
        kdec = kp * jnp.exp(lg_row * (c - 1.0 - tau))
        sp = s_scr[p]
        sn = sp * jnp.where(rlow, jnp.exp(lg_e * c), jnp.exp(lg_o * c))
        for j, lg in ((0, lg_e), (1, lg_o)):
            h = 2 * p + j
            mj = low if j == 0 else jnp.logical_not(low)
            vj = v_ref[:, h * RET_DV:(h + 1) * RET_DV]
            att = _dot3(jnp.where(mj, qp, 0.0), kp, NT) * jnp.where(valid, jnp.exp(lg * dist), 0.0)
            o_ref[0, :, h * RET_DV:(h + 1) * RET_DV] = _dot3(att, vj) + _dot3(jnp.where(mj, qd, 0.0), sp)
            sn = sn + _dot3(jnp.where(mj, kdec, 0.0), vj, TN)
        s_scr[p] = sn
        st_ref[0, 0, p] = sn


def _ret_scan(proj, logit, s0):
    n = proj.shape[0]
    c = RET_CHUNK
    nch = n // c
    ci = lambda d, s: jnp.where(d == 0, s, nch - 1 - s)
    qk_w = RET_HEADS * RET_DK
    st = pl.BlockSpec((1, 1, RET_PAIRS, 2 * RET_DK, RET_DV),
                      lambda d, s: (_seq_index(ci(d, s) * c), d, 0, 0, 0))
    return pl.pallas_call(
        _ret_kernel,
        grid=(2, nch),
        in_specs=[
            pl.BlockSpec((c, qk_w), lambda d, s: (ci(d, s), P_RET // qk_w)),
            pl.BlockSpec((c, qk_w), lambda d, s: (ci(d, s), P_RET // qk_w + 1)),
            pl.BlockSpec((c, RET_W), lambda d, s: (ci(d, s), (P_RET + 2 * qk_w) // RET_W)),
            pl.BlockSpec((2, RET_HEADS), lambda d, s: (0, 0)),
            st,
        ],
        out_specs=[pl.BlockSpec((1, c, RET_W), lambda d, s: (d, ci(d, s), 0)), st],
        out_shape=[jax.ShapeDtypeStruct((2, n, RET_W), F32),
                   jax.ShapeDtypeStruct((N_SEQS, 2, RET_PAIRS, 2 * RET_DK, RET_DV), F32)],
        scratch_shapes=[pltpu.VMEM((RET_PAIRS, 2 * RET_DK, RET_DV), F32)],
        compiler_params=_cparams(("arbitrary", "arbitrary")),
        name="retention_scan",
    )(proj, proj, proj, logit, s0)


def _standardize(x, eps):
    mu = jnp.mean(x, axis=-1, keepdims=True)
    xc = x - mu
    return xc * lax.rsqrt(jnp.mean(xc * xc, axis=-1, keepdims=True) + eps)


def _retpost_kernel(o_ref, g_ref, nw_ref, out_ref):
    o = o_ref[0] + o_ref[1]
    g = g_ref[...]
    for h in range(RET_HEADS):
        sl = slice(h * RET_DV, (h + 1) * RET_DV)
        gh = g[:, sl]
        out_ref[:, sl] = gh * _sigmoid(gh) * (_standardize(o[:, sl], EPS) * nw_ref[:, sl])


def _ret_post(o, proj, norm_w):
    n = proj.shape[0]
    tm = 512
    return pl.pallas_call(
        _retpost_kernel,
        grid=(n // tm,),
        in_specs=[
            pl.BlockSpec((2, tm, RET_W), lambda i: (0, i, 0)),
            pl.BlockSpec((tm, RET_W), lambda i: (i, (P_RET + 1024) // RET_W)),
            pl.BlockSpec((1, RET_W), lambda i: (0, 0)),
        ],
        out_specs=pl.BlockSpec((tm, RET_W), lambda i: (i, 0)),
        out_shape=jax.ShapeDtypeStruct((n, RET_W), F32),
        compiler_params=_cparams(("parallel",)),
        name="retention_post",
    )(o, proj, norm_w.reshape(1, RET_W))


def _retention_branch(proj, state0, p):
    s0 = jnp.concatenate([jnp.zeros((BATCH,) + state0.shape[1:], F32), state0], axis=0)
    s0 = s0.reshape(N_SEQS, 2, RET_PAIRS, 2 * RET_DK, RET_DV)
    o, st = _ret_scan(proj, p['ret_decay_logit'], s0)
    return _ret_post(o, proj, p['ret_norm_w']), st.reshape(N_SEQS, 2, RET_HEADS, RET_DK, RET_DV)


CV_TM = 256
CV_HALO = 16


def _conv_kernel(a_ref, g_ref, ap_ref, gp_ref, an_ref, gn_ref, w_ref, b_ref, lnw_ref, lnb_ref, o_ref, buf):
    row0 = pl.program_id(0) * CV_TM
    pos0, len0 = _seq_pos(row0)
    buf[CV_HALO:CV_HALO + CV_TM, :] = a_ref[...] * _sigmoid(g_ref[...])
    buf[0:CV_HALO, :] = jnp.where(pos0 == 0, 0.0, ap_ref[...] * _sigmoid(gp_ref[...]))
    buf[CV_HALO + CV_TM:, :] = jnp.where(pos0 + CV_TM == len0, 0.0, an_ref[...] * _sigmoid(gn_ref[...]))
    base = CV_HALO - CONV_K // 2
    acc = jnp.zeros((CV_TM, CV_W), F32)
    for j in range(CONV_K):
        acc = acc + w_ref[j:j + 1, :] * buf[base + j:base + j + CV_TM, :]
    z = _standardize(acc + b_ref[...], EPS) * lnw_ref[...] + lnb_ref[...]
    o_ref[...] = z * _sigmoid(z)


def _conv_branch(proj, p):
    n = proj.shape[0]
    nh = n // CV_HALO
    steps = CV_TM // CV_HALO
    ca = P_CV // CV_W
    prev = lambda i: jnp.maximum(i * steps - 1, 0)
    nxt = lambda i: jnp.minimum((i + 1) * steps, nh - 1)
    vec = pl.BlockSpec((1, CV_W), lambda i: (0, 0))
    return pl.pallas_call(
        _conv_kernel,
        grid=(n // CV_TM,),
        in_specs=[
            pl.BlockSpec((CV_TM, CV_W), lambda i: (i, ca)),
            pl.BlockSpec((CV_TM, CV_W), lambda i: (i, ca + 1)),
            pl.BlockSpec((CV_HALO, CV_W), lambda i: (prev(i), ca)),
            pl.BlockSpec((CV_HALO, CV_W), lambda i: (prev(i), ca + 1)),
            pl.BlockSpec((CV_HALO, CV_W), lambda i: (nxt(i), ca)),
            pl.BlockSpec((CV_HALO, CV_W), lambda i: (nxt(i), ca + 1)),
            pl.BlockSpec((CONV_K, CV_W), lambda i: (0, 0)),
            vec, vec, vec,
        ],
        out_specs=pl.BlockSpec((CV_TM, CV_W), lambda i: (i, 0)),
        out_shape=jax.ShapeDtypeStruct((n, CV_W), F32),
        scratch_shapes=[pltpu.VMEM((CV_TM + 2 * CV_HALO, CV_W), F32)],
        compiler_params=_cparams(("parallel",)),
        name="conformer_conv",
    )(proj, proj, proj, proj, proj, proj, p['cv_dw_w'], p['cv_dw_b'].reshape(1, CV_W),
      p['cv_ln_w'].reshape(1, CV_W), p['cv_ln_b'].reshape(1, CV_W))


MG_TM = 512


def _merge_kernel(da_ref, rw_ref, ret_ref, cv_ref, g0_ref, g1_ref, g2_ref, g3_ref, x_ref, mod_ref, wb_ref, wo_ref,
                  nw_ref, x_o, h_o):
    m = None
    for n, (br, gt) in enumerate(((da_ref, g0_ref), (rw_ref, g1_ref), (ret_ref, g2_ref), (cv_ref, g3_ref))):
        t = _sigmoid(gt[...]) * _dg(br[...].astype(BF16), wb_ref[n], NN)
        m = t if m is None else m + t
    out = _dg(m.astype(BF16), wo_ref[...], NN)
    mrow = _mod_row(mod_ref, pl.program_id(0) * MG_TM)
    gate1 = mrow[:, 2 * D_MODEL:3 * D_MODEL]
    sh2 = mrow[:, 3 * D_MODEL:4 * D_MODEL]
    sc2 = mrow[:, 4 * D_MODEL:5 * D_MODEL]
    x1 = x_ref[...] + gate1 * out
    x_o[...] = x1
    h_o[...] = _rms(x1, nw_ref[...]) * (1.0 + sc2) + sh2


def _merge(branches, proj, x, mod, w_branch, w_out, norm_w):
    n = x.shape[0]
    br = pl.BlockSpec((MG_TM, BR_W), lambda i: (i, 0))
    gspec = lambda j: pl.BlockSpec((MG_TM, D_MODEL), lambda i: (i, P_GATE // D_MODEL + j))
    full = pl.BlockSpec((MG_TM, D_MODEL), lambda i: (i, 0))
    out = jax.ShapeDtypeStruct((n, D_MODEL), F32)
    return pl.pallas_call(
        _merge_kernel,
        grid=(n // MG_TM,),
        in_specs=[br, br, br, br, gspec(0), gspec(1), gspec(2), gspec(3), full,
                  pl.BlockSpec((MOD_ROWS, 6 * D_MODEL), lambda i: (0, 0)),
                  pl.BlockSpec((N_BRANCH, BR_W, D_MODEL), lambda i: (0, 0, 0)),
                  pl.BlockSpec((D_MODEL, D_MODEL), lambda i: (0, 0)),
                  pl.BlockSpec((1, D_MODEL), lambda i: (0, 0))],
        out_specs=[full, full],
        out_shape=[out, out],
        compiler_params=_cparams(("parallel",)),
        name="gated_merge",
    )(*branches, proj, proj, proj, proj, x, mod, w_branch.astype(BF16), w_out.astype(BF16),
      norm_w.reshape(1, D_MODEL))


RT_TM = 256
MOE_BM = 256
MOE_ROWS = N_TOK * TOP_K + N_EXPERTS * MOE_BM
DP_TM = 512
DP_GROUP = 64
CB_TM = 256


def _router_kernel(h_ref, w_ref, b_ref, tri_ref, idx_o, gate_o, rank_o, cnt_o, carry):
    @pl.when(pl.program_id(0) == 0)
    def _():
        carry[...] = jnp.zeros_like(carry)

    logits = _dot3(w_ref[...], h_ref[...], NT) + b_ref[...]
    e_iota = lax.broadcasted_iota(jnp.int32, logits.shape, 0)
    work = logits
    vals, idxs, hots = [], [], []
    for _ in range(TOP_K):
        mx = jnp.max(work, axis=0, keepdims=True)
        ix = jnp.min(jnp.where(work == mx, e_iota, N_EXPERTS), axis=0, keepdims=True)
        hot = e_iota == ix
        vals.append(mx)
        idxs.append(ix)
        hots.append(hot.astype(F32))
        work = jnp.where(hot, -jnp.inf, work)
    es = [jnp.exp(v - vals[0]) for v in vals]
    inv = 1.0 / (es[0] + es[1] + es[2] + es[3])
    chosen = hots[0] + hots[1] + hots[2] + hots[3]
    ahead = carry[...][:, 0:1] + _dg(chosen.astype(BF16), tri_ref[...], NN)
    idx_o[...] = jnp.concatenate(idxs, axis=0)
    gate_o[...] = jnp.concatenate([e * inv for e in es], axis=0)
    rank_o[...] = jnp.concatenate(
        [jnp.sum(hot * ahead, axis=0, keepdims=True) for hot in hots], axis=0).astype(jnp.int32)
    carry[...] = carry[...] + jnp.sum(chosen, axis=1, keepdims=True)
    cnt_o[...] = carry[...]


def _router(h, router_w, router_b):
    n = h.shape[0]
    tri = (jnp.arange(RT_TM)[:, None] < jnp.arange(RT_TM)[None, :]).astype(BF16)
    col = pl.BlockSpec((TOP_K, RT_TM), lambda i: (0, i))
    return pl.pallas_call(
        _router_kernel,
        grid=(n // RT_TM,),
        in_specs=[
            pl.BlockSpec((RT_TM, D_MODEL), lambda i: (i, 0)),
            pl.BlockSpec((N_EXPERTS, D_MODEL), lambda i: (0, 0)),
            pl.BlockSpec((N_EXPERTS, 1), lambda i: (0, 0)),
            pl.BlockSpec((RT_TM, RT_TM), lambda i: (0, 0)),
        ],
        out_specs=[col, col, col, pl.BlockSpec((N_EXPERTS, 128), lambda i: (0, 0))],
        out_shape=[jax.ShapeDtypeStruct((TOP_K, n), jnp.int32), jax.ShapeDtypeStruct((TOP_K, n), F32),
                   jax.ShapeDtypeStruct((TOP_K, n), jnp.int32), jax.ShapeDtypeStruct((N_EXPERTS, 128), F32)],
        scratch_shapes=[pltpu.VMEM((N_EXPERTS, 128), F32)],
        compiler_params=_cparams(("arbitrary",)),
        name="router",
    )(h, router_w.T, router_b.reshape(N_EXPERTS, 1), tri)


def _tile_major(t, tm):
    k, n = t.shape
    return t.reshape(k, n // tm, tm).transpose(1, 0, 2).reshape(n // tm, 1, k * tm)


def _dispatch_kernel(dest_ref, h_ref, z_hbm, o_hbm, sem):
    del z_hbm
    n_groups = DP_TM // DP_GROUP

    def wait_group(slot):
        pltpu.make_async_copy(h_ref.at[pl.ds(0, TOP_K * DP_GROUP)], o_hbm.at[pl.ds(0, TOP_K * DP_GROUP)],
                              sem.at[slot]).wait()

    def group(gi, carry):
        slot = gi % 2

        def issue(t, c):
            tok = gi * DP_GROUP + t
            for k in range(TOP_K):
                dst = dest_ref[0, 0, k * DP_TM + tok]
                pltpu.make_async_copy(h_ref.at[pl.ds(tok, 1)], o_hbm.at[pl.ds(dst, 1)], sem.at[slot]).start()
            return c

        lax.fori_loop(0, DP_GROUP, issue, 0)

        @pl.when(gi > 0)
        def _():
            wait_group(1 - slot)

        return carry

    lax.fori_loop(0, n_groups, group, 0)
    wait_group((n_groups - 1) % 2)


def _dispatch(h, dest):
    n = h.shape[0]
    zeros = jnp.zeros((MOE_ROWS, D_MODEL), F32)
    return pl.pallas_call(
        _dispatch_kernel,
        grid=(n // DP_TM,),
        in_specs=[
            pl.BlockSpec((1, 1, TOP_K * DP_TM), lambda i: (i, 0, 0), memory_space=pltpu.SMEM),
            pl.BlockSpec((DP_TM, D_MODEL), lambda i: (i, 0)),
            pl.BlockSpec(memory_space=pl.ANY),
        ],
        out_specs=pl.BlockSpec(memory_space=pl.ANY),
        out_shape=jax.ShapeDtypeStruct((MOE_ROWS, D_MODEL), F32),
        scratch_shapes=[pltpu.SemaphoreType.DMA((2,))],
        input_output_aliases={2: 0},
        compiler_params=_cparams(("arbitrary",)),
        name="moe_dispatch",
    )(_tile_major(dest, DP_TM), h, zeros)


def _expert_kernel(be_ref, nu_ref, x_ref, w1_ref, b1_ref, w2_ref, b2_ref, o_ref, w1b, w2b):
    i = pl.program_id(0)
    changed = jnp.logical_or(i == 0, be_ref[i] != be_ref[jnp.maximum(i - 1, 0)])

    @pl.when(changed)
    def _():
        w1b[...] = w1_ref[0, 0].astype(BF16)
        w2b[...] = w2_ref[0, 0].astype(BF16)

    @pl.when(i < nu_ref[0])
    def _():
        hb = _dg(x_ref[...].astype(BF16), w1b[...], NN) + b1_ref[0, 0]
        hg = jnp.minimum(hb[:, :D_FF], SWIGLU_LIMIT)
        hu = jnp.clip(hb[:, D_FF:], -SWIGLU_LIMIT, SWIGLU_LIMIT)
        act = hg * _sigmoid(SWIGLU_ALPHA * hg) * (hu + 1.0)
        o_ref[...] = _dg(act.astype(BF16), w2b[...], NN) + b2_ref[0, 0]

    @pl.when(i >= nu_ref[0])
    def _():
        o_ref[...] = jnp.zeros_like(o_ref)


def _experts(x_rows, blk_e, n_used, layer, w1, b1, w2, b2):
    nb = MOE_ROWS // MOE_BM
    grid_spec = pltpu.PrefetchScalarGridSpec(
        num_scalar_prefetch=2,
        grid=(nb,),
        in_specs=[
            pl.BlockSpec((MOE_BM, D_MODEL), lambda i, be, nu: (i, 0)),
            pl.BlockSpec((1, 1, D_MODEL, 2 * D_FF), lambda i, be, nu: (layer, be[i], 0, 0)),
            pl.BlockSpec((1, 1, 1, 2 * D_FF), lambda i, be, nu: (layer, be[i], 0, 0)),
            pl.BlockSpec((1, 1, D_FF, D_MODEL), lambda i, be, nu: (layer, be[i], 0, 0)),
            pl.BlockSpec((1, 1, 1, D_MODEL), lambda i, be, nu: (layer, be[i], 0, 0)),
        ],
        out_specs=pl.BlockSpec((MOE_BM, D_MODEL), lambda i, be, nu: (i, 0)),
        scratch_shapes=[pltpu.VMEM((D_MODEL, 2 * D_FF), BF16), pltpu.VMEM((D_FF, D_MODEL), BF16)],
    )
    return pl.pallas_call(
        _expert_kernel,
        grid_spec=grid_spec,
        out_shape=jax.ShapeDtypeStruct((MOE_ROWS, D_MODEL), F32),
        compiler_params=pltpu.CompilerParams(dimension_semantics=("arbitrary",),
                                             vmem_limit_bytes=56 * 1024 * 1024),
        name="moe_experts",
    )(blk_e, n_used, x_rows, w1, b1.reshape(DEPTH, N_EXPERTS, 1, 2 * D_FF), w2,
      b2.reshape(DEPTH, N_EXPERTS, 1, D_MODEL))


def _combine_kernel(dest_ref, gate_ref, x_ref, mod_ref, fw_ref, y_hbm, o_ref, buf, sem, *, final):
    def issue(t, c):
        for k in range(TOP_K):
            dst = dest_ref[0, 0, k * CB_TM + t]
            pltpu.make_async_copy(y_hbm.at[pl.ds(dst, 1)], buf.at[k, pl.ds(t, 1)], sem.at[0]).start()
        return c

    lax.fori_loop(0, CB_TM, issue, 0)
    for k in range(TOP_K):
        pltpu.make_async_copy(y_hbm.at[pl.ds(0, CB_TM)], buf.at[k], sem.at[0]).wait()
    g = gate_ref[...]
    acc = g[:, 0:1] * buf[0]
    for k in range(1, TOP_K):
        acc = acc + g[:, k:k + 1] * buf[k]
    gate2 = _mod_row(mod_ref, pl.program_id(0) * CB_TM)[:, 5 * D_MODEL:6 * D_MODEL]
    x2 = x_ref[...] + gate2 * acc
    o_ref[...] = _rms(x2, fw_ref[...]) if final else x2


def _combine(y_rows, dest, gates, x, mod, final_w, final):
    n = x.shape[0]
    full = pl.BlockSpec((CB_TM, D_MODEL), lambda i: (i, 0))
    return pl.pallas_call(
        functools.partial(_combine_kernel, final=final),
        grid=(n // CB_TM,),
        in_specs=[
            pl.BlockSpec((1, 1, TOP_K * CB_TM), lambda i: (i, 0, 0), memory_space=pltpu.SMEM),
            pl.BlockSpec((CB_TM, TOP_K), lambda i: (i, 0)),
            full,
            pl.BlockSpec((MOD_ROWS, 6 * D_MODEL), lambda i: (0, 0)),
            pl.BlockSpec((1, D_MODEL), lambda i: (0, 0)),
            pl.BlockSpec(memory_space=pl.ANY),
        ],
        out_specs=full,
        out_shape=jax.ShapeDtypeStruct((n, D_MODEL), F32),
        scratch_shapes=[pltpu.VMEM((TOP_K, CB_TM, D_MODEL), F32), pltpu.SemaphoreType.DMA((1,))],
        compiler_params=_cparams(("arbitrary",)),
        name="moe_combine",
    )(_tile_major(dest, CB_TM), gates.T, x, mod, final_w.reshape(1, D_MODEL), y_rows)


def _routed_ffn(h, x, mod, p, moe, layer, final_w, final):
    idx, gates, rank, counts = _router(h, p['router_w'], p['router_b'])
    counts = counts[:, 0].astype(jnp.int32)
    padded = (counts + MOE_BM - 1) // MOE_BM * MOE_BM
    pad_end = jnp.cumsum(padded)
    pad_start = pad_end - padded
    experts = jnp.arange(N_EXPERTS, dtype=jnp.int32)
    start_of = jnp.sum(jnp.where(idx[:, :, None] == experts, pad_start, 0), axis=-1)
    dest = start_of + rank
    nb = MOE_ROWS // MOE_BM
    first_row = jnp.arange(nb, dtype=jnp.int32) * MOE_BM
    blk_e = jnp.minimum(jnp.sum((pad_end[None, :] <= first_row[:, None]).astype(jnp.int32), axis=1),
                        N_EXPERTS - 1)
    n_used = (pad_end[-1:] // MOE_BM).astype(jnp.int32)
    x_rows = _dispatch(h, dest)
    y_rows = _experts(x_rows, blk_e, n_used, layer, *moe)
    return _combine(y_rows, dest, gates, x, mod, final_w, final)


_LAYER_PARAMS = ('norm_mix_w', 'norm_ffn_w', 'da_lambda', 'da_norm_w', 'rw_shift', 'rw_w0', 'rw_w_up', 'rw_a0',
                 'rw_a_up', 'rw_g_up', 'rw_k_k', 'rw_k_a', 'rw_r_k', 'rw_ln_w', 'rw_ln_b', 'ret_decay_logit',
                 'ret_norm_w', 'cv_dw_w', 'cv_dw_b', 'cv_ln_w', 'cv_ln_b', 'w_branch', 'w_out', 'router_w',
                 'router_b')


def _layer(x, mod, w_p, p, moe, layer, lam_init, caches, tables, final_w, final):
    cache_k, cache_v, state_rw, state_ret = caches
    proj = _input_projection(x, mod, p['norm_mix_w'], w_p)
    o_da = _da_branch(proj, cache_k, cache_v, p['da_lambda'], p['da_norm_w'], lam_init, tables)
    o_rw, rw_state = _rwkv_branch(proj, state_rw, p)
    o_ret, ret_state = _retention_branch(proj, state_ret, p)
    o_cv = _conv_branch(proj, p)
    x1, h2 = _merge((o_da, o_rw, o_ret, o_cv), proj, x, mod, p['w_branch'], p['w_out'], p['norm_ffn_w'])
    x2 = _routed_ffn(h2, x1, mod, p, moe, layer, final_w, final)
    new_k = proj[:N_CTX, P_DA + DA_W:P_DA + 2 * DA_W].reshape(BATCH, SEQ, DA_HEADS, 2, DA_QK)
    new_v = proj[:N_CTX, P_DA + 2 * DA_W:P_DA + 3 * DA_W].reshape(BATCH, SEQ, DA_HEADS, DA_V)
    return x2, (new_k, new_v, rw_state[:BATCH], ret_state[:BATCH])


def kernel(x_prompt, x_sample, c, cache_da_k, cache_da_v, state_rwkv, state_ret, c_ctx, ada_w, ada_b, norm_mix_w,
           norm_ffn_w, w_in, da_lambda, da_norm_w, rw_shift, rw_w0, rw_w_up, rw_a0, rw_a_up, rw_g_up, rw_k_k,
           rw_k_a, rw_r_k, rw_ln_w, rw_ln_b, ret_decay_logit, ret_norm_w, cv_dw_w, cv_dw_b, cv_ln_w, cv_ln_b,
           w_branch, w_out, router_w, router_b, moe_w1, moe_b1, moe_w2, moe_b2, final_norm_w):
    weights = dict(norm_mix_w=norm_mix_w, norm_ffn_w=norm_ffn_w, da_lambda=da_lambda, da_norm_w=da_norm_w,
                   rw_shift=rw_shift, rw_w0=rw_w0, rw_w_up=rw_w_up, rw_a0=rw_a0, rw_a_up=rw_a_up, rw_g_up=rw_g_up,
                   rw_k_k=rw_k_k, rw_k_a=rw_k_a, rw_r_k=rw_r_k, rw_ln_w=rw_ln_w, rw_ln_b=rw_ln_b,
                   ret_decay_logit=ret_decay_logit, ret_norm_w=ret_norm_w, cv_dw_w=cv_dw_w, cv_dw_b=cv_dw_b,
                   cv_ln_w=cv_ln_w, cv_ln_b=cv_ln_b, w_branch=w_branch, w_out=w_out, router_w=router_w,
                   router_b=router_b, moe_w1=moe_w1, moe_b1=moe_b1, moe_w2=moe_w2, moe_b2=moe_b2)
    x = jnp.concatenate([x_prompt.reshape(N_CTX, D_MODEL), x_sample.reshape(N_LAT, D_MODEL)], axis=0)
    cvec = jnp.concatenate([c_ctx[None, :], c, jnp.zeros((MOD_ROWS - 1 - DEC_BATCH, D_MODEL), F32)], axis=0)
    mod = _modulation(cvec, ada_w, ada_b)
    w_p = _pad_w_in(w_in).astype(BF16)
    tables = _rope_tables()
    outs = []
    moe = (moe_w1, moe_b1, moe_w2, moe_b2)
    for i in range(DEPTH):
        p = {name: weights[name][i] for name in _LAYER_PARAMS}
        lam_init = 0.8 - 0.6 * math.exp(-0.3 * i)
        caches = (cache_da_k[:, i], cache_da_v[:, i], state_rwkv[:, i], state_ret[:, i])
        x, ctx_out = _layer(x, mod[i], w_p[i], p, moe, i, lam_init, caches, tables, final_norm_w, i == DEPTH - 1)
        outs.append(ctx_out)
    y_prompt = x[:N_CTX].reshape(BATCH, SEQ, D_MODEL)
    y_sample = x[N_CTX:].reshape(DEC_BATCH, DEC_SEQ, D_MODEL)
    stack = lambda j: jnp.stack([o[j] for o in outs], axis=1)
    return (y_prompt, y_sample, stack(0), stack(1), stack(2), stack(3))


def _pad_w_in(w_in):
    da, rw, ret, cv, gate = jnp.split(w_in, [1536, 3456, 4992, 6016], axis=-1)
    pad = jnp.zeros(w_in.shape[:-1] + (P_DA - RW_COLS,), w_in.dtype)
    return jnp.concatenate([rw, pad, da, ret, cv, gate], axis=-1)
```

```python
import functools
import math

import jax
import jax.numpy as jnp
from jax import lax
from jax.experimental import pallas as pl
from jax.experimental.pallas import tpu as pltpu

F32 = jnp.float32
BF16 = jnp.bfloat16

D_MODEL = 1024
BATCH = 16
SEQ = 256
DEPTH = 2
DEC_BATCH = 2
DEC_SEQ = 4096
PAST_LEN = 512
GRID_W = 64
EPS = 1e-6

DA_HEADS = 4
DA_QK = 64
DA_V = 128
ROPE_BASE = 10000.0

RW_HEADS = 8
RW_HD = 64
RW_W = 512
RW_LORA = 64
RW_G_LORA = 128
RW_GN_EPS = 64e-5

RET_HEADS = 4
RET_DK = 64
RET_DV = 128
RET_CHUNK = 128

CV_W = 512
CONV_K = 31
BR_W = 512
N_BRANCH = 4

N_EXPERTS = 32
TOP_K = 4
D_FF = 1024
SWIGLU_LIMIT = 7.0
SWIGLU_ALPHA = 1.702

N_CTX = BATCH * SEQ
N_LAT = DEC_BATCH * DEC_SEQ
N_TOK = N_CTX + N_LAT
N_SEQS = BATCH + DEC_BATCH
MOD_ROWS = 8
MOD_GROUP = 4096

RW_COLS = 1920
P_RW = 0
P_DA = 2048
P_RET = 3584
P_CV = 5120
P_GATE = 6144
P_COLS = 10240

VMEM_LIMIT = 48 * 1024 * 1024


def _cparams(sem):
    return pltpu.CompilerParams(dimension_semantics=sem, vmem_limit_bytes=VMEM_LIMIT)


def _dg(a, b, dims):
    return lax.dot_general(a, b, (dims, ((), ())), preferred_element_type=F32)


NN = ((1,), (0,))
NT = ((1,), (1,))
TN = ((0,), (0,))


def _dot(a, b, dims=NN):
    return _dg(a.astype(BF16), b.astype(BF16), dims)


def _split(x):
    hi = x.astype(BF16)
    lo = (x - hi.astype(F32)).astype(BF16)
    return hi, lo


def _dot3(a, b, dims=NN):
    ah, al = _split(a)
    bh, bl = _split(b)
    return _dg(ah, bh, dims) + (_dg(ah, bl, dims) + _dg(al, bh, dims))


def _dot2x(a, e, dims=NN):
    ah, al = _split(a)
    am = (a - ah.astype(F32) - al.astype(F32)).astype(BF16)
    eb = e.astype(BF16)
    return _dg(ah, eb, dims) + (_dg(al, eb, dims) + _dg(am, eb, dims))


def _sigmoid(x):
    return 1.0 / (1.0 + jnp.exp(-x))


def _softplus(x):
    return jnp.maximum(x, 0.0) + jnp.log(1.0 + jnp.exp(-jnp.abs(x)))


def _seq_pos(row):
    in_ctx = row < N_CTX
    pos = jnp.where(in_ctx, row & (SEQ - 1), (row - N_CTX) & (DEC_SEQ - 1))
    length = jnp.where(in_ctx, SEQ, DEC_SEQ)
    return pos, length


def _seq_index(row):
    return jnp.where(row < N_CTX, row // SEQ, BATCH + (row - N_CTX) // DEC_SEQ)


def _dotx(e, b, dims=NN):
    bh, bl = _split(b)
    bm = (b - bh.astype(F32) - bl.astype(F32)).astype(BF16)
    eb = e.astype(BF16)
    return _dg(eb, bh, dims) + (_dg(eb, bl, dims) + _dg(eb, bm, dims))


def _mod_kernel(c_ref, w_ref, b_ref, o_ref):
    c = c_ref[...]
    s = c * _sigmoid(c)
    o_ref[0] = _dot3(s, w_ref[0]) + b_ref[0]


def _modulation(cvec, ada_w, ada_b):
    tn = 1536
    return pl.pallas_call(
        _mod_kernel,
        grid=(DEPTH, 6 * D_MODEL // tn),
        in_specs=[
            pl.BlockSpec((MOD_ROWS, D_MODEL), lambda l, j: (0, 0)),
            pl.BlockSpec((1, D_MODEL, tn), lambda l, j: (l, 0, j)),
            pl.BlockSpec((1, 1, tn), lambda l, j: (l, 0, j)),
        ],
        out_specs=pl.BlockSpec((1, MOD_ROWS, tn), lambda l, j: (l, 0, j)),
        out_shape=jax.ShapeDtypeStruct((DEPTH, MOD_ROWS, 6 * D_MODEL), F32),
        compiler_params=_cparams(("parallel", "parallel")),
        name="modulation",
    )(cvec, ada_w, ada_b.reshape(DEPTH, 1, 6 * D_MODEL))


def _mod_row(mod_ref, first_row):
    g = first_row // MOD_GROUP
    return mod_ref[pl.ds(g, 1), :]


def _rms(x, w):
    return x * lax.rsqrt(jnp.mean(x * x, axis=-1, keepdims=True) + EPS) * w


IN_TM = 1024
IN_TN = 1024


def _inproj_kernel(x_ref, mod_ref, nw_ref, w_ref, o_ref, h_ref):
    i = pl.program_id(0)

    @pl.when(pl.program_id(1) == 0)
    def _():
        m = _mod_row(mod_ref, i * IN_TM)
        sh = m[:, 0:D_MODEL]
        sc = m[:, D_MODEL:2 * D_MODEL]
        h_ref[...] = (_rms(x_ref[...], nw_ref[...]) * (1.0 + sc) + sh).astype(BF16)

    o_ref[...] = _dg(h_ref[...], w_ref[...].astype(BF16), NN)


def _input_projection(x, mod, norm_w, w_p):
    n = x.shape[0]
    return pl.pallas_call(
        _inproj_kernel,
        grid=(n // IN_TM, P_COLS // IN_TN),
        in_specs=[
            pl.BlockSpec((IN_TM, D_MODEL), lambda i, j: (i, 0)),
            pl.BlockSpec((MOD_ROWS, 6 * D_MODEL), lambda i, j: (0, 0)),
            pl.BlockSpec((1, D_MODEL), lambda i, j: (0, 0)),
            pl.BlockSpec((D_MODEL, IN_TN), lambda i, j: (0, j)),
        ],
        out_specs=pl.BlockSpec((IN_TM, IN_TN), lambda i, j: (i, j)),
        out_shape=jax.ShapeDtypeStruct((n, P_COLS), F32),
        scratch_shapes=[pltpu.VMEM((IN_TM, D_MODEL), BF16)],
        compiler_params=_cparams(("parallel", "arbitrary")),
        name="input_projection",
    )(x, mod, norm_w.reshape(1, D_MODEL), w_p)


QK_TM = 512
DA_W = DA_HEADS * 2 * DA_QK


def _qkprep_kernel(q_ref, k_ref, v_ref, c_ref, se_ref, so_ref, qo_ref, ko_ref, vo_ref):
    i = pl.program_id(0)
    vo_ref[...] = v_ref[...].astype(BF16)
    scale = DA_QK ** -0.5

    @pl.when(i * QK_TM < N_CTX)
    def _():
        qo_ref[...] = (q_ref[...] * scale).astype(BF16)
        ko_ref[...] = k_ref[...].astype(BF16)

    @pl.when(i * QK_TM >= N_CTX)
    def _():
        c = c_ref[...]
        se = se_ref[...]
        so = so_ref[...]

        def rope(x):
            nxt = pltpu.roll(x, DA_W - 1, axis=1)
            prv = pltpu.roll(x, 1, axis=1)
            return x * c + nxt * se + prv * so

        qo_ref[...] = (rope(q_ref[...]) * scale).astype(BF16)
        ko_ref[...] = rope(k_ref[...]).astype(BF16)


def _rope_tables():
    rows = DEC_SEQ // GRID_W
    row = jnp.repeat(jnp.arange(rows, dtype=F32), GRID_W)
    col = jnp.tile(jnp.arange(GRID_W, dtype=F32), rows)
    n_pairs = DA_QK // 4
    inv = ROPE_BASE ** (-jnp.arange(n_pairs, dtype=F32) / n_pairs)
    ang = jnp.concatenate([row[:, None] * inv, col[:, None] * inv], axis=-1)
    cos = jnp.repeat(jnp.cos(ang), 2, axis=-1)
    sin = jnp.repeat(jnp.sin(ang), 2, axis=-1)
    even = (jnp.arange(DA_QK) % 2 == 0)[None, :]
    s_even = jnp.where(even, -sin, 0.0)
    s_odd = jnp.where(even, 0.0, sin)
    rep = lambda t: jnp.tile(t, (1, DA_W // DA_QK))
    return rep(cos), rep(s_even), rep(s_odd)


def _qk_prepare(proj, tables):
    n = proj.shape[0]
    lat0 = N_CTX // QK_TM
    nlat = DEC_SEQ // QK_TM
    tab = pl.BlockSpec((QK_TM, DA_W), lambda i: (jnp.maximum(i - lat0, 0) % nlat, 0))
    c0 = P_DA // DA_W
    out = jax.ShapeDtypeStruct((n, DA_W), BF16)
    return pl.pallas_call(
        _qkprep_kernel,
        grid=(n // QK_TM,),
        in_specs=[
            pl.BlockSpec((QK_TM, DA_W), lambda i: (i, c0)),
            pl.BlockSpec((QK_TM, DA_W), lambda i: (i, c0 + 1)),
            pl.BlockSpec((QK_TM, DA_W), lambda i: (i, c0 + 2)),
            tab, tab, tab,
        ],
        out_specs=[pl.BlockSpec((QK_TM, DA_W), lambda i: (i, 0))] * 3,
        out_shape=[out, out, out],
        compiler_params=_cparams(("parallel",)),
        name="qk_prepare",
    )(proj, proj, proj, *tables)


DA_TQ = 256


def _da_kernel(q_ref, k_ref, v_ref, dl_ref, nw_ref, o_ref, *, lam_init):
    dl = dl_ref[...]
    lam = (jnp.exp(jnp.sum(dl[0:1] * dl[1:2], axis=1, keepdims=True))
           - jnp.exp(jnp.sum(dl[2:3] * dl[3:4], axis=1, keepdims=True)) + lam_init)
    nw = nw_ref[...] * (1.0 - lam_init)
    q = q_ref[0]
    for h in range(DA_HEADS):
        vext = v_ref[0, :, h * 2 * DA_V:(h + 1) * 2 * DA_V]
        os = []
        for m in range(2):
            c0 = (2 * h + m) * DA_QK
            s = _dg(q[:, c0:c0 + DA_QK], k_ref[0, :, c0:c0 + DA_QK], NT)
            e = jnp.exp(s - jnp.max(s, axis=-1, keepdims=True)).astype(BF16)
            oe = _dg(e, vext, NN)
            os.append(oe[:, :DA_V] * (1.0 / oe[:, DA_V:DA_V + 1]))
        o_ref[0, :, h * DA_V:(h + 1) * DA_V] = _rms(os[0] - lam * os[1], nw)


def _diff_attention(q, k, v, da_lambda, da_norm_w, lam_init):
    b, tq, _ = q.shape
    tk = k.shape[1]
    return pl.pallas_call(
        functools.partial(_da_kernel, lam_init=lam_init),
        grid=(b, tq // DA_TQ),
        in_specs=[
            pl.BlockSpec((1, DA_TQ, DA_W), lambda i, j: (i, j, 0)),
            pl.BlockSpec((1, tk, DA_W), lambda i, j: (i, 0, 0), pipeline_mode=pl.Buffered(1)),
            pl.BlockSpec((1, tk, 2 * DA_W), lambda i, j: (i, 0, 0), pipeline_mode=pl.Buffered(1)),
            pl.BlockSpec((4, DA_QK), lambda i, j: (0, 0)),
            pl.BlockSpec((1, DA_V), lambda i, j: (0, 0)),
        ],
        out_specs=pl.BlockSpec((1, DA_TQ, DA_W), lambda i, j: (i, j, 0)),
        out_shape=jax.ShapeDtypeStruct((b, tq, DA_W), F32),
        compiler_params=_cparams(("parallel", "arbitrary")),
        name="diff_attention",
    )(q, k, v, da_lambda, da_norm_w.reshape(1, DA_V))


def _da_branch(proj, cache_k, cache_v, da_lambda, da_norm_w, lam_init, tables):
    qb, kb, vb = _qk_prepare(proj, tables)
    ctx = lambda t: t[:N_CTX].reshape(BATCH, SEQ, DA_W)
    lat = lambda t: t[N_CTX:].reshape(DEC_BATCH, DEC_SEQ, DA_W)

    def with_ones(v):
        b, t, _ = v.shape
        v4 = v.reshape(b, t, DA_HEADS, DA_V)
        return jnp.concatenate([v4, jnp.ones_like(v4)], axis=-1).reshape(b, t, 2 * DA_W)

    o_ctx = _diff_attention(ctx(qb), ctx(kb), with_ones(ctx(vb)), da_lambda, da_norm_w, lam_init)
    k_all = jnp.concatenate([cache_k.reshape(DEC_BATCH, PAST_LEN, DA_W).astype(BF16), lat(kb)], axis=1)
    v_all = jnp.concatenate([cache_v.reshape(DEC_BATCH, PAST_LEN, DA_W).astype(BF16), lat(vb)], axis=1)
    o_lat = _diff_attention(lat(qb), k_all, with_ones(v_all), da_lambda, da_norm_w, lam_init)
    return jnp.concatenate([o_ctx.reshape(N_CTX, DA_W), o_lat.reshape(N_LAT, DA_W)], axis=0)


RW_TM = 256
RW_C = 64
RW_PAIRS = RW_HEADS // 2
HALO = 8


def _head_ones():
    idx = jnp.arange(RW_W) // RW_HD
    return (idx[:, None] == idx[None, :]).astype(BF16)


def _rwprep_kernel(x_ref, xp_ref, xn_ref, mu_ref, w0_ref, wup_ref, a0_ref, aup_ref, gup_ref, kk_ref, ka_ref,
                   ones_ref, r_o, v_o, kk_o, g_o, kd_o, lw_o, a_o, buf):
    row0 = pl.program_id(0) * RW_TM
    x = x_ref[:, 0:RW_COLS]
    buf[HALO:HALO + RW_TM, :] = x
    buf[HALO - 1:HALO, :] = xp_ref[HALO - 1:HALO, 0:RW_COLS]
    buf[HALO + RW_TM:HALO + RW_TM + 1, :] = xn_ref[0:1, 0:RW_COLS]
    rows = row0 + lax.broadcasted_iota(jnp.int32, (RW_TM, 1), 0)
    pos, length = _seq_pos(rows)
    prev = jnp.where(pos == 0, 0.0, buf[HALO - 1:HALO - 1 + RW_TM, :])
    nxt = jnp.where(pos == length - 1, 0.0, buf[HALO + 1:HALO + 1 + RW_TM, :])
    mu = mu_ref[...]
    u = x + mu[0:1] * (prev - x) + mu[1:2] * (nxt - x)

    r = u[:, 0:RW_W]
    k = u[:, RW_W:2 * RW_W]
    v = u[:, 2 * RW_W:3 * RW_W]
    wl = u[:, 3 * RW_W:3 * RW_W + 128]
    al = u[:, 3 * RW_W + 128:3 * RW_W + 256]
    gl = u[:, 3 * RW_W + 256:3 * RW_W + 384]
    w_raw = w0_ref[...] + _dot3(jnp.tanh(wl), wup_ref[...])
    lw = -jnp.exp(-_softplus(-w_raw) - 0.5)
    a = _sigmoid(a0_ref[...] + _dot3(al, aup_ref[...]))
    g = _dot3(_sigmoid(gl), gup_ref[...])
    kk = k * kk_ref[...]
    kk = kk * lax.rsqrt(jnp.maximum(_dot2x(kk * kk, ones_ref[...]), 1e-12))
    kd = jnp.concatenate([k, k], axis=1) * (1.0 + (a - 1.0) * ka_ref[...])

    r_o[...] = r
    v_o[...] = v
    kk_o[...] = kk
    g_o[...] = g
    for d in range(2):
        kd_o[d] = kd[:, d * RW_W:(d + 1) * RW_W]
        lw_o[d] = lw[:, d * RW_W:(d + 1) * RW_W]
        a_o[d] = a[:, d * RW_W:(d + 1) * RW_W]


def _rw_prepare(proj, p):
    n = proj.shape[0]
    nh = n // HALO
    steps = RW_TM // HALO
    wide = P_DA - P_RW
    cat2 = lambda t: t.reshape(1, 2 * RW_W)
    blockdiag = lambda t: jnp.concatenate(
        [jnp.concatenate([t[0], jnp.zeros_like(t[0])], axis=1),
         jnp.concatenate([jnp.zeros_like(t[1]), t[1]], axis=1)], axis=0)
    const = lambda shape: pl.BlockSpec(shape, lambda i: (0,) * len(shape))
    row = pl.BlockSpec((RW_TM, RW_W), lambda i: (i, 0))
    row2 = pl.BlockSpec((2, RW_TM, RW_W), lambda i: (0, i, 0))
    o1 = jax.ShapeDtypeStruct((n, RW_W), F32)
    o2 = jax.ShapeDtypeStruct((2, n, RW_W), F32)
    return pl.pallas_call(
        _rwprep_kernel,
        grid=(n // RW_TM,),
        in_specs=[
            pl.BlockSpec((RW_TM, wide), lambda i: (i, 0)),
            pl.BlockSpec((HALO, wide), lambda i: (jnp.maximum(i * steps - 1, 0), 0)),
            pl.BlockSpec((HALO, wide), lambda i: (jnp.minimum((i + 1) * steps, nh - 1), 0)),
            const((2, RW_COLS)), const((1, 2 * RW_W)), const((128, 2 * RW_W)), const((1, 2 * RW_W)),
            const((128, 2 * RW_W)), const((RW_G_LORA, RW_W)), const((1, RW_W)), const((1, 2 * RW_W)),
            const((RW_W, RW_W)),
        ],
        out_specs=[row, row, row, row, row2, row2, row2],
        out_shape=[o1, o1, o1, o1, o2, o2, o2],
        scratch_shapes=[pltpu.VMEM((RW_TM + 2 * HALO, RW_COLS), F32)],
        compiler_params=_cparams(("parallel",)),
        name="rwkv_prepare",
    )(proj, proj, proj, p['rw_shift'], cat2(p['rw_w0']), blockdiag(p['rw_w_up']), cat2(p['rw_a0']),
      blockdiag(p['rw_a_up']), p['rw_g_up'], p['rw_k_k'].reshape(1, RW_W),
      jnp.tile(p['rw_k_a'].reshape(1, RW_W), (1, 2)), _head_ones())


RW_TB = 256
RW_NC = RW_TB // RW_C
RW_SIDE = 8


def _rwscan_kernel(r_ref, v_ref, kk_ref, kd_ref, lw_ref, a_ref, tri_ref, inc_ref, str_ref, s0_ref, y_ref, st_ref,
                   s_scr, *, backward):
    step = pl.program_id(0)
    nb = pl.num_programs(0)
    bi = (nb - 1 - step) if backward else step
    pos, length = _seq_pos(bi * RW_TB)
    first = (pos + RW_TB == length) if backward else (pos == 0)

    @pl.when(first)
    def _():
        s_scr[...] = s0_ref[0, 0]

    c = RW_C
    incl2 = inc_ref[...] > 0.5
    strict2 = str_ref[...] > 0.5
    eye2 = (lax.broadcasted_iota(jnp.int32, (2 * c, 2 * c), 0)
            == lax.broadcasted_iota(jnp.int32, (2 * c, 2 * c), 1))
    m_e = lax.broadcasted_iota(jnp.int32, (1, 2 * c), 1) < c

    lw = lw_ref[0]
    cum = _dotx(tri_ref[...], lw)
    tots = [cum[(j * c if backward else j * c + c - 1):(j * c + 1 if backward else j * c + c), :]
            for j in range(RW_NC)]
    tot_b = jnp.concatenate([jnp.broadcast_to(t, (c, RW_W)) for t in tots], axis=0)
    kk = kk_ref[...]
    kd = kd_ref[0]
    bp = kk * a_ref[0]
    g_inv = jnp.exp(-cum)
    g_rem = jnp.exp(tot_b - cum)
    ag = -kk * jnp.exp(cum - lw)
    rg = r_ref[...] * jnp.exp(cum)
    bdn = bp * g_inv
    kdn = kd * g_inv
    bc = bp * g_rem
    kc = kd * g_rem
    v = v_ref[...]

    def stack(x, j, p):
        xs = x[j * c:(j + 1) * c, p * 2 * c:(p + 1) * 2 * c]
        return jnp.concatenate([jnp.where(m_e, xs, 0.0), jnp.where(m_e, 0.0, xs)], axis=0).astype(BF16)

    pre = {}
    keys = [(j, p) for j in range(RW_NC) for p in range(RW_PAIRS)]
    for g0 in range(0, len(keys), RW_SIDE):
        grp = keys[g0:g0 + RW_SIDE]
        ops = {k: tuple(stack(t, *k) for t in (ag, rg, bdn, kdn, bc, kc, v)) for k in grp}
        gm = {k: _dg(jnp.concatenate([ops[k][0], ops[k][1]], axis=0),
                     jnp.concatenate([ops[k][2], ops[k][3]], axis=0), NT) for k in grp}
        lbb = {k: jnp.where(strict2, gm[k][:2 * c, :2 * c], 0.0) for k in grp}
        lkb = {k: jnp.where(strict2, gm[k][:2 * c, 2 * c:], 0.0).astype(BF16) for k in grp}
        lrk = {k: jnp.concatenate([jnp.where(incl2, gm[k][2 * c:, :2 * c], 0.0),
                                   jnp.where(incl2, gm[k][2 * c:, 2 * c:], 0.0)], axis=1).astype(BF16)
               for k in grp}
        lv = {k: _dg(lkb[k], ops[k][6], NN) for k in grp}
        x = {k: jnp.where(eye2, 1.0, lbb[k]) for k in grp}
        pw = lbb
        for _ in range(int(math.log2(c)) - 1):
            pwb = {k: pw[k].astype(BF16) for k in grp}
            pw = {k: _dg(pwb[k], pwb[k], NN) for k in grp}
            x = {k: x[k] + _dg(x[k].astype(BF16), pw[k].astype(BF16), NN) for k in grp}
        tw = {k: _dg(x[k].astype(BF16), jnp.concatenate([lv[k].astype(BF16), ops[k][0]], axis=1), NN)
              for k in grp}
        for k in grp:
            pre[k] = (tw[k][:, :2 * c], tw[k][:, 2 * c:].astype(BF16), ops[k][1], lrk[k], ops[k][6],
                      jnp.concatenate([ops[k][4], ops[k][5]], axis=0))

    order = range(RW_NC - 1, -1, -1) if backward else range(RW_NC)
    pairs = range(RW_PAIRS)
    s = [s_scr[p] for p in pairs]
    for j in order:
        sb = [s[p].astype(BF16) for p in pairs]
        u = [_dg(pre[j, p][1], sb[p], NT) + pre[j, p][0] for p in pairs]
        uv = [jnp.concatenate([u[p].astype(BF16), pre[j, p][4]], axis=0) for p in pairs]
        y = [_dg(pre[j, p][2], sb[p], NT) + _dg(pre[j, p][3], uv[p], NN) for p in pairs]
        s = [s[p] * jnp.exp(tots[j][:, p * 2 * c:(p + 1) * 2 * c]) + _dg(uv[p], pre[j, p][5], TN) for p in pairs]
        for p in pairs:
            y_ref[j * c:(j + 1) * c, p * 2 * c:(p + 1) * 2 * c] = y[p][:c] + y[p][c:]
    for p in pairs:
        s_scr[p] = s[p]
        st_ref[0, p] = s[p]


def _scan_masks(backward):
    t = jnp.arange(RW_TB)
    sgn = -1 if backward else 1
    same_chunk = (t[:, None] // RW_C) == (t[None, :] // RW_C)
    tri = (same_chunk & ((t[:, None] - t[None, :]) * sgn >= 0)).astype(BF16)
    q = jnp.arange(2 * RW_C)
    same_head = (q[:, None] // RW_C) == (q[None, :] // RW_C)
    dif = ((q[:, None] % RW_C) - (q[None, :] % RW_C)) * sgn
    return tri, (same_head & (dif >= 0)).astype(F32), (same_head & (dif > 0)).astype(F32)


def _rw_scan(r, v, kk, kd, lw, a, s0, backward):
    n = r.shape[0]
    nb = n // RW_TB
    d = 1 if backward else 0
    blk = (lambda i: nb - 1 - i) if backward else (lambda i: i)
    row = pl.BlockSpec((RW_TB, RW_W), lambda i: (blk(i), 0))
    row2 = pl.BlockSpec((1, RW_TB, RW_W), lambda i: (d, blk(i), 0))
    const = lambda shape: pl.BlockSpec(shape, lambda i: (0,) * len(shape))
    pair = 2 * RW_C
    return pl.pallas_call(
        functools.partial(_rwscan_kernel, backward=backward),
        grid=(nb,),
        in_specs=[row, row, row, row2, row2, row2, const((RW_TB, RW_TB)), const((pair, pair)), const((pair, pair)),
                  pl.BlockSpec((1, 1, RW_PAIRS, pair, pair), lambda i: (_seq_index(blk(i) * RW_TB), d, 0, 0, 0))],
        out_specs=[row, pl.BlockSpec((1, RW_PAIRS, pair, pair), lambda i: (_seq_index(blk(i) * RW_TB), 0, 0, 0))],
        out_shape=[jax.ShapeDtypeStruct((n, RW_W), F32),
                   jax.ShapeDtypeStruct((N_SEQS, RW_PAIRS, pair, pair), F32)],
        scratch_shapes=[pltpu.VMEM((RW_PAIRS, pair, pair), F32)],
        compiler_params=_cparams(("arbitrary",)),
        name="rwkv_scan_bwd" if backward else "rwkv_scan_fwd",
    )(r, v, kk, kd, lw, a, *_scan_masks(backward), s0)


def _rwpost_kernel(yf_ref, yb_ref, r_ref, v_ref, g_ref, kd_ref, rk_ref, lnw_ref, lnb_ref, ones_ref, o_ref):
    ones = ones_ref[...]
    inv = 1.0 / RW_HD
    y = yf_ref[...] + yb_ref[...]
    xc = y - _dot2x(y, ones) * inv
    var = _dot2x(xc * xc, ones) * inv
    yn = xc * lax.rsqrt(var + RW_GN_EPS) * lnw_ref[...] + lnb_ref[...]
    bonus = _dot2x(r_ref[...] * (kd_ref[0] + kd_ref[1]) * rk_ref[...], ones) * v_ref[...]
    o_ref[...] = (yn + bonus) * g_ref[...]


def _rw_post(yf, yb, r, v, g, kd, p):
    n = r.shape[0]
    row = pl.BlockSpec((RW_TM, RW_W), lambda i: (i, 0))
    row2 = pl.BlockSpec((2, RW_TM, RW_W), lambda i: (0, i, 0))
    vec = pl.BlockSpec((1, RW_W), lambda i: (0, 0))
    return pl.pallas_call(
        _rwpost_kernel,
        grid=(n // RW_TM,),
        in_specs=[row, row, row, row, row, row2, vec, vec, vec, pl.BlockSpec((RW_W, RW_W), lambda i: (0, 0))],
        out_specs=row,
        out_shape=jax.ShapeDtypeStruct((n, RW_W), F32),
        compiler_params=_cparams(("parallel",)),
        name="rwkv_post",
    )(yf, yb, r, v, g, kd, p['rw_r_k'].reshape(1, RW_W), p['rw_ln_w'].reshape(1, RW_W),
      p['rw_ln_b'].reshape(1, RW_W), _head_ones())


def _pairs_from_heads(s):
    lead = s.shape[:-3]
    st = s.reshape(lead + (RW_PAIRS, 2, RW_HD, RW_HD))
    eye = jnp.eye(2, dtype=s.dtype)
    out = st[..., :, :, :, None, :] * eye[:, None, :, None]
    return out.reshape(lead + (RW_PAIRS, 2 * RW_HD, 2 * RW_HD))


def _heads_from_pairs(s):
    lead = s.shape[:-3]
    s6 = s.reshape(lead + (RW_PAIRS, 2, RW_HD, 2, RW_HD))
    diag = jnp.stack([s6[..., 0, :, 0, :], s6[..., 1, :, 1, :]], axis=-3)
    return diag.reshape(lead + (RW_HEADS, RW_HD, RW_HD))


def _rwkv_branch(proj, state0, p):
    r, v, kk, g, kd, lw, a = _rw_prepare(proj, p)
    s0 = jnp.concatenate([jnp.zeros((BATCH,) + state0.shape[1:], F32), state0], axis=0)
    s0 = _pairs_from_heads(s0)
    yf, sf = _rw_scan(r, v, kk, kd, lw, a, s0, backward=False)
    yb, sb = _rw_scan(r, v, kk, kd, lw, a, s0, backward=True)
    st = _heads_from_pairs(jnp.stack([sf, sb], axis=1))
    return _rw_post(yf, yb, r, v, g, kd, p), st


RET_W = RET_HEADS * RET_DV
RET_PAIRS = RET_HEADS // 2


def _ret_kernel(q_ref, k_ref, v_ref, lg_ref, s0_ref, o_ref, st_ref, s_scr):
    d = pl.program_id(0)
    step = pl.program_id(1)
    nch = pl.num_programs(1)
    c = RET_CHUNK
    ci = jnp.where(d == 0, step, nch - 1 - step)
    pos, length = _seq_pos(ci * c)
    first = jnp.where(d == 0, pos == 0, pos + c == length)

    @pl.when(first)
    def _():
        s_scr[...] = s0_ref[0, 0]

    sgn = jnp.where(d == 0, 1, -1)
    lgs = -_softplus(-lg_ref[pl.ds(d, 1), :])
    ri = lax.broadcasted_iota(jnp.int32, (c, c), 0)
    cj = lax.broadcasted_iota(jnp.int32, (c, c), 1)
    dif = (ri - cj) * sgn
    valid = dif >= 0
    dist = jnp.maximum(dif, 0).astype(F32)
    pr = lax.broadcasted_iota(jnp.int32, (c, 2 * RET_DK), 0)
    tau = jnp.where(d == 0, pr, c - 1 - pr).astype(F32)
    low = lax.broadcasted_iota(jnp.int32, (1, 2 * RET_DK), 1) < RET_DK
    rlow = lax.broadcasted_iota(jnp.int32, (2 * RET_DK, RET_DV), 0) < RET_DK
    q = q_ref[...]
    k = k_ref[...] * (RET_DK ** -0.5)
    heads = range(RET_HEADS)
    lg_h = [lgs[:, h:h + 1] for h in heads]
    lg_row = [jnp.where(low, lg_h[2 * p], lg_h[2 * p + 1]) for p in range(RET_PAIRS)]
    qp = [q[:, p * 128:(p + 1) * 128] for p in range(RET_PAIRS)]
    kp = [k[:, p * 128:(p + 1) * 128] for p in range(RET_PAIRS)]
    kpb = [t.astype(BF16) for t in kp]
    qd = [qp[p] * jnp.exp(lg_row[p] * (tau + 1.0)) for p in range(RET_PAIRS)]
    kdec = [kp[p] * jnp.exp(lg_row[p] * (c - 1.0 - tau)) for p in range(RET_PAIRS)]
    sp = [s_scr[p] for p in range(RET_PAIRS)]
    spb = [t.astype(BF16) for t in sp]
    mask = [low, jnp.logical_not(low)]
    vb = [v_ref[:, h * RET_DV:(h + 1) * RET_DV].astype(BF16) for h in heads]
    att = [_dg(jnp.where(mask[h % 2], qp[h // 2], 0.0).astype(BF16), kpb[h // 2], NT)
           * jnp.where(valid, jnp.exp(lg_h[h] * dist), 0.0) for h in heads]
    cross = [_dg(jnp.where(mask[h % 2], qd[h // 2], 0.0).astype(BF16), spb[h // 2], NN) for h in heads]
    upd = [_dg(jnp.where(mask[h % 2], kdec[h // 2], 0.0).astype(BF16), vb[h], TN) for h in heads]
    for h in heads:
        o_ref[0, :, h * RET_DV:(h + 1) * RET_DV] = _dg(att[h].astype(BF16), vb[h], NN) + cross[h]
    for p in range(RET_PAIRS):
        sn = (sp[p] * jnp.where(rlow, jnp.exp(lg_h[2 * p] * c), jnp.exp(lg_h[2 * p + 1] * c))
              + upd[2 * p] + upd[2 * p + 1])
        s_scr[p] = sn
        st_ref[0, 0, p] = sn


def _ret_scan(proj, logit, s0):
    n = proj.shape[0]
    c = RET_CHUNK
    nch = n // c
    ci = lambda d, s: jnp.where(d == 0, s, nch - 1 - s)
    qk_w = RET_HEADS * RET_DK
    st = pl.BlockSpec((1, 1, RET_PAIRS, 2 * RET_DK, RET_DV),
                      lambda d, s: (_seq_index(ci(d, s) * c), d, 0, 0, 0))
    return pl.pallas_call(
        _ret_kernel,
        grid=(2, nch),
        in_specs=[
            pl.BlockSpec((c, qk_w), lambda d, s: (ci(d, s), P_RET // qk_w)),
            pl.BlockSpec((c, qk_w), lambda d, s: (ci(d, s), P_RET // qk_w + 1)),
            pl.BlockSpec((c, RET_W), lambda d, s: (ci(d, s), (P_RET + 2 * qk_w) // RET_W)),
            pl.BlockSpec((2, RET_HEADS), lambda d, s: (0, 0)),
            st,
        ],
        out_specs=[pl.BlockSpec((1, c, RET_W), lambda d, s: (d, ci(d, s), 0)), st],
        out_shape=[jax.ShapeDtypeStruct((2, n, RET_W), F32),
                   jax.ShapeDtypeStruct((N_SEQS, 2, RET_PAIRS, 2 * RET_DK, RET_DV), F32)],
        scratch_shapes=[pltpu.VMEM((RET_PAIRS, 2 * RET_DK, RET_DV), F32)],
        compiler_params=_cparams(("arbitrary", "arbitrary")),
        name="retention_scan",
    )(proj, proj, proj, logit, s0)


def _standardize(x, eps):
    mu = jnp.mean(x, axis=-1, keepdims=True)
    xc = x - mu
    return xc * lax.rsqrt(jnp.mean(xc * xc, axis=-1, keepdims=True) + eps)


def _retpost_kernel(o_ref, g_ref, nw_ref, out_ref):
    o = o_ref[0] + o_ref[1]
    g = g_ref[...]
    for h in range(RET_HEADS):
        sl = slice(h * RET_DV, (h + 1) * RET_DV)
        gh = g[:, sl]
        out_ref[:, sl] = gh * _sigmoid(gh) * (_standardize(o[:, sl], EPS) * nw_ref[:, sl])


def _ret_post(o, proj, norm_w):
    n = proj.shape[0]
    tm = 512
    return pl.pallas_call(
        _retpost_kernel,
        grid=(n // tm,),
        in_specs=[
            pl.BlockSpec((2, tm, RET_W), lambda i: (0, i, 0)),
            pl.BlockSpec((tm, RET_W), lambda i: (i, (P_RET + 1024) // RET_W)),
            pl.BlockSpec((1, RET_W), lambda i: (0, 0)),
        ],
        out_specs=pl.BlockSpec((tm, RET_W), lambda i: (i, 0)),
        out_shape=jax.ShapeDtypeStruct((n, RET_W), F32),
        compiler_params=_cparams(("parallel",)),
        name="retention_post",
    )(o, proj, norm_w.reshape(1, RET_W))


def _retention_branch(proj, state0, p):
    s0 = jnp.concatenate([jnp.zeros((BATCH,) + state0.shape[1:], F32), state0], axis=0)
    s0 = s0.reshape(N_SEQS, 2, RET_PAIRS, 2 * RET_DK, RET_DV)
    o, st = _ret_scan(proj, p['ret_decay_logit'], s0)
    return _ret_post(o, proj, p['ret_norm_w']), st.reshape(N_SEQS, 2, RET_HEADS, RET_DK, RET_DV)


CV_TM = 256
CV_HALO = 16


def _conv_kernel(a_ref, g_ref, ap_ref, gp_ref, an_ref, gn_ref, w_ref, b_ref, lnw_ref, lnb_ref, o_ref, buf):
    row0 = pl.program_id(0) * CV_TM
    pos0, len0 = _seq_pos(row0)
    buf[CV_HALO:CV_HALO + CV_TM, :] = a_ref[...] * _sigmoid(g_ref[...])
    buf[0:CV_HALO, :] = jnp.where(pos0 == 0, 0.0, ap_ref[...] * _sigmoid(gp_ref[...]))
    buf[CV_HALO + CV_TM:, :] = jnp.where(pos0 + CV_TM == len0, 0.0, an_ref[...] * _sigmoid(gn_ref[...]))
    base = CV_HALO - CONV_K // 2
    acc = jnp.zeros((CV_TM, CV_W), F32)
    for j in range(CONV_K):
        acc = acc + w_ref[j:j + 1, :] * buf[base + j:base + j + CV_TM, :]
    z = _standardize(acc + b_ref[...], EPS) * lnw_ref[...] + lnb_ref[...]
    o_ref[...] = z * _sigmoid(z)


def _conv_branch(proj, p):
    n = proj.shape[0]
    nh = n // CV_HALO
    steps = CV_TM // CV_HALO
    ca = P_CV // CV_W
    prev = lambda i: jnp.maximum(i * steps - 1, 0)
    nxt = lambda i: jnp.minimum((i + 1) * steps, nh - 1)
    vec = pl.BlockSpec((1, CV_W), lambda i: (0, 0))
    return pl.pallas_call(
        _conv_kernel,
        grid=(n // CV_TM,),
        in_specs=[
            pl.BlockSpec((CV_TM, CV_W), lambda i: (i, ca)),
            pl.BlockSpec((CV_TM, CV_W), lambda i: (i, ca + 1)),
            pl.BlockSpec((CV_HALO, CV_W), lambda i: (prev(i), ca)),
            pl.BlockSpec((CV_HALO, CV_W), lambda i: (prev(i), ca + 1)),
            pl.BlockSpec((CV_HALO, CV_W), lambda i: (nxt(i), ca)),
            pl.BlockSpec((CV_HALO, CV_W), lambda i: (nxt(i), ca + 1)),
            pl.BlockSpec((CONV_K, CV_W), lambda i: (0, 0)),
            vec, vec, vec,
        ],
        out_specs=pl.BlockSpec((CV_TM, CV_W), lambda i: (i, 0)),
        out_shape=jax.ShapeDtypeStruct((n, CV_W), F32),
        scratch_shapes=[pltpu.VMEM((CV_TM + 2 * CV_HALO, CV_W), F32)],
        compiler_params=_cparams(("parallel",)),
        name="conformer_conv",
    )(proj, proj, proj, proj, proj, proj, p['cv_dw_w'], p['cv_dw_b'].reshape(1, CV_W),
      p['cv_ln_w'].reshape(1, CV_W), p['cv_ln_b'].reshape(1, CV_W))


MG_TM = 512


def _merge_kernel(da_ref, rw_ref, ret_ref, cv_ref, g0_ref, g1_ref, g2_ref, g3_ref, x_ref, mod_ref, wb_ref, wo_ref,
                  nw_ref, x_o, h_o):
    m = None
    for n, (br, gt) in enumerate(((da_ref, g0_ref), (rw_ref, g1_ref), (ret_ref, g2_ref), (cv_ref, g3_ref))):
        t = _sigmoid(gt[...]) * _dg(br[...].astype(BF16), wb_ref[n], NN)
        m = t if m is None else m + t
    out = _dg(m.astype(BF16), wo_ref[...], NN)
    mrow = _mod_row(mod_ref, pl.program_id(0) * MG_TM)
    gate1 = mrow[:, 2 * D_MODEL:3 * D_MODEL]
    sh2 = mrow[:, 3 * D_MODEL:4 * D_MODEL]
    sc2 = mrow[:, 4 * D_MODEL:5 * D_MODEL]
    x1 = x_ref[...] + gate1 * out
    x_o[...] = x1
    h_o[...] = _rms(x1, nw_ref[...]) * (1.0 + sc2) + sh2


def _merge(branches, proj, x, mod, w_branch, w_out, norm_w):
    n = x.shape[0]
    br = pl.BlockSpec((MG_TM, BR_W), lambda i: (i, 0))
    gspec = lambda j: pl.BlockSpec((MG_TM, D_MODEL), lambda i: (i, P_GATE // D_MODEL + j))
    full = pl.BlockSpec((MG_TM, D_MODEL), lambda i: (i, 0))
    out = jax.ShapeDtypeStruct((n, D_MODEL), F32)
    return pl.pallas_call(
        _merge_kernel,
        grid=(n // MG_TM,),
        in_specs=[br, br, br, br, gspec(0), gspec(1), gspec(2), gspec(3), full,
                  pl.BlockSpec((MOD_ROWS, 6 * D_MODEL), lambda i: (0, 0)),
                  pl.BlockSpec((N_BRANCH, BR_W, D_MODEL), lambda i: (0, 0, 0)),
                  pl.BlockSpec((D_MODEL, D_MODEL), lambda i: (0, 0)),
                  pl.BlockSpec((1, D_MODEL), lambda i: (0, 0))],
        out_specs=[full, full],
        out_shape=[out, out],
        compiler_params=_cparams(("parallel",)),
        name="gated_merge",
    )(*branches, proj, proj, proj, proj, x, mod, w_branch.astype(BF16), w_out.astype(BF16),
      norm_w.reshape(1, D_MODEL))


RT_TM = 256
MOE_BM = 512
MOE_ROWS = N_TOK * TOP_K + N_EXPERTS * MOE_BM
DP_TM = 512
DP_GROUP = 64
CB_TM = 256


def _router_kernel(h_ref, w_ref, b_ref, tri_ref, idx_o, gate_o, rank_o, cnt_o, carry):
    @pl.when(pl.program_id(0) == 0)
    def _():
        carry[...] = jnp.zeros_like(carry)

    logits = _dot3(w_ref[...], h_ref[...], NT) + b_ref[...]
    e_iota = lax.broadcasted_iota(jnp.int32, logits.shape, 0)
    work = logits
    vals, idxs, hots = [], [], []
    for _ in range(TOP_K):
        mx = jnp.max(work, axis=0, keepdims=True)
        ix = jnp.min(jnp.where(work == mx, e_iota, N_EXPERTS), axis=0, keepdims=True)
        hot = e_iota == ix
        vals.append(mx)
        idxs.append(ix)
        hots.append(hot.astype(F32))
        work = jnp.where(hot, -jnp.inf, work)
    es = [jnp.exp(v - vals[0]) for v in vals]
    inv = 1.0 / (es[0] + es[1] + es[2] + es[3])
    chosen = hots[0] + hots[1] + hots[2] + hots[3]
    ahead = carry[...][:, 0:1] + _dg(chosen.astype(BF16), tri_ref[...], NN)
    idx_o[...] = jnp.concatenate(idxs, axis=0)
    gate_o[...] = jnp.concatenate([e * inv for e in es], axis=0)
    rank_o[...] = jnp.concatenate(
        [jnp.sum(hot * ahead, axis=0, keepdims=True) for hot in hots], axis=0).astype(jnp.int32)
    carry[...] = carry[...] + jnp.sum(chosen, axis=1, keepdims=True)
    cnt_o[...] = carry[...]


def _router(h, router_w, router_b):
    n = h.shape[0]
    tri = (jnp.arange(RT_TM)[:, None] < jnp.arange(RT_TM)[None, :]).astype(BF16)
    col = pl.BlockSpec((TOP_K, RT_TM), lambda i: (0, i))
    return pl.pallas_call(
        _router_kernel,
        grid=(n // RT_TM,),
        in_specs=[
            pl.BlockSpec((RT_TM, D_MODEL), lambda i: (i, 0)),
            pl.BlockSpec((N_EXPERTS, D_MODEL), lambda i: (0, 0)),
            pl.BlockSpec((N_EXPERTS, 1), lambda i: (0, 0)),
            pl.BlockSpec((RT_TM, RT_TM), lambda i: (0, 0)),
        ],
        out_specs=[col, col, col, pl.BlockSpec((N_EXPERTS, 128), lambda i: (0, 0))],
        out_shape=[jax.ShapeDtypeStruct((TOP_K, n), jnp.int32), jax.ShapeDtypeStruct((TOP_K, n), F32),
                   jax.ShapeDtypeStruct((TOP_K, n), jnp.int32), jax.ShapeDtypeStruct((N_EXPERTS, 128), F32)],
        scratch_shapes=[pltpu.VMEM((N_EXPERTS, 128), F32)],
        compiler_params=_cparams(("arbitrary",)),
        name="router",
    )(h, router_w.T, router_b.reshape(N_EXPERTS, 1), tri)


def _tile_major(t, tm):
    k, n = t.shape
    return t.reshape(k, n // tm, tm).transpose(1, 0, 2).reshape(n // tm, 1, k * tm)


def _dispatch_kernel(dest_ref, h_ref, z_hbm, o_hbm, sem):
    del z_hbm
    n_groups = DP_TM // DP_GROUP

    def wait_group(slot):
        pltpu.make_async_copy(h_ref.at[pl.ds(0, TOP_K * DP_GROUP)], o_hbm.at[pl.ds(0, TOP_K * DP_GROUP)],
                              sem.at[slot]).wait()

    def group(gi, carry):
        slot = gi % 2

        def issue(t, c):
            tok = gi * DP_GROUP + t
            for k in range(TOP_K):
                dst = dest_ref[0, 0, k * DP_TM + tok]
                pltpu.make_async_copy(h_ref.at[pl.ds(tok, 1)], o_hbm.at[pl.ds(dst, 1)], sem.at[slot]).start()
            return c

        lax.fori_loop(0, DP_GROUP, issue, 0)

        @pl.when(gi > 0)
        def _():
            wait_group(1 - slot)

        return carry

    lax.fori_loop(0, n_groups, group, 0)
    wait_group((n_groups - 1) % 2)


def _dispatch(h, dest):
    n = h.shape[0]
    zeros = jnp.zeros((MOE_ROWS, D_MODEL), F32)
    return pl.pallas_call(
        _dispatch_kernel,
        grid=(n // DP_TM,),
        in_specs=[
            pl.BlockSpec((1, 1, TOP_K * DP_TM), lambda i: (i, 0, 0), memory_space=pltpu.SMEM),
            pl.BlockSpec((DP_TM, D_MODEL), lambda i: (i, 0)),
            pl.BlockSpec(memory_space=pl.ANY),
        ],
        out_specs=pl.BlockSpec(memory_space=pl.ANY),
        out_shape=jax.ShapeDtypeStruct((MOE_ROWS, D_MODEL), F32),
        scratch_shapes=[pltpu.SemaphoreType.DMA((2,))],
        input_output_aliases={2: 0},
        compiler_params=_cparams(("arbitrary",)),
        name="moe_dispatch",
    )(_tile_major(dest, DP_TM), h, zeros)


def _expert_kernel(be_ref, nv_ref, x_ref, w1_ref, b1_ref, w2_ref, b2_ref, o_ref, w1b, w2b):
    i = pl.program_id(0)
    changed = jnp.logical_or(i == 0, be_ref[i] != be_ref[jnp.maximum(i - 1, 0)])

    @pl.when(changed)
    def _():
        w1b[...] = w1_ref[0, 0].astype(BF16)
        w2b[...] = w2_ref[0, 0].astype(BF16)

    def ffn(rows):
        hb = _dg(x_ref[0:rows, :].astype(BF16), w1b[...], NN) + b1_ref[0, 0]
        hg = jnp.minimum(hb[:, :D_FF], SWIGLU_LIMIT)
        hu = jnp.clip(hb[:, D_FF:], -SWIGLU_LIMIT, SWIGLU_LIMIT)
        act = hg * _sigmoid(SWIGLU_ALPHA * hg) * (hu + 1.0)
        o_ref[0:rows, :] = _dg(act.astype(BF16), w2b[...], NN) + b2_ref[0, 0]

    nv = nv_ref[i]
    half = MOE_BM // 2

    @pl.when(nv > half)
    def _():
        ffn(MOE_BM)

    @pl.when(jnp.logical_and(nv > 0, nv <= half))
    def _():
        ffn(half)
        o_ref[half:, :] = jnp.zeros((MOE_BM - half, D_MODEL), F32)

    @pl.when(nv == 0)
    def _():
        o_ref[...] = jnp.zeros_like(o_ref)


def _experts(x_rows, blk_e, n_valid, layer, w1, b1, w2, b2):
    nb = MOE_ROWS // MOE_BM
    grid_spec = pltpu.PrefetchScalarGridSpec(
        num_scalar_prefetch=2,
        grid=(nb,),
        in_specs=[
            pl.BlockSpec((MOE_BM, D_MODEL), lambda i, be, nu: (i, 0)),
            pl.BlockSpec((1, 1, D_MODEL, 2 * D_FF), lambda i, be, nu: (layer, be[i], 0, 0)),
            pl.BlockSpec((1, 1, 1, 2 * D_FF), lambda i, be, nu: (layer, be[i], 0, 0)),
            pl.BlockSpec((1, 1, D_FF, D_MODEL), lambda i, be, nu: (layer, be[i], 0, 0)),
            pl.BlockSpec((1, 1, 1, D_MODEL), lambda i, be, nu: (layer, be[i], 0, 0)),
        ],
        out_specs=pl.BlockSpec((MOE_BM, D_MODEL), lambda i, be, nu: (i, 0)),
        scratch_shapes=[pltpu.VMEM((D_MODEL, 2 * D_FF), BF16), pltpu.VMEM((D_FF, D_MODEL), BF16)],
    )
    return pl.pallas_call(
        _expert_kernel,
        grid_spec=grid_spec,
        out_shape=jax.ShapeDtypeStruct((MOE_ROWS, D_MODEL), F32),
        compiler_params=pltpu.CompilerParams(dimension_semantics=("arbitrary",),
                                             vmem_limit_bytes=56 * 1024 * 1024),
        name="moe_experts",
    )(blk_e, n_valid, x_rows, w1, b1.reshape(DEPTH, N_EXPERTS, 1, 2 * D_FF), w2,
      b2.reshape(DEPTH, N_EXPERTS, 1, D_MODEL))


def _combine_kernel(dest_ref, gate_ref, x_ref, mod_ref, fw_ref, y_hbm, o_ref, buf, sem, *, final):
    def issue(t, c):
        for k in range(TOP_K):
            dst = dest_ref[0, 0, k * CB_TM + t]
            pltpu.make_async_copy(y_hbm.at[pl.ds(dst, 1)], buf.at[k, pl.ds(t, 1)], sem.at[0]).start()
        return c

    lax.fori_loop(0, CB_TM, issue, 0)
    for k in range(TOP_K):
        pltpu.make_async_copy(y_hbm.at[pl.ds(0, CB_TM)], buf.at[k], sem.at[0]).wait()
    g = gate_ref[...]
    acc = g[:, 0:1] * buf[0]
    for k in range(1, TOP_K):
        acc = acc + g[:, k:k + 1] * buf[k]
    gate2 = _mod_row(mod_ref, pl.program_id(0) * CB_TM)[:, 5 * D_MODEL:6 * D_MODEL]
    x2 = x_ref[...] + gate2 * acc
    o_ref[...] = _rms(x2, fw_ref[...]) if final else x2


def _combine(y_rows, dest, gates, x, mod, final_w, final):
    n = x.shape[0]
    full = pl.BlockSpec((CB_TM, D_MODEL), lambda i: (i, 0))
    return pl.pallas_call(
        functools.partial(_combine_kernel, final=final),
        grid=(n // CB_TM,),
        in_specs=[
            pl.BlockSpec((1, 1, TOP_K * CB_TM), lambda i: (i, 0, 0), memory_space=pltpu.SMEM),
            pl.BlockSpec((CB_TM, TOP_K), lambda i: (i, 0)),
            full,
            pl.BlockSpec((MOD_ROWS, 6 * D_MODEL), lambda i: (0, 0)),
            pl.BlockSpec((1, D_MODEL), lambda i: (0, 0)),
            pl.BlockSpec(memory_space=pl.ANY),
        ],
        out_specs=full,
        out_shape=jax.ShapeDtypeStruct((n, D_MODEL), F32),
        scratch_shapes=[pltpu.VMEM((TOP_K, CB_TM, D_MODEL), F32), pltpu.SemaphoreType.DMA((1,))],
        compiler_params=_cparams(("arbitrary",)),
        name="moe_combine",
    )(_tile_major(dest, CB_TM), gates.T, x, mod, final_w.reshape(1, D_MODEL), y_rows)


def _routed_ffn(h, x, mod, p, moe, layer, final_w, final):
    idx, gates, rank, counts = _router(h, p['router_w'], p['router_b'])
    counts = counts[:, 0].astype(jnp.int32)
    padded = (counts + MOE_BM - 1) // MOE_BM * MOE_BM
    pad_end = jnp.cumsum(padded)
    pad_start = pad_end - padded
    experts = jnp.arange(N_EXPERTS, dtype=jnp.int32)
    start_of = jnp.sum(jnp.where(idx[:, :, None] == experts, pad_start, 0), axis=-1)
    dest = start_of + rank
    nb = MOE_ROWS // MOE_BM
    first_row = jnp.arange(nb, dtype=jnp.int32) * MOE_BM
    blk_e = jnp.minimum(jnp.sum((pad_end[None, :] <= first_row[:, None]).astype(jnp.int32), axis=1),
                        N_EXPERTS - 1)
    is_e = blk_e[:, None] == experts[None, :]
    end_of = jnp.sum(jnp.where(is_e, pad_start + counts, 0), axis=1)
    n_valid = jnp.clip(end_of - first_row, 0, MOE_BM).astype(jnp.int32)
    x_rows = _dispatch(h, dest)
    y_rows = _experts(x_rows, blk_e, n_valid, layer, *moe)
    return _combine(y_rows, dest, gates, x, mod, final_w, final)


_LAYER_PARAMS = ('norm_mix_w', 'norm_ffn_w', 'da_lambda', 'da_norm_w', 'rw_shift', 'rw_w0', 'rw_w_up', 'rw_a0',
                 'rw_a_up', 'rw_g_up', 'rw_k_k', 'rw_k_a', 'rw_r_k', 'rw_ln_w', 'rw_ln_b', 'ret_decay_logit',
                 'ret_norm_w', 'cv_dw_w', 'cv_dw_b', 'cv_ln_w', 'cv_ln_b', 'w_branch', 'w_out', 'router_w',
                 'router_b')


def _layer(x, mod, w_p, p, moe, layer, lam_init, caches, tables, final_w, final):
    cache_k, cache_v, state_rw, state_ret = caches
    proj = _input_projection(x, mod, p['norm_mix_w'], w_p)
    o_da = _da_branch(proj, cache_k, cache_v, p['da_lambda'], p['da_norm_w'], lam_init, tables)
    o_rw, rw_state = _rwkv_branch(proj, state_rw, p)
    o_ret, ret_state = _retention_branch(proj, state_ret, p)
    o_cv = _conv_branch(proj, p)
    x1, h2 = _merge((o_da, o_rw, o_ret, o_cv), proj, x, mod, p['w_branch'], p['w_out'], p['norm_ffn_w'])
    x2 = _routed_ffn(h2, x1, mod, p, moe, layer, final_w, final)
    new_k = proj[:N_CTX, P_DA + DA_W:P_DA + 2 * DA_W].reshape(BATCH, SEQ, DA_HEADS, 2, DA_QK)
    new_v = proj[:N_CTX, P_DA + 2 * DA_W:P_DA + 3 * DA_W].reshape(BATCH, SEQ, DA_HEADS, DA_V)
    return x2, (new_k, new_v, rw_state[:BATCH], ret_state[:BATCH])


def kernel(x_prompt, x_sample, c, cache_da_k, cache_da_v, state_rwkv, state_ret, c_ctx, ada_w, ada_b, norm_mix_w,
           norm_ffn_w, w_in, da_lambda, da_norm_w, rw_shift, rw_w0, rw_w_up, rw_a0, rw_a_up, rw_g_up, rw_k_k,
           rw_k_a, rw_r_k, rw_ln_w, rw_ln_b, ret_decay_logit, ret_norm_w, cv_dw_w, cv_dw_b, cv_ln_w, cv_ln_b,
           w_branch, w_out, router_w, router_b, moe_w1, moe_b1, moe_w2, moe_b2, final_norm_w):
    weights = dict(norm_mix_w=norm_mix_w, norm_ffn_w=norm_ffn_w, da_lambda=da_lambda, da_norm_w=da_norm_w,
                   rw_shift=rw_shift, rw_w0=rw_w0, rw_w_up=rw_w_up, rw_a0=rw_a0, rw_a_up=rw_a_up, rw_g_up=rw_g_up,
                   rw_k_k=rw_k_k, rw_k_a=rw_k_a, rw_r_k=rw_r_k, rw_ln_w=rw_ln_w, rw_ln_b=rw_ln_b,
                   ret_decay_logit=ret_decay_logit, ret_norm_w=ret_norm_w, cv_dw_w=cv_dw_w, cv_dw_b=cv_dw_b,
                   cv_ln_w=cv_ln_w, cv_ln_b=cv_ln_b, w_branch=w_branch, w_out=w_out, router_w=router_w,
                   router_b=router_b, moe_w1=moe_w1, moe_b1=moe_b1, moe_w2=moe_w2, moe_b2=moe_b2)
    x = jnp.concatenate([x_prompt.reshape(N_CTX, D_MODEL), x_sample.reshape(N_LAT, D_MODEL)], axis=0)
    cvec = jnp.concatenate([c_ctx[None, :], c, jnp.zeros((MOD_ROWS - 1 - DEC_BATCH, D_MODEL), F32)], axis=0)
    mod = _modulation(cvec, ada_w, ada_b)
    w_p = _pad_w_in(w_in).astype(BF16)
    tables = _rope_tables()
    outs = []
    moe = (moe_w1, moe_b1, moe_w2, moe_b2)
    for i in range(DEPTH):
        p = {name: weights[name][i] for name in _LAYER_PARAMS}
        lam_init = 0.8 - 0.6 * math.exp(-0.3 * i)
        caches = (cache_da_k[:, i], cache_da_v[:, i], state_rwkv[:, i], state_ret[:, i])
        x, ctx_out = _layer(x, mod[i], w_p[i], p, moe, i, lam_init, caches, tables, final_norm_w, i == DEPTH - 1)
        outs.append(ctx_out)
    y_prompt = x[:N_CTX].reshape(BATCH, SEQ, D_MODEL)
    y_sample = x[N_CTX:].reshape(DEC_BATCH, DEC_SEQ, D_MODEL)
    stack = lambda j: jnp.stack([o[j] for o in outs], axis=1)
    return (y_prompt, y_sample, stack(0), stack(1), stack(2), stack(3))


def _pad_w_in(w_in):
    da, rw, ret, cv, gate = jnp.split(w_in, [1536, 3456, 4992, 6016], axis=-1)
    pad = jnp.zeros(w_in.shape[:-1] + (P_DA - RW_COLS,), w_in.dtype)
    return jnp.concatenate([rw, pad, da, ret, cv, gate], axis=-1)
```

```python
import functools
import math

import jax
import jax.numpy as jnp
from jax import lax
from jax.experimental import pallas as pl
from jax.experimental.pallas import tpu as pltpu

F32 = jnp.float32
BF16 = jnp.bfloat16

D_MODEL = 1024
BATCH = 16
SEQ = 256
DEPTH = 2
DEC_BATCH = 2
DEC_SEQ = 4096
PAST_LEN = 512
GRID_W = 64
EPS = 1e-6

DA_HEADS = 4
DA_QK = 64
DA_V = 128
ROPE_BASE = 10000.0

RW_HEADS = 8
RW_HD = 64
RW_W = 512
RW_LORA = 64
RW_G_LORA = 128
RW_GN_EPS = 64e-5

RET_HEADS = 4
RET_DK = 64
RET_DV = 128
RET_CHUNK = 128

CV_W = 512
CONV_K = 31
BR_W = 512
N_BRANCH = 4

N_EXPERTS = 32
TOP_K = 4
D_FF = 1024
SWIGLU_LIMIT = 7.0
SWIGLU_ALPHA = 1.702

N_CTX = BATCH * SEQ
N_LAT = DEC_BATCH * DEC_SEQ
N_TOK = N_CTX + N_LAT
N_SEQS = BATCH + DEC_BATCH
MOD_ROWS = 8
MOD_GROUP = 4096

RW_COLS = 1920
P_RW = 0
P_DA = 2048
P_RET = 3584
P_CV = 5120
P_GATE = 6144
P_COLS = 10240

VMEM_LIMIT = 48 * 1024 * 1024


def _cparams(sem):
    return pltpu.CompilerParams(dimension_semantics=sem, vmem_limit_bytes=VMEM_LIMIT)


def _dg(a, b, dims):
    return lax.dot_general(a, b, (dims, ((), ())), preferred_element_type=F32)


NN = ((1,), (0,))
NT = ((1,), (1,))
TN = ((0,), (0,))


def _dot(a, b, dims=NN):
    return _dg(a.astype(BF16), b.astype(BF16), dims)


def _split(x):
    hi = x.astype(BF16)
    lo = (x - hi.astype(F32)).astype(BF16)
    return hi, lo


def _dot3(a, b, dims=NN):
    ah, al = _split(a)
    bh, bl = _split(b)
    return _dg(ah, bh, dims) + (_dg(ah, bl, dims) + _dg(al, bh, dims))


def _dot2x(a, e, dims=NN):
    ah, al = _split(a)
    am = (a - ah.astype(F32) - al.astype(F32)).astype(BF16)
    eb = e.astype(BF16)
    return _dg(ah, eb, dims) + (_dg(al, eb, dims) + _dg(am, eb, dims))


def _sigmoid(x):
    return 1.0 / (1.0 + jnp.exp(-x))


def _softplus(x):
    return jnp.maximum(x, 0.0) + jnp.log(1.0 + jnp.exp(-jnp.abs(x)))


def _seq_pos(row):
    in_ctx = row < N_CTX
    pos = jnp.where(in_ctx, row & (SEQ - 1), (row - N_CTX) & (DEC_SEQ - 1))
    length = jnp.where(in_ctx, SEQ, DEC_SEQ)
    return pos, length


def _seq_index(row):
    return jnp.where(row < N_CTX, row // SEQ, BATCH + (row - N_CTX) // DEC_SEQ)


def _dotx(e, b, dims=NN):
    bh, bl = _split(b)
    bm = (b - bh.astype(F32) - bl.astype(F32)).astype(BF16)
    eb = e.astype(BF16)
    return _dg(eb, bh, dims) + (_dg(eb, bl, dims) + _dg(eb, bm, dims))


def _mod_kernel(c_ref, w_ref, b_ref, o_ref):
    c = c_ref[...]
    s = c * _sigmoid(c)
    o_ref[0] = _dot3(s, w_ref[0]) + b_ref[0]


def _modulation(cvec, ada_w, ada_b):
    tn = 1536
    return pl.pallas_call(
        _mod_kernel,
        grid=(DEPTH, 6 * D_MODEL // tn),
        in_specs=[
            pl.BlockSpec((MOD_ROWS, D_MODEL), lambda l, j: (0, 0)),
            pl.BlockSpec((1, D_MODEL, tn), lambda l, j: (l, 0, j)),
            pl.BlockSpec((1, 1, tn), lambda l, j: (l, 0, j)),
        ],
        out_specs=pl.BlockSpec((1, MOD_ROWS, tn), lambda l, j: (l, 0, j)),
        out_shape=jax.ShapeDtypeStruct((DEPTH, MOD_ROWS, 6 * D_MODEL), F32),
        compiler_params=_cparams(("parallel", "parallel")),
        name="modulation",
    )(cvec, ada_w, ada_b.reshape(DEPTH, 1, 6 * D_MODEL))


def _mod_row(mod_ref, first_row):
    g = first_row // MOD_GROUP
    return mod_ref[pl.ds(g, 1), :]


def _rms(x, w):
    return x * lax.rsqrt(jnp.mean(x * x, axis=-1, keepdims=True) + EPS) * w


IN_TM = 1024
IN_TN = 1024


IN_NA = P_GATE // IN_TN


def _inproj_kernel(x_ref, mod_ref, nw_ref, w_ref, oa_ref, og_ref, h_ref):
    i = pl.program_id(0)
    j = pl.program_id(1)

    @pl.when(j == 0)
    def _():
        m = _mod_row(mod_ref, i * IN_TM)
        sh = m[:, 0:D_MODEL]
        sc = m[:, D_MODEL:2 * D_MODEL]
        h_ref[...] = (_rms(x_ref[...], nw_ref[...]) * (1.0 + sc) + sh).astype(BF16)

    acc = _dg(h_ref[...], w_ref[0].astype(BF16), NN)

    @pl.when(j < IN_NA)
    def _():
        oa_ref[...] = acc

    @pl.when(j >= IN_NA)
    def _():
        og_ref[...] = acc.astype(BF16)


def _input_projection(x, mod, norm_w, w_p, layer):
    n = x.shape[0]
    return pl.pallas_call(
        _inproj_kernel,
        grid=(n // IN_TM, P_COLS // IN_TN),
        in_specs=[
            pl.BlockSpec((IN_TM, D_MODEL), lambda i, j: (i, 0)),
            pl.BlockSpec((MOD_ROWS, 6 * D_MODEL), lambda i, j: (0, 0)),
            pl.BlockSpec((1, D_MODEL), lambda i, j: (0, 0)),
            pl.BlockSpec((1, D_MODEL, IN_TN), lambda i, j: (layer, 0, j)),
        ],
        out_specs=[pl.BlockSpec((IN_TM, IN_TN), lambda i, j: (i, jnp.minimum(j, IN_NA - 1))),
                   pl.BlockSpec((IN_TM, IN_TN), lambda i, j: (i, jnp.maximum(j - IN_NA, 0)))],
        out_shape=[jax.ShapeDtypeStruct((n, P_GATE), F32), jax.ShapeDtypeStruct((n, P_COLS - P_GATE), BF16)],
        scratch_shapes=[pltpu.VMEM((IN_TM, D_MODEL), BF16)],
        compiler_params=_cparams(("parallel", "arbitrary")),
        name="input_projection",
    )(x, mod, norm_w.reshape(1, D_MODEL), w_p)


QK_TM = 512
DA_W = DA_HEADS * 2 * DA_QK


def _qkprep_kernel(q_ref, k_ref, v_ref, c_ref, se_ref, so_ref, qo_ref, ko_ref, vo_ref):
    i = pl.program_id(0)
    vo_ref[...] = v_ref[...].astype(BF16)
    scale = DA_QK ** -0.5

    @pl.when(i * QK_TM < N_CTX)
    def _():
        qo_ref[...] = (q_ref[...] * scale).astype(BF16)
        ko_ref[...] = k_ref[...].astype(BF16)

    @pl.when(i * QK_TM >= N_CTX)
    def _():
        c = c_ref[...]
        se = se_ref[...]
        so = so_ref[...]

        def rope(x):
            nxt = pltpu.roll(x, DA_W - 1, axis=1)
            prv = pltpu.roll(x, 1, axis=1)
            return x * c + nxt * se + prv * so

        qo_ref[...] = (rope(q_ref[...]) * scale).astype(BF16)
        ko_ref[...] = rope(k_ref[...]).astype(BF16)


def _rope_tables():
    rows = DEC_SEQ // GRID_W
    row = jnp.repeat(jnp.arange(rows, dtype=F32), GRID_W)
    col = jnp.tile(jnp.arange(GRID_W, dtype=F32), rows)
    n_pairs = DA_QK // 4
    inv = ROPE_BASE ** (-jnp.arange(n_pairs, dtype=F32) / n_pairs)
    ang = jnp.concatenate([row[:, None] * inv, col[:, None] * inv], axis=-1)
    cos = jnp.repeat(jnp.cos(ang), 2, axis=-1)
    sin = jnp.repeat(jnp.sin(ang), 2, axis=-1)
    even = (jnp.arange(DA_QK) % 2 == 0)[None, :]
    s_even = jnp.where(even, -sin, 0.0)
    s_odd = jnp.where(even, 0.0, sin)
    rep = lambda t: jnp.tile(t, (1, DA_W // DA_QK))
    return rep(cos), rep(s_even), rep(s_odd)


def _qk_prepare(proj, tables):
    n = proj.shape[0]
    lat0 = N_CTX // QK_TM
    nlat = DEC_SEQ // QK_TM
    tab = pl.BlockSpec((QK_TM, DA_W), lambda i: (jnp.maximum(i - lat0, 0) % nlat, 0))
    c0 = P_DA // DA_W
    out = jax.ShapeDtypeStruct((n, DA_W), BF16)
    return pl.pallas_call(
        _qkprep_kernel,
        grid=(n // QK_TM,),
        in_specs=[
            pl.BlockSpec((QK_TM, DA_W), lambda i: (i, c0)),
            pl.BlockSpec((QK_TM, DA_W), lambda i: (i, c0 + 1)),
            pl.BlockSpec((QK_TM, DA_W), lambda i: (i, c0 + 2)),
            tab, tab, tab,
        ],
        out_specs=[pl.BlockSpec((QK_TM, DA_W), lambda i: (i, 0))] * 3,
        out_shape=[out, out, out],
        compiler_params=_cparams(("parallel",)),
        name="qk_prepare",
    )(proj, proj, proj, *tables)


DA_TQ = 256


def _da_kernel(q_ref, k_ref, v_ref, dl_ref, nw_ref, o_ref, *, lam_init):
    dl = dl_ref[...]
    lam = (jnp.exp(jnp.sum(dl[0:1] * dl[1:2], axis=1, keepdims=True))
           - jnp.exp(jnp.sum(dl[2:3] * dl[3:4], axis=1, keepdims=True)) + lam_init)
    nw = nw_ref[...] * (1.0 - lam_init)
    q = q_ref[0]
    for h in range(DA_HEADS):
        vext = v_ref[0, :, h * 2 * DA_V:(h + 1) * 2 * DA_V]
        os = []
        for m in range(2):
            c0 = (2 * h + m) * DA_QK
            s = _dg(q[:, c0:c0 + DA_QK], k_ref[0, :, c0:c0 + DA_QK], NT)
            e = jnp.exp(s - jnp.max(s, axis=-1, keepdims=True)).astype(BF16)
            oe = _dg(e, vext, NN)
            os.append(oe[:, :DA_V] * (1.0 / oe[:, DA_V:DA_V + 1]))
        o_ref[0, :, h * DA_V:(h + 1) * DA_V] = _rms(os[0] - lam * os[1], nw).astype(BF16)


def _diff_attention(q, k, v, da_lambda, da_norm_w, lam_init):
    b, tq, _ = q.shape
    tk = k.shape[1]
    return pl.pallas_call(
        functools.partial(_da_kernel, lam_init=lam_init),
        grid=(b, tq // DA_TQ),
        in_specs=[
            pl.BlockSpec((1, DA_TQ, DA_W), lambda i, j: (i, j, 0)),
            pl.BlockSpec((1, tk, DA_W), lambda i, j: (i, 0, 0), pipeline_mode=pl.Buffered(1)),
            pl.BlockSpec((1, tk, 2 * DA_W), lambda i, j: (i, 0, 0), pipeline_mode=pl.Buffered(1)),
            pl.BlockSpec((4, DA_QK), lambda i, j: (0, 0)),
            pl.BlockSpec((1, DA_V), lambda i, j: (0, 0)),
        ],
        out_specs=pl.BlockSpec((1, DA_TQ, DA_W), lambda i, j: (i, j, 0)),
        out_shape=jax.ShapeDtypeStruct((b, tq, DA_W), BF16),
        compiler_params=_cparams(("parallel", "arbitrary")),
        name="diff_attention",
    )(q, k, v, da_lambda, da_norm_w.reshape(1, DA_V))


def _da_branch(proj, cache_k, cache_v, da_lambda, da_norm_w, lam_init, tables):
    qb, kb, vb = _qk_prepare(proj, tables)
    ctx = lambda t: t[:N_CTX].reshape(BATCH, SEQ, DA_W)
    lat = lambda t: t[N_CTX:].reshape(DEC_BATCH, DEC_SEQ, DA_W)

    def with_ones(v):
        b, t, _ = v.shape
        v4 = v.reshape(b, t, DA_HEADS, DA_V)
        return jnp.concatenate([v4, jnp.ones_like(v4)], axis=-1).reshape(b, t, 2 * DA_W)

    o_ctx = _diff_attention(ctx(qb), ctx(kb), with_ones(ctx(vb)), da_lambda, da_norm_w, lam_init)
    k_all = jnp.concatenate([cache_k.reshape(DEC_BATCH, PAST_LEN, DA_W).astype(BF16), lat(kb)], axis=1)
    v_all = jnp.concatenate([cache_v.reshape(DEC_BATCH, PAST_LEN, DA_W).astype(BF16), lat(vb)], axis=1)
    o_lat = _diff_attention(lat(qb), k_all, with_ones(v_all), da_lambda, da_norm_w, lam_init)
    return jnp.concatenate([o_ctx.reshape(N_CTX, DA_W), o_lat.reshape(N_LAT, DA_W)], axis=0)


RW_TM = 256
RW_C = 64
RW_PAIRS = RW_HEADS // 2
HALO = 8


def _head_ones():
    idx = jnp.arange(RW_W) // RW_HD
    return (idx[:, None] == idx[None, :]).astype(BF16)


def _rwprep_kernel(x_ref, xp_ref, xn_ref, mu_ref, w0_ref, wup_ref, a0_ref, aup_ref, gup_ref, kk_ref, ka_ref,
                   ones_ref, r_o, v_o, kk_o, g_o, kd_o, lw_o, a_o, buf):
    row0 = pl.program_id(0) * RW_TM
    x = x_ref[:, 0:RW_COLS]
    buf[HALO:HALO + RW_TM, :] = x
    buf[HALO - 1:HALO, :] = xp_ref[HALO - 1:HALO, 0:RW_COLS]
    buf[HALO + RW_TM:HALO + RW_TM + 1, :] = xn_ref[0:1, 0:RW_COLS]
    rows = row0 + lax.broadcasted_iota(jnp.int32, (RW_TM, 1), 0)
    pos, length = _seq_pos(rows)
    prev = jnp.where(pos == 0, 0.0, buf[HALO - 1:HALO - 1 + RW_TM, :])
    nxt = jnp.where(pos == length - 1, 0.0, buf[HALO + 1:HALO + 1 + RW_TM, :])
    mu = mu_ref[...]
    u = x + mu[0:1] * (prev - x) + mu[1:2] * (nxt - x)

    r = u[:, 0:RW_W]
    k = u[:, RW_W:2 * RW_W]
    v = u[:, 2 * RW_W:3 * RW_W]
    wl = u[:, 3 * RW_W:3 * RW_W + 128]
    al = u[:, 3 * RW_W + 128:3 * RW_W + 256]
    gl = u[:, 3 * RW_W + 256:3 * RW_W + 384]
    w_raw = w0_ref[...] + _dot3(jnp.tanh(wl), wup_ref[...])
    lw = -jnp.exp(-_softplus(-w_raw) - 0.5)
    a = _sigmoid(a0_ref[...] + _dot3(al, aup_ref[...]))
    g = _dot3(_sigmoid(gl), gup_ref[...])
    kk = k * kk_ref[...]
    kk = kk * lax.rsqrt(jnp.maximum(_dot2x(kk * kk, ones_ref[...]), 1e-12))
    kd = jnp.concatenate([k, k], axis=1) * (1.0 + (a - 1.0) * ka_ref[...])

    r_o[...] = r
    v_o[...] = v
    kk_o[...] = kk
    g_o[...] = g
    for d in range(2):
        kd_o[d] = kd[:, d * RW_W:(d + 1) * RW_W]
        lw_o[d] = lw[:, d * RW_W:(d + 1) * RW_W]
        a_o[d] = a[:, d * RW_W:(d + 1) * RW_W]


def _rw_prepare(proj, p):
    n = proj.shape[0]
    nh = n // HALO
    steps = RW_TM // HALO
    wide = P_DA - P_RW
    cat2 = lambda t: t.reshape(1, 2 * RW_W)
    blockdiag = lambda t: jnp.concatenate(
        [jnp.concatenate([t[0], jnp.zeros_like(t[0])], axis=1),
         jnp.concatenate([jnp.zeros_like(t[1]), t[1]], axis=1)], axis=0)
    const = lambda shape: pl.BlockSpec(shape, lambda i: (0,) * len(shape))
    row = pl.BlockSpec((RW_TM, RW_W), lambda i: (i, 0))
    row2 = pl.BlockSpec((2, RW_TM, RW_W), lambda i: (0, i, 0))
    o1 = jax.ShapeDtypeStruct((n, RW_W), F32)
    o2 = jax.ShapeDtypeStruct((2, n, RW_W), F32)
    return pl.pallas_call(
        _rwprep_kernel,
        grid=(n // RW_TM,),
        in_specs=[
            pl.BlockSpec((RW_TM, wide), lambda i: (i, 0)),
            pl.BlockSpec((HALO, wide), lambda i: (jnp.maximum(i * steps - 1, 0), 0)),
            pl.BlockSpec((HALO, wide), lambda i: (jnp.minimum((i + 1) * steps, nh - 1), 0)),
            const((2, RW_COLS)), const((1, 2 * RW_W)), const((128, 2 * RW_W)), const((1, 2 * RW_W)),
            const((128, 2 * RW_W)), const((RW_G_LORA, RW_W)), const((1, RW_W)), const((1, 2 * RW_W)),
            const((RW_W, RW_W)),
        ],
        out_specs=[row, row, row, row, row2, row2, row2],
        out_shape=[o1, o1, o1, o1, o2, o2, o2],
        scratch_shapes=[pltpu.VMEM((RW_TM + 2 * HALO, RW_COLS), F32)],
        compiler_params=_cparams(("parallel",)),
        name="rwkv_prepare",
    )(proj, proj, proj, p['rw_shift'], cat2(p['rw_w0']), blockdiag(p['rw_w_up']), cat2(p['rw_a0']),
      blockdiag(p['rw_a_up']), p['rw_g_up'], p['rw_k_k'].reshape(1, RW_W),
      jnp.tile(p['rw_k_a'].reshape(1, RW_W), (1, 2)), _head_ones())


RW_TB = 256
RW_NC = RW_TB // RW_C
RW_SIDE = 8


def _rwscan_kernel(r_ref, v_ref, kk_ref, kd_ref, lw_ref, a_ref, tri_ref, inc_ref, str_ref, s0_ref, y_ref, st_ref,
                   s_scr, *, backward):
    step = pl.program_id(0)
    nb = pl.num_programs(0)
    bi = (nb - 1 - step) if backward else step
    pos, length = _seq_pos(bi * RW_TB)
    first = (pos + RW_TB == length) if backward else (pos == 0)
    last = (pos == 0) if backward else (pos + RW_TB == length)
    in_ctx = bi * RW_TB < N_CTX
    c = RW_C

    @pl.when(jnp.logical_and(first, in_ctx))
    def _():
        s_scr[...] = jnp.zeros_like(s_scr)

    @pl.when(jnp.logical_and(first, jnp.logical_not(in_ctx)))
    def _():
        z = jnp.zeros((c, c), F32)
        for p in range(RW_PAIRS):
            s_scr[p] = jnp.concatenate(
                [jnp.concatenate([s0_ref[0, 0, 2 * p], z], axis=1),
                 jnp.concatenate([z, s0_ref[0, 0, 2 * p + 1]], axis=1)], axis=0)

    incl2 = inc_ref[...] > 0.5
    strict2 = str_ref[...] > 0.5
    eye2 = (lax.broadcasted_iota(jnp.int32, (2 * c, 2 * c), 0)
            == lax.broadcasted_iota(jnp.int32, (2 * c, 2 * c), 1))
    m_e = lax.broadcasted_iota(jnp.int32, (1, 2 * c), 1) < c

    lw = lw_ref[0]
    cum = _dotx(tri_ref[...], lw)
    tots = [cum[(j * c if backward else j * c + c - 1):(j * c + 1 if backward else j * c + c), :]
            for j in range(RW_NC)]
    tot_b = jnp.concatenate([jnp.broadcast_to(t, (c, RW_W)) for t in tots], axis=0)
    kk = kk_ref[...]
    kd = kd_ref[0]
    bp = kk * a_ref[0]
    g_inv = jnp.exp(-cum)
    g_rem = jnp.exp(tot_b - cum)
    ag = -kk * jnp.exp(cum - lw)
    rg = r_ref[...] * jnp.exp(cum)
    bdn = bp * g_inv
    kdn = kd * g_inv
    bc = bp * g_rem
    kc = kd * g_rem
    v = v_ref[...]

    def stack(x, j, p):
        xs = x[j * c:(j + 1) * c, p * 2 * c:(p + 1) * 2 * c]
        return jnp.concatenate([jnp.where(m_e, xs, 0.0), jnp.where(m_e, 0.0, xs)], axis=0).astype(BF16)

    pre = {}
    keys = [(j, p) for j in range(RW_NC) for p in range(RW_PAIRS)]
    for g0 in range(0, len(keys), RW_SIDE):
        grp = keys[g0:g0 + RW_SIDE]
        ops = {k: tuple(stack(t, *k) for t in (ag, rg, bdn, kdn, bc, kc, v)) for k in grp}
        gm = {k: _dg(jnp.concatenate([ops[k][0], ops[k][1]], axis=0),
                     jnp.concatenate([ops[k][2], ops[k][3]], axis=0), NT) for k in grp}
        lbb = {k: jnp.where(strict2, gm[k][:2 * c, :2 * c], 0.0) for k in grp}
        lkb = {k: jnp.where(strict2, gm[k][:2 * c, 2 * c:], 0.0).astype(BF16) for k in grp}
        lrk = {k: jnp.concatenate([jnp.where(incl2, gm[k][2 * c:, :2 * c], 0.0),
                                   jnp.where(incl2, gm[k][2 * c:, 2 * c:], 0.0)], axis=1).astype(BF16)
               for k in grp}
        lv = {k: _dg(lkb[k], ops[k][6], NN) for k in grp}
        x = {k: jnp.where(eye2, 1.0, lbb[k]) for k in grp}
        pw = lbb
        for _ in range(int(math.log2(c)) - 1):
            pwb = {k: pw[k].astype(BF16) for k in grp}
            pw = {k: _dg(pwb[k], pwb[k], NN) for k in grp}
            x = {k: x[k] + _dg(x[k].astype(BF16), pw[k].astype(BF16), NN) for k in grp}
        tw = {k: _dg(x[k].astype(BF16), jnp.concatenate([lv[k].astype(BF16), ops[k][0]], axis=1), NN)
              for k in grp}
        for k in grp:
            pre[k] = (tw[k][:, :2 * c], tw[k][:, 2 * c:].astype(BF16), ops[k][1], lrk[k], ops[k][6],
                      jnp.concatenate([ops[k][4], ops[k][5]], axis=0))

    order = range(RW_NC - 1, -1, -1) if backward else range(RW_NC)
    pairs = range(RW_PAIRS)
    s = [s_scr[p] for p in pairs]
    for j in order:
        sb = [s[p].astype(BF16) for p in pairs]
        u = [_dg(pre[j, p][1], sb[p], NT) + pre[j, p][0] for p in pairs]
        uv = [jnp.concatenate([u[p].astype(BF16), pre[j, p][4]], axis=0) for p in pairs]
        y = [_dg(pre[j, p][2], sb[p], NT) + _dg(pre[j, p][3], uv[p], NN) for p in pairs]
        s = [s[p] * jnp.exp(tots[j][:, p * 2 * c:(p + 1) * 2 * c]) + _dg(uv[p], pre[j, p][5], TN) for p in pairs]
        for p in pairs:
            y_ref[j * c:(j + 1) * c, p * 2 * c:(p + 1) * 2 * c] = y[p][:c] + y[p][c:]
    for p in pairs:
        s_scr[p] = s[p]

    @pl.when(last)
    def _():
        for p in pairs:
            st_ref[0, 2 * p] = s[p][:c, :c]
            st_ref[0, 2 * p + 1] = s[p][c:, c:]


def _scan_masks(backward):
    t = jnp.arange(RW_TB)
    sgn = -1 if backward else 1
    same_chunk = (t[:, None] // RW_C) == (t[None, :] // RW_C)
    tri = (same_chunk & ((t[:, None] - t[None, :]) * sgn >= 0)).astype(BF16)
    q = jnp.arange(2 * RW_C)
    same_head = (q[:, None] // RW_C) == (q[None, :] // RW_C)
    dif = ((q[:, None] % RW_C) - (q[None, :] % RW_C)) * sgn
    return tri, (same_head & (dif >= 0)).astype(F32), (same_head & (dif > 0)).astype(F32)


def _rw_scan(r, v, kk, kd, lw, a, s0, backward):
    n = r.shape[0]
    nb = n // RW_TB
    d = 1 if backward else 0
    blk = (lambda i: nb - 1 - i) if backward else (lambda i: i)
    row = pl.BlockSpec((RW_TB, RW_W), lambda i: (blk(i), 0))
    row2 = pl.BlockSpec((1, RW_TB, RW_W), lambda i: (d, blk(i), 0))
    const = lambda shape: pl.BlockSpec(shape, lambda i: (0,) * len(shape))
    pair = 2 * RW_C
    return pl.pallas_call(
        functools.partial(_rwscan_kernel, backward=backward),
        grid=(nb,),
        in_specs=[row, row, row, row2, row2, row2, const((RW_TB, RW_TB)), const((pair, pair)), const((pair, pair)),
                  pl.BlockSpec((1, 1, RW_HEADS, RW_HD, RW_HD),
                               lambda i: (jnp.maximum(_seq_index(blk(i) * RW_TB) - BATCH, 0), d, 0, 0, 0))],
        out_specs=[row, pl.BlockSpec((1, RW_HEADS, RW_HD, RW_HD), lambda i: (_seq_index(blk(i) * RW_TB), 0, 0, 0))],
        out_shape=[jax.ShapeDtypeStruct((n, RW_W), F32),
                   jax.ShapeDtypeStruct((N_SEQS, RW_HEADS, RW_HD, RW_HD), F32)],
        scratch_shapes=[pltpu.VMEM((RW_PAIRS, pair, pair), F32)],
        compiler_params=_cparams(("arbitrary",)),
        name="rwkv_scan_bwd" if backward else "rwkv_scan_fwd",
    )(r, v, kk, kd, lw, a, *_scan_masks(backward), s0)


def _rwpost_kernel(yf_ref, yb_ref, r_ref, v_ref, g_ref, kd_ref, rk_ref, lnw_ref, lnb_ref, ones_ref, o_ref):
    ones = ones_ref[...]
    inv = 1.0 / RW_HD
    y = yf_ref[...] + yb_ref[...]
    xc = y - _dot2x(y, ones) * inv
    var = _dot2x(xc * xc, ones) * inv
    yn = xc * lax.rsqrt(var + RW_GN_EPS) * lnw_ref[...] + lnb_ref[...]
    bonus = _dot2x(r_ref[...] * (kd_ref[0] + kd_ref[1]) * rk_ref[...], ones) * v_ref[...]
    o_ref[...] = ((yn + bonus) * g_ref[...]).astype(BF16)


def _rw_post(yf, yb, r, v, g, kd, p):
    n = r.shape[0]
    row = pl.BlockSpec((RW_TM, RW_W), lambda i: (i, 0))
    row2 = pl.BlockSpec((2, RW_TM, RW_W), lambda i: (0, i, 0))
    vec = pl.BlockSpec((1, RW_W), lambda i: (0, 0))
    return pl.pallas_call(
        _rwpost_kernel,
        grid=(n // RW_TM,),
        in_specs=[row, row, row, row, row, row2, vec, vec, vec, pl.BlockSpec((RW_W, RW_W), lambda i: (0, 0))],
        out_specs=row,
        out_shape=jax.ShapeDtypeStruct((n, RW_W), BF16),
        compiler_params=_cparams(("parallel",)),
        name="rwkv_post",
    )(yf, yb, r, v, g, kd, p['rw_r_k'].reshape(1, RW_W), p['rw_ln_w'].reshape(1, RW_W),
      p['rw_ln_b'].reshape(1, RW_W), _head_ones())


def _rwkv_branch(proj, state0, p):
    r, v, kk, g, kd, lw, a = _rw_prepare(proj, p)
    yf, sf = _rw_scan(r, v, kk, kd, lw, a, state0, backward=False)
    yb, sb = _rw_scan(r, v, kk, kd, lw, a, state0, backward=True)
    return _rw_post(yf, yb, r, v, g, kd, p), jnp.stack([sf, sb], axis=1)


RET_W = RET_HEADS * RET_DV
RET_PAIRS = RET_HEADS // 2


RET_TB = 256
RET_NC = RET_TB // RET_CHUNK


def _ret_kernel(q_ref, k_ref, v_ref, lg_ref, s0_ref, o_ref, st_ref, s_scr, *, backward):
    step = pl.program_id(0)
    nb = pl.num_programs(0)
    bi = (nb - 1 - step) if backward else step
    pos, length = _seq_pos(bi * RET_TB)
    first = (pos + RET_TB == length) if backward else (pos == 0)
    in_ctx = bi * RET_TB < N_CTX

    @pl.when(jnp.logical_and(first, in_ctx))
    def _():
        s_scr[...] = jnp.zeros_like(s_scr)

    @pl.when(jnp.logical_and(first, jnp.logical_not(in_ctx)))
    def _():
        s_scr[...] = s0_ref[0, 0]

    c = RET_CHUNK
    d = 1 if backward else 0
    sgn = -1 if backward else 1
    lgs = -_softplus(-lg_ref[d:d + 1, :])
    ri = lax.broadcasted_iota(jnp.int32, (c, c), 0)
    cj = lax.broadcasted_iota(jnp.int32, (c, c), 1)
    dif = (ri - cj) * sgn
    valid = dif >= 0
    dist = jnp.maximum(dif, 0).astype(F32)
    pr = lax.broadcasted_iota(jnp.int32, (c, 2 * RET_DK), 0)
    tau = ((c - 1 - pr) if backward else pr).astype(F32)
    low = lax.broadcasted_iota(jnp.int32, (1, 2 * RET_DK), 1) < RET_DK
    rlow = lax.broadcasted_iota(jnp.int32, (2 * RET_DK, RET_DV), 0) < RET_DK
    heads = range(RET_HEADS)
    pairs = range(RET_PAIRS)
    mask = [low, jnp.logical_not(low)]
    lg_h = [lgs[:, h:h + 1] for h in heads]
    lg_row = [jnp.where(low, lg_h[2 * p], lg_h[2 * p + 1]) for p in pairs]
    dmat = [jnp.where(valid, jnp.exp(lg_h[h] * dist), 0.0) for h in heads]
    q_dec = [jnp.exp(lg_row[p] * (tau + 1.0)) for p in pairs]
    k_dec = [jnp.exp(lg_row[p] * (c - 1.0 - tau)) for p in pairs]
    c_dec = [jnp.where(rlow, jnp.exp(lg_h[2 * p] * c), jnp.exp(lg_h[2 * p + 1] * c)) for p in pairs]

    pre = []
    for j in range(RET_NC):
        rows = slice(j * c, (j + 1) * c)
        qp = [q_ref[rows, p * 128:(p + 1) * 128] for p in pairs]
        kp = [k_ref[rows, p * 128:(p + 1) * 128] * (RET_DK ** -0.5) for p in pairs]
        kpb = [t.astype(BF16) for t in kp]
        vb = [v_ref[rows, h * RET_DV:(h + 1) * RET_DV].astype(BF16) for h in heads]
        att = [_dg(jnp.where(mask[h % 2], qp[h // 2], 0.0).astype(BF16), kpb[h // 2], NT) * dmat[h]
               for h in heads]
        upd = [_dg(jnp.where(mask[h % 2], kp[h // 2] * k_dec[h // 2], 0.0).astype(BF16), vb[h], TN)
               for h in heads]
        inner = [_dg(att[h].astype(BF16), vb[h], NN) for h in heads]
        qd = [jnp.where(mask[h % 2], qp[h // 2] * q_dec[h // 2], 0.0).astype(BF16) for h in heads]
        pre.append((inner, qd, upd))

    s = [s_scr[p] for p in pairs]
    for j in (range(RET_NC - 1, -1, -1) if backward else range(RET_NC)):
        inner, qd, upd = pre[j]
        sb = [t.astype(BF16) for t in s]
        for h in heads:
            o_ref[j * c:(j + 1) * c, h * RET_DV:(h + 1) * RET_DV] = inner[h] + _dg(qd[h], sb[h // 2], NN)
        s = [s[p] * c_dec[p] + upd[2 * p] + upd[2 * p + 1] for p in pairs]
    for p in pairs:
        s_scr[p] = s[p]
        st_ref[0, p] = s[p]


def _ret_scan(proj, logit, s0, backward):
    n = proj.shape[0]
    nb = n // RET_TB
    d = 1 if backward else 0
    blk = (lambda i: nb - 1 - i) if backward else (lambda i: i)
    qk_w = RET_HEADS * RET_DK
    state = (RET_PAIRS, 2 * RET_DK, RET_DV)
    return pl.pallas_call(
        functools.partial(_ret_kernel, backward=backward),
        grid=(nb,),
        in_specs=[
            pl.BlockSpec((RET_TB, qk_w), lambda i: (blk(i), P_RET // qk_w)),
            pl.BlockSpec((RET_TB, qk_w), lambda i: (blk(i), P_RET // qk_w + 1)),
            pl.BlockSpec((RET_TB, RET_W), lambda i: (blk(i), (P_RET + 2 * qk_w) // RET_W)),
            pl.BlockSpec((2, RET_HEADS), lambda i: (0, 0)),
            pl.BlockSpec((1, 1) + state,
                         lambda i: (jnp.maximum(_seq_index(blk(i) * RET_TB) - BATCH, 0), d, 0, 0, 0)),
        ],
        out_specs=[pl.BlockSpec((RET_TB, RET_W), lambda i: (blk(i), 0)),
                   pl.BlockSpec((1,) + state, lambda i: (_seq_index(blk(i) * RET_TB), 0, 0, 0))],
        out_shape=[jax.ShapeDtypeStruct((n, RET_W), F32), jax.ShapeDtypeStruct((N_SEQS,) + state, F32)],
        scratch_shapes=[pltpu.VMEM(state, F32)],
        compiler_params=_cparams(("arbitrary",)),
        name="retention_scan_bwd" if backward else "retention_scan_fwd",
    )(proj, proj, proj, logit, s0)


def _standardize(x, eps):
    mu = jnp.mean(x, axis=-1, keepdims=True)
    xc = x - mu
    return xc * lax.rsqrt(jnp.mean(xc * xc, axis=-1, keepdims=True) + eps)


def _retpost_kernel(of_ref, ob_ref, g_ref, nw_ref, out_ref):
    o = of_ref[...] + ob_ref[...]
    g = g_ref[...]
    for h in range(RET_HEADS):
        sl = slice(h * RET_DV, (h + 1) * RET_DV)
        gh = g[:, sl]
        out_ref[:, sl] = (gh * _sigmoid(gh) * (_standardize(o[:, sl], EPS) * nw_ref[:, sl])).astype(BF16)


def _ret_post(of, ob, proj, norm_w):
    n = proj.shape[0]
    tm = 512
    return pl.pallas_call(
        _retpost_kernel,
        grid=(n // tm,),
        in_specs=[
            pl.BlockSpec((tm, RET_W), lambda i: (i, 0)),
            pl.BlockSpec((tm, RET_W), lambda i: (i, 0)),
            pl.BlockSpec((tm, RET_W), lambda i: (i, (P_RET + 1024) // RET_W)),
            pl.BlockSpec((1, RET_W), lambda i: (0, 0)),
        ],
        out_specs=pl.BlockSpec((tm, RET_W), lambda i: (i, 0)),
        out_shape=jax.ShapeDtypeStruct((n, RET_W), BF16),
        compiler_params=_cparams(("parallel",)),
        name="retention_post",
    )(of, ob, proj, norm_w.reshape(1, RET_W))


def _retention_branch(proj, state0, p):
    s0 = state0.reshape(DEC_BATCH, 2, RET_PAIRS, 2 * RET_DK, RET_DV)
    of, sf = _ret_scan(proj, p['ret_decay_logit'], s0, backward=False)
    ob, sb = _ret_scan(proj, p['ret_decay_logit'], s0, backward=True)
    st = jnp.stack([sf, sb], axis=1).reshape(N_SEQS, 2, RET_HEADS, RET_DK, RET_DV)
    return _ret_post(of, ob, proj, p['ret_norm_w']), st


CV_TM = 256
CV_HALO = 16


def _conv_kernel(a_ref, g_ref, ap_ref, gp_ref, an_ref, gn_ref, w_ref, b_ref, lnw_ref, lnb_ref, o_ref, buf):
    row0 = pl.program_id(0) * CV_TM
    pos0, len0 = _seq_pos(row0)
    buf[CV_HALO:CV_HALO + CV_TM, :] = a_ref[...] * _sigmoid(g_ref[...])
    buf[0:CV_HALO, :] = jnp.where(pos0 == 0, 0.0, ap_ref[...] * _sigmoid(gp_ref[...]))
    buf[CV_HALO + CV_TM:, :] = jnp.where(pos0 + CV_TM == len0, 0.0, an_ref[...] * _sigmoid(gn_ref[...]))
    base = CV_HALO - CONV_K // 2
    acc = jnp.zeros((CV_TM, CV_W), F32)
    for j in range(CONV_K):
        acc = acc + w_ref[j:j + 1, :] * buf[base + j:base + j + CV_TM, :]
    z = _standardize(acc + b_ref[...], EPS) * lnw_ref[...] + lnb_ref[...]
    o_ref[...] = (z * _sigmoid(z)).astype(BF16)


def _conv_branch(proj, p):
    n = proj.shape[0]
    nh = n // CV_HALO
    steps = CV_TM // CV_HALO
    ca = P_CV // CV_W
    prev = lambda i: jnp.maximum(i * steps - 1, 0)
    nxt = lambda i: jnp.minimum((i + 1) * steps, nh - 1)
    vec = pl.BlockSpec((1, CV_W), lambda i: (0, 0))
    return pl.pallas_call(
        _conv_kernel,
        grid=(n // CV_TM,),
        in_specs=[
            pl.BlockSpec((CV_TM, CV_W), lambda i: (i, ca)),
            pl.BlockSpec((CV_TM, CV_W), lambda i: (i, ca + 1)),
            pl.BlockSpec((CV_HALO, CV_W), lambda i: (prev(i), ca)),
            pl.BlockSpec((CV_HALO, CV_W), lambda i: (prev(i), ca + 1)),
            pl.BlockSpec((CV_HALO, CV_W), lambda i: (nxt(i), ca)),
            pl.BlockSpec((CV_HALO, CV_W), lambda i: (nxt(i), ca + 1)),
            pl.BlockSpec((CONV_K, CV_W), lambda i: (0, 0)),
            vec, vec, vec,
        ],
        out_specs=pl.BlockSpec((CV_TM, CV_W), lambda i: (i, 0)),
        out_shape=jax.ShapeDtypeStruct((n, CV_W), BF16),
        scratch_shapes=[pltpu.VMEM((CV_TM + 2 * CV_HALO, CV_W), F32)],
        compiler_params=_cparams(("parallel",)),
        name="conformer_conv",
    )(proj, proj, proj, proj, proj, proj, p['cv_dw_w'], p['cv_dw_b'].reshape(1, CV_W),
      p['cv_ln_w'].reshape(1, CV_W), p['cv_ln_b'].reshape(1, CV_W))


MG_TM = 512


def _merge_kernel(da_ref, rw_ref, ret_ref, cv_ref, g0_ref, g1_ref, g2_ref, g3_ref, x_ref, mod_ref, wb_ref, wo_ref,
                  nw_ref, x_o, h_o):
    m = None
    for n, (br, gt) in enumerate(((da_ref, g0_ref), (rw_ref, g1_ref), (ret_ref, g2_ref), (cv_ref, g3_ref))):
        t = _sigmoid(gt[...].astype(F32)) * _dg(br[...], wb_ref[n], NN)
        m = t if m is None else m + t
    out = _dg(m.astype(BF16), wo_ref[...], NN)
    mrow = _mod_row(mod_ref, pl.program_id(0) * MG_TM)
    gate1 = mrow[:, 2 * D_MODEL:3 * D_MODEL]
    sh2 = mrow[:, 3 * D_MODEL:4 * D_MODEL]
    sc2 = mrow[:, 4 * D_MODEL:5 * D_MODEL]
    x1 = x_ref[...] + gate1 * out
    x_o[...] = x1
    h_o[...] = _rms(x1, nw_ref[...]) * (1.0 + sc2) + sh2


def _merge(branches, gates, x, mod, w_branch, w_out, norm_w):
    n = x.shape[0]
    br = pl.BlockSpec((MG_TM, BR_W), lambda i: (i, 0))
    gspec = lambda j: pl.BlockSpec((MG_TM, D_MODEL), lambda i: (i, j))
    full = pl.BlockSpec((MG_TM, D_MODEL), lambda i: (i, 0))
    out = jax.ShapeDtypeStruct((n, D_MODEL), F32)
    return pl.pallas_call(
        _merge_kernel,
        grid=(n // MG_TM,),
        in_specs=[br, br, br, br, gspec(0), gspec(1), gspec(2), gspec(3), full,
                  pl.BlockSpec((MOD_ROWS, 6 * D_MODEL), lambda i: (0, 0)),
                  pl.BlockSpec((N_BRANCH, BR_W, D_MODEL), lambda i: (0, 0, 0)),
                  pl.BlockSpec((D_MODEL, D_MODEL), lambda i: (0, 0)),
                  pl.BlockSpec((1, D_MODEL), lambda i: (0, 0))],
        out_specs=[full, full],
        out_shape=[out, out],
        compiler_params=_cparams(("parallel",)),
        name="gated_merge",
    )(*branches, gates, gates, gates, gates, x, mod, w_branch.astype(BF16), w_out.astype(BF16),
      norm_w.reshape(1, D_MODEL))


RT_TM = 256
MOE_BM = 512
MOE_ROWS = N_TOK * TOP_K + N_EXPERTS * MOE_BM
DP_TM = 512
DP_GROUP = 64
CB_TM = 256


def _router_kernel(h_ref, w_ref, b_ref, tri_ref, idx_o, gate_o, rank_o, cnt_o, carry):
    @pl.when(pl.program_id(0) == 0)
    def _():
        carry[...] = jnp.zeros_like(carry)

    logits = _dot3(w_ref[...], h_ref[...], NT) + b_ref[...]
    e_iota = lax.broadcasted_iota(jnp.int32, logits.shape, 0)
    work = logits
    vals, idxs, hots = [], [], []
    for _ in range(TOP_K):
        mx = jnp.max(work, axis=0, keepdims=True)
        ix = jnp.min(jnp.where(work == mx, e_iota, N_EXPERTS), axis=0, keepdims=True)
        hot = e_iota == ix
        vals.append(mx)
        idxs.append(ix)
        hots.append(hot.astype(F32))
        work = jnp.where(hot, -jnp.inf, work)
    es = [jnp.exp(v - vals[0]) for v in vals]
    inv = 1.0 / (es[0] + es[1] + es[2] + es[3])
    chosen = hots[0] + hots[1] + hots[2] + hots[3]
    ahead = carry[...][:, 0:1] + _dg(chosen.astype(BF16), tri_ref[...], NN)
    idx_o[...] = jnp.concatenate(idxs, axis=0)
    gate_o[...] = jnp.concatenate([e * inv for e in es], axis=0)
    rank_o[...] = jnp.concatenate(
        [jnp.sum(hot * ahead, axis=0, keepdims=True) for hot in hots], axis=0).astype(jnp.int32)
    carry[...] = carry[...] + jnp.sum(chosen, axis=1, keepdims=True)
    cnt_o[...] = carry[...]


def _router(h, router_w, router_b):
    n = h.shape[0]
    tri = (jnp.arange(RT_TM)[:, None] < jnp.arange(RT_TM)[None, :]).astype(BF16)
    col = pl.BlockSpec((TOP_K, RT_TM), lambda i: (0, i))
    return pl.pallas_call(
        _router_kernel,
        grid=(n // RT_TM,),
        in_specs=[
            pl.BlockSpec((RT_TM, D_MODEL), lambda i: (i, 0)),
            pl.BlockSpec((N_EXPERTS, D_MODEL), lambda i: (0, 0)),
            pl.BlockSpec((N_EXPERTS, 1), lambda i: (0, 0)),
            pl.BlockSpec((RT_TM, RT_TM), lambda i: (0, 0)),
        ],
        out_specs=[col, col, col, pl.BlockSpec((N_EXPERTS, 128), lambda i: (0, 0))],
        out_shape=[jax.ShapeDtypeStruct((TOP_K, n), jnp.int32), jax.ShapeDtypeStruct((TOP_K, n), F32),
                   jax.ShapeDtypeStruct((TOP_K, n), jnp.int32), jax.ShapeDtypeStruct((N_EXPERTS, 128), F32)],
        scratch_shapes=[pltpu.VMEM((N_EXPERTS, 128), F32)],
        compiler_params=_cparams(("arbitrary",)),
        name="router",
    )(h, router_w.T, router_b.reshape(N_EXPERTS, 1), tri)


def _tile_major(t, tm):
    k, n = t.shape
    return t.reshape(k, n // tm, tm).transpose(1, 0, 2).reshape(n // tm, 1, k * tm)


def _dispatch_kernel(dest_ref, pe_ref, h_ref, o_hbm, zbuf, sem, zsem):
    n_groups = DP_TM // DP_GROUP

    @pl.when(pl.program_id(0) == 0)
    def _():
        zbuf[...] = jnp.zeros_like(zbuf)

        def fill(e):
            end = pe_ref[e]
            begin = pe_ref[e - 1] if e else 0
            return end > begin, pltpu.make_async_copy(
                zbuf, o_hbm.at[pl.ds(pl.multiple_of(jnp.maximum(end - MOE_BM, 0), MOE_BM), MOE_BM)], zsem.at[0])

        for e in range(N_EXPERTS):
            nonempty, cp = fill(e)
            pl.when(nonempty)(cp.start)
        for e in range(N_EXPERTS):
            nonempty, cp = fill(e)
            pl.when(nonempty)(cp.wait)

        def tail(b):
            return pltpu.make_async_copy(zbuf, o_hbm.at[pl.ds(pl.multiple_of(b * MOE_BM, MOE_BM), MOE_BM)],
                                         zsem.at[0])

        first_unused = pe_ref[N_EXPERTS - 1] // MOE_BM
        lax.fori_loop(first_unused, MOE_ROWS // MOE_BM, lambda b, c: (tail(b).start(), c)[1], 0)
        lax.fori_loop(first_unused, MOE_ROWS // MOE_BM, lambda b, c: (tail(b).wait(), c)[1], 0)

    def wait_group(slot):
        pltpu.make_async_copy(h_ref.at[pl.ds(0, TOP_K * DP_GROUP)], o_hbm.at[pl.ds(0, TOP_K * DP_GROUP)],
                              sem.at[slot]).wait()

    def group(gi, carry):
        slot = gi % 2

        def issue(t, c):
            tok = gi * DP_GROUP + t
            for k in range(TOP_K):
                dst = dest_ref[0, 0, k * DP_TM + tok]
                pltpu.make_async_copy(h_ref.at[pl.ds(tok, 1)], o_hbm.at[pl.ds(dst, 1)], sem.at[slot]).start()
            return c

        lax.fori_loop(0, DP_GROUP, issue, 0)

        @pl.when(gi > 0)
        def _():
            wait_group(1 - slot)

        return carry

    lax.fori_loop(0, n_groups, group, 0)
    wait_group((n_groups - 1) % 2)


def _dispatch(h, dest, pad_end):
    n = h.shape[0]
    return pl.pallas_call(
        _dispatch_kernel,
        grid=(n // DP_TM,),
        in_specs=[
            pl.BlockSpec((1, 1, TOP_K * DP_TM), lambda i: (i, 0, 0), memory_space=pltpu.SMEM),
            pl.BlockSpec(memory_space=pltpu.SMEM),
            pl.BlockSpec((DP_TM, D_MODEL), lambda i: (i, 0)),
        ],
        out_specs=pl.BlockSpec(memory_space=pl.ANY),
        out_shape=jax.ShapeDtypeStruct((MOE_ROWS, D_MODEL), F32),
        scratch_shapes=[pltpu.VMEM((MOE_BM, D_MODEL), F32), pltpu.SemaphoreType.DMA((2,)),
                        pltpu.SemaphoreType.DMA((1,))],
        compiler_params=_cparams(("arbitrary",)),
        name="moe_dispatch",
    )(_tile_major(dest, DP_TM), pad_end, h)


def _expert_kernel(bx_ref, be_ref, nv_ref, x_ref, w1_ref, b1_ref, w2_ref, b2_ref, o_ref, w1b, w2b):
    i = pl.program_id(0)
    changed = jnp.logical_or(i == 0, be_ref[i] != be_ref[jnp.maximum(i - 1, 0)])

    @pl.when(changed)
    def _():
        w1b[...] = w1_ref[0, 0].astype(BF16)
        w2b[...] = w2_ref[0, 0].astype(BF16)

    def ffn(rows):
        hb = _dg(x_ref[0:rows, :].astype(BF16), w1b[...], NN) + b1_ref[0, 0]
        hg = jnp.minimum(hb[:, :D_FF], SWIGLU_LIMIT)
        hu = jnp.clip(hb[:, D_FF:], -SWIGLU_LIMIT, SWIGLU_LIMIT)
        act = hg * _sigmoid(SWIGLU_ALPHA * hg) * (hu + 1.0)
        o_ref[0:rows, :] = _dg(act.astype(BF16), w2b[...], NN) + b2_ref[0, 0]

    nv = nv_ref[i]
    half = MOE_BM // 2

    @pl.when(nv > half)
    def _():
        ffn(MOE_BM)

    @pl.when(jnp.logical_and(nv > 0, nv <= half))
    def _():
        ffn(half)
        o_ref[half:, :] = jnp.zeros((MOE_BM - half, D_MODEL), F32)

    @pl.when(nv == 0)
    def _():
        o_ref[...] = jnp.zeros_like(o_ref)


def _experts(x_rows, blk_x, blk_e, n_valid, layer, w1, b1, w2, b2):
    nb = MOE_ROWS // MOE_BM
    grid_spec = pltpu.PrefetchScalarGridSpec(
        num_scalar_prefetch=3,
        grid=(nb,),
        in_specs=[
            pl.BlockSpec((MOE_BM, D_MODEL), lambda i, bx, be, nv: (bx[i], 0)),
            pl.BlockSpec((1, 1, D_MODEL, 2 * D_FF), lambda i, bx, be, nv: (layer, be[i], 0, 0)),
            pl.BlockSpec((1, 1, 1, 2 * D_FF), lambda i, bx, be, nv: (layer, be[i], 0, 0)),
            pl.BlockSpec((1, 1, D_FF, D_MODEL), lambda i, bx, be, nv: (layer, be[i], 0, 0)),
            pl.BlockSpec((1, 1, 1, D_MODEL), lambda i, bx, be, nv: (layer, be[i], 0, 0)),
        ],
        out_specs=pl.BlockSpec((MOE_BM, D_MODEL), lambda i, bx, be, nv: (i, 0)),
        scratch_shapes=[pltpu.VMEM((D_MODEL, 2 * D_FF), BF16), pltpu.VMEM((D_FF, D_MODEL), BF16)],
    )
    return pl.pallas_call(
        _expert_kernel,
        grid_spec=grid_spec,
        out_shape=jax.ShapeDtypeStruct((MOE_ROWS, D_MODEL), F32),
        compiler_params=pltpu.CompilerParams(dimension_semantics=("arbitrary",),
                                             vmem_limit_bytes=56 * 1024 * 1024),
        name="moe_experts",
    )(blk_x, blk_e, n_valid, x_rows, w1, b1.reshape(DEPTH, N_EXPERTS, 1, 2 * D_FF), w2,
      b2.reshape(DEPTH, N_EXPERTS, 1, D_MODEL))


def _combine_kernel(dest_ref, gate_ref, x_ref, mod_ref, fw_ref, y_hbm, o_ref, buf, sem, *, final):
    def issue(t, c):
        for k in range(TOP_K):
            dst = dest_ref[0, 0, k * CB_TM + t]
            pltpu.make_async_copy(y_hbm.at[pl.ds(dst, 1)], buf.at[k, pl.ds(t, 1)], sem.at[0]).start()
        return c

    lax.fori_loop(0, CB_TM, issue, 0)
    for k in range(TOP_K):
        pltpu.make_async_copy(y_hbm.at[pl.ds(0, CB_TM)], buf.at[k], sem.at[0]).wait()
    g = gate_ref[...]
    acc = g[:, 0:1] * buf[0]
    for k in range(1, TOP_K):
        acc = acc + g[:, k:k + 1] * buf[k]
    gate2 = _mod_row(mod_ref, pl.program_id(0) * CB_TM)[:, 5 * D_MODEL:6 * D_MODEL]
    x2 = x_ref[...] + gate2 * acc
    o_ref[...] = _rms(x2, fw_ref[...]) if final else x2


def _combine(y_rows, dest, gates, x, mod, final_w, final):
    n = x.shape[0]
    full = pl.BlockSpec((CB_TM, D_MODEL), lambda i: (i, 0))
    return pl.pallas_call(
        functools.partial(_combine_kernel, final=final),
        grid=(n // CB_TM,),
        in_specs=[
            pl.BlockSpec((1, 1, TOP_K * CB_TM), lambda i: (i, 0, 0), memory_space=pltpu.SMEM),
            pl.BlockSpec((CB_TM, TOP_K), lambda i: (i, 0)),
            full,
            pl.BlockSpec((MOD_ROWS, 6 * D_MODEL), lambda i: (0, 0)),
            pl.BlockSpec((1, D_MODEL), lambda i: (0, 0)),
            pl.BlockSpec(memory_space=pl.ANY),
        ],
        out_specs=full,
        out_shape=jax.ShapeDtypeStruct((n, D_MODEL), F32),
        scratch_shapes=[pltpu.VMEM((TOP_K, CB_TM, D_MODEL), F32), pltpu.SemaphoreType.DMA((1,))],
        compiler_params=_cparams(("arbitrary",)),
        name="moe_combine",
    )(_tile_major(dest, CB_TM), gates.T, x, mod, final_w.reshape(1, D_MODEL), y_rows)


def _routed_ffn(h, x, mod, p, moe, layer, final_w, final):
    idx, gates, rank, counts = _router(h, p['router_w'], p['router_b'])
    counts = counts[:, 0].astype(jnp.int32)
    padded = (counts + MOE_BM - 1) // MOE_BM * MOE_BM
    pad_end = jnp.cumsum(padded)
    pad_start = pad_end - padded
    experts = jnp.arange(N_EXPERTS, dtype=jnp.int32)
    start_of = jnp.sum(jnp.where(idx[:, :, None] == experts, pad_start, 0), axis=-1)
    dest = start_of + rank
    nb = MOE_ROWS // MOE_BM
    first_row = jnp.arange(nb, dtype=jnp.int32) * MOE_BM
    blk_e = jnp.minimum(jnp.sum((pad_end[None, :] <= first_row[:, None]).astype(jnp.int32), axis=1),
                        N_EXPERTS - 1)
    is_e = blk_e[:, None] == experts[None, :]
    end_of = jnp.sum(jnp.where(is_e, pad_start + counts, 0), axis=1)
    n_valid = jnp.clip(end_of - first_row, 0, MOE_BM).astype(jnp.int32)
    blk_x = jnp.minimum(jnp.arange(nb, dtype=jnp.int32), pad_end[-1] // MOE_BM - 1)
    x_rows = _dispatch(h, dest, pad_end)
    y_rows = _experts(x_rows, blk_x, blk_e, n_valid, layer, *moe)
    return _combine(y_rows, dest, gates, x, mod, final_w, final)


_LAYER_PARAMS = ('norm_mix_w', 'norm_ffn_w', 'da_lambda', 'da_norm_w', 'rw_shift', 'rw_w0', 'rw_w_up', 'rw_a0',
                 'rw_a_up', 'rw_g_up', 'rw_k_k', 'rw_k_a', 'rw_r_k', 'rw_ln_w', 'rw_ln_b', 'ret_decay_logit',
                 'ret_norm_w', 'cv_dw_w', 'cv_dw_b', 'cv_ln_w', 'cv_ln_b', 'w_branch', 'w_out', 'router_w',
                 'router_b')


def _layer(x, mod, w_p, p, moe, layer, lam_init, caches, tables, final_w, final):
    cache_k, cache_v, state_rw, state_ret = caches
    proj, gates = _input_projection(x, mod, p['norm_mix_w'], w_p, layer)
    o_da = _da_branch(proj, cache_k, cache_v, p['da_lambda'], p['da_norm_w'], lam_init, tables)
    o_rw, rw_state = _rwkv_branch(proj, state_rw, p)
    o_ret, ret_state = _retention_branch(proj, state_ret, p)
    o_cv = _conv_branch(proj, p)
    x1, h2 = _merge((o_da, o_rw, o_ret, o_cv), gates, x, mod, p['w_branch'], p['w_out'], p['norm_ffn_w'])
    x2 = _routed_ffn(h2, x1, mod, p, moe, layer, final_w, final)
    new_k = proj[:N_CTX, P_DA + DA_W:P_DA + 2 * DA_W].reshape(BATCH, SEQ, DA_HEADS, 2, DA_QK)
    new_v = proj[:N_CTX, P_DA + 2 * DA_W:P_DA + 3 * DA_W].reshape(BATCH, SEQ, DA_HEADS, DA_V)
    return x2, (new_k, new_v, rw_state[:BATCH], ret_state[:BATCH])


def kernel(x_prompt, x_sample, c, cache_da_k, cache_da_v, state_rwkv, state_ret, c_ctx, ada_w, ada_b, norm_mix_w,
           norm_ffn_w, w_in, da_lambda, da_norm_w, rw_shift, rw_w0, rw_w_up, rw_a0, rw_a_up, rw_g_up, rw_k_k,
           rw_k_a, rw_r_k, rw_ln_w, rw_ln_b, ret_decay_logit, ret_norm_w, cv_dw_w, cv_dw_b, cv_ln_w, cv_ln_b,
           w_branch, w_out, router_w, router_b, moe_w1, moe_b1, moe_w2, moe_b2, final_norm_w):
    weights = dict(norm_mix_w=norm_mix_w, norm_ffn_w=norm_ffn_w, da_lambda=da_lambda, da_norm_w=da_norm_w,
                   rw_shift=rw_shift, rw_w0=rw_w0, rw_w_up=rw_w_up, rw_a0=rw_a0, rw_a_up=rw_a_up, rw_g_up=rw_g_up,
                   rw_k_k=rw_k_k, rw_k_a=rw_k_a, rw_r_k=rw_r_k, rw_ln_w=rw_ln_w, rw_ln_b=rw_ln_b,
                   ret_decay_logit=ret_decay_logit, ret_norm_w=ret_norm_w, cv_dw_w=cv_dw_w, cv_dw_b=cv_dw_b,
                   cv_ln_w=cv_ln_w, cv_ln_b=cv_ln_b, w_branch=w_branch, w_out=w_out, router_w=router_w,
                   router_b=router_b)
    x = jnp.concatenate([x_prompt.reshape(N_CTX, D_MODEL), x_sample.reshape(N_LAT, D_MODEL)], axis=0)
    cvec = jnp.concatenate([c_ctx[None, :], c, jnp.zeros((MOD_ROWS - 1 - DEC_BATCH, D_MODEL), F32)], axis=0)
    mod = _modulation(cvec, ada_w, ada_b)
    w_p = _pad_w_in(w_in).astype(BF16)
    tables = _rope_tables()
    outs = []
    moe = (moe_w1, moe_b1, moe_w2, moe_b2)
    for i in range(DEPTH):
        p = {name: weights[name][i] for name in _LAYER_PARAMS}
        lam_init = 0.8 - 0.6 * math.exp(-0.3 * i)
        caches = (cache_da_k[:, i], cache_da_v[:, i], state_rwkv[:, i], state_ret[:, i])
        x, ctx_out = _layer(x, mod[i], w_p, p, moe, i, lam_init, caches, tables, final_norm_w, i == DEPTH - 1)
        outs.append(ctx_out)
    y_prompt = x[:N_CTX].reshape(BATCH, SEQ, D_MODEL)
    y_sample = x[N_CTX:].reshape(DEC_BATCH, DEC_SEQ, D_MODEL)
    stack = lambda j: jnp.stack([o[j] for o in outs], axis=1)
    return (y_prompt, y_sample, stack(0), stack(1), stack(2), stack(3))


def _pad_w_in(w_in):
    da, rw, ret, cv, gate = jnp.split(w_in, [1536, 3456, 4992, 6016], axis=-1)
    pad = jnp.zeros(w_in.shape[:-1] + (P_DA - RW_COLS,), w_in.dtype)
    return jnp.concatenate([rw, pad, da, ret, cv, gate], axis=-1)
```

```python
import functools
import math

import jax
import jax.numpy as jnp
from jax import lax
from jax.experimental import pallas as pl
from jax.experimental.pallas import tpu as pltpu

F32 = jnp.float32
BF16 = jnp.bfloat16

D_MODEL = 1024
BATCH = 16
SEQ = 256
DEPTH = 2
DEC_BATCH = 2
DEC_SEQ = 4096
PAST_LEN = 512
GRID_W = 64
EPS = 1e-6

DA_HEADS = 4
DA_QK = 64
DA_V = 128
ROPE_BASE = 10000.0

RW_HEADS = 8
RW_HD = 64
RW_W = 512
RW_LORA = 64
RW_G_LORA = 128
RW_GN_EPS = 64e-5

RET_HEADS = 4
RET_DK = 64
RET_DV = 128
RET_CHUNK = 128

CV_W = 512
CONV_K = 31
BR_W = 512
N_BRANCH = 4

N_EXPERTS = 32
TOP_K = 4
D_FF = 1024
SWIGLU_LIMIT = 7.0
SWIGLU_ALPHA = 1.702

N_CTX = BATCH * SEQ
N_LAT = DEC_BATCH * DEC_SEQ
N_TOK = N_CTX + N_LAT
N_SEQS = BATCH + DEC_BATCH
MOD_ROWS = 8
MOD_GROUP = 4096

RW_COLS = 1920
P_RW = 0
P_DA = 2048
P_RET = 3584
P_CV = 5120
P_GATE = 6144
P_COLS = 10240

VMEM_LIMIT = 48 * 1024 * 1024


def _cparams(sem):
    return pltpu.CompilerParams(dimension_semantics=sem, vmem_limit_bytes=VMEM_LIMIT)


def _dg(a, b, dims):
    return lax.dot_general(a, b, (dims, ((), ())), preferred_element_type=F32)


NN = ((1,), (0,))
NT = ((1,), (1,))
TN = ((0,), (0,))


def _dot(a, b, dims=NN):
    return _dg(a.astype(BF16), b.astype(BF16), dims)


def _split(x):
    hi = x.astype(BF16)
    lo = (x - hi.astype(F32)).astype(BF16)
    return hi, lo


def _dot3(a, b, dims=NN):
    ah, al = _split(a)
    bh, bl = _split(b)
    return _dg(ah, bh, dims) + (_dg(ah, bl, dims) + _dg(al, bh, dims))


def _dot2x(a, e, dims=NN):
    ah, al = _split(a)
    am = (a - ah.astype(F32) - al.astype(F32)).astype(BF16)
    eb = e.astype(BF16)
    return _dg(ah, eb, dims) + (_dg(al, eb, dims) + _dg(am, eb, dims))


def _sigmoid(x):
    return 1.0 / (1.0 + jnp.exp(-x))


def _softplus(x):
    return jnp.maximum(x, 0.0) + jnp.log(1.0 + jnp.exp(-jnp.abs(x)))


def _seq_pos(row):
    in_ctx = row < N_CTX
    pos = jnp.where(in_ctx, row & (SEQ - 1), (row - N_CTX) & (DEC_SEQ - 1))
    length = jnp.where(in_ctx, SEQ, DEC_SEQ)
    return pos, length


def _seq_index(row):
    return jnp.where(row < N_CTX, row // SEQ, BATCH + (row - N_CTX) // DEC_SEQ)


def _dotx(e, b, dims=NN):
    bh, bl = _split(b)
    bm = (b - bh.astype(F32) - bl.astype(F32)).astype(BF16)
    eb = e.astype(BF16)
    return _dg(eb, bh, dims) + (_dg(eb, bl, dims) + _dg(eb, bm, dims))


def _mod_kernel(c_ref, w_ref, b_ref, o_ref):
    c = c_ref[...]
    s = c * _sigmoid(c)
    o_ref[0] = _dot3(s, w_ref[0]) + b_ref[0]


def _modulation(cvec, ada_w, ada_b):
    tn = 1536
    return pl.pallas_call(
        _mod_kernel,
        grid=(DEPTH, 6 * D_MODEL // tn),
        in_specs=[
            pl.BlockSpec((MOD_ROWS, D_MODEL), lambda l, j: (0, 0)),
            pl.BlockSpec((1, D_MODEL, tn), lambda l, j: (l, 0, j)),
            pl.BlockSpec((1, 1, tn), lambda l, j: (l, 0, j)),
        ],
        out_specs=pl.BlockSpec((1, MOD_ROWS, tn), lambda l, j: (l, 0, j)),
        out_shape=jax.ShapeDtypeStruct((DEPTH, MOD_ROWS, 6 * D_MODEL), F32),
        compiler_params=_cparams(("parallel", "parallel")),
        name="modulation",
    )(cvec, ada_w, ada_b.reshape(DEPTH, 1, 6 * D_MODEL))


def _mod_row(mod_ref, first_row):
    g = first_row // MOD_GROUP
    return mod_ref[pl.ds(g, 1), :]


def _rms(x, w):
    return x * lax.rsqrt(jnp.mean(x * x, axis=-1, keepdims=True) + EPS) * w


IN_TM = 1024
IN_TN = 1024


IN_NA = P_GATE // IN_TN


def _inproj_kernel(x_ref, mod_ref, nw_ref, w_ref, oa_ref, og_ref, h_ref):
    i = pl.program_id(0)
    j = pl.program_id(1)

    @pl.when(j == 0)
    def _():
        m = _mod_row(mod_ref, i * IN_TM)
        sh = m[:, 0:D_MODEL]
        sc = m[:, D_MODEL:2 * D_MODEL]
        h_ref[...] = (_rms(x_ref[...], nw_ref[...]) * (1.0 + sc) + sh).astype(BF16)

    acc = _dg(h_ref[...], w_ref[0].astype(BF16), NN)

    @pl.when(j < IN_NA)
    def _():
        oa_ref[...] = acc

    @pl.when(j >= IN_NA)
    def _():
        og_ref[...] = acc.astype(BF16)


def _input_projection(x, mod, norm_w, w_p, layer):
    n = x.shape[0]
    return pl.pallas_call(
        _inproj_kernel,
        grid=(n // IN_TM, P_COLS // IN_TN),
        in_specs=[
            pl.BlockSpec((IN_TM, D_MODEL), lambda i, j: (i, 0)),
            pl.BlockSpec((MOD_ROWS, 6 * D_MODEL), lambda i, j: (0, 0)),
            pl.BlockSpec((1, D_MODEL), lambda i, j: (0, 0)),
            pl.BlockSpec((1, D_MODEL, IN_TN), lambda i, j: (layer, 0, j)),
        ],
        out_specs=[pl.BlockSpec((IN_TM, IN_TN), lambda i, j: (i, jnp.minimum(j, IN_NA - 1))),
                   pl.BlockSpec((IN_TM, IN_TN), lambda i, j: (i, jnp.maximum(j - IN_NA, 0)))],
        out_shape=[jax.ShapeDtypeStruct((n, P_GATE), F32), jax.ShapeDtypeStruct((n, P_COLS - P_GATE), BF16)],
        scratch_shapes=[pltpu.VMEM((IN_TM, D_MODEL), BF16)],
        compiler_params=_cparams(("parallel", "arbitrary")),
        name="input_projection",
    )(x, mod, norm_w.reshape(1, D_MODEL), w_p)


QK_TM = 512
DA_W = DA_HEADS * 2 * DA_QK


def _qkprep_kernel(q_ref, k_ref, v_ref, c_ref, se_ref, so_ref, qo_ref, ko_ref, vo_ref):
    i = pl.program_id(0)
    for h in range(DA_HEADS):
        vo_ref[:, h * 2 * DA_V:h * 2 * DA_V + DA_V] = v_ref[:, h * DA_V:(h + 1) * DA_V].astype(BF16)
        vo_ref[:, h * 2 * DA_V + DA_V:(h + 1) * 2 * DA_V] = jnp.ones((QK_TM, DA_V), BF16)
    scale = DA_QK ** -0.5

    @pl.when(i * QK_TM < N_CTX)
    def _():
        qo_ref[...] = (q_ref[...] * scale).astype(BF16)
        ko_ref[...] = k_ref[...].astype(BF16)

    @pl.when(i * QK_TM >= N_CTX)
    def _():
        c = c_ref[...]
        se = se_ref[...]
        so = so_ref[...]

        def rope(x):
            nxt = pltpu.roll(x, DA_W - 1, axis=1)
            prv = pltpu.roll(x, 1, axis=1)
            return x * c + nxt * se + prv * so

        qo_ref[...] = (rope(q_ref[...]) * scale).astype(BF16)
        ko_ref[...] = rope(k_ref[...]).astype(BF16)


def _rope_tables():
    rows = DEC_SEQ // GRID_W
    row = jnp.repeat(jnp.arange(rows, dtype=F32), GRID_W)
    col = jnp.tile(jnp.arange(GRID_W, dtype=F32), rows)
    n_pairs = DA_QK // 4
    inv = ROPE_BASE ** (-jnp.arange(n_pairs, dtype=F32) / n_pairs)
    ang = jnp.concatenate([row[:, None] * inv, col[:, None] * inv], axis=-1)
    cos = jnp.repeat(jnp.cos(ang), 2, axis=-1)
    sin = jnp.repeat(jnp.sin(ang), 2, axis=-1)
    even = (jnp.arange(DA_QK) % 2 == 0)[None, :]
    s_even = jnp.where(even, -sin, 0.0)
    s_odd = jnp.where(even, 0.0, sin)
    rep = lambda t: jnp.tile(t, (1, DA_W // DA_QK))
    return rep(cos), rep(s_even), rep(s_odd)


def _qk_prepare(proj, tables):
    n = proj.shape[0]
    lat0 = N_CTX // QK_TM
    nlat = DEC_SEQ // QK_TM
    tab = pl.BlockSpec((QK_TM, DA_W), lambda i: (jnp.maximum(i - lat0, 0) % nlat, 0))
    c0 = P_DA // DA_W
    out = jax.ShapeDtypeStruct((n, DA_W), BF16)
    return pl.pallas_call(
        _qkprep_kernel,
        grid=(n // QK_TM,),
        in_specs=[
            pl.BlockSpec((QK_TM, DA_W), lambda i: (i, c0)),
            pl.BlockSpec((QK_TM, DA_W), lambda i: (i, c0 + 1)),
            pl.BlockSpec((QK_TM, DA_W), lambda i: (i, c0 + 2)),
            tab, tab, tab,
        ],
        out_specs=[pl.BlockSpec((QK_TM, DA_W), lambda i: (i, 0))] * 2
        + [pl.BlockSpec((QK_TM, 2 * DA_W), lambda i: (i, 0))],
        out_shape=[out, out, jax.ShapeDtypeStruct((n, 2 * DA_W), BF16)],
        compiler_params=_cparams(("parallel",)),
        name="qk_prepare",
    )(proj, proj, proj, *tables)


DA_TQ = 256


def _da_kernel(q_ref, k_ref, v_ref, dl_ref, nw_ref, o_ref, *, lam_init):
    dl = dl_ref[...]
    lam = (jnp.exp(jnp.sum(dl[0:1] * dl[1:2], axis=1, keepdims=True))
           - jnp.exp(jnp.sum(dl[2:3] * dl[3:4], axis=1, keepdims=True)) + lam_init)
    nw = nw_ref[...] * (1.0 - lam_init)
    q = q_ref[0]
    for h in range(DA_HEADS):
        vext = v_ref[0, :, h * 2 * DA_V:(h + 1) * 2 * DA_V]
        os = []
        for m in range(2):
            c0 = (2 * h + m) * DA_QK
            s = _dg(q[:, c0:c0 + DA_QK], k_ref[0, :, c0:c0 + DA_QK], NT)
            e = jnp.exp(s - jnp.max(s, axis=-1, keepdims=True)).astype(BF16)
            oe = _dg(e, vext, NN)
            os.append(oe[:, :DA_V] * (1.0 / oe[:, DA_V:DA_V + 1]))
        o_ref[0, :, h * DA_V:(h + 1) * DA_V] = _rms(os[0] - lam * os[1], nw).astype(BF16)


def _diff_attention(q, k, v, da_lambda, da_norm_w, lam_init):
    b, tq, _ = q.shape
    tk = k.shape[1]
    return pl.pallas_call(
        functools.partial(_da_kernel, lam_init=lam_init),
        grid=(b, tq // DA_TQ),
        in_specs=[
            pl.BlockSpec((1, DA_TQ, DA_W), lambda i, j: (i, j, 0)),
            pl.BlockSpec((1, tk, DA_W), lambda i, j: (i, 0, 0), pipeline_mode=pl.Buffered(1)),
            pl.BlockSpec((1, tk, 2 * DA_W), lambda i, j: (i, 0, 0), pipeline_mode=pl.Buffered(1)),
            pl.BlockSpec((4, DA_QK), lambda i, j: (0, 0)),
            pl.BlockSpec((1, DA_V), lambda i, j: (0, 0)),
        ],
        out_specs=pl.BlockSpec((1, DA_TQ, DA_W), lambda i, j: (i, j, 0)),
        out_shape=jax.ShapeDtypeStruct((b, tq, DA_W), BF16),
        compiler_params=_cparams(("parallel", "arbitrary")),
        name="diff_attention",
    )(q, k, v, da_lambda, da_norm_w.reshape(1, DA_V))


def _da_branch(proj, cache_k, cache_v, da_lambda, da_norm_w, lam_init, tables):
    qb, kb, vb = _qk_prepare(proj, tables)
    ctx = lambda t: t[:N_CTX].reshape(BATCH, SEQ, t.shape[-1])
    lat = lambda t: t[N_CTX:].reshape(DEC_BATCH, DEC_SEQ, t.shape[-1])
    cv = cache_v.astype(BF16)
    cv = jnp.concatenate([cv, jnp.ones_like(cv)], axis=-1).reshape(DEC_BATCH, PAST_LEN, 2 * DA_W)
    o_ctx = _diff_attention(ctx(qb), ctx(kb), ctx(vb), da_lambda, da_norm_w, lam_init)
    k_all = jnp.concatenate([cache_k.reshape(DEC_BATCH, PAST_LEN, DA_W).astype(BF16), lat(kb)], axis=1)
    v_all = jnp.concatenate([cv, lat(vb)], axis=1)
    o_lat = _diff_attention(lat(qb), k_all, v_all, da_lambda, da_norm_w, lam_init)
    return jnp.concatenate([o_ctx.reshape(N_CTX, DA_W), o_lat.reshape(N_LAT, DA_W)], axis=0)


RW_TM = 256
RW_C = 64
RW_PAIRS = RW_HEADS // 2
HALO = 8


def _head_ones():
    idx = jnp.arange(RW_W) // RW_HD
    return (idx[:, None] == idx[None, :]).astype(BF16)


def _rwprep_kernel(x_ref, xp_ref, xn_ref, mu_ref, w0_ref, wup_ref, a0_ref, aup_ref, gup_ref, kk_ref, ka_ref,
                   ones_ref, r_o, v_o, kk_o, g_o, kd_o, lw_o, a_o, buf):
    row0 = pl.program_id(0) * RW_TM
    x = x_ref[:, 0:RW_COLS]
    buf[HALO:HALO + RW_TM, :] = x
    buf[HALO - 1:HALO, :] = xp_ref[HALO - 1:HALO, 0:RW_COLS]
    buf[HALO + RW_TM:HALO + RW_TM + 1, :] = xn_ref[0:1, 0:RW_COLS]
    rows = row0 + lax.broadcasted_iota(jnp.int32, (RW_TM, 1), 0)
    pos, length = _seq_pos(rows)
    prev = jnp.where(pos == 0, 0.0, buf[HALO - 1:HALO - 1 + RW_TM, :])
    nxt = jnp.where(pos == length - 1, 0.0, buf[HALO + 1:HALO + 1 + RW_TM, :])
    mu = mu_ref[...]
    u = x + mu[0:1] * (prev - x) + mu[1:2] * (nxt - x)

    r = u[:, 0:RW_W]
    k = u[:, RW_W:2 * RW_W]
    v = u[:, 2 * RW_W:3 * RW_W]
    wl = u[:, 3 * RW_W:3 * RW_W + 128]
    al = u[:, 3 * RW_W + 128:3 * RW_W + 256]
    gl = u[:, 3 * RW_W + 256:3 * RW_W + 384]
    w_raw = w0_ref[...] + _dot3(jnp.tanh(wl), wup_ref[...])
    lw = -jnp.exp(-_softplus(-w_raw) - 0.5)
    a = _sigmoid(a0_ref[...] + _dot3(al, aup_ref[...]))
    g = _dot3(_sigmoid(gl), gup_ref[...])
    kk = k * kk_ref[...]
    kk = kk * lax.rsqrt(jnp.maximum(_dot2x(kk * kk, ones_ref[...]), 1e-12))
    kd = jnp.concatenate([k, k], axis=1) * (1.0 + (a - 1.0) * ka_ref[...])

    r_o[...] = r
    v_o[...] = v
    kk_o[...] = kk
    g_o[...] = g
    for d in range(2):
        kd_o[d] = kd[:, d * RW_W:(d + 1) * RW_W]
        lw_o[d] = lw[:, d * RW_W:(d + 1) * RW_W]
        a_o[d] = a[:, d * RW_W:(d + 1) * RW_W]


def _rw_prepare(proj, p):
    n = proj.shape[0]
    nh = n // HALO
    steps = RW_TM // HALO
    wide = P_DA - P_RW
    cat2 = lambda t: t.reshape(1, 2 * RW_W)
    blockdiag = lambda t: jnp.concatenate(
        [jnp.concatenate([t[0], jnp.zeros_like(t[0])], axis=1),
         jnp.concatenate([jnp.zeros_like(t[1]), t[1]], axis=1)], axis=0)
    const = lambda shape: pl.BlockSpec(shape, lambda i: (0,) * len(shape))
    row = pl.BlockSpec((RW_TM, RW_W), lambda i: (i, 0))
    row2 = pl.BlockSpec((2, RW_TM, RW_W), lambda i: (0, i, 0))
    o1 = jax.ShapeDtypeStruct((n, RW_W), F32)
    o2 = jax.ShapeDtypeStruct((2, n, RW_W), F32)
    return pl.pallas_call(
        _rwprep_kernel,
        grid=(n // RW_TM,),
        in_specs=[
            pl.BlockSpec((RW_TM, wide), lambda i: (i, 0)),
            pl.BlockSpec((HALO, wide), lambda i: (jnp.maximum(i * steps - 1, 0), 0)),
            pl.BlockSpec((HALO, wide), lambda i: (jnp.minimum((i + 1) * steps, nh - 1), 0)),
            const((2, RW_COLS)), const((1, 2 * RW_W)), const((128, 2 * RW_W)), const((1, 2 * RW_W)),
            const((128, 2 * RW_W)), const((RW_G_LORA, RW_W)), const((1, RW_W)), const((1, 2 * RW_W)),
            const((RW_W, RW_W)),
        ],
        out_specs=[row, row, row, row, row2, row2, row2],
        out_shape=[o1, o1, o1, o1, o2, o2, o2],
        scratch_shapes=[pltpu.VMEM((RW_TM + 2 * HALO, RW_COLS), F32)],
        compiler_params=_cparams(("parallel",)),
        name="rwkv_prepare",
    )(proj, proj, proj, p['rw_shift'], cat2(p['rw_w0']), blockdiag(p['rw_w_up']), cat2(p['rw_a0']),
      blockdiag(p['rw_a_up']), p['rw_g_up'], p['rw_k_k'].reshape(1, RW_W),
      jnp.tile(p['rw_k_a'].reshape(1, RW_W), (1, 2)), _head_ones())


RW_TB = 256
RW_NC = RW_TB // RW_C
RW_SIDE = 8


def _rw_finish(y, r, v, g, kd_sum, rk, lnw, lnb, ones):
    inv = 1.0 / RW_HD
    xc = y - _dot2x(y, ones) * inv
    var = _dot2x(xc * xc, ones) * inv
    yn = xc * lax.rsqrt(var + RW_GN_EPS) * lnw + lnb
    bonus = _dot2x(r * kd_sum * rk, ones) * v
    return ((yn + bonus) * g).astype(BF16)


def _rwscan_kernel(r_ref, v_ref, kk_ref, kd_ref, lw_ref, a_ref, tri_ref, inc_ref, str_ref, s0_ref, *rest, backward):
    if backward:
        yf_ref, g_ref, rk_ref, lnw_ref, lnb_ref, ones_ref, y_ref, st_ref, s_scr = rest
    else:
        y_ref, st_ref, s_scr = rest
    _rwscan_body(r_ref, v_ref, kk_ref, kd_ref, lw_ref, a_ref, tri_ref, inc_ref, str_ref, s0_ref, y_ref, st_ref,
                 s_scr, backward,
                 (lambda yb: _rw_finish(yf_ref[...] + yb, r_ref[...], v_ref[...], g_ref[...],
                                        kd_ref[0] + kd_ref[1], rk_ref[...], lnw_ref[...], lnb_ref[...],
                                        ones_ref[...])) if backward else None)


def _rwscan_body(r_ref, v_ref, kk_ref, kd_ref, lw_ref, a_ref, tri_ref, inc_ref, str_ref, s0_ref, y_ref, st_ref,
                 s_scr, backward, finish):
    step = pl.program_id(0)
    nb = pl.num_programs(0)
    bi = (nb - 1 - step) if backward else step
    pos, length = _seq_pos(bi * RW_TB)
    first = (pos + RW_TB == length) if backward else (pos == 0)
    last = (pos == 0) if backward else (pos + RW_TB == length)
    in_ctx = bi * RW_TB < N_CTX
    c = RW_C

    @pl.when(jnp.logical_and(first, in_ctx))
    def _():
        s_scr[...] = jnp.zeros_like(s_scr)

    @pl.when(jnp.logical_and(first, jnp.logical_not(in_ctx)))
    def _():
        z = jnp.zeros((c, c), F32)
        for p in range(RW_PAIRS):
            s_scr[p] = jnp.concatenate(
                [jnp.concatenate([s0_ref[0, 0, 2 * p], z], axis=1),
                 jnp.concatenate([z, s0_ref[0, 0, 2 * p + 1]], axis=1)], axis=0)

    incl2 = inc_ref[...] > 0.5
    strict2 = str_ref[...] > 0.5
    eye2 = (lax.broadcasted_iota(jnp.int32, (2 * c, 2 * c), 0)
            == lax.broadcasted_iota(jnp.int32, (2 * c, 2 * c), 1))
    m_e = lax.broadcasted_iota(jnp.int32, (1, 2 * c), 1) < c

    lw = lw_ref[0]
    cum = _dotx(tri_ref[...], lw)
    tots = [cum[(j * c if backward else j * c + c - 1):(j * c + 1 if backward else j * c + c), :]
            for j in range(RW_NC)]
    tot_b = jnp.concatenate([jnp.broadcast_to(t, (c, RW_W)) for t in tots], axis=0)
    kk = kk_ref[...]
    kd = kd_ref[1 if backward else 0]
    bp = kk * a_ref[0]
    g_inv = jnp.exp(-cum)
    g_rem = jnp.exp(tot_b - cum)
    ag = -kk * jnp.exp(cum - lw)
    rg = r_ref[...] * jnp.exp(cum)
    bdn = bp * g_inv
    kdn = kd * g_inv
    bc = bp * g_rem
    kc = kd * g_rem
    v = v_ref[...]

    def stack(x, j, p):
        xs = x[j * c:(j + 1) * c, p * 2 * c:(p + 1) * 2 * c]
        return jnp.concatenate([jnp.where(m_e, xs, 0.0), jnp.where(m_e, 0.0, xs)], axis=0).astype(BF16)

    pre = {}
    keys = [(j, p) for j in range(RW_NC) for p in range(RW_PAIRS)]
    for g0 in range(0, len(keys), RW_SIDE):
        grp = keys[g0:g0 + RW_SIDE]
        ops = {k: tuple(stack(t, *k) for t in (ag, rg, bdn, kdn, bc, kc, v)) for k in grp}
        gm = {k: _dg(jnp.concatenate([ops[k][0], ops[k][1]], axis=0),
                     jnp.concatenate([ops[k][2], ops[k][3]], axis=0), NT) for k in grp}
        lbb = {k: jnp.where(strict2, gm[k][:2 * c, :2 * c], 0.0) for k in grp}
        lkb = {k: jnp.where(strict2, gm[k][:2 * c, 2 * c:], 0.0).astype(BF16) for k in grp}
        lrk = {k: jnp.concatenate([jnp.where(incl2, gm[k][2 * c:, :2 * c], 0.0),
                                   jnp.where(incl2, gm[k][2 * c:, 2 * c:], 0.0)], axis=1).astype(BF16)
               for k in grp}
        lv = {k: _dg(lkb[k], ops[k][6], NN) for k in grp}
        x = {k: jnp.where(eye2, 1.0, lbb[k]) for k in grp}
        pw = lbb
        for _ in range(int(math.log2(c)) - 1):
            pwb = {k: pw[k].astype(BF16) for k in grp}
            pw = {k: _dg(pwb[k], pwb[k], NN) for k in grp}
            x = {k: x[k] + _dg(x[k].astype(BF16), pw[k].astype(BF16), NN) for k in grp}
        tw = {k: _dg(x[k].astype(BF16), jnp.concatenate([lv[k].astype(BF16), ops[k][0]], axis=1), NN)
              for k in grp}
        for k in grp:
            pre[k] = (tw[k][:, :2 * c], tw[k][:, 2 * c:].astype(BF16), ops[k][1], lrk[k], ops[k][6],
                      jnp.concatenate([ops[k][4], ops[k][5]], axis=0))

    order = range(RW_NC - 1, -1, -1) if backward else range(RW_NC)
    pairs = range(RW_PAIRS)
    s = [s_scr[p] for p in pairs]
    ys = {}
    for j in order:
        sb = [s[p].astype(BF16) for p in pairs]
        u = [_dg(pre[j, p][1], sb[p], NT) + pre[j, p][0] for p in pairs]
        uv = [jnp.concatenate([u[p].astype(BF16), pre[j, p][4]], axis=0) for p in pairs]
        y = [_dg(pre[j, p][2], sb[p], NT) + _dg(pre[j, p][3], uv[p], NN) for p in pairs]
        s = [s[p] * jnp.exp(tots[j][:, p * 2 * c:(p + 1) * 2 * c]) + _dg(uv[p], pre[j, p][5], TN) for p in pairs]
        ys[j] = [y[p][:c] + y[p][c:] for p in pairs]
    y_blk = jnp.concatenate([jnp.concatenate(ys[j], axis=1) for j in range(RW_NC)], axis=0)
    y_ref[...] = finish(y_blk) if finish else y_blk
    for p in pairs:
        s_scr[p] = s[p]

    @pl.when(last)
    def _():
        for p in pairs:
            st_ref[0, 2 * p] = s[p][:c, :c]
            st_ref[0, 2 * p + 1] = s[p][c:, c:]


def _scan_masks(backward):
    t = jnp.arange(RW_TB)
    sgn = -1 if backward else 1
    same_chunk = (t[:, None] // RW_C) == (t[None, :] // RW_C)
    tri = (same_chunk & ((t[:, None] - t[None, :]) * sgn >= 0)).astype(BF16)
    q = jnp.arange(2 * RW_C)
    same_head = (q[:, None] // RW_C) == (q[None, :] // RW_C)
    dif = ((q[:, None] % RW_C) - (q[None, :] % RW_C)) * sgn
    return tri, (same_head & (dif >= 0)).astype(F32), (same_head & (dif > 0)).astype(F32)


def _rw_scan(r, v, kk, kd, lw, a, s0, backward, finish=()):
    n = r.shape[0]
    nb = n // RW_TB
    d = 1 if backward else 0
    blk = (lambda i: nb - 1 - i) if backward else (lambda i: i)
    row = pl.BlockSpec((RW_TB, RW_W), lambda i: (blk(i), 0))
    row2 = pl.BlockSpec((1, RW_TB, RW_W), lambda i: (d, blk(i), 0))
    both = pl.BlockSpec((2, RW_TB, RW_W), lambda i: (0, blk(i), 0))
    const = lambda shape: pl.BlockSpec(shape, lambda i: (0,) * len(shape))
    vec = const((1, RW_W))
    pair = 2 * RW_C
    extra = [row, row, vec, vec, vec, const((RW_W, RW_W))] if backward else []
    return pl.pallas_call(
        functools.partial(_rwscan_kernel, backward=backward),
        grid=(nb,),
        in_specs=[row, row, row, both, row2, row2, const((RW_TB, RW_TB)), const((pair, pair)), const((pair, pair)),
                  pl.BlockSpec((1, 1, RW_HEADS, RW_HD, RW_HD),
                               lambda i: (jnp.maximum(_seq_index(blk(i) * RW_TB) - BATCH, 0), d, 0, 0, 0))] + extra,
        out_specs=[row, pl.BlockSpec((1, RW_HEADS, RW_HD, RW_HD), lambda i: (_seq_index(blk(i) * RW_TB), 0, 0, 0))],
        out_shape=[jax.ShapeDtypeStruct((n, RW_W), BF16 if backward else F32),
                   jax.ShapeDtypeStruct((N_SEQS, RW_HEADS, RW_HD, RW_HD), F32)],
        scratch_shapes=[pltpu.VMEM((RW_PAIRS, pair, pair), F32)],
        compiler_params=_cparams(("arbitrary",)),
        name="rwkv_scan_bwd" if backward else "rwkv_scan_fwd",
    )(r, v, kk, kd, lw, a, *_scan_masks(backward), s0, *finish)


def _rwkv_branch(proj, state0, p):
    r, v, kk, g, kd, lw, a = _rw_prepare(proj, p)
    yf, sf = _rw_scan(r, v, kk, kd, lw, a, state0, backward=False)
    finish = (yf, g, p['rw_r_k'].reshape(1, RW_W), p['rw_ln_w'].reshape(1, RW_W), p['rw_ln_b'].reshape(1, RW_W),
              _head_ones())
    out, sb = _rw_scan(r, v, kk, kd, lw, a, state0, backward=True, finish=finish)
    return out, jnp.stack([sf, sb], axis=1)


RET_W = RET_HEADS * RET_DV
RET_PAIRS = RET_HEADS // 2


RET_TB = 256
RET_NC = RET_TB // RET_CHUNK


def _ret_kernel(q_ref, k_ref, v_ref, lg_ref, s0_ref, o_ref, st_ref, s_scr, *, backward):
    step = pl.program_id(0)
    nb = pl.num_programs(0)
    bi = (nb - 1 - step) if backward else step
    pos, length = _seq_pos(bi * RET_TB)
    first = (pos + RET_TB == length) if backward else (pos == 0)
    in_ctx = bi * RET_TB < N_CTX

    @pl.when(jnp.logical_and(first, in_ctx))
    def _():
        s_scr[...] = jnp.zeros_like(s_scr)

    @pl.when(jnp.logical_and(first, jnp.logical_not(in_ctx)))
    def _():
        s_scr[...] = s0_ref[0, 0]

    c = RET_CHUNK
    d = 1 if backward else 0
    sgn = -1 if backward else 1
    lgs = -_softplus(-lg_ref[d:d + 1, :])
    ri = lax.broadcasted_iota(jnp.int32, (c, c), 0)
    cj = lax.broadcasted_iota(jnp.int32, (c, c), 1)
    dif = (ri - cj) * sgn
    valid = dif >= 0
    dist = jnp.maximum(dif, 0).astype(F32)
    pr = lax.broadcasted_iota(jnp.int32, (c, 2 * RET_DK), 0)
    tau = ((c - 1 - pr) if backward else pr).astype(F32)
    low = lax.broadcasted_iota(jnp.int32, (1, 2 * RET_DK), 1) < RET_DK
    rlow = lax.broadcasted_iota(jnp.int32, (2 * RET_DK, RET_DV), 0) < RET_DK
    heads = range(RET_HEADS)
    pairs = range(RET_PAIRS)
    mask = [low, jnp.logical_not(low)]
    lg_h = [lgs[:, h:h + 1] for h in heads]
    lg_row = [jnp.where(low, lg_h[2 * p], lg_h[2 * p + 1]) for p in pairs]
    dmat = [jnp.where(valid, jnp.exp(lg_h[h] * dist), 0.0) for h in heads]
    q_dec = [jnp.exp(lg_row[p] * (tau + 1.0)) for p in pairs]
    k_dec = [jnp.exp(lg_row[p] * (c - 1.0 - tau)) for p in pairs]
    c_dec = [jnp.where(rlow, jnp.exp(lg_h[2 * p] * c), jnp.exp(lg_h[2 * p + 1] * c)) for p in pairs]

    pre = []
    for j in range(RET_NC):
        rows = slice(j * c, (j + 1) * c)
        qp = [q_ref[rows, p * 128:(p + 1) * 128] for p in pairs]
        kp = [k_ref[rows, p * 128:(p + 1) * 128] * (RET_DK ** -0.5) for p in pairs]
        kpb = [t.astype(BF16) for t in kp]
        vb = [v_ref[rows, h * RET_DV:(h + 1) * RET_DV].astype(BF16) for h in heads]
        att = [_dg(jnp.where(mask[h % 2], qp[h // 2], 0.0).astype(BF16), kpb[h // 2], NT) * dmat[h]
               for h in heads]
        upd = [_dg(jnp.where(mask[h % 2], kp[h // 2] * k_dec[h // 2], 0.0).astype(BF16), vb[h], TN)
               for h in heads]
        inner = [_dg(att[h].astype(BF16), vb[h], NN) for h in heads]
        qd = [jnp.where(mask[h % 2], qp[h // 2] * q_dec[h // 2], 0.0).astype(BF16) for h in heads]
        pre.append((inner, qd, upd))

    s = [s_scr[p] for p in pairs]
    for j in (range(RET_NC - 1, -1, -1) if backward else range(RET_NC)):
        inner, qd, upd = pre[j]
        sb = [t.astype(BF16) for t in s]
        for h in heads:
            o_ref[j * c:(j + 1) * c, h * RET_DV:(h + 1) * RET_DV] = inner[h] + _dg(qd[h], sb[h // 2], NN)
        s = [s[p] * c_dec[p] + upd[2 * p] + upd[2 * p + 1] for p in pairs]
    for p in pairs:
        s_scr[p] = s[p]
        st_ref[0, p] = s[p]


def _ret_scan(proj, logit, s0, backward):
    n = proj.shape[0]
    nb = n // RET_TB
    d = 1 if backward else 0
    blk = (lambda i: nb - 1 - i) if backward else (lambda i: i)
    qk_w = RET_HEADS * RET_DK
    state = (RET_PAIRS, 2 * RET_DK, RET_DV)
    return pl.pallas_call(
        functools.partial(_ret_kernel, backward=backward),
        grid=(nb,),
        in_specs=[
            pl.BlockSpec((RET_TB, qk_w), lambda i: (blk(i), P_RET // qk_w)),
            pl.BlockSpec((RET_TB, qk_w), lambda i: (blk(i), P_RET // qk_w + 1)),
            pl.BlockSpec((RET_TB, RET_W), lambda i: (blk(i), (P_RET + 2 * qk_w) // RET_W)),
            pl.BlockSpec((2, RET_HEADS), lambda i: (0, 0)),
            pl.BlockSpec((1, 1) + state,
                         lambda i: (jnp.maximum(_seq_index(blk(i) * RET_TB) - BATCH, 0), d, 0, 0, 0)),
        ],
        out_specs=[pl.BlockSpec((RET_TB, RET_W), lambda i: (blk(i), 0)),
                   pl.BlockSpec((1,) + state, lambda i: (_seq_index(blk(i) * RET_TB), 0, 0, 0))],
        out_shape=[jax.ShapeDtypeStruct((n, RET_W), F32), jax.ShapeDtypeStruct((N_SEQS,) + state, F32)],
        scratch_shapes=[pltpu.VMEM(state, F32)],
        compiler_params=_cparams(("arbitrary",)),
        name="retention_scan_bwd" if backward else "retention_scan_fwd",
    )(proj, proj, proj, logit, s0)


def _standardize(x, eps):
    mu = jnp.mean(x, axis=-1, keepdims=True)
    xc = x - mu
    return xc * lax.rsqrt(jnp.mean(xc * xc, axis=-1, keepdims=True) + eps)


def _retpost_kernel(of_ref, ob_ref, g_ref, nw_ref, out_ref):
    o = of_ref[...] + ob_ref[...]
    g = g_ref[...]
    for h in range(RET_HEADS):
        sl = slice(h * RET_DV, (h + 1) * RET_DV)
        gh = g[:, sl]
        out_ref[:, sl] = (gh * _sigmoid(gh) * (_standardize(o[:, sl], EPS) * nw_ref[:, sl])).astype(BF16)


def _ret_post(of, ob, proj, norm_w):
    n = proj.shape[0]
    tm = 512
    return pl.pallas_call(
        _retpost_kernel,
        grid=(n // tm,),
        in_specs=[
            pl.BlockSpec((tm, RET_W), lambda i: (i, 0)),
            pl.BlockSpec((tm, RET_W), lambda i: (i, 0)),
            pl.BlockSpec((tm, RET_W), lambda i: (i, (P_RET + 1024) // RET_W)),
            pl.BlockSpec((1, RET_W), lambda i: (0, 0)),
        ],
        out_specs=pl.BlockSpec((tm, RET_W), lambda i: (i, 0)),
        out_shape=jax.ShapeDtypeStruct((n, RET_W), BF16),
        compiler_params=_cparams(("parallel",)),
        name="retention_post",
    )(of, ob, proj, norm_w.reshape(1, RET_W))


def _retention_branch(proj, state0, p):
    s0 = state0.reshape(DEC_BATCH, 2, RET_PAIRS, 2 * RET_DK, RET_DV)
    of, sf = _ret_scan(proj, p['ret_decay_logit'], s0, backward=False)
    ob, sb = _ret_scan(proj, p['ret_decay_logit'], s0, backward=True)
    st = jnp.stack([sf, sb], axis=1).reshape(N_SEQS, 2, RET_HEADS, RET_DK, RET_DV)
    return _ret_post(of, ob, proj, p['ret_norm_w']), st


CV_TM = 256
CV_HALO = 16


def _conv_kernel(a_ref, g_ref, ap_ref, gp_ref, an_ref, gn_ref, w_ref, b_ref, lnw_ref, lnb_ref, o_ref, buf, sbuf):
    row0 = pl.program_id(0) * CV_TM
    pos0, len0 = _seq_pos(row0)
    buf[CV_HALO:CV_HALO + CV_TM, :] = a_ref[...] * _sigmoid(g_ref[...])
    buf[0:CV_HALO, :] = jnp.where(pos0 == 0, 0.0, ap_ref[...] * _sigmoid(gp_ref[...]))
    buf[CV_HALO + CV_TM:, :] = jnp.where(pos0 + CV_TM == len0, 0.0, an_ref[...] * _sigmoid(gn_ref[...]))
    base = CV_HALO - CONV_K // 2
    acc = jnp.zeros((CV_TM, CV_W), F32)
    for r in range(8):
        taps = [m for m in range((base + CONV_K + 7) // 8) if 0 <= r + 8 * m - base < CONV_K]
        span = CV_TM + 8 * max(taps)
        if r:
            sbuf[0:span, :] = buf[r:r + span, :]
        src = sbuf if r else buf
        for m in taps:
            j = r + 8 * m - base
            acc = acc + w_ref[j:j + 1, :] * src[8 * m:8 * m + CV_TM, :]
    z = _standardize(acc + b_ref[...], EPS) * lnw_ref[...] + lnb_ref[...]
    o_ref[...] = (z * _sigmoid(z)).astype(BF16)


def _conv_branch(proj, p):
    n = proj.shape[0]
    nh = n // CV_HALO
    steps = CV_TM // CV_HALO
    ca = P_CV // CV_W
    prev = lambda i: jnp.maximum(i * steps - 1, 0)
    nxt = lambda i: jnp.minimum((i + 1) * steps, nh - 1)
    vec = pl.BlockSpec((1, CV_W), lambda i: (0, 0))
    return pl.pallas_call(
        _conv_kernel,
        grid=(n // CV_TM,),
        in_specs=[
            pl.BlockSpec((CV_TM, CV_W), lambda i: (i, ca)),
            pl.BlockSpec((CV_TM, CV_W), lambda i: (i, ca + 1)),
            pl.BlockSpec((CV_HALO, CV_W), lambda i: (prev(i), ca)),
            pl.BlockSpec((CV_HALO, CV_W), lambda i: (prev(i), ca + 1)),
            pl.BlockSpec((CV_HALO, CV_W), lambda i: (nxt(i), ca)),
            pl.BlockSpec((CV_HALO, CV_W), lambda i: (nxt(i), ca + 1)),
            pl.BlockSpec((CONV_K, CV_W), lambda i: (0, 0)),
            vec, vec, vec,
        ],
        out_specs=pl.BlockSpec((CV_TM, CV_W), lambda i: (i, 0)),
        out_shape=jax.ShapeDtypeStruct((n, CV_W), BF16),
        scratch_shapes=[pltpu.VMEM((CV_TM + 2 * CV_HALO, CV_W), F32)] * 2,
        compiler_params=_cparams(("parallel",)),
        name="conformer_conv",
    )(proj, proj, proj, proj, proj, proj, p['cv_dw_w'], p['cv_dw_b'].reshape(1, CV_W),
      p['cv_ln_w'].reshape(1, CV_W), p['cv_ln_b'].reshape(1, CV_W))


MG_TM = 512


def _merge_kernel(da_ref, rw_ref, ret_ref, cv_ref, g0_ref, g1_ref, g2_ref, g3_ref, x_ref, mod_ref, wb_ref, wo_ref,
                  nw_ref, x_o, h_o):
    m = None
    for n, (br, gt) in enumerate(((da_ref, g0_ref), (rw_ref, g1_ref), (ret_ref, g2_ref), (cv_ref, g3_ref))):
        t = _sigmoid(gt[...].astype(F32)) * _dg(br[...], wb_ref[n], NN)
        m = t if m is None else m + t
    out = _dg(m.astype(BF16), wo_ref[...], NN)
    mrow = _mod_row(mod_ref, pl.program_id(0) * MG_TM)
    gate1 = mrow[:, 2 * D_MODEL:3 * D_MODEL]
    sh2 = mrow[:, 3 * D_MODEL:4 * D_MODEL]
    sc2 = mrow[:, 4 * D_MODEL:5 * D_MODEL]
    x1 = x_ref[...] + gate1 * out
    x_o[...] = x1
    h_o[...] = _rms(x1, nw_ref[...]) * (1.0 + sc2) + sh2


def _merge(branches, gates, x, mod, w_branch, w_out, norm_w):
    n = x.shape[0]
    br = pl.BlockSpec((MG_TM, BR_W), lambda i: (i, 0))
    gspec = lambda j: pl.BlockSpec((MG_TM, D_MODEL), lambda i: (i, j))
    full = pl.BlockSpec((MG_TM, D_MODEL), lambda i: (i, 0))
    out = jax.ShapeDtypeStruct((n, D_MODEL), F32)
    return pl.pallas_call(
        _merge_kernel,
        grid=(n // MG_TM,),
        in_specs=[br, br, br, br, gspec(0), gspec(1), gspec(2), gspec(3), full,
                  pl.BlockSpec((MOD_ROWS, 6 * D_MODEL), lambda i: (0, 0)),
                  pl.BlockSpec((N_BRANCH, BR_W, D_MODEL), lambda i: (0, 0, 0)),
                  pl.BlockSpec((D_MODEL, D_MODEL), lambda i: (0, 0)),
                  pl.BlockSpec((1, D_MODEL), lambda i: (0, 0))],
        out_specs=[full, full],
        out_shape=[out, out],
        compiler_params=_cparams(("parallel",)),
        name="gated_merge",
    )(*branches, gates, gates, gates, gates, x, mod, w_branch.astype(BF16), w_out.astype(BF16),
      norm_w.reshape(1, D_MODEL))


RT_TM = 256
MOE_BM = 512
MOE_ROWS = N_TOK * TOP_K + N_EXPERTS * MOE_BM
DP_TM = 512
DP_GROUP = 64
CB_TM = 256
FF_CHUNK = 256


def _router_kernel(h_ref, w_ref, b_ref, tri_ref, idx_o, gate_o, rank_o, cnt_o, carry):
    @pl.when(pl.program_id(0) == 0)
    def _():
        carry[...] = jnp.zeros_like(carry)

    logits = _dot3(w_ref[...], h_ref[...], NT) + b_ref[...]
    e_iota = lax.broadcasted_iota(jnp.int32, logits.shape, 0)
    work = logits
    vals, idxs, hots = [], [], []
    for _ in range(TOP_K):
        mx = jnp.max(work, axis=0, keepdims=True)
        ix = jnp.min(jnp.where(work == mx, e_iota, N_EXPERTS), axis=0, keepdims=True)
        hot = e_iota == ix
        vals.append(mx)
        idxs.append(ix)
        hots.append(hot.astype(F32))
        work = jnp.where(hot, -jnp.inf, work)
    es = [jnp.exp(v - vals[0]) for v in vals]
    inv = 1.0 / (es[0] + es[1] + es[2] + es[3])
    chosen = hots[0] + hots[1] + hots[2] + hots[3]
    ahead = carry[...][:, 0:1] + _dg(chosen.astype(BF16), tri_ref[...], NN)
    idx_o[...] = jnp.concatenate(idxs, axis=0)
    gate_o[...] = jnp.concatenate([e * inv for e in es], axis=0)
    rank_o[...] = jnp.concatenate(
        [jnp.sum(hot * ahead, axis=0, keepdims=True) for hot in hots], axis=0).astype(jnp.int32)
    carry[...] = carry[...] + jnp.sum(chosen, axis=1, keepdims=True)
    cnt_o[...] = carry[...]


def _router(h, router_w, router_b):
    n = h.shape[0]
    tri = (jnp.arange(RT_TM)[:, None] < jnp.arange(RT_TM)[None, :]).astype(BF16)
    col = pl.BlockSpec((TOP_K, RT_TM), lambda i: (0, i))
    return pl.pallas_call(
        _router_kernel,
        grid=(n // RT_TM,),
        in_specs=[
            pl.BlockSpec((RT_TM, D_MODEL), lambda i: (i, 0)),
            pl.BlockSpec((N_EXPERTS, D_MODEL), lambda i: (0, 0)),
            pl.BlockSpec((N_EXPERTS, 1), lambda i: (0, 0)),
            pl.BlockSpec((RT_TM, RT_TM), lambda i: (0, 0)),
        ],
        out_specs=[col, col, col, pl.BlockSpec((N_EXPERTS, 128), lambda i: (0, 0))],
        out_shape=[jax.ShapeDtypeStruct((TOP_K, n), jnp.int32), jax.ShapeDtypeStruct((TOP_K, n), F32),
                   jax.ShapeDtypeStruct((TOP_K, n), jnp.int32), jax.ShapeDtypeStruct((N_EXPERTS, 128), F32)],
        scratch_shapes=[pltpu.VMEM((N_EXPERTS, 128), F32)],
        compiler_params=_cparams(("arbitrary",)),
        name="router",
    )(h, router_w.T, router_b.reshape(N_EXPERTS, 1), tri)


def _tile_major(t, tm):
    k, n = t.shape
    return t.reshape(k, n // tm, tm).transpose(1, 0, 2).reshape(n // tm, 1, k * tm)


def _dispatch_kernel(dest_ref, pe_ref, h_ref, o_hbm, zbuf, sem, zsem):
    n_groups = DP_TM // DP_GROUP

    @pl.when(pl.program_id(0) == 0)
    def _():
        zbuf[...] = jnp.zeros_like(zbuf)

        def fill(e):
            end = pe_ref[e]
            begin = pe_ref[e - 1] if e else 0
            return end > begin, pltpu.make_async_copy(
                zbuf, o_hbm.at[pl.ds(pl.multiple_of(jnp.maximum(end - MOE_BM, 0), MOE_BM), MOE_BM)], zsem.at[0])

        for e in range(N_EXPERTS):
            nonempty, cp = fill(e)
            pl.when(nonempty)(cp.start)
        for e in range(N_EXPERTS):
            nonempty, cp = fill(e)
            pl.when(nonempty)(cp.wait)

        def tail(b):
            return pltpu.make_async_copy(zbuf, o_hbm.at[pl.ds(pl.multiple_of(b * MOE_BM, MOE_BM), MOE_BM)],
                                         zsem.at[0])

        first_unused = pe_ref[N_EXPERTS - 1] // MOE_BM
        lax.fori_loop(first_unused, MOE_ROWS // MOE_BM, lambda b, c: (tail(b).start(), c)[1], 0)
        lax.fori_loop(first_unused, MOE_ROWS // MOE_BM, lambda b, c: (tail(b).wait(), c)[1], 0)

    def wait_group(slot):
        pltpu.make_async_copy(h_ref.at[pl.ds(0, TOP_K * DP_GROUP)], o_hbm.at[pl.ds(0, TOP_K * DP_GROUP)],
                              sem.at[slot]).wait()

    def group(gi, carry):
        slot = gi % 2

        def issue(t, c):
            tok = gi * DP_GROUP + t
            for k in range(TOP_K):
                dst = dest_ref[0, 0, k * DP_TM + tok]
                pltpu.make_async_copy(h_ref.at[pl.ds(tok, 1)], o_hbm.at[pl.ds(dst, 1)], sem.at[slot]).start()
            return c

        lax.fori_loop(0, DP_GROUP, issue, 0)

        @pl.when(gi > 0)
        def _():
            wait_group(1 - slot)

        return carry

    lax.fori_loop(0, n_groups, group, 0)
    wait_group((n_groups - 1) % 2)


def _dispatch(h, dest, pad_end):
    n = h.shape[0]
    return pl.pallas_call(
        _dispatch_kernel,
        grid=(n // DP_TM,),
        in_specs=[
            pl.BlockSpec((1, 1, TOP_K * DP_TM), lambda i: (i, 0, 0), memory_space=pltpu.SMEM),
            pl.BlockSpec(memory_space=pltpu.SMEM),
            pl.BlockSpec((DP_TM, D_MODEL), lambda i: (i, 0)),
        ],
        out_specs=pl.BlockSpec(memory_space=pl.ANY),
        out_shape=jax.ShapeDtypeStruct((MOE_ROWS, D_MODEL), F32),
        scratch_shapes=[pltpu.VMEM((MOE_BM, D_MODEL), F32), pltpu.SemaphoreType.DMA((2,)),
                        pltpu.SemaphoreType.DMA((1,))],
        compiler_params=_cparams(("arbitrary",)),
        name="moe_dispatch",
    )(_tile_major(dest, DP_TM), pad_end, h)


def _expert_kernel(bx_ref, be_ref, nv_ref, x_ref, w1_ref, b1_ref, w2_ref, b2_ref, o_ref, w1b, w2b):
    i = pl.program_id(0)
    changed = jnp.logical_or(i == 0, be_ref[i] != be_ref[jnp.maximum(i - 1, 0)])

    @pl.when(changed)
    def _():
        w1b[...] = w1_ref[0, 0].astype(BF16)
        w2b[...] = w2_ref[0, 0].astype(BF16)

    def ffn(rows):
        xb = x_ref[0:rows, :].astype(BF16)
        b1 = b1_ref[0, 0]
        acc = None
        for c0 in range(0, D_FF, FF_CHUNK):
            hg = _dg(xb, w1b[:, c0:c0 + FF_CHUNK], NN) + b1[:, c0:c0 + FF_CHUNK]
            hu = _dg(xb, w1b[:, D_FF + c0:D_FF + c0 + FF_CHUNK], NN) + b1[:, D_FF + c0:D_FF + c0 + FF_CHUNK]
            hg = jnp.minimum(hg, SWIGLU_LIMIT)
            hu = jnp.clip(hu, -SWIGLU_LIMIT, SWIGLU_LIMIT)
            act = hg * _sigmoid(SWIGLU_ALPHA * hg) * (hu + 1.0)
            part = _dg(act.astype(BF16), w2b[c0:c0 + FF_CHUNK, :], NN)
            acc = part if acc is None else acc + part
        o_ref[0:rows, :] = acc + b2_ref[0, 0]

    nv = nv_ref[i]
    half = MOE_BM // 2

    @pl.when(nv > half)
    def _():
        ffn(MOE_BM)

    @pl.when(jnp.logical_and(nv > 0, nv <= half))
    def _():
        ffn(half)
        o_ref[half:, :] = jnp.zeros((MOE_BM - half, D_MODEL), F32)

    @pl.when(nv == 0)
    def _():
        o_ref[...] = jnp.zeros_like(o_ref)


def _experts(x_rows, blk_x, blk_e, n_valid, layer, w1, b1, w2, b2):
    nb = MOE_ROWS // MOE_BM
    grid_spec = pltpu.PrefetchScalarGridSpec(
        num_scalar_prefetch=3,
        grid=(nb,),
        in_specs=[
            pl.BlockSpec((MOE_BM, D_MODEL), lambda i, bx, be, nv: (bx[i], 0)),
            pl.BlockSpec((1, 1, D_MODEL, 2 * D_FF), lambda i, bx, be, nv: (layer, be[i], 0, 0)),
            pl.BlockSpec((1, 1, 1, 2 * D_FF), lambda i, bx, be, nv: (layer, be[i], 0, 0)),
            pl.BlockSpec((1, 1, D_FF, D_MODEL), lambda i, bx, be, nv: (layer, be[i], 0, 0)),
            pl.BlockSpec((1, 1, 1, D_MODEL), lambda i, bx, be, nv: (layer, be[i], 0, 0)),
        ],
        out_specs=pl.BlockSpec((MOE_BM, D_MODEL), lambda i, bx, be, nv: (i, 0)),
        scratch_shapes=[pltpu.VMEM((D_MODEL, 2 * D_FF), BF16), pltpu.VMEM((D_FF, D_MODEL), BF16)],
    )
    return pl.pallas_call(
        _expert_kernel,
        grid_spec=grid_spec,
        out_shape=jax.ShapeDtypeStruct((MOE_ROWS, D_MODEL), F32),
        compiler_params=pltpu.CompilerParams(dimension_semantics=("arbitrary",),
                                             vmem_limit_bytes=56 * 1024 * 1024),
        name="moe_experts",
    )(blk_x, blk_e, n_valid, x_rows, w1, b1.reshape(DEPTH, N_EXPERTS, 1, 2 * D_FF), w2,
      b2.reshape(DEPTH, N_EXPERTS, 1, D_MODEL))


def _combine_kernel(dest_ref, gate_ref, x_ref, mod_ref, fw_ref, y_hbm, o_ref, buf, sem, *, final):
    def issue(t, c):
        for k in range(TOP_K):
            dst = dest_ref[0, 0, k * CB_TM + t]
            pltpu.make_async_copy(y_hbm.at[pl.ds(dst, 1)], buf.at[k, pl.ds(t, 1)], sem.at[0]).start()
        return c

    lax.fori_loop(0, CB_TM, issue, 0)
    for k in range(TOP_K):
        pltpu.make_async_copy(y_hbm.at[pl.ds(0, CB_TM)], buf.at[k], sem.at[0]).wait()
    g = gate_ref[...]
    acc = g[:, 0:1] * buf[0]
    for k in range(1, TOP_K):
        acc = acc + g[:, k:k + 1] * buf[k]
    gate2 = _mod_row(mod_ref, pl.program_id(0) * CB_TM)[:, 5 * D_MODEL:6 * D_MODEL]
    x2 = x_ref[...] + gate2 * acc
    o_ref[...] = _rms(x2, fw_ref[...]) if final else x2


def _combine(y_rows, dest, gates, x, mod, final_w, final):
    n = x.shape[0]
    full = pl.BlockSpec((CB_TM, D_MODEL), lambda i: (i, 0))
    return pl.pallas_call(
        functools.partial(_combine_kernel, final=final),
        grid=(n // CB_TM,),
        in_specs=[
            pl.BlockSpec((1, 1, TOP_K * CB_TM), lambda i: (i, 0, 0), memory_space=pltpu.SMEM),
            pl.BlockSpec((CB_TM, TOP_K), lambda i: (i, 0)),
            full,
            pl.BlockSpec((MOD_ROWS, 6 * D_MODEL), lambda i: (0, 0)),
            pl.BlockSpec((1, D_MODEL), lambda i: (0, 0)),
            pl.BlockSpec(memory_space=pl.ANY),
        ],
        out_specs=full,
        out_shape=jax.ShapeDtypeStruct((n, D_MODEL), F32),
        scratch_shapes=[pltpu.VMEM((TOP_K, CB_TM, D_MODEL), F32), pltpu.SemaphoreType.DMA((1,))],
        compiler_params=_cparams(("arbitrary",)),
        name="moe_combine",
    )(_tile_major(dest, CB_TM), gates.T, x, mod, final_w.reshape(1, D_MODEL), y_rows)


def _routed_ffn(h, x, mod, p, moe, layer, final_w, final):
    idx, gates, rank, counts = _router(h, p['router_w'], p['router_b'])
    counts = counts[:, 0].astype(jnp.int32)
    padded = (counts + MOE_BM - 1) // MOE_BM * MOE_BM
    pad_end = jnp.cumsum(padded)
    pad_start = pad_end - padded
    experts = jnp.arange(N_EXPERTS, dtype=jnp.int32)
    start_of = jnp.sum(jnp.where(idx[:, :, None] == experts, pad_start, 0), axis=-1)
    dest = start_of + rank
    nb = MOE_ROWS // MOE_BM
    first_row = jnp.arange(nb, dtype=jnp.int32) * MOE_BM
    blk_e = jnp.minimum(jnp.sum((pad_end[None, :] <= first_row[:, None]).astype(jnp.int32), axis=1),
                        N_EXPERTS - 1)
    is_e = blk_e[:, None] == experts[None, :]
    end_of = jnp.sum(jnp.where(is_e, pad_start + counts, 0), axis=1)
    n_valid = jnp.clip(end_of - first_row, 0, MOE_BM).astype(jnp.int32)
    blk_x = jnp.minimum(jnp.arange(nb, dtype=jnp.int32), pad_end[-1] // MOE_BM - 1)
    x_rows = _dispatch(h, dest, pad_end)
    y_rows = _experts(x_rows, blk_x, blk_e, n_valid, layer, *moe)
    return _combine(y_rows, dest, gates, x, mod, final_w, final)


_LAYER_PARAMS = ('norm_mix_w', 'norm_ffn_w', 'da_lambda', 'da_norm_w', 'rw_shift', 'rw_w0', 'rw_w_up', 'rw_a0',
                 'rw_a_up', 'rw_g_up', 'rw_k_k', 'rw_k_a', 'rw_r_k', 'rw_ln_w', 'rw_ln_b', 'ret_decay_logit',
                 'ret_norm_w', 'cv_dw_w', 'cv_dw_b', 'cv_ln_w', 'cv_ln_b', 'w_branch', 'w_out', 'router_w',
                 'router_b')


def _layer(x, mod, w_p, p, moe, layer, lam_init, caches, tables, final_w, final):
    cache_k, cache_v, state_rw, state_ret = caches
    proj, gates = _input_projection(x, mod, p['norm_mix_w'], w_p, layer)
    o_da = _da_branch(proj, cache_k, cache_v, p['da_lambda'], p['da_norm_w'], lam_init, tables)
    o_rw, rw_state = _rwkv_branch(proj, state_rw, p)
    o_ret, ret_state = _retention_branch(proj, state_ret, p)
    o_cv = _conv_branch(proj, p)
    x1, h2 = _merge((o_da, o_rw, o_ret, o_cv), gates, x, mod, p['w_branch'], p['w_out'], p['norm_ffn_w'])
    x2 = _routed_ffn(h2, x1, mod, p, moe, layer, final_w, final)
    new_k = proj[:N_CTX, P_DA + DA_W:P_DA + 2 * DA_W].reshape(BATCH, SEQ, DA_HEADS, 2, DA_QK)
    new_v = proj[:N_CTX, P_DA + 2 * DA_W:P_DA + 3 * DA_W].reshape(BATCH, SEQ, DA_HEADS, DA_V)
    return x2, (new_k, new_v, rw_state[:BATCH], ret_state[:BATCH])


def kernel(x_prompt, x_sample, c, cache_da_k, cache_da_v, state_rwkv, state_ret, c_ctx, ada_w, ada_b, norm_mix_w,
           norm_ffn_w, w_in, da_lambda, da_norm_w, rw_shift, rw_w0, rw_w_up, rw_a0, rw_a_up, rw_g_up, rw_k_k,
           rw_k_a, rw_r_k, rw_ln_w, rw_ln_b, ret_decay_logit, ret_norm_w, cv_dw_w, cv_dw_b, cv_ln_w, cv_ln_b,
           w_branch, w_out, router_w, router_b, moe_w1, moe_b1, moe_w2, moe_b2, final_norm_w):
    weights = dict(norm_mix_w=norm_mix_w, norm_ffn_w=norm_ffn_w, da_lambda=da_lambda, da_norm_w=da_norm_w,
                   rw_shift=rw_shift, rw_w0=rw_w0, rw_w_up=rw_w_up, rw_a0=rw_a0, rw_a_up=rw_a_up, rw_g_up=rw_g_up,
                   rw_k_k=rw_k_k, rw_k_a=rw_k_a, rw_r_k=rw_r_k, rw_ln_w=rw_ln_w, rw_ln_b=rw_ln_b,
                   ret_decay_logit=ret_decay_logit, ret_norm_w=ret_norm_w, cv_dw_w=cv_dw_w, cv_dw_b=cv_dw_b,
                   cv_ln_w=cv_ln_w, cv_ln_b=cv_ln_b, w_branch=w_branch, w_out=w_out, router_w=router_w,
                   router_b=router_b)
    x = jnp.concatenate([x_prompt.reshape(N_CTX, D_MODEL), x_sample.reshape(N_LAT, D_MODEL)], axis=0)
    cvec = jnp.concatenate([c_ctx[None, :], c, jnp.zeros((MOD_ROWS - 1 - DEC_BATCH, D_MODEL), F32)], axis=0)
    mod = _modulation(cvec, ada_w, ada_b)
    w_p = _pad_w_in(w_in).astype(BF16)
    tables = _rope_tables()
    outs = []
    moe = (moe_w1, moe_b1, moe_w2, moe_b2)
    for i in range(DEPTH):
        p = {name: weights[name][i] for name in _LAYER_PARAMS}
        lam_init = 0.8 - 0.6 * math.exp(-0.3 * i)
        caches = (cache_da_k[:, i], cache_da_v[:, i], state_rwkv[:, i], state_ret[:, i])
        x, ctx_out = _layer(x, mod[i], w_p, p, moe, i, lam_init, caches, tables, final_norm_w, i == DEPTH - 1)
        outs.append(ctx_out)
    y_prompt = x[:N_CTX].reshape(BATCH, SEQ, D_MODEL)
    y_sample = x[N_CTX:].reshape(DEC_BATCH, DEC_SEQ, D_MODEL)
    stack = lambda j: jnp.stack([o[j] for o in outs], axis=1)
    return (y_prompt, y_sample, stack(0), stack(1), stack(2), stack(3))


def _pad_w_in(w_in):
    da, rw, ret, cv, gate = jnp.split(w_in, [1536, 3456, 4992, 6016], axis=-1)
    pad = jnp.zeros(w_in.shape[:-1] + (P_DA - RW_COLS,), w_in.dtype)
    return jnp.concatenate([rw, pad, da, ret, cv, gate], axis=-1)
```

```python
import functools
import math

import jax
import jax.numpy as jnp
from jax import lax
from jax.experimental import pallas as pl
from jax.experimental.pallas import tpu as pltpu

F32 = jnp.float32
BF16 = jnp.bfloat16

D_MODEL = 1024
BATCH = 16
SEQ = 256
DEPTH = 2
DEC_BATCH = 2
DEC_SEQ = 4096
PAST_LEN = 512
GRID_W = 64
EPS = 1e-6

DA_HEADS = 4
DA_QK = 64
DA_V = 128
ROPE_BASE = 10000.0

RW_HEADS = 8
RW_HD = 64
RW_W = 512
RW_LORA = 64
RW_G_LORA = 128
RW_GN_EPS = 64e-5

RET_HEADS = 4
RET_DK = 64
RET_DV = 128
RET_CHUNK = 128

CV_W = 512
CONV_K = 31
BR_W = 512
N_BRANCH = 4

N_EXPERTS = 32
TOP_K = 4
D_FF = 1024
SWIGLU_LIMIT = 7.0
SWIGLU_ALPHA = 1.702

N_CTX = BATCH * SEQ
N_LAT = DEC_BATCH * DEC_SEQ
N_TOK = N_CTX + N_LAT
N_SEQS = BATCH + DEC_BATCH
MOD_ROWS = 8
MOD_GROUP = 4096

RW_COLS = 1920
P_RW = 0
P_DA = 2048
P_RET = 3584
P_CV = 5120
P_GATE = 6144
P_COLS = 10240

VMEM_LIMIT = 48 * 1024 * 1024


def _cparams(sem):
    return pltpu.CompilerParams(dimension_semantics=sem, vmem_limit_bytes=VMEM_LIMIT)


def _dg(a, b, dims):
    return lax.dot_general(a, b, (dims, ((), ())), preferred_element_type=F32)


NN = ((1,), (0,))
NT = ((1,), (1,))
TN = ((0,), (0,))


def _dot(a, b, dims=NN):
    return _dg(a.astype(BF16), b.astype(BF16), dims)


def _split(x):
    hi = x.astype(BF16)
    lo = (x - hi.astype(F32)).astype(BF16)
    return hi, lo


def _dot3(a, b, dims=NN):
    ah, al = _split(a)
    bh, bl = _split(b)
    return _dg(ah, bh, dims) + (_dg(ah, bl, dims) + _dg(al, bh, dims))


def _dot2x(a, e, dims=NN):
    ah, al = _split(a)
    am = (a - ah.astype(F32) - al.astype(F32)).astype(BF16)
    eb = e.astype(BF16)
    return _dg(ah, eb, dims) + (_dg(al, eb, dims) + _dg(am, eb, dims))


def _sigmoid(x):
    return 1.0 / (1.0 + jnp.exp(-x))


def _softplus(x):
    return jnp.maximum(x, 0.0) + jnp.log(1.0 + jnp.exp(-jnp.abs(x)))


def _seq_pos(row):
    in_ctx = row < N_CTX
    pos = jnp.where(in_ctx, row & (SEQ - 1), (row - N_CTX) & (DEC_SEQ - 1))
    length = jnp.where(in_ctx, SEQ, DEC_SEQ)
    return pos, length


def _seq_index(row):
    return jnp.where(row < N_CTX, row // SEQ, BATCH + (row - N_CTX) // DEC_SEQ)


def _dotx(e, b, dims=NN):
    bh, bl = _split(b)
    bm = (b - bh.astype(F32) - bl.astype(F32)).astype(BF16)
    eb = e.astype(BF16)
    return _dg(eb, bh, dims) + (_dg(eb, bl, dims) + _dg(eb, bm, dims))


def _mod_kernel(c_ref, w_ref, b_ref, o_ref):
    c = c_ref[...]
    s = c * _sigmoid(c)
    o_ref[0] = _dot3(s, w_ref[0]) + b_ref[0]


def _modulation(cvec, ada_w, ada_b):
    tn = 1536
    return pl.pallas_call(
        _mod_kernel,
        grid=(DEPTH, 6 * D_MODEL // tn),
        in_specs=[
            pl.BlockSpec((MOD_ROWS, D_MODEL), lambda l, j: (0, 0)),
            pl.BlockSpec((1, D_MODEL, tn), lambda l, j: (l, 0, j)),
            pl.BlockSpec((1, 1, tn), lambda l, j: (l, 0, j)),
        ],
        out_specs=pl.BlockSpec((1, MOD_ROWS, tn), lambda l, j: (l, 0, j)),
        out_shape=jax.ShapeDtypeStruct((DEPTH, MOD_ROWS, 6 * D_MODEL), F32),
        compiler_params=_cparams(("parallel", "parallel")),
        name="modulation",
    )(cvec, ada_w, ada_b.reshape(DEPTH, 1, 6 * D_MODEL))


def _mod_row(mod_ref, first_row):
    g = first_row // MOD_GROUP
    return mod_ref[pl.ds(g, 1), :]


def _rms(x, w):
    return x * lax.rsqrt(jnp.mean(x * x, axis=-1, keepdims=True) + EPS) * w


IN_TM = 1024
IN_TN = 1024


IN_NA = P_GATE // IN_TN


def _inproj_kernel(x_ref, mod_ref, nw_ref, w_ref, oa_ref, og_ref, h_ref):
    i = pl.program_id(0)
    j = pl.program_id(1)

    @pl.when(j == 0)
    def _():
        m = _mod_row(mod_ref, i * IN_TM)
        sh = m[:, 0:D_MODEL]
        sc = m[:, D_MODEL:2 * D_MODEL]
        h_ref[...] = (_rms(x_ref[...], nw_ref[...]) * (1.0 + sc) + sh).astype(BF16)

    acc = _dg(h_ref[...], w_ref[0].astype(BF16), NN)

    @pl.when(j < IN_NA)
    def _():
        oa_ref[...] = acc

    @pl.when(j >= IN_NA)
    def _():
        og_ref[...] = acc.astype(BF16)


def _input_projection(x, mod, norm_w, w_p, layer):
    n = x.shape[0]
    return pl.pallas_call(
        _inproj_kernel,
        grid=(n // IN_TM, P_COLS // IN_TN),
        in_specs=[
            pl.BlockSpec((IN_TM, D_MODEL), lambda i, j: (i, 0)),
            pl.BlockSpec((MOD_ROWS, 6 * D_MODEL), lambda i, j: (0, 0)),
            pl.BlockSpec((1, D_MODEL), lambda i, j: (0, 0)),
            pl.BlockSpec((1, D_MODEL, IN_TN), lambda i, j: (layer, 0, j)),
        ],
        out_specs=[pl.BlockSpec((IN_TM, IN_TN), lambda i, j: (i, jnp.minimum(j, IN_NA - 1))),
                   pl.BlockSpec((IN_TM, IN_TN), lambda i, j: (i, jnp.maximum(j - IN_NA, 0)))],
        out_shape=[jax.ShapeDtypeStruct((n, P_GATE), F32), jax.ShapeDtypeStruct((n, P_COLS - P_GATE), BF16)],
        scratch_shapes=[pltpu.VMEM((IN_TM, D_MODEL), BF16)],
        compiler_params=_cparams(("parallel", "arbitrary")),
        name="input_projection",
    )(x, mod, norm_w.reshape(1, D_MODEL), w_p)


QK_TM = 512
DA_W = DA_HEADS * 2 * DA_QK


def _qkprep_kernel(q_ref, k_ref, v_ref, c_ref, se_ref, so_ref, qo_ref, ko_ref, vo_ref):
    i = pl.program_id(0)
    for h in range(DA_HEADS):
        vo_ref[:, h * 2 * DA_V:h * 2 * DA_V + DA_V] = v_ref[:, h * DA_V:(h + 1) * DA_V].astype(BF16)
        vo_ref[:, h * 2 * DA_V + DA_V:(h + 1) * 2 * DA_V] = jnp.ones((QK_TM, DA_V), BF16)
    scale = DA_QK ** -0.5

    @pl.when(i * QK_TM < N_CTX)
    def _():
        qo_ref[...] = (q_ref[...] * scale).astype(BF16)
        ko_ref[...] = k_ref[...].astype(BF16)

    @pl.when(i * QK_TM >= N_CTX)
    def _():
        c = c_ref[...]
        se = se_ref[...]
        so = so_ref[...]

        def rope(x):
            nxt = pltpu.roll(x, DA_W - 1, axis=1)
            prv = pltpu.roll(x, 1, axis=1)
            return x * c + nxt * se + prv * so

        qo_ref[...] = (rope(q_ref[...]) * scale).astype(BF16)
        ko_ref[...] = rope(k_ref[...]).astype(BF16)


def _rope_tables():
    rows = DEC_SEQ // GRID_W
    row = jnp.repeat(jnp.arange(rows, dtype=F32), GRID_W)
    col = jnp.tile(jnp.arange(GRID_W, dtype=F32), rows)
    n_pairs = DA_QK // 4
    inv = ROPE_BASE ** (-jnp.arange(n_pairs, dtype=F32) / n_pairs)
    ang = jnp.concatenate([row[:, None] * inv, col[:, None] * inv], axis=-1)
    cos = jnp.repeat(jnp.cos(ang), 2, axis=-1)
    sin = jnp.repeat(jnp.sin(ang), 2, axis=-1)
    even = (jnp.arange(DA_QK) % 2 == 0)[None, :]
    s_even = jnp.where(even, -sin, 0.0)
    s_odd = jnp.where(even, 0.0, sin)
    rep = lambda t: jnp.tile(t, (1, DA_W // DA_QK))
    return rep(cos), rep(s_even), rep(s_odd)


def _qk_prepare(proj, tables):
    n = proj.shape[0]
    lat0 = N_CTX // QK_TM
    nlat = DEC_SEQ // QK_TM
    tab = pl.BlockSpec((QK_TM, DA_W), lambda i: (jnp.maximum(i - lat0, 0) % nlat, 0))
    c0 = P_DA // DA_W
    out = jax.ShapeDtypeStruct((n, DA_W), BF16)
    return pl.pallas_call(
        _qkprep_kernel,
        grid=(n // QK_TM,),
        in_specs=[
            pl.BlockSpec((QK_TM, DA_W), lambda i: (i, c0)),
            pl.BlockSpec((QK_TM, DA_W), lambda i: (i, c0 + 1)),
            pl.BlockSpec((QK_TM, DA_W), lambda i: (i, c0 + 2)),
            tab, tab, tab,
        ],
        out_specs=[pl.BlockSpec((QK_TM, DA_W), lambda i: (i, 0))] * 2
        + [pl.BlockSpec((QK_TM, 2 * DA_W), lambda i: (i, 0))],
        out_shape=[out, out, jax.ShapeDtypeStruct((n, 2 * DA_W), BF16)],
        compiler_params=_cparams(("parallel",)),
        name="qk_prepare",
    )(proj, proj, proj, *tables)


DA_TQ = 256


def _da_kernel(q_ref, k_ref, v_ref, dl_ref, nw_ref, o_ref, *, lam_init):
    dl = dl_ref[...]
    lam = (jnp.exp(jnp.sum(dl[0:1] * dl[1:2], axis=1, keepdims=True))
           - jnp.exp(jnp.sum(dl[2:3] * dl[3:4], axis=1, keepdims=True)) + lam_init)
    nw = nw_ref[...] * (1.0 - lam_init)
    q = q_ref[0]
    for h in range(DA_HEADS):
        vext = v_ref[0, :, h * 2 * DA_V:(h + 1) * 2 * DA_V]
        os = []
        for m in range(2):
            c0 = (2 * h + m) * DA_QK
            s = _dg(q[:, c0:c0 + DA_QK], k_ref[0, :, c0:c0 + DA_QK], NT)
            e = jnp.exp(s - jnp.max(s, axis=-1, keepdims=True)).astype(BF16)
            oe = _dg(e, vext, NN)
            os.append(oe[:, :DA_V] * (1.0 / oe[:, DA_V:DA_V + 1]))
        o_ref[0, :, h * DA_V:(h + 1) * DA_V] = _rms(os[0] - lam * os[1], nw).astype(BF16)


def _diff_attention(q, k, v, da_lambda, da_norm_w, lam_init):
    b, tq, _ = q.shape
    tk = k.shape[1]
    return pl.pallas_call(
        functools.partial(_da_kernel, lam_init=lam_init),
        grid=(b, tq // DA_TQ),
        in_specs=[
            pl.BlockSpec((1, DA_TQ, DA_W), lambda i, j: (i, j, 0)),
            pl.BlockSpec((1, tk, DA_W), lambda i, j: (i, 0, 0), pipeline_mode=pl.Buffered(1)),
            pl.BlockSpec((1, tk, 2 * DA_W), lambda i, j: (i, 0, 0), pipeline_mode=pl.Buffered(1)),
            pl.BlockSpec((4, DA_QK), lambda i, j: (0, 0)),
            pl.BlockSpec((1, DA_V), lambda i, j: (0, 0)),
        ],
        out_specs=pl.BlockSpec((1, DA_TQ, DA_W), lambda i, j: (i, j, 0)),
        out_shape=jax.ShapeDtypeStruct((b, tq, DA_W), BF16),
        compiler_params=_cparams(("parallel", "arbitrary")),
        name="diff_attention",
    )(q, k, v, da_lambda, da_norm_w.reshape(1, DA_V))


def _da_branch(proj, cache_k, cache_v, da_lambda, da_norm_w, lam_init, tables):
    qb, kb, vb = _qk_prepare(proj, tables)
    ctx = lambda t: t[:N_CTX].reshape(BATCH, SEQ, t.shape[-1])
    lat = lambda t: t[N_CTX:].reshape(DEC_BATCH, DEC_SEQ, t.shape[-1])
    cv = cache_v.astype(BF16)
    cv = jnp.concatenate([cv, jnp.ones_like(cv)], axis=-1).reshape(DEC_BATCH, PAST_LEN, 2 * DA_W)
    o_ctx = _diff_attention(ctx(qb), ctx(kb), ctx(vb), da_lambda, da_norm_w, lam_init)
    k_all = jnp.concatenate([cache_k.reshape(DEC_BATCH, PAST_LEN, DA_W).astype(BF16), lat(kb)], axis=1)
    v_all = jnp.concatenate([cv, lat(vb)], axis=1)
    o_lat = _diff_attention(lat(qb), k_all, v_all, da_lambda, da_norm_w, lam_init)
    return jnp.concatenate([o_ctx.reshape(N_CTX, DA_W), o_lat.reshape(N_LAT, DA_W)], axis=0)


RW_TM = 256
RW_C = 64
RW_PAIRS = RW_HEADS // 2
HALO = 8


def _head_ones():
    idx = jnp.arange(2 * RW_HD) // RW_HD
    return (idx[:, None] == idx[None, :]).astype(BF16)


def _head_sum(x, ones_pair):
    w = 2 * RW_HD
    return jnp.concatenate([_dot2x(x[:, p * w:(p + 1) * w], ones_pair) for p in range(RW_PAIRS)], axis=1)


def _rwprep_kernel(x_ref, xp_ref, xn_ref, mu_ref, w0_ref, wup_ref, a0_ref, aup_ref, gup_ref, kk_ref, ka_ref,
                   ones_ref, r_o, v_o, kk_o, g_o, kd_o, lw_o, a_o, buf):
    row0 = pl.program_id(0) * RW_TM
    x = x_ref[:, 0:RW_COLS]
    buf[HALO:HALO + RW_TM, :] = x
    buf[HALO - 1:HALO, :] = xp_ref[HALO - 1:HALO, 0:RW_COLS]
    buf[HALO + RW_TM:HALO + RW_TM + 1, :] = xn_ref[0:1, 0:RW_COLS]
    rows = row0 + lax.broadcasted_iota(jnp.int32, (RW_TM, 1), 0)
    pos, length = _seq_pos(rows)
    prev = jnp.where(pos == 0, 0.0, buf[HALO - 1:HALO - 1 + RW_TM, :])
    nxt = jnp.where(pos == length - 1, 0.0, buf[HALO + 1:HALO + 1 + RW_TM, :])
    mu = mu_ref[...]
    u = x + mu[0:1] * (prev - x) + mu[1:2] * (nxt - x)

    r = u[:, 0:RW_W]
    k = u[:, RW_W:2 * RW_W]
    v = u[:, 2 * RW_W:3 * RW_W]
    wl = u[:, 3 * RW_W:3 * RW_W + 128]
    al = u[:, 3 * RW_W + 128:3 * RW_W + 256]
    gl = u[:, 3 * RW_W + 256:3 * RW_W + 384]
    w_raw = w0_ref[...] + _dot3(jnp.tanh(wl), wup_ref[...])
    lw = -jnp.exp(-_softplus(-w_raw) - 0.5)
    a = _sigmoid(a0_ref[...] + _dot3(al, aup_ref[...]))
    g = _dot3(_sigmoid(gl), gup_ref[...])
    kk = k * kk_ref[...]
    kk = kk * lax.rsqrt(jnp.maximum(_head_sum(kk * kk, ones_ref[...]), 1e-12))
    kd = jnp.concatenate([k, k], axis=1) * (1.0 + (a - 1.0) * ka_ref[...])

    r_o[...] = r
    v_o[...] = v
    kk_o[...] = kk
    g_o[...] = g
    for d in range(2):
        kd_o[d] = kd[:, d * RW_W:(d + 1) * RW_W]
        lw_o[d] = lw[:, d * RW_W:(d + 1) * RW_W]
        a_o[d] = a[:, d * RW_W:(d + 1) * RW_W]


def _rw_prepare(proj, p):
    n = proj.shape[0]
    nh = n // HALO
    steps = RW_TM // HALO
    wide = P_DA - P_RW
    cat2 = lambda t: t.reshape(1, 2 * RW_W)
    blockdiag = lambda t: jnp.concatenate(
        [jnp.concatenate([t[0], jnp.zeros_like(t[0])], axis=1),
         jnp.concatenate([jnp.zeros_like(t[1]), t[1]], axis=1)], axis=0)
    const = lambda shape: pl.BlockSpec(shape, lambda i: (0,) * len(shape))
    row = pl.BlockSpec((RW_TM, RW_W), lambda i: (i, 0))
    row2 = pl.BlockSpec((2, RW_TM, RW_W), lambda i: (0, i, 0))
    o1 = jax.ShapeDtypeStruct((n, RW_W), F32)
    o2 = jax.ShapeDtypeStruct((2, n, RW_W), F32)
    return pl.pallas_call(
        _rwprep_kernel,
        grid=(n // RW_TM,),
        in_specs=[
            pl.BlockSpec((RW_TM, wide), lambda i: (i, 0)),
            pl.BlockSpec((HALO, wide), lambda i: (jnp.maximum(i * steps - 1, 0), 0)),
            pl.BlockSpec((HALO, wide), lambda i: (jnp.minimum((i + 1) * steps, nh - 1), 0)),
            const((2, RW_COLS)), const((1, 2 * RW_W)), const((128, 2 * RW_W)), const((1, 2 * RW_W)),
            const((128, 2 * RW_W)), const((RW_G_LORA, RW_W)), const((1, RW_W)), const((1, 2 * RW_W)),
            const((2 * RW_HD, 2 * RW_HD)),
        ],
        out_specs=[row, row, row, row, row2, row2, row2],
        out_shape=[o1, o1, o1, o1, o2, o2, o2],
        scratch_shapes=[pltpu.VMEM((RW_TM + 2 * HALO, RW_COLS), F32)],
        compiler_params=_cparams(("parallel",)),
        name="rwkv_prepare",
    )(proj, proj, proj, p['rw_shift'], cat2(p['rw_w0']), blockdiag(p['rw_w_up']), cat2(p['rw_a0']),
      blockdiag(p['rw_a_up']), p['rw_g_up'], p['rw_k_k'].reshape(1, RW_W),
      jnp.tile(p['rw_k_a'].reshape(1, RW_W), (1, 2)), _head_ones())


RW_TB = 256
RW_NC = RW_TB // RW_C
RW_SIDE = 8


def _rw_finish(y, r, v, g, kd_sum, rk, lnw, lnb, ones):
    inv = 1.0 / RW_HD
    xc = y - _head_sum(y, ones) * inv
    var = _head_sum(xc * xc, ones) * inv
    yn = xc * lax.rsqrt(var + RW_GN_EPS) * lnw + lnb
    bonus = _head_sum(r * kd_sum * rk, ones) * v
    return ((yn + bonus) * g).astype(BF16)


def _rwscan_kernel(r_ref, v_ref, kk_ref, kd_ref, lw_ref, a_ref, tri_ref, inc_ref, str_ref, s0_ref, *rest, backward):
    if backward:
        yf_ref, g_ref, rk_ref, lnw_ref, lnb_ref, ones_ref, y_ref, st_ref, s_scr = rest
    else:
        y_ref, st_ref, s_scr = rest
    _rwscan_body(r_ref, v_ref, kk_ref, kd_ref, lw_ref, a_ref, tri_ref, inc_ref, str_ref, s0_ref, y_ref, st_ref,
                 s_scr, backward,
                 (lambda yb: _rw_finish(yf_ref[...] + yb, r_ref[...], v_ref[...], g_ref[...],
                                        kd_ref[0] + kd_ref[1], rk_ref[...], lnw_ref[...], lnb_ref[...],
                                        ones_ref[...])) if backward else None)


def _rwscan_body(r_ref, v_ref, kk_ref, kd_ref, lw_ref, a_ref, tri_ref, inc_ref, str_ref, s0_ref, y_ref, st_ref,
                 s_scr, backward, finish):
    step = pl.program_id(0)
    nb = pl.num_programs(0)
    bi = (nb - 1 - step) if backward else step
    pos, length = _seq_pos(bi * RW_TB)
    first = (pos + RW_TB == length) if backward else (pos == 0)
    last = (pos == 0) if backward else (pos + RW_TB == length)
    in_ctx = bi * RW_TB < N_CTX
    c = RW_C

    @pl.when(jnp.logical_and(first, in_ctx))
    def _():
        s_scr[...] = jnp.zeros_like(s_scr)

    @pl.when(jnp.logical_and(first, jnp.logical_not(in_ctx)))
    def _():
        z = jnp.zeros((c, c), F32)
        for p in range(RW_PAIRS):
            s_scr[p] = jnp.concatenate(
                [jnp.concatenate([s0_ref[0, 0, 2 * p], z], axis=1),
                 jnp.concatenate([z, s0_ref[0, 0, 2 * p + 1]], axis=1)], axis=0)

    incl2 = inc_ref[...] > 0.5
    strict2 = str_ref[...] > 0.5
    eye2 = (lax.broadcasted_iota(jnp.int32, (2 * c, 2 * c), 0)
            == lax.broadcasted_iota(jnp.int32, (2 * c, 2 * c), 1))
    m_e = lax.broadcasted_iota(jnp.int32, (1, 2 * c), 1) < c

    lw = lw_ref[0]
    cum = _dotx(tri_ref[...], lw)
    tots = [cum[(j * c if backward else j * c + c - 1):(j * c + 1 if backward else j * c + c), :]
            for j in range(RW_NC)]
    tot_b = jnp.concatenate([jnp.broadcast_to(t, (c, RW_W)) for t in tots], axis=0)
    kk = kk_ref[...]
    kd = kd_ref[1 if backward else 0]
    bp = kk * a_ref[0]
    g_inv = jnp.exp(-cum)
    g_rem = jnp.exp(tot_b - cum)
    ag = -kk * jnp.exp(cum - lw)
    rg = r_ref[...] * jnp.exp(cum)
    bdn = bp * g_inv
    kdn = kd * g_inv
    bc = bp * g_rem
    kc = kd * g_rem
    v = v_ref[...]

    def stack(x, j, p):
        xs = x[j * c:(j + 1) * c, p * 2 * c:(p + 1) * 2 * c]
        return jnp.concatenate([jnp.where(m_e, xs, 0.0), jnp.where(m_e, 0.0, xs)], axis=0).astype(BF16)

    pre = {}
    keys = [(j, p) for j in range(RW_NC) for p in range(RW_PAIRS)]
    for g0 in range(0, len(keys), RW_SIDE):
        grp = keys[g0:g0 + RW_SIDE]
        ops = {k: tuple(stack(t, *k) for t in (ag, rg, bdn, kdn, bc, kc, v)) for k in grp}
        gm = {k: _dg(jnp.concatenate([ops[k][0], ops[k][1]], axis=0),
                     jnp.concatenate([ops[k][2], ops[k][3]], axis=0), NT) for k in grp}
        lbb = {k: jnp.where(strict2, gm[k][:2 * c, :2 * c], 0.0) for k in grp}
        lkb = {k: jnp.where(strict2, gm[k][:2 * c, 2 * c:], 0.0).astype(BF16) for k in grp}
        lrk = {k: jnp.concatenate([jnp.where(incl2, gm[k][2 * c:, :2 * c], 0.0),
                                   jnp.where(incl2, gm[k][2 * c:, 2 * c:], 0.0)], axis=1).astype(BF16)
               for k in grp}
        lv = {k: _dg(lkb[k], ops[k][6], NN) for k in grp}
        x = {k: jnp.where(eye2, 1.0, lbb[k]) for k in grp}
        pw = lbb
        for _ in range(int(math.log2(c)) - 1):
            pwb = {k: pw[k].astype(BF16) for k in grp}
            pw = {k: _dg(pwb[k], pwb[k], NN) for k in grp}
            x = {k: x[k] + _dg(x[k].astype(BF16), pw[k].astype(BF16), NN) for k in grp}
        tw = {k: _dg(x[k].astype(BF16), jnp.concatenate([lv[k].astype(BF16), ops[k][0]], axis=1), NN)
              for k in grp}
        for k in grp:
            pre[k] = (tw[k][:, :2 * c], tw[k][:, 2 * c:].astype(BF16), ops[k][1], lrk[k], ops[k][6],
                      jnp.concatenate([ops[k][4], ops[k][5]], axis=0))

    order = range(RW_NC - 1, -1, -1) if backward else range(RW_NC)
    pairs = range(RW_PAIRS)
    s = [s_scr[p] for p in pairs]
    ys = {}
    for j in order:
        sb = [s[p].astype(BF16) for p in pairs]
        u = [_dg(pre[j, p][1], sb[p], NT) + pre[j, p][0] for p in pairs]
        uv = [jnp.concatenate([u[p].astype(BF16), pre[j, p][4]], axis=0) for p in pairs]
        y = [_dg(pre[j, p][2], sb[p], NT) + _dg(pre[j, p][3], uv[p], NN) for p in pairs]
        s = [s[p] * jnp.exp(tots[j][:, p * 2 * c:(p + 1) * 2 * c]) + _dg(uv[p], pre[j, p][5], TN) for p in pairs]
        ys[j] = [y[p][:c] + y[p][c:] for p in pairs]
    y_blk = jnp.concatenate([jnp.concatenate(ys[j], axis=1) for j in range(RW_NC)], axis=0)
    y_ref[...] = finish(y_blk) if finish else y_blk
    for p in pairs:
        s_scr[p] = s[p]

    @pl.when(last)
    def _():
        for p in pairs:
            st_ref[0, 2 * p] = s[p][:c, :c]
            st_ref[0, 2 * p + 1] = s[p][c:, c:]


def _scan_masks(backward):
    t = jnp.arange(RW_TB)
    sgn = -1 if backward else 1
    same_chunk = (t[:, None] // RW_C) == (t[None, :] // RW_C)
    tri = (same_chunk & ((t[:, None] - t[None, :]) * sgn >= 0)).astype(BF16)
    q = jnp.arange(2 * RW_C)
    same_head = (q[:, None] // RW_C) == (q[None, :] // RW_C)
    dif = ((q[:, None] % RW_C) - (q[None, :] % RW_C)) * sgn
    return tri, (same_head & (dif >= 0)).astype(F32), (same_head & (dif > 0)).astype(F32)


def _rw_scan(r, v, kk, kd, lw, a, s0, backward, finish=()):
    n = r.shape[0]
    nb = n // RW_TB
    d = 1 if backward else 0
    blk = (lambda i: nb - 1 - i) if backward else (lambda i: i)
    row = pl.BlockSpec((RW_TB, RW_W), lambda i: (blk(i), 0))
    row2 = pl.BlockSpec((1, RW_TB, RW_W), lambda i: (d, blk(i), 0))
    both = pl.BlockSpec((2, RW_TB, RW_W), lambda i: (0, blk(i), 0))
    const = lambda shape: pl.BlockSpec(shape, lambda i: (0,) * len(shape))
    vec = const((1, RW_W))
    pair = 2 * RW_C
    extra = [row, row, vec, vec, vec, const((pair, pair))] if backward else []
    return pl.pallas_call(
        functools.partial(_rwscan_kernel, backward=backward),
        grid=(nb,),
        in_specs=[row, row, row, both, row2, row2, const((RW_TB, RW_TB)), const((pair, pair)), const((pair, pair)),
                  pl.BlockSpec((1, 1, RW_HEADS, RW_HD, RW_HD),
                               lambda i: (jnp.maximum(_seq_index(blk(i) * RW_TB) - BATCH, 0), d, 0, 0, 0))] + extra,
        out_specs=[row, pl.BlockSpec((1, RW_HEADS, RW_HD, RW_HD), lambda i: (_seq_index(blk(i) * RW_TB), 0, 0, 0))],
        out_shape=[jax.ShapeDtypeStruct((n, RW_W), BF16 if backward else F32),
                   jax.ShapeDtypeStruct((N_SEQS, RW_HEADS, RW_HD, RW_HD), F32)],
        scratch_shapes=[pltpu.VMEM((RW_PAIRS, pair, pair), F32)],
        compiler_params=_cparams(("arbitrary",)),
        name="rwkv_scan_bwd" if backward else "rwkv_scan_fwd",
    )(r, v, kk, kd, lw, a, *_scan_masks(backward), s0, *finish)


def _rwkv_branch(proj, state0, p):
    r, v, kk, g, kd, lw, a = _rw_prepare(proj, p)
    yf, sf = _rw_scan(r, v, kk, kd, lw, a, state0, backward=False)
    finish = (yf, g, p['rw_r_k'].reshape(1, RW_W), p['rw_ln_w'].reshape(1, RW_W), p['rw_ln_b'].reshape(1, RW_W),
              _head_ones())
    out, sb = _rw_scan(r, v, kk, kd, lw, a, state0, backward=True, finish=finish)
    return out, jnp.stack([sf, sb], axis=1)


RET_W = RET_HEADS * RET_DV
RET_PAIRS = RET_HEADS // 2


RET_TB = 256
RET_NC = RET_TB // RET_CHUNK


def _ret_kernel(q_ref, k_ref, v_ref, lg_ref, s0_ref, o_ref, st_ref, s_scr, *, backward):
    step = pl.program_id(0)
    nb = pl.num_programs(0)
    bi = (nb - 1 - step) if backward else step
    pos, length = _seq_pos(bi * RET_TB)
    first = (pos + RET_TB == length) if backward else (pos == 0)
    in_ctx = bi * RET_TB < N_CTX

    @pl.when(jnp.logical_and(first, in_ctx))
    def _():
        s_scr[...] = jnp.zeros_like(s_scr)

    @pl.when(jnp.logical_and(first, jnp.logical_not(in_ctx)))
    def _():
        s_scr[...] = s0_ref[0, 0]

    c = RET_CHUNK
    d = 1 if backward else 0
    sgn = -1 if backward else 1
    lgs = -_softplus(-lg_ref[d:d + 1, :])
    ri = lax.broadcasted_iota(jnp.int32, (c, c), 0)
    cj = lax.broadcasted_iota(jnp.int32, (c, c), 1)
    dif = (ri - cj) * sgn
    valid = dif >= 0
    dist = jnp.maximum(dif, 0).astype(F32)
    pr = lax.broadcasted_iota(jnp.int32, (c, 2 * RET_DK), 0)
    tau = ((c - 1 - pr) if backward else pr).astype(F32)
    low = lax.broadcasted_iota(jnp.int32, (1, 2 * RET_DK), 1) < RET_DK
    rlow = lax.broadcasted_iota(jnp.int32, (2 * RET_DK, RET_DV), 0) < RET_DK
    heads = range(RET_HEADS)
    pairs = range(RET_PAIRS)
    mask = [low, jnp.logical_not(low)]
    lg_h = [lgs[:, h:h + 1] for h in heads]
    lg_row = [jnp.where(low, lg_h[2 * p], lg_h[2 * p + 1]) for p in pairs]
    dmat = [jnp.where(valid, jnp.exp(lg_h[h] * dist), 0.0) for h in heads]
    q_dec = [jnp.exp(lg_row[p] * (tau + 1.0)) for p in pairs]
    k_dec = [jnp.exp(lg_row[p] * (c - 1.0 - tau)) for p in pairs]
    c_dec = [jnp.where(rlow, jnp.exp(lg_h[2 * p] * c), jnp.exp(lg_h[2 * p + 1] * c)) for p in pairs]

    pre = []
    for j in range(RET_NC):
        rows = slice(j * c, (j + 1) * c)
        qp = [q_ref[rows, p * 128:(p + 1) * 128] for p in pairs]
        kp = [k_ref[rows, p * 128:(p + 1) * 128] * (RET_DK ** -0.5) for p in pairs]
        kpb = [t.astype(BF16) for t in kp]
        vb = [v_ref[rows, h * RET_DV:(h + 1) * RET_DV].astype(BF16) for h in heads]
        att = [_dg(jnp.where(mask[h % 2], qp[h // 2], 0.0).astype(BF16), kpb[h // 2], NT) * dmat[h]
               for h in heads]
        upd = [_dg(jnp.where(mask[h % 2], kp[h // 2] * k_dec[h // 2], 0.0).astype(BF16), vb[h], TN)
               for h in heads]
        inner = [_dg(att[h].astype(BF16), vb[h], NN) for h in heads]
        qd = [jnp.where(mask[h % 2], qp[h // 2] * q_dec[h // 2], 0.0).astype(BF16) for h in heads]
        pre.append((inner, qd, upd))

    s = [s_scr[p] for p in pairs]
    for j in (range(RET_NC - 1, -1, -1) if backward else range(RET_NC)):
        inner, qd, upd = pre[j]
        sb = [t.astype(BF16) for t in s]
        for h in heads:
            o_ref[j * c:(j + 1) * c, h * RET_DV:(h + 1) * RET_DV] = inner[h] + _dg(qd[h], sb[h // 2], NN)
        s = [s[p] * c_dec[p] + upd[2 * p] + upd[2 * p + 1] for p in pairs]
    for p in pairs:
        s_scr[p] = s[p]
        st_ref[0, p] = s[p]


def _ret_scan(proj, logit, s0, backward):
    n = proj.shape[0]
    nb = n // RET_TB
    d = 1 if backward else 0
    blk = (lambda i: nb - 1 - i) if backward else (lambda i: i)
    qk_w = RET_HEADS * RET_DK
    state = (RET_PAIRS, 2 * RET_DK, RET_DV)
    return pl.pallas_call(
        functools.partial(_ret_kernel, backward=backward),
        grid=(nb,),
        in_specs=[
            pl.BlockSpec((RET_TB, qk_w), lambda i: (blk(i), P_RET // qk_w)),
            pl.BlockSpec((RET_TB, qk_w), lambda i: (blk(i), P_RET // qk_w + 1)),
            pl.BlockSpec((RET_TB, RET_W), lambda i: (blk(i), (P_RET + 2 * qk_w) // RET_W)),
            pl.BlockSpec((2, RET_HEADS), lambda i: (0, 0)),
            pl.BlockSpec((1, 1) + state,
                         lambda i: (jnp.maximum(_seq_index(blk(i) * RET_TB) - BATCH, 0), d, 0, 0, 0)),
        ],
        out_specs=[pl.BlockSpec((RET_TB, RET_W), lambda i: (blk(i), 0)),
                   pl.BlockSpec((1,) + state, lambda i: (_seq_index(blk(i) * RET_TB), 0, 0, 0))],
        out_shape=[jax.ShapeDtypeStruct((n, RET_W), F32), jax.ShapeDtypeStruct((N_SEQS,) + state, F32)],
        scratch_shapes=[pltpu.VMEM(state, F32)],
        compiler_params=_cparams(("arbitrary",)),
        name="retention_scan_bwd" if backward else "retention_scan_fwd",
    )(proj, proj, proj, logit, s0)


def _standardize(x, eps):
    mu = jnp.mean(x, axis=-1, keepdims=True)
    xc = x - mu
    return xc * lax.rsqrt(jnp.mean(xc * xc, axis=-1, keepdims=True) + eps)


def _retpost_kernel(of_ref, ob_ref, g_ref, nw_ref, out_ref):
    o = of_ref[...] + ob_ref[...]
    g = g_ref[...]
    for h in range(RET_HEADS):
        sl = slice(h * RET_DV, (h + 1) * RET_DV)
        gh = g[:, sl]
        out_ref[:, sl] = (gh * _sigmoid(gh) * (_standardize(o[:, sl], EPS) * nw_ref[:, sl])).astype(BF16)


def _ret_post(of, ob, proj, norm_w):
    n = proj.shape[0]
    tm = 512
    return pl.pallas_call(
        _retpost_kernel,
        grid=(n // tm,),
        in_specs=[
            pl.BlockSpec((tm, RET_W), lambda i: (i, 0)),
            pl.BlockSpec((tm, RET_W), lambda i: (i, 0)),
            pl.BlockSpec((tm, RET_W), lambda i: (i, (P_RET + 1024) // RET_W)),
            pl.BlockSpec((1, RET_W), lambda i: (0, 0)),
        ],
        out_specs=pl.BlockSpec((tm, RET_W), lambda i: (i, 0)),
        out_shape=jax.ShapeDtypeStruct((n, RET_W), BF16),
        compiler_params=_cparams(("parallel",)),
        name="retention_post",
    )(of, ob, proj, norm_w.reshape(1, RET_W))


def _retention_branch(proj, state0, p):
    s0 = state0.reshape(DEC_BATCH, 2, RET_PAIRS, 2 * RET_DK, RET_DV)
    of, sf = _ret_scan(proj, p['ret_decay_logit'], s0, backward=False)
    ob, sb = _ret_scan(proj, p['ret_decay_logit'], s0, backward=True)
    st = jnp.stack([sf, sb], axis=1).reshape(N_SEQS, 2, RET_HEADS, RET_DK, RET_DV)
    return _ret_post(of, ob, proj, p['ret_norm_w']), st


CV_TM = 256
CV_HALO = 16


def _conv_kernel(a_ref, g_ref, ap_ref, gp_ref, an_ref, gn_ref, w_ref, b_ref, lnw_ref, lnb_ref, o_ref, buf, sbuf):
    row0 = pl.program_id(0) * CV_TM
    pos0, len0 = _seq_pos(row0)
    buf[CV_HALO:CV_HALO + CV_TM, :] = a_ref[...] * _sigmoid(g_ref[...])
    buf[0:CV_HALO, :] = jnp.where(pos0 == 0, 0.0, ap_ref[...] * _sigmoid(gp_ref[...]))
    buf[CV_HALO + CV_TM:, :] = jnp.where(pos0 + CV_TM == len0, 0.0, an_ref[...] * _sigmoid(gn_ref[...]))
    base = CV_HALO - CONV_K // 2
    acc = jnp.zeros((CV_TM, CV_W), F32)
    for r in range(8):
        taps = [m for m in range((base + CONV_K + 7) // 8) if 0 <= r + 8 * m - base < CONV_K]
        span = CV_TM + 8 * max(taps)
        if r:
            sbuf[0:span, :] = buf[r:r + span, :]
        src = sbuf if r else buf
        for m in taps:
            j = r + 8 * m - base
            acc = acc + w_ref[j:j + 1, :] * src[8 * m:8 * m + CV_TM, :]
    z = _standardize(acc + b_ref[...], EPS) * lnw_ref[...] + lnb_ref[...]
    o_ref[...] = (z * _sigmoid(z)).astype(BF16)


def _conv_branch(proj, p):
    n = proj.shape[0]
    nh = n // CV_HALO
    steps = CV_TM // CV_HALO
    ca = P_CV // CV_W
    prev = lambda i: jnp.maximum(i * steps - 1, 0)
    nxt = lambda i: jnp.minimum((i + 1) * steps, nh - 1)
    vec = pl.BlockSpec((1, CV_W), lambda i: (0, 0))
    return pl.pallas_call(
        _conv_kernel,
        grid=(n // CV_TM,),
        in_specs=[
            pl.BlockSpec((CV_TM, CV_W), lambda i: (i, ca)),
            pl.BlockSpec((CV_TM, CV_W), lambda i: (i, ca + 1)),
            pl.BlockSpec((CV_HALO, CV_W), lambda i: (prev(i), ca)),
            pl.BlockSpec((CV_HALO, CV_W), lambda i: (prev(i), ca + 1)),
            pl.BlockSpec((CV_HALO, CV_W), lambda i: (nxt(i), ca)),
            pl.BlockSpec((CV_HALO, CV_W), lambda i: (nxt(i), ca + 1)),
            pl.BlockSpec((CONV_K, CV_W), lambda i: (0, 0)),
            vec, vec, vec,
        ],
        out_specs=pl.BlockSpec((CV_TM, CV_W), lambda i: (i, 0)),
        out_shape=jax.ShapeDtypeStruct((n, CV_W), BF16),
        scratch_shapes=[pltpu.VMEM((CV_TM + 2 * CV_HALO, CV_W), F32)] * 2,
        compiler_params=_cparams(("parallel",)),
        name="conformer_conv",
    )(proj, proj, proj, proj, proj, proj, p['cv_dw_w'], p['cv_dw_b'].reshape(1, CV_W),
      p['cv_ln_w'].reshape(1, CV_W), p['cv_ln_b'].reshape(1, CV_W))


MG_TM = 512


def _merge_kernel(da_ref, rw_ref, ret_ref, cv_ref, g0_ref, g1_ref, g2_ref, g3_ref, x_ref, mod_ref, wb_ref, wo_ref,
                  nw_ref, x_o, h_o):
    m = None
    for n, (br, gt) in enumerate(((da_ref, g0_ref), (rw_ref, g1_ref), (ret_ref, g2_ref), (cv_ref, g3_ref))):
        t = _sigmoid(gt[...].astype(F32)) * _dg(br[...], wb_ref[n], NN)
        m = t if m is None else m + t
    out = _dg(m.astype(BF16), wo_ref[...], NN)
    mrow = _mod_row(mod_ref, pl.program_id(0) * MG_TM)
    gate1 = mrow[:, 2 * D_MODEL:3 * D_MODEL]
    sh2 = mrow[:, 3 * D_MODEL:4 * D_MODEL]
    sc2 = mrow[:, 4 * D_MODEL:5 * D_MODEL]
    x1 = x_ref[...] + gate1 * out
    x_o[...] = x1
    h_o[...] = _rms(x1, nw_ref[...]) * (1.0 + sc2) + sh2


def _merge(branches, gates, x, mod, w_branch, w_out, norm_w):
    n = x.shape[0]
    br = pl.BlockSpec((MG_TM, BR_W), lambda i: (i, 0))
    gspec = lambda j: pl.BlockSpec((MG_TM, D_MODEL), lambda i: (i, j))
    full = pl.BlockSpec((MG_TM, D_MODEL), lambda i: (i, 0))
    out = jax.ShapeDtypeStruct((n, D_MODEL), F32)
    return pl.pallas_call(
        _merge_kernel,
        grid=(n // MG_TM,),
        in_specs=[br, br, br, br, gspec(0), gspec(1), gspec(2), gspec(3), full,
                  pl.BlockSpec((MOD_ROWS, 6 * D_MODEL), lambda i: (0, 0)),
                  pl.BlockSpec((N_BRANCH, BR_W, D_MODEL), lambda i: (0, 0, 0)),
                  pl.BlockSpec((D_MODEL, D_MODEL), lambda i: (0, 0)),
                  pl.BlockSpec((1, D_MODEL), lambda i: (0, 0))],
        out_specs=[full, full],
        out_shape=[out, out],
        compiler_params=_cparams(("parallel",)),
        name="gated_merge",
    )(*branches, gates, gates, gates, gates, x, mod, w_branch.astype(BF16), w_out.astype(BF16),
      norm_w.reshape(1, D_MODEL))


RT_TM = 256
MOE_BM = 512
MOE_ROWS = N_TOK * TOP_K + N_EXPERTS * MOE_BM
DP_TM = 512
DP_GROUP = 64
CB_TM = 256
DMA_UNROLL = 8


def _router_kernel(h_ref, w_ref, b_ref, tri_ref, idx_o, gate_o, rank_o, cnt_o, carry):
    @pl.when(pl.program_id(0) == 0)
    def _():
        carry[...] = jnp.zeros_like(carry)

    logits = _dot3(w_ref[...], h_ref[...], NT) + b_ref[...]
    e_iota = lax.broadcasted_iota(jnp.int32, logits.shape, 0)
    work = logits
    vals, idxs, hots = [], [], []
    for _ in range(TOP_K):
        mx = jnp.max(work, axis=0, keepdims=True)
        ix = jnp.min(jnp.where(work == mx, e_iota, N_EXPERTS), axis=0, keepdims=True)
        hot = e_iota == ix
        vals.append(mx)
        idxs.append(ix)
        hots.append(hot.astype(F32))
        work = jnp.where(hot, -jnp.inf, work)
    es = [jnp.exp(v - vals[0]) for v in vals]
    inv = 1.0 / (es[0] + es[1] + es[2] + es[3])
    chosen = hots[0] + hots[1] + hots[2] + hots[3]
    ahead = carry[...][:, 0:1] + _dg(chosen.astype(BF16), tri_ref[...], NN)
    idx_o[...] = jnp.concatenate(idxs, axis=0)
    gate_o[...] = jnp.concatenate([e * inv for e in es], axis=0)
    rank_o[...] = jnp.concatenate(
        [jnp.sum(hot * ahead, axis=0, keepdims=True) for hot in hots], axis=0).astype(jnp.int32)
    carry[...] = carry[...] + jnp.sum(chosen, axis=1, keepdims=True)
    cnt_o[...] = carry[...]


def _router(h, router_w, router_b):
    n = h.shape[0]
    tri = (jnp.arange(RT_TM)[:, None] < jnp.arange(RT_TM)[None, :]).astype(BF16)
    col = pl.BlockSpec((TOP_K, RT_TM), lambda i: (0, i))
    return pl.pallas_call(
        _router_kernel,
        grid=(n // RT_TM,),
        in_specs=[
            pl.BlockSpec((RT_TM, D_MODEL), lambda i: (i, 0)),
            pl.BlockSpec((N_EXPERTS, D_MODEL), lambda i: (0, 0)),
            pl.BlockSpec((N_EXPERTS, 1), lambda i: (0, 0)),
            pl.BlockSpec((RT_TM, RT_TM), lambda i: (0, 0)),
        ],
        out_specs=[col, col, col, pl.BlockSpec((N_EXPERTS, 128), lambda i: (0, 0))],
        out_shape=[jax.ShapeDtypeStruct((TOP_K, n), jnp.int32), jax.ShapeDtypeStruct((TOP_K, n), F32),
                   jax.ShapeDtypeStruct((TOP_K, n), jnp.int32), jax.ShapeDtypeStruct((N_EXPERTS, 128), F32)],
        scratch_shapes=[pltpu.VMEM((N_EXPERTS, 128), F32)],
        compiler_params=_cparams(("arbitrary",)),
        name="router",
    )(h, router_w.T, router_b.reshape(N_EXPERTS, 1), tri)


def _tile_major(t, tm):
    k, n = t.shape
    return t.reshape(k, n // tm, tm).transpose(1, 0, 2).reshape(n // tm, 1, k * tm)


def _dispatch_kernel(dest_ref, pe_ref, h_ref, o_hbm, zbuf, sem, zsem):
    n_groups = DP_TM // DP_GROUP

    @pl.when(pl.program_id(0) == 0)
    def _():
        zbuf[...] = jnp.zeros_like(zbuf)

        def fill(e):
            end = pe_ref[e]
            begin = pe_ref[e - 1] if e else 0
            return end > begin, pltpu.make_async_copy(
                zbuf, o_hbm.at[pl.ds(pl.multiple_of(jnp.maximum(end - MOE_BM, 0), MOE_BM), MOE_BM)], zsem.at[0])

        for e in range(N_EXPERTS):
            nonempty, cp = fill(e)
            pl.when(nonempty)(cp.start)
        for e in range(N_EXPERTS):
            nonempty, cp = fill(e)
            pl.when(nonempty)(cp.wait)

        def tail(b):
            return pltpu.make_async_copy(zbuf, o_hbm.at[pl.ds(pl.multiple_of(b * MOE_BM, MOE_BM), MOE_BM)],
                                         zsem.at[0])

        first_unused = pe_ref[N_EXPERTS - 1] // MOE_BM
        lax.fori_loop(first_unused, MOE_ROWS // MOE_BM, lambda b, c: (tail(b).start(), c)[1], 0)
        lax.fori_loop(first_unused, MOE_ROWS // MOE_BM, lambda b, c: (tail(b).wait(), c)[1], 0)

    def wait_group(slot):
        pltpu.make_async_copy(h_ref.at[pl.ds(0, TOP_K * DP_GROUP)], o_hbm.at[pl.ds(0, TOP_K * DP_GROUP)],
                              sem.at[slot]).wait()

    def group(gi, carry):
        slot = gi % 2

        def issue(t, c):
            tok = gi * DP_GROUP + t
            for k in range(TOP_K):
                dst = dest_ref[0, 0, k * DP_TM + tok]
                pltpu.make_async_copy(h_ref.at[pl.ds(tok, 1)], o_hbm.at[pl.ds(dst, 1)], sem.at[slot]).start()
            return c

        lax.fori_loop(0, DP_GROUP, issue, 0, unroll=DMA_UNROLL)

        @pl.when(gi > 0)
        def _():
            wait_group(1 - slot)

        return carry

    lax.fori_loop(0, n_groups, group, 0)
    wait_group((n_groups - 1) % 2)


def _dispatch(h, dest, pad_end):
    n = h.shape[0]
    return pl.pallas_call(
        _dispatch_kernel,
        grid=(n // DP_TM,),
        in_specs=[
            pl.BlockSpec((1, 1, TOP_K * DP_TM), lambda i: (i, 0, 0), memory_space=pltpu.SMEM),
            pl.BlockSpec(memory_space=pltpu.SMEM),
            pl.BlockSpec((DP_TM, D_MODEL), lambda i: (i, 0)),
        ],
        out_specs=pl.BlockSpec(memory_space=pl.ANY),
        out_shape=jax.ShapeDtypeStruct((MOE_ROWS, D_MODEL), F32),
        scratch_shapes=[pltpu.VMEM((MOE_BM, D_MODEL), F32), pltpu.SemaphoreType.DMA((2,)),
                        pltpu.SemaphoreType.DMA((1,))],
        compiler_params=_cparams(("arbitrary",)),
        name="moe_dispatch",
    )(_tile_major(dest, DP_TM), pad_end, h)


def _expert_kernel(bx_ref, be_ref, nv_ref, x_ref, w1_ref, b1_ref, w2_ref, b2_ref, o_ref, w1b, w2b):
    i = pl.program_id(0)
    changed = jnp.logical_or(i == 0, be_ref[i] != be_ref[jnp.maximum(i - 1, 0)])

    @pl.when(changed)
    def _():
        w1b[...] = w1_ref[0, 0].astype(BF16)
        w2b[...] = w2_ref[0, 0].astype(BF16)

    def ffn(rows):
        hb = _dg(x_ref[0:rows, :].astype(BF16), w1b[...], NN) + b1_ref[0, 0]
        hg = jnp.minimum(hb[:, :D_FF], SWIGLU_LIMIT)
        hu = jnp.clip(hb[:, D_FF:], -SWIGLU_LIMIT, SWIGLU_LIMIT)
        act = hg * _sigmoid(SWIGLU_ALPHA * hg) * (hu + 1.0)
        o_ref[0:rows, :] = _dg(act.astype(BF16), w2b[...], NN) + b2_ref[0, 0]

    nv = nv_ref[i]
    half = MOE_BM // 2

    @pl.when(nv > half)
    def _():
        ffn(MOE_BM)

    @pl.when(jnp.logical_and(nv > 0, nv <= half))
    def _():
        ffn(half)
        o_ref[half:, :] = jnp.zeros((MOE_BM - half, D_MODEL), F32)

    @pl.when(nv == 0)
    def _():
        o_ref[...] = jnp.zeros_like(o_ref)


def _experts(x_rows, blk_x, blk_e, n_valid, layer, w1, b1, w2, b2):
    nb = MOE_ROWS // MOE_BM
    grid_spec = pltpu.PrefetchScalarGridSpec(
        num_scalar_prefetch=3,
        grid=(nb,),
        in_specs=[
            pl.BlockSpec((MOE_BM, D_MODEL), lambda i, bx, be, nv: (bx[i], 0)),
            pl.BlockSpec((1, 1, D_MODEL, 2 * D_FF), lambda i, bx, be, nv: (layer, be[i], 0, 0)),
            pl.BlockSpec((1, 1, 1, 2 * D_FF), lambda i, bx, be, nv: (layer, be[i], 0, 0)),
            pl.BlockSpec((1, 1, D_FF, D_MODEL), lambda i, bx, be, nv: (layer, be[i], 0, 0)),
            pl.BlockSpec((1, 1, 1, D_MODEL), lambda i, bx, be, nv: (layer, be[i], 0, 0)),
        ],
        out_specs=pl.BlockSpec((MOE_BM, D_MODEL), lambda i, bx, be, nv: (i, 0)),
        scratch_shapes=[pltpu.VMEM((D_MODEL, 2 * D_FF), BF16), pltpu.VMEM((D_FF, D_MODEL), BF16)],
    )
    return pl.pallas_call(
        _expert_kernel,
        grid_spec=grid_spec,
        out_shape=jax.ShapeDtypeStruct((MOE_ROWS, D_MODEL), F32),
        compiler_params=pltpu.CompilerParams(dimension_semantics=("arbitrary",),
                                             vmem_limit_bytes=56 * 1024 * 1024),
        name="moe_experts",
    )(blk_x, blk_e, n_valid, x_rows, w1, b1.reshape(DEPTH, N_EXPERTS, 1, 2 * D_FF), w2,
      b2.reshape(DEPTH, N_EXPERTS, 1, D_MODEL))


def _combine_kernel(dest_ref, gate_ref, x_ref, mod_ref, fw_ref, y_hbm, o_ref, buf, sem, *, final):
    def issue(t, c):
        for k in range(TOP_K):
            dst = dest_ref[0, 0, k * CB_TM + t]
            pltpu.make_async_copy(y_hbm.at[pl.ds(dst, 1)], buf.at[k, pl.ds(t, 1)], sem.at[0]).start()
        return c

    lax.fori_loop(0, CB_TM, issue, 0, unroll=DMA_UNROLL)
    for k in range(TOP_K):
        pltpu.make_async_copy(y_hbm.at[pl.ds(0, CB_TM)], buf.at[k], sem.at[0]).wait()
    g = gate_ref[...]
    acc = g[:, 0:1] * buf[0]
    for k in range(1, TOP_K):
        acc = acc + g[:, k:k + 1] * buf[k]
    gate2 = _mod_row(mod_ref, pl.program_id(0) * CB_TM)[:, 5 * D_MODEL:6 * D_MODEL]
    x2 = x_ref[...] + gate2 * acc
    o_ref[...] = _rms(x2, fw_ref[...]) if final else x2


def _combine(y_rows, dest, gates, x, mod, final_w, final):
    n = x.shape[0]
    full = pl.BlockSpec((CB_TM, D_MODEL), lambda i: (i, 0))
    return pl.pallas_call(
        functools.partial(_combine_kernel, final=final),
        grid=(n // CB_TM,),
        in_specs=[
            pl.BlockSpec((1, 1, TOP_K * CB_TM), lambda i: (i, 0, 0), memory_space=pltpu.SMEM),
            pl.BlockSpec((CB_TM, TOP_K), lambda i: (i, 0)),
            full,
            pl.BlockSpec((MOD_ROWS, 6 * D_MODEL), lambda i: (0, 0)),
            pl.BlockSpec((1, D_MODEL), lambda i: (0, 0)),
            pl.BlockSpec(memory_space=pl.ANY),
        ],
        out_specs=full,
        out_shape=jax.ShapeDtypeStruct((n, D_MODEL), F32),
        scratch_shapes=[pltpu.VMEM((TOP_K, CB_TM, D_MODEL), F32), pltpu.SemaphoreType.DMA((1,))],
        compiler_params=_cparams(("arbitrary",)),
        name="moe_combine",
    )(_tile_major(dest, CB_TM), gates.T, x, mod, final_w.reshape(1, D_MODEL), y_rows)


def _routed_ffn(h, x, mod, p, moe, layer, final_w, final):
    idx, gates, rank, counts = _router(h, p['router_w'], p['router_b'])
    counts = counts[:, 0].astype(jnp.int32)
    padded = (counts + MOE_BM - 1) // MOE_BM * MOE_BM
    pad_end = jnp.cumsum(padded)
    pad_start = pad_end - padded
    experts = jnp.arange(N_EXPERTS, dtype=jnp.int32)
    start_of = jnp.sum(jnp.where(idx[:, :, None] == experts, pad_start, 0), axis=-1)
    dest = start_of + rank
    nb = MOE_ROWS // MOE_BM
    first_row = jnp.arange(nb, dtype=jnp.int32) * MOE_BM
    blk_e = jnp.minimum(jnp.sum((pad_end[None, :] <= first_row[:, None]).astype(jnp.int32), axis=1),
                        N_EXPERTS - 1)
    is_e = blk_e[:, None] == experts[None, :]
    end_of = jnp.sum(jnp.where(is_e, pad_start + counts, 0), axis=1)
    n_valid = jnp.clip(end_of - first_row, 0, MOE_BM).astype(jnp.int32)
    blk_x = jnp.minimum(jnp.arange(nb, dtype=jnp.int32), pad_end[-1] // MOE_BM - 1)
    x_rows = _dispatch(h, dest, pad_end)
    y_rows = _experts(x_rows, blk_x, blk_e, n_valid, layer, *moe)
    return _combine(y_rows, dest, gates, x, mod, final_w, final)


_LAYER_PARAMS = ('norm_mix_w', 'norm_ffn_w', 'da_lambda', 'da_norm_w', 'rw_shift', 'rw_w0', 'rw_w_up', 'rw_a0',
                 'rw_a_up', 'rw_g_up', 'rw_k_k', 'rw_k_a', 'rw_r_k', 'rw_ln_w', 'rw_ln_b', 'ret_decay_logit',
                 'ret_norm_w', 'cv_dw_w', 'cv_dw_b', 'cv_ln_w', 'cv_ln_b', 'w_branch', 'w_out', 'router_w',
                 'router_b')


def _layer(x, mod, w_p, p, moe, layer, lam_init, caches, tables, final_w, final):
    cache_k, cache_v, state_rw, state_ret = caches
    proj, gates = _input_projection(x, mod, p['norm_mix_w'], w_p, layer)
    o_da = _da_branch(proj, cache_k, cache_v, p['da_lambda'], p['da_norm_w'], lam_init, tables)
    o_rw, rw_state = _rwkv_branch(proj, state_rw, p)
    o_ret, ret_state = _retention_branch(proj, state_ret, p)
    o_cv = _conv_branch(proj, p)
    x1, h2 = _merge((o_da, o_rw, o_ret, o_cv), gates, x, mod, p['w_branch'], p['w_out'], p['norm_ffn_w'])
    x2 = _routed_ffn(h2, x1, mod, p, moe, layer, final_w, final)
    new_k = proj[:N_CTX, P_DA + DA_W:P_DA + 2 * DA_W].reshape(BATCH, SEQ, DA_HEADS, 2, DA_QK)
    new_v = proj[:N_CTX, P_DA + 2 * DA_W:P_DA + 3 * DA_W].reshape(BATCH, SEQ, DA_HEADS, DA_V)
    return x2, (new_k, new_v, rw_state[:BATCH], ret_state[:BATCH])


def kernel(x_prompt, x_sample, c, cache_da_k, cache_da_v, state_rwkv, state_ret, c_ctx, ada_w, ada_b, norm_mix_w,
           norm_ffn_w, w_in, da_lambda, da_norm_w, rw_shift, rw_w0, rw_w_up, rw_a0, rw_a_up, rw_g_up, rw_k_k,
           rw_k_a, rw_r_k, rw_ln_w, rw_ln_b, ret_decay_logit, ret_norm_w, cv_dw_w, cv_dw_b, cv_ln_w, cv_ln_b,
           w_branch, w_out, router_w, router_b, moe_w1, moe_b1, moe_w2, moe_b2, final_norm_w):
    weights = dict(norm_mix_w=norm_mix_w, norm_ffn_w=norm_ffn_w, da_lambda=da_lambda, da_norm_w=da_norm_w,
                   rw_shift=rw_shift, rw_w0=rw_w0, rw_w_up=rw_w_up, rw_a0=rw_a0, rw_a_up=rw_a_up, rw_g_up=rw_g_up,
                   rw_k_k=rw_k_k, rw_k_a=rw_k_a, rw_r_k=rw_r_k, rw_ln_w=rw_ln_w, rw_ln_b=rw_ln_b,
                   ret_decay_logit=ret_decay_logit, ret_norm_w=ret_norm_w, cv_dw_w=cv_dw_w, cv_dw_b=cv_dw_b,
                   cv_ln_w=cv_ln_w, cv_ln_b=cv_ln_b, w_branch=w_branch, w_out=w_out, router_w=router_w,
                   router_b=router_b)
    x = jnp.concatenate([x_prompt.reshape(N_CTX, D_MODEL), x_sample.reshape(N_LAT, D_MODEL)], axis=0)
    cvec = jnp.concatenate([c_ctx[None, :], c, jnp.zeros((MOD_ROWS - 1 - DEC_BATCH, D_MODEL), F32)], axis=0)
    mod = _modulation(cvec, ada_w, ada_b)
    w_p = _pad_w_in(w_in).astype(BF16)
    tables = _rope_tables()
    outs = []
    moe = (moe_w1, moe_b1, moe_w2, moe_b2)
    for i in range(DEPTH):
        p = {name: weights[name][i] for name in _LAYER_PARAMS}
        lam_init = 0.8 - 0.6 * math.exp(-0.3 * i)
        caches = (cache_da_k[:, i], cache_da_v[:, i], state_rwkv[:, i], state_ret[:, i])
        x, ctx_out = _layer(x, mod[i], w_p, p, moe, i, lam_init, caches, tables, final_norm_w, i == DEPTH - 1)
        outs.append(ctx_out)
    y_prompt = x[:N_CTX].reshape(BATCH, SEQ, D_MODEL)
    y_sample = x[N_CTX:].reshape(DEC_BATCH, DEC_SEQ, D_MODEL)
    stack = lambda j: jnp.stack([o[j] for o in outs], axis=1)
    return (y_prompt, y_sample, stack(0), stack(1), stack(2), stack(3))


def _pad_w_in(w_in):
    da, rw, ret, cv, gate = jnp.split(w_in, [1536, 3456, 4992, 6016], axis=-1)
    pad = jnp.zeros(w_in.shape[:-1] + (P_DA - RW_COLS,), w_in.dtype)
    return jnp.concatenate([rw, pad, da, ret, cv, gate], axis=-1)
```

```python
import functools
import math

import jax
import jax.numpy as jnp
from jax import lax
from jax.experimental import pallas as pl
from jax.experimental.pallas import tpu as pltpu

F32 = jnp.float32
BF16 = jnp.bfloat16

D_MODEL = 1024
BATCH = 16
SEQ = 256
DEPTH = 2
DEC_BATCH = 2
DEC_SEQ = 4096
PAST_LEN = 512
GRID_W = 64
EPS = 1e-6

DA_HEADS = 4
DA_QK = 64
DA_V = 128
ROPE_BASE = 10000.0

RW_HEADS = 8
RW_HD = 64
RW_W = 512
RW_LORA = 64
RW_G_LORA = 128
RW_GN_EPS = 64e-5

RET_HEADS = 4
RET_DK = 64
RET_DV = 128
RET_CHUNK = 128

CV_W = 512
CONV_K = 31
BR_W = 512
N_BRANCH = 4

N_EXPERTS = 32
TOP_K = 4
D_FF = 1024
SWIGLU_LIMIT = 7.0
SWIGLU_ALPHA = 1.702

N_CTX = BATCH * SEQ
N_LAT = DEC_BATCH * DEC_SEQ
N_TOK = N_CTX + N_LAT
N_SEQS = BATCH + DEC_BATCH
MOD_ROWS = 8
MOD_GROUP = 4096

RW_COLS = 1920
P_RW = 0
P_DA = 2048
P_RET = 3584
P_CV = 5120
P_GATE = 6144
P_COLS = 10240

VMEM_LIMIT = 48 * 1024 * 1024


def _cparams(sem):
    return pltpu.CompilerParams(dimension_semantics=sem, vmem_limit_bytes=VMEM_LIMIT)


def _dg(a, b, dims):
    return lax.dot_general(a, b, (dims, ((), ())), preferred_element_type=F32)


NN = ((1,), (0,))
NT = ((1,), (1,))
TN = ((0,), (0,))


def _dot(a, b, dims=NN):
    return _dg(a.astype(BF16), b.astype(BF16), dims)


def _split(x):
    hi = x.astype(BF16)
    lo = (x - hi.astype(F32)).astype(BF16)
    return hi, lo


def _dot3(a, b, dims=NN):
    ah, al = _split(a)
    bh, bl = _split(b)
    return _dg(ah, bh, dims) + (_dg(ah, bl, dims) + _dg(al, bh, dims))


def _dot2x(a, e, dims=NN):
    ah, al = _split(a)
    am = (a - ah.astype(F32) - al.astype(F32)).astype(BF16)
    eb = e.astype(BF16)
    return _dg(ah, eb, dims) + (_dg(al, eb, dims) + _dg(am, eb, dims))


def _sigmoid(x):
    return 1.0 / (1.0 + jnp.exp(-x))


def _softplus(x):
    return jnp.maximum(x, 0.0) + jnp.log(1.0 + jnp.exp(-jnp.abs(x)))


def _seq_pos(row):
    in_ctx = row < N_CTX
    pos = jnp.where(in_ctx, row & (SEQ - 1), (row - N_CTX) & (DEC_SEQ - 1))
    length = jnp.where(in_ctx, SEQ, DEC_SEQ)
    return pos, length


def _seq_index(row):
    return jnp.where(row < N_CTX, row // SEQ, BATCH + (row - N_CTX) // DEC_SEQ)


def _dotx(e, b, dims=NN):
    bh, bl = _split(b)
    bm = (b - bh.astype(F32) - bl.astype(F32)).astype(BF16)
    eb = e.astype(BF16)
    return _dg(eb, bh, dims) + (_dg(eb, bl, dims) + _dg(eb, bm, dims))


def _mod_kernel(c_ref, w_ref, b_ref, o_ref):
    c = c_ref[...]
    s = c * _sigmoid(c)
    o_ref[0] = _dot3(s, w_ref[0]) + b_ref[0]


def _modulation(cvec, ada_w, ada_b):
    tn = 1536
    return pl.pallas_call(
        _mod_kernel,
        grid=(DEPTH, 6 * D_MODEL // tn),
        in_specs=[
            pl.BlockSpec((MOD_ROWS, D_MODEL), lambda l, j: (0, 0)),
            pl.BlockSpec((1, D_MODEL, tn), lambda l, j: (l, 0, j)),
            pl.BlockSpec((1, 1, tn), lambda l, j: (l, 0, j)),
        ],
        out_specs=pl.BlockSpec((1, MOD_ROWS, tn), lambda l, j: (l, 0, j)),
        out_shape=jax.ShapeDtypeStruct((DEPTH, MOD_ROWS, 6 * D_MODEL), F32),
        compiler_params=_cparams(("parallel", "parallel")),
        name="modulation",
    )(cvec, ada_w, ada_b.reshape(DEPTH, 1, 6 * D_MODEL))


def _mod_row(mod_ref, first_row):
    g = first_row // MOD_GROUP
    return mod_ref[pl.ds(g, 1), :]


def _rms(x, w):
    return x * lax.rsqrt(jnp.mean(x * x, axis=-1, keepdims=True) + EPS) * w


IN_TM = 1024
IN_TN = 1024


IN_NA = P_GATE // IN_TN


def _inproj_kernel(x_ref, mod_ref, nw_ref, w_ref, oa_ref, og_ref, h_ref):
    i = pl.program_id(0)
    j = pl.program_id(1)

    @pl.when(j == 0)
    def _():
        m = _mod_row(mod_ref, i * IN_TM)
        sh = m[:, 0:D_MODEL]
        sc = m[:, D_MODEL:2 * D_MODEL]
        h_ref[...] = (_rms(x_ref[...], nw_ref[...]) * (1.0 + sc) + sh).astype(BF16)

    acc = _dg(h_ref[...], w_ref[0].astype(BF16), NN)

    @pl.when(j < IN_NA)
    def _():
        oa_ref[...] = acc

    @pl.when(j >= IN_NA)
    def _():
        og_ref[...] = acc.astype(BF16)


def _input_projection(x, mod, norm_w, w_p, layer):
    n = x.shape[0]
    return pl.pallas_call(
        _inproj_kernel,
        grid=(n // IN_TM, P_COLS // IN_TN),
        in_specs=[
            pl.BlockSpec((IN_TM, D_MODEL), lambda i, j: (i, 0)),
            pl.BlockSpec((MOD_ROWS, 6 * D_MODEL), lambda i, j: (0, 0)),
            pl.BlockSpec((1, D_MODEL), lambda i, j: (0, 0)),
            pl.BlockSpec((1, D_MODEL, IN_TN), lambda i, j: (layer, 0, j)),
        ],
        out_specs=[pl.BlockSpec((IN_TM, IN_TN), lambda i, j: (i, jnp.minimum(j, IN_NA - 1))),
                   pl.BlockSpec((IN_TM, IN_TN), lambda i, j: (i, jnp.maximum(j - IN_NA, 0)))],
        out_shape=[jax.ShapeDtypeStruct((n, P_GATE), F32), jax.ShapeDtypeStruct((n, P_COLS - P_GATE), BF16)],
        scratch_shapes=[pltpu.VMEM((IN_TM, D_MODEL), BF16)],
        compiler_params=_cparams(("parallel", "arbitrary")),
        name="input_projection",
    )(x, mod, norm_w.reshape(1, D_MODEL), w_p)


QK_TM = 512
DA_W = DA_HEADS * 2 * DA_QK


def _qkprep_kernel(q_ref, k_ref, v_ref, c_ref, se_ref, so_ref, qo_ref, ko_ref, vo_ref):
    i = pl.program_id(0)
    for h in range(DA_HEADS):
        vo_ref[:, h * 2 * DA_V:h * 2 * DA_V + DA_V] = v_ref[:, h * DA_V:(h + 1) * DA_V].astype(BF16)
        vo_ref[:, h * 2 * DA_V + DA_V:(h + 1) * 2 * DA_V] = jnp.ones((QK_TM, DA_V), BF16)
    scale = DA_QK ** -0.5

    @pl.when(i * QK_TM < N_CTX)
    def _():
        qo_ref[...] = (q_ref[...] * scale).astype(BF16)
        ko_ref[...] = k_ref[...].astype(BF16)

    @pl.when(i * QK_TM >= N_CTX)
    def _():
        c = c_ref[...]
        se = se_ref[...]
        so = so_ref[...]

        def rope(x):
            nxt = pltpu.roll(x, DA_W - 1, axis=1)
            prv = pltpu.roll(x, 1, axis=1)
            return x * c + nxt * se + prv * so

        qo_ref[...] = (rope(q_ref[...]) * scale).astype(BF16)
        ko_ref[...] = rope(k_ref[...]).astype(BF16)


def _rope_tables():
    rows = DEC_SEQ // GRID_W
    row = jnp.repeat(jnp.arange(rows, dtype=F32), GRID_W)
    col = jnp.tile(jnp.arange(GRID_W, dtype=F32), rows)
    n_pairs = DA_QK // 4
    inv = ROPE_BASE ** (-jnp.arange(n_pairs, dtype=F32) / n_pairs)
    ang = jnp.concatenate([row[:, None] * inv, col[:, None] * inv], axis=-1)
    cos = jnp.repeat(jnp.cos(ang), 2, axis=-1)
    sin = jnp.repeat(jnp.sin(ang), 2, axis=-1)
    even = (jnp.arange(DA_QK) % 2 == 0)[None, :]
    s_even = jnp.where(even, -sin, 0.0)
    s_odd = jnp.where(even, 0.0, sin)
    rep = lambda t: jnp.tile(t, (1, DA_W // DA_QK))
    return rep(cos), rep(s_even), rep(s_odd)


def _qk_prepare(proj, tables):
    n = proj.shape[0]
    lat0 = N_CTX // QK_TM
    nlat = DEC_SEQ // QK_TM
    tab = pl.BlockSpec((QK_TM, DA_W), lambda i: (jnp.maximum(i - lat0, 0) % nlat, 0))
    c0 = P_DA // DA_W
    out = jax.ShapeDtypeStruct((n, DA_W), BF16)
    return pl.pallas_call(
        _qkprep_kernel,
        grid=(n // QK_TM,),
        in_specs=[
            pl.BlockSpec((QK_TM, DA_W), lambda i: (i, c0)),
            pl.BlockSpec((QK_TM, DA_W), lambda i: (i, c0 + 1)),
            pl.BlockSpec((QK_TM, DA_W), lambda i: (i, c0 + 2)),
            tab, tab, tab,
        ],
        out_specs=[pl.BlockSpec((QK_TM, DA_W), lambda i: (i, 0))] * 2
        + [pl.BlockSpec((QK_TM, 2 * DA_W), lambda i: (i, 0))],
        out_shape=[out, out, jax.ShapeDtypeStruct((n, 2 * DA_W), BF16)],
        compiler_params=_cparams(("parallel",)),
        name="qk_prepare",
    )(proj, proj, proj, *tables)


DA_TQ = 256


def _attend(q, segments, lam, nw, o_ref):
    for h in range(DA_HEADS):
        os = []
        for m in range(2):
            c0 = (2 * h + m) * DA_QK
            ss = [_dg(q[:, c0:c0 + DA_QK], k[:, c0:c0 + DA_QK], NT) for k, _ in segments]
            mx = jnp.max(ss[0], axis=-1, keepdims=True)
            for s in ss[1:]:
                mx = jnp.maximum(mx, jnp.max(s, axis=-1, keepdims=True))
            oe = None
            for s, (_, v) in zip(ss, segments):
                part = _dg(jnp.exp(s - mx).astype(BF16), v[:, h * 2 * DA_V:(h + 1) * 2 * DA_V], NN)
                oe = part if oe is None else oe + part
            os.append(oe[:, :DA_V] * (1.0 / oe[:, DA_V:DA_V + 1]))
        o_ref[:, h * DA_V:(h + 1) * DA_V] = _rms(os[0] - lam * os[1], nw).astype(BF16)


def _da_kernel(q_ref, ks_ref, vs_ref, kl_ref, vl_ref, kc_ref, vc_ref, dl_ref, nw_ref, o_ref, *, lam_init):
    dl = dl_ref[...]
    lam = (jnp.exp(jnp.sum(dl[0:1] * dl[1:2], axis=1, keepdims=True))
           - jnp.exp(jnp.sum(dl[2:3] * dl[3:4], axis=1, keepdims=True)) + lam_init)
    nw = nw_ref[...] * (1.0 - lam_init)
    q = q_ref[...]
    is_ctx = pl.program_id(0) < BATCH * (SEQ // DA_TQ)

    @pl.when(is_ctx)
    def _():
        _attend(q, [(ks_ref, vs_ref)], lam, nw, o_ref)

    @pl.when(jnp.logical_not(is_ctx))
    def _():
        _attend(q, [(kc_ref.at[0], vc_ref.at[0]), (kl_ref, vl_ref)], lam, nw, o_ref)


def _da_branch(proj, cache_k, cache_v, da_lambda, da_norm_w, lam_init, tables):
    qb, kb, vb = _qk_prepare(proj, tables)
    n = proj.shape[0]
    ck = cache_k.reshape(DEC_BATCH, PAST_LEN, DA_W).astype(BF16)
    cv = cache_v.astype(BF16)
    cv = jnp.concatenate([cv, jnp.ones_like(cv)], axis=-1).reshape(DEC_BATCH, PAST_LEN, 2 * DA_W)
    n_ctx_tiles = N_CTX // DA_TQ
    per_seq = DEC_SEQ // DA_TQ
    seq_c = lambda i: jnp.minimum(i * DA_TQ // SEQ, BATCH - 1)
    seq_l = lambda i: jnp.clip((i - n_ctx_tiles) // per_seq, 0, DEC_BATCH - 1)
    lat0 = N_CTX // DEC_SEQ
    once = pl.Buffered(1)
    return pl.pallas_call(
        functools.partial(_da_kernel, lam_init=lam_init),
        grid=(n // DA_TQ,),
        in_specs=[
            pl.BlockSpec((DA_TQ, DA_W), lambda i: (i, 0)),
            pl.BlockSpec((SEQ, DA_W), lambda i: (seq_c(i), 0)),
            pl.BlockSpec((SEQ, 2 * DA_W), lambda i: (seq_c(i), 0)),
            pl.BlockSpec((DEC_SEQ, DA_W), lambda i: (lat0 + seq_l(i), 0), pipeline_mode=once),
            pl.BlockSpec((DEC_SEQ, 2 * DA_W), lambda i: (lat0 + seq_l(i), 0), pipeline_mode=once),
            pl.BlockSpec((1, PAST_LEN, DA_W), lambda i: (seq_l(i), 0, 0), pipeline_mode=once),
            pl.BlockSpec((1, PAST_LEN, 2 * DA_W), lambda i: (seq_l(i), 0, 0), pipeline_mode=once),
            pl.BlockSpec((4, DA_QK), lambda i: (0, 0)),
            pl.BlockSpec((1, DA_V), lambda i: (0, 0)),
        ],
        out_specs=pl.BlockSpec((DA_TQ, DA_W), lambda i: (i, 0)),
        out_shape=jax.ShapeDtypeStruct((n, DA_W), BF16),
        compiler_params=_cparams(("arbitrary",)),
        name="diff_attention",
    )(qb, kb, vb, kb, vb, ck, cv, da_lambda, da_norm_w.reshape(1, DA_V))


RW_TM = 256
RW_C = 64
RW_PAIRS = RW_HEADS // 2
HALO = 8


def _head_ones():
    idx = jnp.arange(2 * RW_HD) // RW_HD
    return (idx[:, None] == idx[None, :]).astype(BF16)


def _head_sum(x, ones_pair):
    w = 2 * RW_HD
    return jnp.concatenate([_dot2x(x[:, p * w:(p + 1) * w], ones_pair) for p in range(RW_PAIRS)], axis=1)


def _rwprep_kernel(x_ref, xp_ref, xn_ref, mu_ref, w0_ref, wup_ref, a0_ref, aup_ref, gup_ref, kk_ref, ka_ref,
                   ones_ref, r_o, v_o, kk_o, g_o, kd_o, lw_o, a_o, buf):
    row0 = pl.program_id(0) * RW_TM
    x = x_ref[:, 0:RW_COLS]
    buf[HALO:HALO + RW_TM, :] = x
    buf[HALO - 1:HALO, :] = xp_ref[HALO - 1:HALO, 0:RW_COLS]
    buf[HALO + RW_TM:HALO + RW_TM + 1, :] = xn_ref[0:1, 0:RW_COLS]
    rows = row0 + lax.broadcasted_iota(jnp.int32, (RW_TM, 1), 0)
    pos, length = _seq_pos(rows)
    prev = jnp.where(pos == 0, 0.0, buf[HALO - 1:HALO - 1 + RW_TM, :])
    nxt = jnp.where(pos == length - 1, 0.0, buf[HALO + 1:HALO + 1 + RW_TM, :])
    mu = mu_ref[...]
    u = x + mu[0:1] * (prev - x) + mu[1:2] * (nxt - x)

    r = u[:, 0:RW_W]
    k = u[:, RW_W:2 * RW_W]
    v = u[:, 2 * RW_W:3 * RW_W]
    wl = u[:, 3 * RW_W:3 * RW_W + 128]
    al = u[:, 3 * RW_W + 128:3 * RW_W + 256]
    gl = u[:, 3 * RW_W + 256:3 * RW_W + 384]
    w_raw = w0_ref[...] + _dot3(jnp.tanh(wl), wup_ref[...])
    lw = -jnp.exp(-_softplus(-w_raw) - 0.5)
    a = _sigmoid(a0_ref[...] + _dot3(al, aup_ref[...]))
    g = _dot3(_sigmoid(gl), gup_ref[...])
    kk = k * kk_ref[...]
    kk = kk * lax.rsqrt(jnp.maximum(_head_sum(kk * kk, ones_ref[...]), 1e-12))
    kd = jnp.concatenate([k, k], axis=1) * (1.0 + (a - 1.0) * ka_ref[...])

    r_o[...] = r
    v_o[...] = v
    kk_o[...] = kk
    g_o[...] = g
    for d in range(2):
        kd_o[d] = kd[:, d * RW_W:(d + 1) * RW_W]
        lw_o[d] = lw[:, d * RW_W:(d + 1) * RW_W]
        a_o[d] = a[:, d * RW_W:(d + 1) * RW_W]


def _rw_prepare(proj, p):
    n = proj.shape[0]
    nh = n // HALO
    steps = RW_TM // HALO
    wide = P_DA - P_RW
    cat2 = lambda t: t.reshape(1, 2 * RW_W)
    blockdiag = lambda t: jnp.concatenate(
        [jnp.concatenate([t[0], jnp.zeros_like(t[0])], axis=1),
         jnp.concatenate([jnp.zeros_like(t[1]), t[1]], axis=1)], axis=0)
    const = lambda shape: pl.BlockSpec(shape, lambda i: (0,) * len(shape))
    row = pl.BlockSpec((RW_TM, RW_W), lambda i: (i, 0))
    row2 = pl.BlockSpec((2, RW_TM, RW_W), lambda i: (0, i, 0))
    o1 = jax.ShapeDtypeStruct((n, RW_W), F32)
    o2 = jax.ShapeDtypeStruct((2, n, RW_W), F32)
    return pl.pallas_call(
        _rwprep_kernel,
        grid=(n // RW_TM,),
        in_specs=[
            pl.BlockSpec((RW_TM, wide), lambda i: (i, 0)),
            pl.BlockSpec((HALO, wide), lambda i: (jnp.maximum(i * steps - 1, 0), 0)),
            pl.BlockSpec((HALO, wide), lambda i: (jnp.minimum((i + 1) * steps, nh - 1), 0)),
            const((2, RW_COLS)), const((1, 2 * RW_W)), const((128, 2 * RW_W)), const((1, 2 * RW_W)),
            const((128, 2 * RW_W)), const((RW_G_LORA, RW_W)), const((1, RW_W)), const((1, 2 * RW_W)),
            const((2 * RW_HD, 2 * RW_HD)),
        ],
        out_specs=[row, row, row, row, row2, row2, row2],
        out_shape=[o1, o1, o1, o1, o2, o2, o2],
        scratch_shapes=[pltpu.VMEM((RW_TM + 2 * HALO, RW_COLS), F32)],
        compiler_params=_cparams(("parallel",)),
        name="rwkv_prepare",
    )(proj, proj, proj, p['rw_shift'], cat2(p['rw_w0']), blockdiag(p['rw_w_up']), cat2(p['rw_a0']),
      blockdiag(p['rw_a_up']), p['rw_g_up'], p['rw_k_k'].reshape(1, RW_W),
      jnp.tile(p['rw_k_a'].reshape(1, RW_W), (1, 2)), _head_ones())


RW_TB = 256
RW_NC = RW_TB // RW_C
RW_SIDE = 8


def _rw_finish(y, r, v, g, kd_sum, rk, lnw, lnb, ones):
    inv = 1.0 / RW_HD
    xc = y - _head_sum(y, ones) * inv
    var = _head_sum(xc * xc, ones) * inv
    yn = xc * lax.rsqrt(var + RW_GN_EPS) * lnw + lnb
    bonus = _head_sum(r * kd_sum * rk, ones) * v
    return ((yn + bonus) * g).astype(BF16)


def _rwscan_kernel(r_ref, v_ref, kk_ref, kd_ref, lw_ref, a_ref, tri_ref, inc_ref, str_ref, s0_ref, *rest, backward):
    if backward:
        yf_ref, g_ref, rk_ref, lnw_ref, lnb_ref, ones_ref, y_ref, st_ref, s_scr = rest
    else:
        y_ref, st_ref, s_scr = rest
    _rwscan_body(r_ref, v_ref, kk_ref, kd_ref, lw_ref, a_ref, tri_ref, inc_ref, str_ref, s0_ref, y_ref, st_ref,
                 s_scr, backward,
                 (lambda yb: _rw_finish(yf_ref[...] + yb, r_ref[...], v_ref[...], g_ref[...],
                                        kd_ref[0] + kd_ref[1], rk_ref[...], lnw_ref[...], lnb_ref[...],
                                        ones_ref[...])) if backward else None)


def _rwscan_body(r_ref, v_ref, kk_ref, kd_ref, lw_ref, a_ref, tri_ref, inc_ref, str_ref, s0_ref, y_ref, st_ref,
                 s_scr, backward, finish):
    step = pl.program_id(0)
    nb = pl.num_programs(0)
    bi = (nb - 1 - step) if backward else step
    pos, length = _seq_pos(bi * RW_TB)
    first = (pos + RW_TB == length) if backward else (pos == 0)
    last = (pos == 0) if backward else (pos + RW_TB == length)
    in_ctx = bi * RW_TB < N_CTX
    c = RW_C

    @pl.when(jnp.logical_and(first, in_ctx))
    def _():
        s_scr[...] = jnp.zeros_like(s_scr)

    @pl.when(jnp.logical_and(first, jnp.logical_not(in_ctx)))
    def _():
        z = jnp.zeros((c, c), F32)
        for p in range(RW_PAIRS):
            s_scr[p] = jnp.concatenate(
                [jnp.concatenate([s0_ref[0, 0, 2 * p], z], axis=1),
                 jnp.concatenate([z, s0_ref[0, 0, 2 * p + 1]], axis=1)], axis=0)

    incl2 = inc_ref[...] > 0.5
    strict2 = str_ref[...] > 0.5
    eye2 = (lax.broadcasted_iota(jnp.int32, (2 * c, 2 * c), 0)
            == lax.broadcasted_iota(jnp.int32, (2 * c, 2 * c), 1))
    m_e = lax.broadcasted_iota(jnp.int32, (1, 2 * c), 1) < c

    lw = lw_ref[0]
    cum = _dotx(tri_ref[...], lw)
    tots = [cum[(j * c if backward else j * c + c - 1):(j * c + 1 if backward else j * c + c), :]
            for j in range(RW_NC)]
    tot_b = jnp.concatenate([jnp.broadcast_to(t, (c, RW_W)) for t in tots], axis=0)
    kk = kk_ref[...]
    kd = kd_ref[1 if backward else 0]
    bp = kk * a_ref[0]
    g_inv = jnp.exp(-cum)
    g_rem = jnp.exp(tot_b - cum)
    ag = -kk * jnp.exp(cum - lw)
    rg = r_ref[...] * jnp.exp(cum)
    bdn = bp * g_inv
    kdn = kd * g_inv
    bc = bp * g_rem
    kc = kd * g_rem
    v = v_ref[...]

    def stack(x, j, p):
        xs = x[j * c:(j + 1) * c, p * 2 * c:(p + 1) * 2 * c]
        return jnp.concatenate([jnp.where(m_e, xs, 0.0), jnp.where(m_e, 0.0, xs)], axis=0).astype(BF16)

    pre = {}
    keys = [(j, p) for j in range(RW_NC) for p in range(RW_PAIRS)]
    for g0 in range(0, len(keys), RW_SIDE):
        grp = keys[g0:g0 + RW_SIDE]
        ops = {k: tuple(stack(t, *k) for t in (ag, rg, bdn, kdn, bc, kc, v)) for k in grp}
        gm = {k: _dg(jnp.concatenate([ops[k][0], ops[k][1]], axis=0),
                     jnp.concatenate([ops[k][2], ops[k][3]], axis=0), NT) for k in grp}
        lbb = {k: jnp.where(strict2, gm[k][:2 * c, :2 * c], 0.0) for k in grp}
        lkb = {k: jnp.where(strict2, gm[k][:2 * c, 2 * c:], 0.0).astype(BF16) for k in grp}
        lrk = {k: jnp.concatenate([jnp.where(incl2, gm[k][2 * c:, :2 * c], 0.0),
                                   jnp.where(incl2, gm[k][2 * c:, 2 * c:], 0.0)], axis=1).astype(BF16)
               for k in grp}
        lv = {k: _dg(lkb[k], ops[k][6], NN) for k in grp}
        x = {k: jnp.where(eye2, 1.0, lbb[k]) for k in grp}
        pw = lbb
        for _ in range(int(math.log2(c)) - 1):
            pwb = {k: pw[k].astype(BF16) for k in grp}
            pw = {k: _dg(pwb[k], pwb[k], NN) for k in grp}
            x = {k: x[k] + _dg(x[k].astype(BF16), pw[k].astype(BF16), NN) for k in grp}
        tw = {k: _dg(x[k].astype(BF16), jnp.concatenate([lv[k].astype(BF16), ops[k][0]], axis=1), NN)
              for k in grp}
        for k in grp:
            pre[k] = (tw[k][:, :2 * c], tw[k][:, 2 * c:].astype(BF16), ops[k][1], lrk[k], ops[k][6],
                      jnp.concatenate([ops[k][4], ops[k][5]], axis=0))

    order = range(RW_NC - 1, -1, -1) if backward else range(RW_NC)
    pairs = range(RW_PAIRS)
    s = [s_scr[p] for p in pairs]
    ys = {}
    for j in order:
        sb = [s[p].astype(BF16) for p in pairs]
        u = [_dg(pre[j, p][1], sb[p], NT) + pre[j, p][0] for p in pairs]
        uv = [jnp.concatenate([u[p].astype(BF16), pre[j, p][4]], axis=0) for p in pairs]
        y = [_dg(pre[j, p][2], sb[p], NT) + _dg(pre[j, p][3], uv[p], NN) for p in pairs]
        s = [s[p] * jnp.exp(tots[j][:, p * 2 * c:(p + 1) * 2 * c]) + _dg(uv[p], pre[j, p][5], TN) for p in pairs]
        ys[j] = [y[p][:c] + y[p][c:] for p in pairs]
    y_blk = jnp.concatenate([jnp.concatenate(ys[j], axis=1) for j in range(RW_NC)], axis=0)
    y_ref[...] = finish(y_blk) if finish else y_blk
    for p in pairs:
        s_scr[p] = s[p]

    @pl.when(last)
    def _():
        for p in pairs:
            st_ref[0, 2 * p] = s[p][:c, :c]
            st_ref[0, 2 * p + 1] = s[p][c:, c:]


def _scan_masks(backward):
    t = jnp.arange(RW_TB)
    sgn = -1 if backward else 1
    same_chunk = (t[:, None] // RW_C) == (t[None, :] // RW_C)
    tri = (same_chunk & ((t[:, None] - t[None, :]) * sgn >= 0)).astype(BF16)
    q = jnp.arange(2 * RW_C)
    same_head = (q[:, None] // RW_C) == (q[None, :] // RW_C)
    dif = ((q[:, None] % RW_C) - (q[None, :] % RW_C)) * sgn
    return tri, (same_head & (dif >= 0)).astype(F32), (same_head & (dif > 0)).astype(F32)


def _rw_scan(r, v, kk, kd, lw, a, s0, backward, finish=()):
    n = r.shape[0]
    nb = n // RW_TB
    d = 1 if backward else 0
    blk = (lambda i: nb - 1 - i) if backward else (lambda i: i)
    row = pl.BlockSpec((RW_TB, RW_W), lambda i: (blk(i), 0))
    row2 = pl.BlockSpec((1, RW_TB, RW_W), lambda i: (d, blk(i), 0))
    both = pl.BlockSpec((2, RW_TB, RW_W), lambda i: (0, blk(i), 0))
    const = lambda shape: pl.BlockSpec(shape, lambda i: (0,) * len(shape))
    vec = const((1, RW_W))
    pair = 2 * RW_C
    extra = [row, row, vec, vec, vec, const((pair, pair))] if backward else []
    return pl.pallas_call(
        functools.partial(_rwscan_kernel, backward=backward),
        grid=(nb,),
        in_specs=[row, row, row, both, row2, row2, const((RW_TB, RW_TB)), const((pair, pair)), const((pair, pair)),
                  pl.BlockSpec((1, 1, RW_HEADS, RW_HD, RW_HD),
                               lambda i: (jnp.maximum(_seq_index(blk(i) * RW_TB) - BATCH, 0), d, 0, 0, 0))] + extra,
        out_specs=[row, pl.BlockSpec((1, RW_HEADS, RW_HD, RW_HD), lambda i: (_seq_index(blk(i) * RW_TB), 0, 0, 0))],
        out_shape=[jax.ShapeDtypeStruct((n, RW_W), BF16 if backward else F32),
                   jax.ShapeDtypeStruct((N_SEQS, RW_HEADS, RW_HD, RW_HD), F32)],
        scratch_shapes=[pltpu.VMEM((RW_PAIRS, pair, pair), F32)],
        compiler_params=_cparams(("arbitrary",)),
        name="rwkv_scan_bwd" if backward else "rwkv_scan_fwd",
    )(r, v, kk, kd, lw, a, *_scan_masks(backward), s0, *finish)


def _rwkv_branch(proj, state0, p):
    r, v, kk, g, kd, lw, a = _rw_prepare(proj, p)
    yf, sf = _rw_scan(r, v, kk, kd, lw, a, state0, backward=False)
    finish = (yf, g, p['rw_r_k'].reshape(1, RW_W), p['rw_ln_w'].reshape(1, RW_W), p['rw_ln_b'].reshape(1, RW_W),
              _head_ones())
    out, sb = _rw_scan(r, v, kk, kd, lw, a, state0, backward=True, finish=finish)
    return out, jnp.stack([sf, sb], axis=1)


RET_W = RET_HEADS * RET_DV
RET_PAIRS = RET_HEADS // 2


RET_TB = 256
RET_NC = RET_TB // RET_CHUNK


def _ret_kernel(q_ref, k_ref, v_ref, lg_ref, s0_ref, o_ref, st_ref, s_scr, *, backward):
    step = pl.program_id(0)
    nb = pl.num_programs(0)
    bi = (nb - 1 - step) if backward else step
    pos, length = _seq_pos(bi * RET_TB)
    first = (pos + RET_TB == length) if backward else (pos == 0)
    in_ctx = bi * RET_TB < N_CTX

    @pl.when(jnp.logical_and(first, in_ctx))
    def _():
        s_scr[...] = jnp.zeros_like(s_scr)

    @pl.when(jnp.logical_and(first, jnp.logical_not(in_ctx)))
    def _():
        s_scr[...] = s0_ref[0, 0]

    c = RET_CHUNK
    d = 1 if backward else 0
    sgn = -1 if backward else 1
    lgs = -_softplus(-lg_ref[d:d + 1, :])
    ri = lax.broadcasted_iota(jnp.int32, (c, c), 0)
    cj = lax.broadcasted_iota(jnp.int32, (c, c), 1)
    dif = (ri - cj) * sgn
    valid = dif >= 0
    dist = jnp.maximum(dif, 0).astype(F32)
    pr = lax.broadcasted_iota(jnp.int32, (c, 2 * RET_DK), 0)
    tau = ((c - 1 - pr) if backward else pr).astype(F32)
    low = lax.broadcasted_iota(jnp.int32, (1, 2 * RET_DK), 1) < RET_DK
    rlow = lax.broadcasted_iota(jnp.int32, (2 * RET_DK, RET_DV), 0) < RET_DK
    heads = range(RET_HEADS)
    pairs = range(RET_PAIRS)
    mask = [low, jnp.logical_not(low)]
    lg_h = [lgs[:, h:h + 1] for h in heads]
    lg_row = [jnp.where(low, lg_h[2 * p], lg_h[2 * p + 1]) for p in pairs]
    dmat = [jnp.where(valid, jnp.exp(lg_h[h] * dist), 0.0) for h in heads]
    q_dec = [jnp.exp(lg_row[p] * (tau + 1.0)) for p in pairs]
    k_dec = [jnp.exp(lg_row[p] * (c - 1.0 - tau)) for p in pairs]
    c_dec = [jnp.where(rlow, jnp.exp(lg_h[2 * p] * c), jnp.exp(lg_h[2 * p + 1] * c)) for p in pairs]

    pre = []
    for j in range(RET_NC):
        rows = slice(j * c, (j + 1) * c)
        qp = [q_ref[rows, p * 128:(p + 1) * 128] for p in pairs]
        kp = [k_ref[rows, p * 128:(p + 1) * 128] * (RET_DK ** -0.5) for p in pairs]
        kpb = [t.astype(BF16) for t in kp]
        vb = [v_ref[rows, h * RET_DV:(h + 1) * RET_DV].astype(BF16) for h in heads]
        att = [_dg(jnp.where(mask[h % 2], qp[h // 2], 0.0).astype(BF16), kpb[h // 2], NT) * dmat[h]
               for h in heads]
        upd = [_dg(jnp.where(mask[h % 2], kp[h // 2] * k_dec[h // 2], 0.0).astype(BF16), vb[h], TN)
               for h in heads]
        inner = [_dg(att[h].astype(BF16), vb[h], NN) for h in heads]
        qd = [jnp.where(mask[h % 2], qp[h // 2] * q_dec[h // 2], 0.0).astype(BF16) for h in heads]
        pre.append((inner, qd, upd))

    s = [s_scr[p] for p in pairs]
    for j in (range(RET_NC - 1, -1, -1) if backward else range(RET_NC)):
        inner, qd, upd = pre[j]
        sb = [t.astype(BF16) for t in s]
        for h in heads:
            o_ref[j * c:(j + 1) * c, h * RET_DV:(h + 1) * RET_DV] = inner[h] + _dg(qd[h], sb[h // 2], NN)
        s = [s[p] * c_dec[p] + upd[2 * p] + upd[2 * p + 1] for p in pairs]
    for p in pairs:
        s_scr[p] = s[p]
        st_ref[0, p] = s[p]


def _ret_scan(proj, logit, s0, backward):
    n = proj.shape[0]
    nb = n // RET_TB
    d = 1 if backward else 0
    blk = (lambda i: nb - 1 - i) if backward else (lambda i: i)
    qk_w = RET_HEADS * RET_DK
    state = (RET_PAIRS, 2 * RET_DK, RET_DV)
    return pl.pallas_call(
        functools.partial(_ret_kernel, backward=backward),
        grid=(nb,),
        in_specs=[
            pl.BlockSpec((RET_TB, qk_w), lambda i: (blk(i), P_RET // qk_w)),
            pl.BlockSpec((RET_TB, qk_w), lambda i: (blk(i), P_RET // qk_w + 1)),
            pl.BlockSpec((RET_TB, RET_W), lambda i: (blk(i), (P_RET + 2 * qk_w) // RET_W)),
            pl.BlockSpec((2, RET_HEADS), lambda i: (0, 0)),
            pl.BlockSpec((1, 1) + state,
                         lambda i: (jnp.maximum(_seq_index(blk(i) * RET_TB) - BATCH, 0), d, 0, 0, 0)),
        ],
        out_specs=[pl.BlockSpec((RET_TB, RET_W), lambda i: (blk(i), 0)),
                   pl.BlockSpec((1,) + state, lambda i: (_seq_index(blk(i) * RET_TB), 0, 0, 0))],
        out_shape=[jax.ShapeDtypeStruct((n, RET_W), F32), jax.ShapeDtypeStruct((N_SEQS,) + state, F32)],
        scratch_shapes=[pltpu.VMEM(state, F32)],
        compiler_params=_cparams(("arbitrary",)),
        name="retention_scan_bwd" if backward else "retention_scan_fwd",
    )(proj, proj, proj, logit, s0)


def _standardize(x, eps):
    mu = jnp.mean(x, axis=-1, keepdims=True)
    xc = x - mu
    return xc * lax.rsqrt(jnp.mean(xc * xc, axis=-1, keepdims=True) + eps)


def _retpost_kernel(of_ref, ob_ref, g_ref, nw_ref, out_ref):
    o = of_ref[...] + ob_ref[...]
    g = g_ref[...]
    for h in range(RET_HEADS):
        sl = slice(h * RET_DV, (h + 1) * RET_DV)
        gh = g[:, sl]
        out_ref[:, sl] = (gh * _sigmoid(gh) * (_standardize(o[:, sl], EPS) * nw_ref[:, sl])).astype(BF16)


def _ret_post(of, ob, proj, norm_w):
    n = proj.shape[0]
    tm = 512
    return pl.pallas_call(
        _retpost_kernel,
        grid=(n // tm,),
        in_specs=[
            pl.BlockSpec((tm, RET_W), lambda i: (i, 0)),
            pl.BlockSpec((tm, RET_W), lambda i: (i, 0)),
            pl.BlockSpec((tm, RET_W), lambda i: (i, (P_RET + 1024) // RET_W)),
            pl.BlockSpec((1, RET_W), lambda i: (0, 0)),
        ],
        out_specs=pl.BlockSpec((tm, RET_W), lambda i: (i, 0)),
        out_shape=jax.ShapeDtypeStruct((n, RET_W), BF16),
        compiler_params=_cparams(("parallel",)),
        name="retention_post",
    )(of, ob, proj, norm_w.reshape(1, RET_W))


def _retention_branch(proj, state0, p):
    s0 = state0.reshape(DEC_BATCH, 2, RET_PAIRS, 2 * RET_DK, RET_DV)
    of, sf = _ret_scan(proj, p['ret_decay_logit'], s0, backward=False)
    ob, sb = _ret_scan(proj, p['ret_decay_logit'], s0, backward=True)
    st = jnp.stack([sf, sb], axis=1).reshape(N_SEQS, 2, RET_HEADS, RET_DK, RET_DV)
    return _ret_post(of, ob, proj, p['ret_norm_w']), st


CV_TM = 256
CV_HALO = 16


def _conv_kernel(a_ref, g_ref, ap_ref, gp_ref, an_ref, gn_ref, w_ref, b_ref, lnw_ref, lnb_ref, o_ref, buf, sbuf):
    row0 = pl.program_id(0) * CV_TM
    pos0, len0 = _seq_pos(row0)
    buf[CV_HALO:CV_HALO + CV_TM, :] = a_ref[...] * _sigmoid(g_ref[...])
    buf[0:CV_HALO, :] = jnp.where(pos0 == 0, 0.0, ap_ref[...] * _sigmoid(gp_ref[...]))
    buf[CV_HALO + CV_TM:, :] = jnp.where(pos0 + CV_TM == len0, 0.0, an_ref[...] * _sigmoid(gn_ref[...]))
    base = CV_HALO - CONV_K // 2
    acc = jnp.zeros((CV_TM, CV_W), F32)
    for r in range(8):
        taps = [m for m in range((base + CONV_K + 7) // 8) if 0 <= r + 8 * m - base < CONV_K]
        span = CV_TM + 8 * max(taps)
        if r:
            sbuf[0:span, :] = buf[r:r + span, :]
        src = sbuf if r else buf
        for m in taps:
            j = r + 8 * m - base
            acc = acc + w_ref[j:j + 1, :] * src[8 * m:8 * m + CV_TM, :]
    z = _standardize(acc + b_ref[...], EPS) * lnw_ref[...] + lnb_ref[...]
    o_ref[...] = (z * _sigmoid(z)).astype(BF16)


def _conv_branch(proj, p):
    n = proj.shape[0]
    nh = n // CV_HALO
    steps = CV_TM // CV_HALO
    ca = P_CV // CV_W
    prev = lambda i: jnp.maximum(i * steps - 1, 0)
    nxt = lambda i: jnp.minimum((i + 1) * steps, nh - 1)
    vec = pl.BlockSpec((1, CV_W), lambda i: (0, 0))
    return pl.pallas_call(
        _conv_kernel,
        grid=(n // CV_TM,),
        in_specs=[
            pl.BlockSpec((CV_TM, CV_W), lambda i: (i, ca)),
            pl.BlockSpec((CV_TM, CV_W), lambda i: (i, ca + 1)),
            pl.BlockSpec((CV_HALO, CV_W), lambda i: (prev(i), ca)),
            pl.BlockSpec((CV_HALO, CV_W), lambda i: (prev(i), ca + 1)),
            pl.BlockSpec((CV_HALO, CV_W), lambda i: (nxt(i), ca)),
            pl.BlockSpec((CV_HALO, CV_W), lambda i: (nxt(i), ca + 1)),
            pl.BlockSpec((CONV_K, CV_W), lambda i: (0, 0)),
            vec, vec, vec,
        ],
        out_specs=pl.BlockSpec((CV_TM, CV_W), lambda i: (i, 0)),
        out_shape=jax.ShapeDtypeStruct((n, CV_W), BF16),
        scratch_shapes=[pltpu.VMEM((CV_TM + 2 * CV_HALO, CV_W), F32)] * 2,
        compiler_params=_cparams(("parallel",)),
        name="conformer_conv",
    )(proj, proj, proj, proj, proj, proj, p['cv_dw_w'], p['cv_dw_b'].reshape(1, CV_W),
      p['cv_ln_w'].reshape(1, CV_W), p['cv_ln_b'].reshape(1, CV_W))


MG_TM = 512


def _merge_kernel(da_ref, rw_ref, ret_ref, cv_ref, g0_ref, g1_ref, g2_ref, g3_ref, x_ref, mod_ref, wb_ref, wo_ref,
                  nw_ref, x_o, h_o):
    m = None
    for n, (br, gt) in enumerate(((da_ref, g0_ref), (rw_ref, g1_ref), (ret_ref, g2_ref), (cv_ref, g3_ref))):
        t = _sigmoid(gt[...].astype(F32)) * _dg(br[...], wb_ref[n], NN)
        m = t if m is None else m + t
    out = _dg(m.astype(BF16), wo_ref[...], NN)
    mrow = _mod_row(mod_ref, pl.program_id(0) * MG_TM)
    gate1 = mrow[:, 2 * D_MODEL:3 * D_MODEL]
    sh2 = mrow[:, 3 * D_MODEL:4 * D_MODEL]
    sc2 = mrow[:, 4 * D_MODEL:5 * D_MODEL]
    x1 = x_ref[...] + gate1 * out
    x_o[...] = x1
    h_o[...] = _rms(x1, nw_ref[...]) * (1.0 + sc2) + sh2


def _merge(branches, gates, x, mod, w_branch, w_out, norm_w):
    n = x.shape[0]
    br = pl.BlockSpec((MG_TM, BR_W), lambda i: (i, 0))
    gspec = lambda j: pl.BlockSpec((MG_TM, D_MODEL), lambda i: (i, j))
    full = pl.BlockSpec((MG_TM, D_MODEL), lambda i: (i, 0))
    out = jax.ShapeDtypeStruct((n, D_MODEL), F32)
    return pl.pallas_call(
        _merge_kernel,
        grid=(n // MG_TM,),
        in_specs=[br, br, br, br, gspec(0), gspec(1), gspec(2), gspec(3), full,
                  pl.BlockSpec((MOD_ROWS, 6 * D_MODEL), lambda i: (0, 0)),
                  pl.BlockSpec((N_BRANCH, BR_W, D_MODEL), lambda i: (0, 0, 0)),
                  pl.BlockSpec((D_MODEL, D_MODEL), lambda i: (0, 0)),
                  pl.BlockSpec((1, D_MODEL), lambda i: (0, 0))],
        out_specs=[full, full],
        out_shape=[out, out],
        compiler_params=_cparams(("parallel",)),
        name="gated_merge",
    )(*branches, gates, gates, gates, gates, x, mod, w_branch.astype(BF16), w_out.astype(BF16),
      norm_w.reshape(1, D_MODEL))


RT_TM = 256
MOE_BM = 512
MOE_ROWS = N_TOK * TOP_K + N_EXPERTS * MOE_BM
DP_TM = 512
DP_GROUP = 64
CB_TM = 256
DMA_UNROLL = 8


def _router_kernel(h_ref, w_ref, b_ref, tri_ref, idx_o, gate_o, rank_o, cnt_o, carry):
    @pl.when(pl.program_id(0) == 0)
    def _():
        carry[...] = jnp.zeros_like(carry)

    logits = _dot3(w_ref[...], h_ref[...], NT) + b_ref[...]
    e_iota = lax.broadcasted_iota(jnp.int32, logits.shape, 0)
    work = logits
    vals, idxs, hots = [], [], []
    for _ in range(TOP_K):
        mx = jnp.max(work, axis=0, keepdims=True)
        ix = jnp.min(jnp.where(work == mx, e_iota, N_EXPERTS), axis=0, keepdims=True)
        hot = e_iota == ix
        vals.append(mx)
        idxs.append(ix)
        hots.append(hot.astype(F32))
        work = jnp.where(hot, -jnp.inf, work)
    es = [jnp.exp(v - vals[0]) for v in vals]
    inv = 1.0 / (es[0] + es[1] + es[2] + es[3])
    chosen = hots[0] + hots[1] + hots[2] + hots[3]
    ahead = carry[...][:, 0:1] + _dg(chosen.astype(BF16), tri_ref[...], NN)
    idx_o[...] = jnp.concatenate(idxs, axis=0)
    gate_o[...] = jnp.concatenate([e * inv for e in es], axis=0)
    rank_o[...] = jnp.concatenate(
        [jnp.sum(hot * ahead, axis=0, keepdims=True) for hot in hots], axis=0).astype(jnp.int32)
    carry[...] = carry[...] + jnp.sum(chosen, axis=1, keepdims=True)
    cnt_o[...] = carry[...]


def _router(h, router_w, router_b):
    n = h.shape[0]
    tri = (jnp.arange(RT_TM)[:, None] < jnp.arange(RT_TM)[None, :]).astype(BF16)
    col = pl.BlockSpec((TOP_K, RT_TM), lambda i: (0, i))
    return pl.pallas_call(
        _router_kernel,
        grid=(n // RT_TM,),
        in_specs=[
            pl.BlockSpec((RT_TM, D_MODEL), lambda i: (i, 0)),
            pl.BlockSpec((N_EXPERTS, D_MODEL), lambda i: (0, 0)),
            pl.BlockSpec((N_EXPERTS, 1), lambda i: (0, 0)),
            pl.BlockSpec((RT_TM, RT_TM), lambda i: (0, 0)),
        ],
        out_specs=[col, col, col, pl.BlockSpec((N_EXPERTS, 128), lambda i: (0, 0))],
        out_shape=[jax.ShapeDtypeStruct((TOP_K, n), jnp.int32), jax.ShapeDtypeStruct((TOP_K, n), F32),
                   jax.ShapeDtypeStruct((TOP_K, n), jnp.int32), jax.ShapeDtypeStruct((N_EXPERTS, 128), F32)],
        scratch_shapes=[pltpu.VMEM((N_EXPERTS, 128), F32)],
        compiler_params=_cparams(("arbitrary",)),
        name="router",
    )(h, router_w.T, router_b.reshape(N_EXPERTS, 1), tri)


def _tile_major(t, tm):
    k, n = t.shape
    return t.reshape(k, n // tm, tm).transpose(1, 0, 2).reshape(n // tm, 1, k * tm)


def _dispatch_kernel(dest_ref, pe_ref, h_ref, o_hbm, zbuf, sem, zsem):
    n_groups = DP_TM // DP_GROUP

    @pl.when(pl.program_id(0) == 0)
    def _():
        zbuf[...] = jnp.zeros_like(zbuf)

        def fill(e):
            end = pe_ref[e]
            begin = pe_ref[e - 1] if e else 0
            return end > begin, pltpu.make_async_copy(
                zbuf, o_hbm.at[pl.ds(pl.multiple_of(jnp.maximum(end - MOE_BM, 0), MOE_BM), MOE_BM)], zsem.at[0])

        for e in range(N_EXPERTS):
            nonempty, cp = fill(e)
            pl.when(nonempty)(cp.start)
        for e in range(N_EXPERTS):
            nonempty, cp = fill(e)
            pl.when(nonempty)(cp.wait)

        def tail(b):
            return pltpu.make_async_copy(zbuf, o_hbm.at[pl.ds(pl.multiple_of(b * MOE_BM, MOE_BM), MOE_BM)],
                                         zsem.at[0])

        first_unused = pe_ref[N_EXPERTS - 1] // MOE_BM
        lax.fori_loop(first_unused, MOE_ROWS // MOE_BM, lambda b, c: (tail(b).start(), c)[1], 0)
        lax.fori_loop(first_unused, MOE_ROWS // MOE_BM, lambda b, c: (tail(b).wait(), c)[1], 0)

    def wait_group(slot):
        pltpu.make_async_copy(h_ref.at[pl.ds(0, TOP_K * DP_GROUP)], o_hbm.at[pl.ds(0, TOP_K * DP_GROUP)],
                              sem.at[slot]).wait()

    def group(gi, carry):
        slot = gi % 2

        def issue(t, c):
            tok = gi * DP_GROUP + t
            for k in range(TOP_K):
                dst = dest_ref[0, 0, k * DP_TM + tok]
                pltpu.make_async_copy(h_ref.at[pl.ds(tok, 1)], o_hbm.at[pl.ds(dst, 1)], sem.at[slot]).start()
            return c

        lax.fori_loop(0, DP_GROUP, issue, 0, unroll=DMA_UNROLL)

        @pl.when(gi > 0)
        def _():
            wait_group(1 - slot)

        return carry

    lax.fori_loop(0, n_groups, group, 0)
    wait_group((n_groups - 1) % 2)


def _dispatch(h, dest, pad_end):
    n = h.shape[0]
    return pl.pallas_call(
        _dispatch_kernel,
        grid=(n // DP_TM,),
        in_specs=[
            pl.BlockSpec((1, 1, TOP_K * DP_TM), lambda i: (i, 0, 0), memory_space=pltpu.SMEM),
            pl.BlockSpec(memory_space=pltpu.SMEM),
            pl.BlockSpec((DP_TM, D_MODEL), lambda i: (i, 0)),
        ],
        out_specs=pl.BlockSpec(memory_space=pl.ANY),
        out_shape=jax.ShapeDtypeStruct((MOE_ROWS, D_MODEL), F32),
        scratch_shapes=[pltpu.VMEM((MOE_BM, D_MODEL), F32), pltpu.SemaphoreType.DMA((2,)),
                        pltpu.SemaphoreType.DMA((1,))],
        compiler_params=_cparams(("arbitrary",)),
        name="moe_dispatch",
    )(_tile_major(dest, DP_TM), pad_end, h)


def _expert_kernel(bx_ref, be_ref, nv_ref, x_ref, w1_ref, b1_ref, w2_ref, b2_ref, o_ref, w1b, w2b):
    i = pl.program_id(0)
    changed = jnp.logical_or(i == 0, be_ref[i] != be_ref[jnp.maximum(i - 1, 0)])

    @pl.when(changed)
    def _():
        w1b[...] = w1_ref[0, 0].astype(BF16)
        w2b[...] = w2_ref[0, 0].astype(BF16)

    def ffn(rows):
        hb = _dg(x_ref[0:rows, :].astype(BF16), w1b[...], NN) + b1_ref[0, 0]
        hg = jnp.minimum(hb[:, :D_FF], SWIGLU_LIMIT)
        hu = jnp.clip(hb[:, D_FF:], -SWIGLU_LIMIT, SWIGLU_LIMIT)
        act = hg * _sigmoid(SWIGLU_ALPHA * hg) * (hu + 1.0)
        o_ref[0:rows, :] = _dg(act.astype(BF16), w2b[...], NN) + b2_ref[0, 0]

    nv = nv_ref[i]
    half = MOE_BM // 2

    @pl.when(nv > half)
    def _():
        ffn(MOE_BM)

    @pl.when(jnp.logical_and(nv > 0, nv <= half))
    def _():
        ffn(half)
        o_ref[half:, :] = jnp.zeros((MOE_BM - half, D_MODEL), F32)

    @pl.when(nv == 0)
    def _():
        o_ref[...] = jnp.zeros_like(o_ref)


def _experts(x_rows, blk_x, blk_e, n_valid, layer, w1, b1, w2, b2):
    nb = MOE_ROWS // MOE_BM
    grid_spec = pltpu.PrefetchScalarGridSpec(
        num_scalar_prefetch=3,
        grid=(nb,),
        in_specs=[
            pl.BlockSpec((MOE_BM, D_MODEL), lambda i, bx, be, nv: (bx[i], 0)),
            pl.BlockSpec((1, 1, D_MODEL, 2 * D_FF), lambda i, bx, be, nv: (layer, be[i], 0, 0)),
            pl.BlockSpec((1, 1, 1, 2 * D_FF), lambda i, bx, be, nv: (layer, be[i], 0, 0)),
            pl.BlockSpec((1, 1, D_FF, D_MODEL), lambda i, bx, be, nv: (layer, be[i], 0, 0)),
            pl.BlockSpec((1, 1, 1, D_MODEL), lambda i, bx, be, nv: (layer, be[i], 0, 0)),
        ],
        out_specs=pl.BlockSpec((MOE_BM, D_MODEL), lambda i, bx, be, nv: (i, 0)),
        scratch_shapes=[pltpu.VMEM((D_MODEL, 2 * D_FF), BF16), pltpu.VMEM((D_FF, D_MODEL), BF16)],
    )
    return pl.pallas_call(
        _expert_kernel,
        grid_spec=grid_spec,
        out_shape=jax.ShapeDtypeStruct((MOE_ROWS, D_MODEL), F32),
        compiler_params=pltpu.CompilerParams(dimension_semantics=("arbitrary",),
                                             vmem_limit_bytes=56 * 1024 * 1024),
        name="moe_experts",
    )(blk_x, blk_e, n_valid, x_rows, w1, b1.reshape(DEPTH, N_EXPERTS, 1, 2 * D_FF), w2,
      b2.reshape(DEPTH, N_EXPERTS, 1, D_MODEL))


def _combine_kernel(dest_ref, gate_ref, x_ref, mod_ref, fw_ref, y_hbm, *rest, final):
    if final:
        o_ctx_ref, o_lat_ref, buf, sem = rest
    else:
        o_ref, buf, sem = rest

    def issue(t, c):
        for k in range(TOP_K):
            dst = dest_ref[0, 0, k * CB_TM + t]
            pltpu.make_async_copy(y_hbm.at[pl.ds(dst, 1)], buf.at[k, pl.ds(t, 1)], sem.at[0]).start()
        return c

    lax.fori_loop(0, CB_TM, issue, 0, unroll=DMA_UNROLL)
    for k in range(TOP_K):
        pltpu.make_async_copy(y_hbm.at[pl.ds(0, CB_TM)], buf.at[k], sem.at[0]).wait()
    g = gate_ref[...]
    acc = g[:, 0:1] * buf[0]
    for k in range(1, TOP_K):
        acc = acc + g[:, k:k + 1] * buf[k]
    gate2 = _mod_row(mod_ref, pl.program_id(0) * CB_TM)[:, 5 * D_MODEL:6 * D_MODEL]
    x2 = x_ref[...] + gate2 * acc
    if not final:
        o_ref[...] = x2
        return
    y = _rms(x2, fw_ref[...])
    in_ctx = pl.program_id(0) < N_CTX // CB_TM

    @pl.when(in_ctx)
    def _():
        o_ctx_ref[...] = y

    @pl.when(jnp.logical_not(in_ctx))
    def _():
        o_lat_ref[...] = y


def _combine(y_rows, dest, gates, x, mod, final_w, final):
    n = x.shape[0]
    full = pl.BlockSpec((CB_TM, D_MODEL), lambda i: (i, 0))
    nct = N_CTX // CB_TM
    if final:
        out_specs = [pl.BlockSpec((CB_TM, D_MODEL), lambda i: (jnp.minimum(i, nct - 1), 0)),
                     pl.BlockSpec((CB_TM, D_MODEL), lambda i: (jnp.maximum(i - nct, 0), 0))]
        out_shape = [jax.ShapeDtypeStruct((N_CTX, D_MODEL), F32), jax.ShapeDtypeStruct((N_LAT, D_MODEL), F32)]
    else:
        out_specs, out_shape = full, jax.ShapeDtypeStruct((n, D_MODEL), F32)
    return pl.pallas_call(
        functools.partial(_combine_kernel, final=final),
        grid=(n // CB_TM,),
        in_specs=[
            pl.BlockSpec((1, 1, TOP_K * CB_TM), lambda i: (i, 0, 0), memory_space=pltpu.SMEM),
            pl.BlockSpec((CB_TM, TOP_K), lambda i: (i, 0)),
            full,
            pl.BlockSpec((MOD_ROWS, 6 * D_MODEL), lambda i: (0, 0)),
            pl.BlockSpec((1, D_MODEL), lambda i: (0, 0)),
            pl.BlockSpec(memory_space=pl.ANY),
        ],
        out_specs=out_specs,
        out_shape=out_shape,
        scratch_shapes=[pltpu.VMEM((TOP_K, CB_TM, D_MODEL), F32), pltpu.SemaphoreType.DMA((1,))],
        compiler_params=_cparams(("arbitrary",)),
        name="moe_combine",
    )(_tile_major(dest, CB_TM), gates.T, x, mod, final_w.reshape(1, D_MODEL), y_rows)


def _routed_ffn(h, x, mod, p, moe, layer, final_w, final):
    idx, gates, rank, counts = _router(h, p['router_w'], p['router_b'])
    counts = counts[:, 0].astype(jnp.int32)
    padded = (counts + MOE_BM - 1) // MOE_BM * MOE_BM
    pad_end = jnp.cumsum(padded)
    pad_start = pad_end - padded
    experts = jnp.arange(N_EXPERTS, dtype=jnp.int32)
    start_of = jnp.sum(jnp.where(idx[:, :, None] == experts, pad_start, 0), axis=-1)
    dest = start_of + rank
    nb = MOE_ROWS // MOE_BM
    first_row = jnp.arange(nb, dtype=jnp.int32) * MOE_BM
    blk_e = jnp.minimum(jnp.sum((pad_end[None, :] <= first_row[:, None]).astype(jnp.int32), axis=1),
                        N_EXPERTS - 1)
    is_e = blk_e[:, None] == experts[None, :]
    end_of = jnp.sum(jnp.where(is_e, pad_start + counts, 0), axis=1)
    n_valid = jnp.clip(end_of - first_row, 0, MOE_BM).astype(jnp.int32)
    blk_x = jnp.minimum(jnp.arange(nb, dtype=jnp.int32), pad_end[-1] // MOE_BM - 1)
    x_rows = _dispatch(h, dest, pad_end)
    y_rows = _experts(x_rows, blk_x, blk_e, n_valid, layer, *moe)
    return _combine(y_rows, dest, gates, x, mod, final_w, final)


_LAYER_PARAMS = ('norm_mix_w', 'norm_ffn_w', 'da_lambda', 'da_norm_w', 'rw_shift', 'rw_w0', 'rw_w_up', 'rw_a0',
                 'rw_a_up', 'rw_g_up', 'rw_k_k', 'rw_k_a', 'rw_r_k', 'rw_ln_w', 'rw_ln_b', 'ret_decay_logit',
                 'ret_norm_w', 'cv_dw_w', 'cv_dw_b', 'cv_ln_w', 'cv_ln_b', 'w_branch', 'w_out', 'router_w',
                 'router_b')


def _layer(x, mod, w_p, p, moe, layer, lam_init, caches, tables, final_w, final):
    cache_k, cache_v, state_rw, state_ret = caches
    proj, gates = _input_projection(x, mod, p['norm_mix_w'], w_p, layer)
    o_da = _da_branch(proj, cache_k, cache_v, p['da_lambda'], p['da_norm_w'], lam_init, tables)
    o_rw, rw_state = _rwkv_branch(proj, state_rw, p)
    o_ret, ret_state = _retention_branch(proj, state_ret, p)
    o_cv = _conv_branch(proj, p)
    x1, h2 = _merge((o_da, o_rw, o_ret, o_cv), gates, x, mod, p['w_branch'], p['w_out'], p['norm_ffn_w'])
    x2 = _routed_ffn(h2, x1, mod, p, moe, layer, final_w, final)
    new_k = proj[:N_CTX, P_DA + DA_W:P_DA + 2 * DA_W].reshape(BATCH, SEQ, DA_HEADS, 2, DA_QK)
    new_v = proj[:N_CTX, P_DA + 2 * DA_W:P_DA + 3 * DA_W].reshape(BATCH, SEQ, DA_HEADS, DA_V)
    return x2, (new_k, new_v, rw_state[:BATCH], ret_state[:BATCH])


def kernel(x_prompt, x_sample, c, cache_da_k, cache_da_v, state_rwkv, state_ret, c_ctx, ada_w, ada_b, norm_mix_w,
           norm_ffn_w, w_in, da_lambda, da_norm_w, rw_shift, rw_w0, rw_w_up, rw_a0, rw_a_up, rw_g_up, rw_k_k,
           rw_k_a, rw_r_k, rw_ln_w, rw_ln_b, ret_decay_logit, ret_norm_w, cv_dw_w, cv_dw_b, cv_ln_w, cv_ln_b,
           w_branch, w_out, router_w, router_b, moe_w1, moe_b1, moe_w2, moe_b2, final_norm_w):
    weights = dict(norm_mix_w=norm_mix_w, norm_ffn_w=norm_ffn_w, da_lambda=da_lambda, da_norm_w=da_norm_w,
                   rw_shift=rw_shift, rw_w0=rw_w0, rw_w_up=rw_w_up, rw_a0=rw_a0, rw_a_up=rw_a_up, rw_g_up=rw_g_up,
                   rw_k_k=rw_k_k, rw_k_a=rw_k_a, rw_r_k=rw_r_k, rw_ln_w=rw_ln_w, rw_ln_b=rw_ln_b,
                   ret_decay_logit=ret_decay_logit, ret_norm_w=ret_norm_w, cv_dw_w=cv_dw_w, cv_dw_b=cv_dw_b,
                   cv_ln_w=cv_ln_w, cv_ln_b=cv_ln_b, w_branch=w_branch, w_out=w_out, router_w=router_w,
                   router_b=router_b)
    x = jnp.concatenate([x_prompt.reshape(N_CTX, D_MODEL), x_sample.reshape(N_LAT, D_MODEL)], axis=0)
    cvec = jnp.concatenate([c_ctx[None, :], c, jnp.zeros((MOD_ROWS - 1 - DEC_BATCH, D_MODEL), F32)], axis=0)
    mod = _modulation(cvec, ada_w, ada_b)
    w_p = _pad_w_in(w_in).astype(BF16)
    tables = _rope_tables()
    outs = []
    moe = (moe_w1, moe_b1, moe_w2, moe_b2)
    for i in range(DEPTH):
        p = {name: weights[name][i] for name in _LAYER_PARAMS}
        lam_init = 0.8 - 0.6 * math.exp(-0.3 * i)
        caches = (cache_da_k[:, i], cache_da_v[:, i], state_rwkv[:, i], state_ret[:, i])
        x, ctx_out = _layer(x, mod[i], w_p, p, moe, i, lam_init, caches, tables, final_norm_w, i == DEPTH - 1)
        outs.append(ctx_out)
    y_prompt = x[0].reshape(BATCH, SEQ, D_MODEL)
    y_sample = x[1].reshape(DEC_BATCH, DEC_SEQ, D_MODEL)
    stack = lambda j: jnp.stack([o[j] for o in outs], axis=1)
    return (y_prompt, y_sample, stack(0), stack(1), stack(2), stack(3))


def _pad_w_in(w_in):
    da, rw, ret, cv, gate = jnp.split(w_in, [1536, 3456, 4992, 6016], axis=-1)
    pad = jnp.zeros(w_in.shape[:-1] + (P_DA - RW_COLS,), w_in.dtype)
    return jnp.concatenate([rw, pad, da, ret, cv, gate], axis=-1)
```

```python
import functools
import math

import jax
import jax.numpy as jnp
from jax import lax
from jax.experimental import pallas as pl
from jax.experimental.pallas import tpu as pltpu

F32 = jnp.float32
BF16 = jnp.bfloat16

D_MODEL = 1024
BATCH = 16
SEQ = 256
DEPTH = 2
DEC_BATCH = 2
DEC_SEQ = 4096
PAST_LEN = 512
GRID_W = 64
EPS = 1e-6

DA_HEADS = 4
DA_QK = 64
DA_V = 128
ROPE_BASE = 10000.0

RW_HEADS = 8
RW_HD = 64
RW_W = 512
RW_LORA = 64
RW_G_LORA = 128
RW_GN_EPS = 64e-5

RET_HEADS = 4
RET_DK = 64
RET_DV = 128
RET_CHUNK = 128

CV_W = 512
CONV_K = 31
BR_W = 512
N_BRANCH = 4

N_EXPERTS = 32
TOP_K = 4
D_FF = 1024
SWIGLU_LIMIT = 7.0
SWIGLU_ALPHA = 1.702

N_CTX = BATCH * SEQ
N_LAT = DEC_BATCH * DEC_SEQ
N_TOK = N_CTX + N_LAT
N_SEQS = BATCH + DEC_BATCH
MOD_ROWS = 8
MOD_GROUP = 4096

RW_COLS = 1920
P_RW = 0
P_DA = 2048
P_RET = 3584
P_CV = 5120
P_GATE = 6144
P_COLS = 10240

VMEM_LIMIT = 48 * 1024 * 1024


def _cparams(sem):
    return pltpu.CompilerParams(dimension_semantics=sem, vmem_limit_bytes=VMEM_LIMIT)


def _dg(a, b, dims):
    return lax.dot_general(a, b, (dims, ((), ())), preferred_element_type=F32)


NN = ((1,), (0,))
NT = ((1,), (1,))
TN = ((0,), (0,))


def _dot(a, b, dims=NN):
    return _dg(a.astype(BF16), b.astype(BF16), dims)


def _split(x):
    hi = x.astype(BF16)
    lo = (x - hi.astype(F32)).astype(BF16)
    return hi, lo


def _dot3(a, b, dims=NN):
    ah, al = _split(a)
    bh, bl = _split(b)
    return _dg(ah, bh, dims) + (_dg(ah, bl, dims) + _dg(al, bh, dims))


def _dot2x(a, e, dims=NN):
    ah, al = _split(a)
    am = (a - ah.astype(F32) - al.astype(F32)).astype(BF16)
    eb = e.astype(BF16)
    return _dg(ah, eb, dims) + (_dg(al, eb, dims) + _dg(am, eb, dims))


def _sigmoid(x):
    return 1.0 / (1.0 + jnp.exp(-x))


def _softplus(x):
    return jnp.maximum(x, 0.0) + jnp.log(1.0 + jnp.exp(-jnp.abs(x)))


def _seq_pos(row):
    in_ctx = row < N_CTX
    pos = jnp.where(in_ctx, row & (SEQ - 1), (row - N_CTX) & (DEC_SEQ - 1))
    length = jnp.where(in_ctx, SEQ, DEC_SEQ)
    return pos, length


def _seq_index(row):
    return jnp.where(row < N_CTX, row // SEQ, BATCH + (row - N_CTX) // DEC_SEQ)


def _dotx(e, b, dims=NN):
    bh, bl = _split(b)
    bm = (b - bh.astype(F32) - bl.astype(F32)).astype(BF16)
    eb = e.astype(BF16)
    return _dg(eb, bh, dims) + (_dg(eb, bl, dims) + _dg(eb, bm, dims))


def _mod_kernel(c_ref, w_ref, b_ref, o_ref):
    c = c_ref[...]
    s = c * _sigmoid(c)
    o_ref[0] = _dot3(s, w_ref[0]) + b_ref[0]


def _modulation(cvec, ada_w, ada_b):
    tn = 1536
    return pl.pallas_call(
        _mod_kernel,
        grid=(DEPTH, 6 * D_MODEL // tn),
        in_specs=[
            pl.BlockSpec((MOD_ROWS, D_MODEL), lambda l, j: (0, 0)),
            pl.BlockSpec((1, D_MODEL, tn), lambda l, j: (l, 0, j)),
            pl.BlockSpec((1, 1, tn), lambda l, j: (l, 0, j)),
        ],
        out_specs=pl.BlockSpec((1, MOD_ROWS, tn), lambda l, j: (l, 0, j)),
        out_shape=jax.ShapeDtypeStruct((DEPTH, MOD_ROWS, 6 * D_MODEL), F32),
        compiler_params=_cparams(("parallel", "parallel")),
        name="modulation",
    )(cvec, ada_w, ada_b.reshape(DEPTH, 1, 6 * D_MODEL))


def _mod_row(mod_ref, first_row):
    g = first_row // MOD_GROUP
    return mod_ref[pl.ds(g, 1), :]


def _rms(x, w):
    return x * lax.rsqrt(jnp.mean(x * x, axis=-1, keepdims=True) + EPS) * w


IN_TM = 1024
IN_TN = 1024


IN_NA = P_GATE // IN_TN


def _inproj_kernel(x_ref, mod_ref, nw_ref, w_ref, oa_ref, og_ref, h_ref):
    i = pl.program_id(0)
    j = pl.program_id(1)

    @pl.when(j == 0)
    def _():
        m = _mod_row(mod_ref, i * IN_TM)
        sh = m[:, 0:D_MODEL]
        sc = m[:, D_MODEL:2 * D_MODEL]
        h_ref[...] = (_rms(x_ref[...], nw_ref[...]) * (1.0 + sc) + sh).astype(BF16)

    acc = _dg(h_ref[...], w_ref[0].astype(BF16), NN)

    @pl.when(j < IN_NA)
    def _():
        oa_ref[...] = acc

    @pl.when(j >= IN_NA)
    def _():
        og_ref[...] = acc.astype(BF16)


def _input_projection(x, mod, norm_w, w_p, layer):
    n = x.shape[0]
    return pl.pallas_call(
        _inproj_kernel,
        grid=(n // IN_TM, P_COLS // IN_TN),
        in_specs=[
            pl.BlockSpec((IN_TM, D_MODEL), lambda i, j: (i, 0)),
            pl.BlockSpec((MOD_ROWS, 6 * D_MODEL), lambda i, j: (0, 0)),
            pl.BlockSpec((1, D_MODEL), lambda i, j: (0, 0)),
            pl.BlockSpec((1, D_MODEL, IN_TN), lambda i, j: (layer, 0, j)),
        ],
        out_specs=[pl.BlockSpec((IN_TM, IN_TN), lambda i, j: (i, jnp.minimum(j, IN_NA - 1))),
                   pl.BlockSpec((IN_TM, IN_TN), lambda i, j: (i, jnp.maximum(j - IN_NA, 0)))],
        out_shape=[jax.ShapeDtypeStruct((n, P_GATE), F32), jax.ShapeDtypeStruct((n, P_COLS - P_GATE), BF16)],
        scratch_shapes=[pltpu.VMEM((IN_TM, D_MODEL), BF16)],
        compiler_params=_cparams(("parallel", "arbitrary")),
        name="input_projection",
    )(x, mod, norm_w.reshape(1, D_MODEL), w_p)


QK_TM = 512
DA_W = DA_HEADS * 2 * DA_QK


def _qkprep_kernel(q_ref, k_ref, v_ref, c_ref, se_ref, so_ref, qo_ref, ko_ref, vo_ref):
    i = pl.program_id(0)
    for h in range(DA_HEADS):
        vo_ref[:, h * 2 * DA_V:h * 2 * DA_V + DA_V] = v_ref[:, h * DA_V:(h + 1) * DA_V].astype(BF16)
        vo_ref[:, h * 2 * DA_V + DA_V:(h + 1) * 2 * DA_V] = jnp.ones((QK_TM, DA_V), BF16)
    scale = DA_QK ** -0.5

    @pl.when(i * QK_TM < N_CTX)
    def _():
        qo_ref[...] = (q_ref[...] * scale).astype(BF16)
        ko_ref[...] = k_ref[...].astype(BF16)

    @pl.when(i * QK_TM >= N_CTX)
    def _():
        c = c_ref[...]
        se = se_ref[...]
        so = so_ref[...]

        def rope(x):
            nxt = pltpu.roll(x, DA_W - 1, axis=1)
            prv = pltpu.roll(x, 1, axis=1)
            return x * c + nxt * se + prv * so

        qo_ref[...] = (rope(q_ref[...]) * scale).astype(BF16)
        ko_ref[...] = rope(k_ref[...]).astype(BF16)


def _rope_tables():
    rows = DEC_SEQ // GRID_W
    row = jnp.repeat(jnp.arange(rows, dtype=F32), GRID_W)
    col = jnp.tile(jnp.arange(GRID_W, dtype=F32), rows)
    n_pairs = DA_QK // 4
    inv = ROPE_BASE ** (-jnp.arange(n_pairs, dtype=F32) / n_pairs)
    ang = jnp.concatenate([row[:, None] * inv, col[:, None] * inv], axis=-1)
    cos = jnp.repeat(jnp.cos(ang), 2, axis=-1)
    sin = jnp.repeat(jnp.sin(ang), 2, axis=-1)
    even = (jnp.arange(DA_QK) % 2 == 0)[None, :]
    s_even = jnp.where(even, -sin, 0.0)
    s_odd = jnp.where(even, 0.0, sin)
    rep = lambda t: jnp.tile(t, (1, DA_W // DA_QK))
    return rep(cos), rep(s_even), rep(s_odd)


def _qk_prepare(proj, tables):
    n = proj.shape[0]
    lat0 = N_CTX // QK_TM
    nlat = DEC_SEQ // QK_TM
    tab = pl.BlockSpec((QK_TM, DA_W), lambda i: (jnp.maximum(i - lat0, 0) % nlat, 0))
    c0 = P_DA // DA_W
    out = jax.ShapeDtypeStruct((n, DA_W), BF16)
    return pl.pallas_call(
        _qkprep_kernel,
        grid=(n // QK_TM,),
        in_specs=[
            pl.BlockSpec((QK_TM, DA_W), lambda i: (i, c0)),
            pl.BlockSpec((QK_TM, DA_W), lambda i: (i, c0 + 1)),
            pl.BlockSpec((QK_TM, DA_W), lambda i: (i, c0 + 2)),
            tab, tab, tab,
        ],
        out_specs=[pl.BlockSpec((QK_TM, DA_W), lambda i: (i, 0))] * 2
        + [pl.BlockSpec((QK_TM, 2 * DA_W), lambda i: (i, 0))],
        out_shape=[out, out, jax.ShapeDtypeStruct((n, 2 * DA_W), BF16)],
        compiler_params=_cparams(("parallel",)),
        name="qk_prepare",
    )(proj, proj, proj, *tables)


DA_TQ = 256


def _attend(q, segments, lam, nw, o_ref):
    for h in range(DA_HEADS):
        os = []
        for m in range(2):
            c0 = (2 * h + m) * DA_QK
            ss = [_dg(q[:, c0:c0 + DA_QK], k[:, c0:c0 + DA_QK], NT) for k, _ in segments]
            mx = jnp.max(ss[0], axis=-1, keepdims=True)
            for s in ss[1:]:
                mx = jnp.maximum(mx, jnp.max(s, axis=-1, keepdims=True))
            oe = None
            for s, (_, v) in zip(ss, segments):
                part = _dg(jnp.exp(s - mx).astype(BF16), v[:, h * 2 * DA_V:(h + 1) * 2 * DA_V], NN)
                oe = part if oe is None else oe + part
            os.append(oe[:, :DA_V] * (1.0 / oe[:, DA_V:DA_V + 1]))
        o_ref[:, h * DA_V:(h + 1) * DA_V] = _rms(os[0] - lam * os[1], nw).astype(BF16)


def _da_kernel(q_ref, ks_ref, vs_ref, kl_ref, vl_ref, dl_ref, nw_ref, o_ref, *, lam_init):
    dl = dl_ref[...]
    lam = (jnp.exp(jnp.sum(dl[0:1] * dl[1:2], axis=1, keepdims=True))
           - jnp.exp(jnp.sum(dl[2:3] * dl[3:4], axis=1, keepdims=True)) + lam_init)
    nw = nw_ref[...] * (1.0 - lam_init)
    q = q_ref[...]
    is_ctx = pl.program_id(0) < BATCH * (SEQ // DA_TQ)

    @pl.when(is_ctx)
    def _():
        _attend(q, [(ks_ref, vs_ref)], lam, nw, o_ref)

    @pl.when(jnp.logical_not(is_ctx))
    def _():
        _attend(q, [(kl_ref.at[0], vl_ref.at[0])], lam, nw, o_ref)


def _da_branch(proj, cache_k, cache_v, da_lambda, da_norm_w, lam_init, tables):
    qb, kb, vb = _qk_prepare(proj, tables)
    n = proj.shape[0]
    ck = cache_k.reshape(DEC_BATCH, PAST_LEN, DA_W).astype(BF16)
    cv = cache_v.astype(BF16)
    cv = jnp.concatenate([cv, jnp.ones_like(cv)], axis=-1).reshape(DEC_BATCH, PAST_LEN, 2 * DA_W)
    lat = lambda t: t[N_CTX:].reshape(DEC_BATCH, DEC_SEQ, t.shape[-1])
    k_all = jnp.concatenate([ck, lat(kb)], axis=1)
    v_all = jnp.concatenate([cv, lat(vb)], axis=1)
    tk = PAST_LEN + DEC_SEQ
    n_ctx_tiles = N_CTX // DA_TQ
    per_seq = DEC_SEQ // DA_TQ
    seq_c = lambda i: jnp.minimum(i * DA_TQ // SEQ, BATCH - 1)
    seq_l = lambda i: jnp.clip((i - n_ctx_tiles) // per_seq, 0, DEC_BATCH - 1)
    once = pl.Buffered(1)
    return pl.pallas_call(
        functools.partial(_da_kernel, lam_init=lam_init),
        grid=(n // DA_TQ,),
        in_specs=[
            pl.BlockSpec((DA_TQ, DA_W), lambda i: (i, 0)),
            pl.BlockSpec((SEQ, DA_W), lambda i: (seq_c(i), 0)),
            pl.BlockSpec((SEQ, 2 * DA_W), lambda i: (seq_c(i), 0)),
            pl.BlockSpec((1, tk, DA_W), lambda i: (seq_l(i), 0, 0), pipeline_mode=once),
            pl.BlockSpec((1, tk, 2 * DA_W), lambda i: (seq_l(i), 0, 0), pipeline_mode=once),
            pl.BlockSpec((4, DA_QK), lambda i: (0, 0)),
            pl.BlockSpec((1, DA_V), lambda i: (0, 0)),
        ],
        out_specs=pl.BlockSpec((DA_TQ, DA_W), lambda i: (i, 0)),
        out_shape=jax.ShapeDtypeStruct((n, DA_W), BF16),
        compiler_params=_cparams(("arbitrary",)),
        name="diff_attention",
    )(qb, kb, vb, k_all, v_all, da_lambda, da_norm_w.reshape(1, DA_V))


RW_TM = 256
RW_C = 64
RW_PAIRS = RW_HEADS // 2
HALO = 8


def _head_ones():
    idx = jnp.arange(2 * RW_HD) // RW_HD
    return (idx[:, None] == idx[None, :]).astype(BF16)


def _head_sum(x, ones_pair):
    w = 2 * RW_HD
    return jnp.concatenate([_dot2x(x[:, p * w:(p + 1) * w], ones_pair) for p in range(RW_PAIRS)], axis=1)


def _rwprep_kernel(x_ref, xp_ref, xn_ref, mu_ref, w0_ref, wup_ref, a0_ref, aup_ref, gup_ref, kk_ref, ka_ref,
                   ones_ref, r_o, v_o, kk_o, g_o, kd_o, lw_o, a_o, buf):
    row0 = pl.program_id(0) * RW_TM
    x = x_ref[:, 0:RW_COLS]
    buf[HALO:HALO + RW_TM, :] = x
    buf[HALO - 1:HALO, :] = xp_ref[HALO - 1:HALO, 0:RW_COLS]
    buf[HALO + RW_TM:HALO + RW_TM + 1, :] = xn_ref[0:1, 0:RW_COLS]
    rows = row0 + lax.broadcasted_iota(jnp.int32, (RW_TM, 1), 0)
    pos, length = _seq_pos(rows)
    prev = jnp.where(pos == 0, 0.0, buf[HALO - 1:HALO - 1 + RW_TM, :])
    nxt = jnp.where(pos == length - 1, 0.0, buf[HALO + 1:HALO + 1 + RW_TM, :])
    mu = mu_ref[...]
    u = x + mu[0:1] * (prev - x) + mu[1:2] * (nxt - x)

    r = u[:, 0:RW_W]
    k = u[:, RW_W:2 * RW_W]
    v = u[:, 2 * RW_W:3 * RW_W]
    wl = u[:, 3 * RW_W:3 * RW_W + 128]
    al = u[:, 3 * RW_W + 128:3 * RW_W + 256]
    gl = u[:, 3 * RW_W + 256:3 * RW_W + 384]
    w_raw = w0_ref[...] + _dot3(jnp.tanh(wl), wup_ref[...])
    lw = -jnp.exp(-_softplus(-w_raw) - 0.5)
    a = _sigmoid(a0_ref[...] + _dot3(al, aup_ref[...]))
    g = _dot3(_sigmoid(gl), gup_ref[...])
    kk = k * kk_ref[...]
    kk = kk * lax.rsqrt(jnp.maximum(_head_sum(kk * kk, ones_ref[...]), 1e-12))
    kd = jnp.concatenate([k, k], axis=1) * (1.0 + (a - 1.0) * ka_ref[...])

    r_o[...] = r
    v_o[...] = v
    kk_o[...] = kk
    g_o[...] = g
    for d in range(2):
        kd_o[d] = kd[:, d * RW_W:(d + 1) * RW_W]
        lw_o[d] = lw[:, d * RW_W:(d + 1) * RW_W]
        a_o[d] = a[:, d * RW_W:(d + 1) * RW_W]


def _rw_prepare(proj, p):
    n = proj.shape[0]
    nh = n // HALO
    steps = RW_TM // HALO
    wide = P_DA - P_RW
    cat2 = lambda t: t.reshape(1, 2 * RW_W)
    blockdiag = lambda t: jnp.concatenate(
        [jnp.concatenate([t[0], jnp.zeros_like(t[0])], axis=1),
         jnp.concatenate([jnp.zeros_like(t[1]), t[1]], axis=1)], axis=0)
    const = lambda shape: pl.BlockSpec(shape, lambda i: (0,) * len(shape))
    row = pl.BlockSpec((RW_TM, RW_W), lambda i: (i, 0))
    row2 = pl.BlockSpec((2, RW_TM, RW_W), lambda i: (0, i, 0))
    o1 = jax.ShapeDtypeStruct((n, RW_W), F32)
    o2 = jax.ShapeDtypeStruct((2, n, RW_W), F32)
    return pl.pallas_call(
        _rwprep_kernel,
        grid=(n // RW_TM,),
        in_specs=[
            pl.BlockSpec((RW_TM, wide), lambda i: (i, 0)),
            pl.BlockSpec((HALO, wide), lambda i: (jnp.maximum(i * steps - 1, 0), 0)),
            pl.BlockSpec((HALO, wide), lambda i: (jnp.minimum((i + 1) * steps, nh - 1), 0)),
            const((2, RW_COLS)), const((1, 2 * RW_W)), const((128, 2 * RW_W)), const((1, 2 * RW_W)),
            const((128, 2 * RW_W)), const((RW_G_LORA, RW_W)), const((1, RW_W)), const((1, 2 * RW_W)),
            const((2 * RW_HD, 2 * RW_HD)),
        ],
        out_specs=[row, row, row, row, row2, row2, row2],
        out_shape=[o1, o1, o1, o1, o2, o2, o2],
        scratch_shapes=[pltpu.VMEM((RW_TM + 2 * HALO, RW_COLS), F32)],
        compiler_params=_cparams(("parallel",)),
        name="rwkv_prepare",
    )(proj, proj, proj, p['rw_shift'], cat2(p['rw_w0']), blockdiag(p['rw_w_up']), cat2(p['rw_a0']),
      blockdiag(p['rw_a_up']), p['rw_g_up'], p['rw_k_k'].reshape(1, RW_W),
      jnp.tile(p['rw_k_a'].reshape(1, RW_W), (1, 2)), _head_ones())


RW_TB = 256
RW_NC = RW_TB // RW_C
RW_SIDE = 8


def _rw_finish(y, r, v, g, kd_sum, rk, lnw, lnb, ones):
    inv = 1.0 / RW_HD
    xc = y - _head_sum(y, ones) * inv
    var = _head_sum(xc * xc, ones) * inv
    yn = xc * lax.rsqrt(var + RW_GN_EPS) * lnw + lnb
    bonus = _head_sum(r * kd_sum * rk, ones) * v
    return ((yn + bonus) * g).astype(BF16)


def _rwscan_kernel(r_ref, v_ref, kk_ref, kd_ref, lw_ref, a_ref, tri_ref, inc_ref, str_ref, s0_ref, *rest, backward):
    if backward:
        yf_ref, g_ref, rk_ref, lnw_ref, lnb_ref, ones_ref, y_ref, st_ref, s_scr = rest
    else:
        y_ref, st_ref, s_scr = rest
    _rwscan_body(r_ref, v_ref, kk_ref, kd_ref, lw_ref, a_ref, tri_ref, inc_ref, str_ref, s0_ref, y_ref, st_ref,
                 s_scr, backward,
                 (lambda yb: _rw_finish(yf_ref[...] + yb, r_ref[...], v_ref[...], g_ref[...],
                                        kd_ref[0] + kd_ref[1], rk_ref[...], lnw_ref[...], lnb_ref[...],
                                        ones_ref[...])) if backward else None)


def _rwscan_body(r_ref, v_ref, kk_ref, kd_ref, lw_ref, a_ref, tri_ref, inc_ref, str_ref, s0_ref, y_ref, st_ref,
                 s_scr, backward, finish):
    step = pl.program_id(0)
    nb = pl.num_programs(0)
    bi = (nb - 1 - step) if backward else step
    pos, length = _seq_pos(bi * RW_TB)
    first = (pos + RW_TB == length) if backward else (pos == 0)
    last = (pos == 0) if backward else (pos + RW_TB == length)
    in_ctx = bi * RW_TB < N_CTX
    c = RW_C

    @pl.when(jnp.logical_and(first, in_ctx))
    def _():
        s_scr[...] = jnp.zeros_like(s_scr)

    @pl.when(jnp.logical_and(first, jnp.logical_not(in_ctx)))
    def _():
        z = jnp.zeros((c, c), F32)
        for p in range(RW_PAIRS):
            s_scr[p] = jnp.concatenate(
                [jnp.concatenate([s0_ref[0, 0, 2 * p], z], axis=1),
                 jnp.concatenate([z, s0_ref[0, 0, 2 * p + 1]], axis=1)], axis=0)

    incl2 = inc_ref[...] > 0.5
    strict2 = str_ref[...] > 0.5
    eye2 = (lax.broadcasted_iota(jnp.int32, (2 * c, 2 * c), 0)
            == lax.broadcasted_iota(jnp.int32, (2 * c, 2 * c), 1))
    m_e = lax.broadcasted_iota(jnp.int32, (1, 2 * c), 1) < c

    lw = lw_ref[0]
    cum = _dotx(tri_ref[...], lw)
    tots = [cum[(j * c if backward else j * c + c - 1):(j * c + 1 if backward else j * c + c), :]
            for j in range(RW_NC)]
    tot_b = jnp.concatenate([jnp.broadcast_to(t, (c, RW_W)) for t in tots], axis=0)
    kk = kk_ref[...]
    kd = kd_ref[1 if backward else 0]
    bp = kk * a_ref[0]
    g_inv = jnp.exp(-cum)
    g_rem = jnp.exp(tot_b - cum)
    ag = -kk * jnp.exp(cum - lw)
    rg = r_ref[...] * jnp.exp(cum)
    bdn = bp * g_inv
    kdn = kd * g_inv
    bc = bp * g_rem
    kc = kd * g_rem
    v = v_ref[...]

    def stack(x, j, p):
        xs = x[j * c:(j + 1) * c, p * 2 * c:(p + 1) * 2 * c]
        return jnp.concatenate([jnp.where(m_e, xs, 0.0), jnp.where(m_e, 0.0, xs)], axis=0).astype(BF16)

    pre = {}
    keys = [(j, p) for j in range(RW_NC) for p in range(RW_PAIRS)]
    for g0 in range(0, len(keys), RW_SIDE):
        grp = keys[g0:g0 + RW_SIDE]
        ops = {k: tuple(stack(t, *k) for t in (ag, rg, bdn, kdn, bc, kc, v)) for k in grp}
        gm = {k: _dg(jnp.concatenate([ops[k][0], ops[k][1]], axis=0),
                     jnp.concatenate([ops[k][2], ops[k][3]], axis=0), NT) for k in grp}
        lbb = {k: jnp.where(strict2, gm[k][:2 * c, :2 * c], 0.0) for k in grp}
        lkb = {k: jnp.where(strict2, gm[k][:2 * c, 2 * c:], 0.0).astype(BF16) for k in grp}
        lrk = {k: jnp.concatenate([jnp.where(incl2, gm[k][2 * c:, :2 * c], 0.0),
                                   jnp.where(incl2, gm[k][2 * c:, 2 * c:], 0.0)], axis=1).astype(BF16)
               for k in grp}
        lv = {k: _dg(lkb[k], ops[k][6], NN) for k in grp}
        x = {k: jnp.where(eye2, 1.0, lbb[k]) for k in grp}
        pw = lbb
        for _ in range(int(math.log2(c)) - 1):
            pwb = {k: pw[k].astype(BF16) for k in grp}
            pw = {k: _dg(pwb[k], pwb[k], NN) for k in grp}
            x = {k: x[k] + _dg(x[k].astype(BF16), pw[k].astype(BF16), NN) for k in grp}
        tw = {k: _dg(x[k].astype(BF16), jnp.concatenate([lv[k].astype(BF16), ops[k][0]], axis=1), NN)
              for k in grp}
        for k in grp:
            pre[k] = (tw[k][:, :2 * c], tw[k][:, 2 * c:].astype(BF16), ops[k][1], lrk[k], ops[k][6],
                      jnp.concatenate([ops[k][4], ops[k][5]], axis=0))

    order = range(RW_NC - 1, -1, -1) if backward else range(RW_NC)
    pairs = range(RW_PAIRS)
    s = [s_scr[p] for p in pairs]
    ys = {}
    for j in order:
        sb = [s[p].astype(BF16) for p in pairs]
        u = [_dg(pre[j, p][1], sb[p], NT) + pre[j, p][0] for p in pairs]
        uv = [jnp.concatenate([u[p].astype(BF16), pre[j, p][4]], axis=0) for p in pairs]
        y = [_dg(pre[j, p][2], sb[p], NT) + _dg(pre[j, p][3], uv[p], NN) for p in pairs]
        s = [s[p] * jnp.exp(tots[j][:, p * 2 * c:(p + 1) * 2 * c]) + _dg(uv[p], pre[j, p][5], TN) for p in pairs]
        ys[j] = [y[p][:c] + y[p][c:] for p in pairs]
    y_blk = jnp.concatenate([jnp.concatenate(ys[j], axis=1) for j in range(RW_NC)], axis=0)
    y_ref[...] = finish(y_blk) if finish else y_blk
    for p in pairs:
        s_scr[p] = s[p]

    @pl.when(last)
    def _():
        for p in pairs:
            st_ref[0, 2 * p] = s[p][:c, :c]
            st_ref[0, 2 * p + 1] = s[p][c:, c:]


def _scan_masks(backward):
    t = jnp.arange(RW_TB)
    sgn = -1 if backward else 1
    same_chunk = (t[:, None] // RW_C) == (t[None, :] // RW_C)
    tri = (same_chunk & ((t[:, None] - t[None, :]) * sgn >= 0)).astype(BF16)
    q = jnp.arange(2 * RW_C)
    same_head = (q[:, None] // RW_C) == (q[None, :] // RW_C)
    dif = ((q[:, None] % RW_C) - (q[None, :] % RW_C)) * sgn
    return tri, (same_head & (dif >= 0)).astype(F32), (same_head & (dif > 0)).astype(F32)


def _rw_scan(r, v, kk, kd, lw, a, s0, backward, finish=()):
    n = r.shape[0]
    nb = n // RW_TB
    d = 1 if backward else 0
    blk = (lambda i: nb - 1 - i) if backward else (lambda i: i)
    row = pl.BlockSpec((RW_TB, RW_W), lambda i: (blk(i), 0))
    row2 = pl.BlockSpec((1, RW_TB, RW_W), lambda i: (d, blk(i), 0))
    both = pl.BlockSpec((2, RW_TB, RW_W), lambda i: (0, blk(i), 0))
    const = lambda shape: pl.BlockSpec(shape, lambda i: (0,) * len(shape))
    vec = const((1, RW_W))
    pair = 2 * RW_C
    extra = [row, row, vec, vec, vec, const((pair, pair))] if backward else []
    return pl.pallas_call(
        functools.partial(_rwscan_kernel, backward=backward),
        grid=(nb,),
        in_specs=[row, row, row, both, row2, row2, const((RW_TB, RW_TB)), const((pair, pair)), const((pair, pair)),
                  pl.BlockSpec((1, 1, RW_HEADS, RW_HD, RW_HD),
                               lambda i: (jnp.maximum(_seq_index(blk(i) * RW_TB) - BATCH, 0), d, 0, 0, 0))] + extra,
        out_specs=[row, pl.BlockSpec((1, RW_HEADS, RW_HD, RW_HD), lambda i: (_seq_index(blk(i) * RW_TB), 0, 0, 0))],
        out_shape=[jax.ShapeDtypeStruct((n, RW_W), BF16 if backward else F32),
                   jax.ShapeDtypeStruct((N_SEQS, RW_HEADS, RW_HD, RW_HD), F32)],
        scratch_shapes=[pltpu.VMEM((RW_PAIRS, pair, pair), F32)],
        compiler_params=_cparams(("arbitrary",)),
        name="rwkv_scan_bwd" if backward else "rwkv_scan_fwd",
    )(r, v, kk, kd, lw, a, *_scan_masks(backward), s0, *finish)


def _rwkv_branch(proj, state0, p):
    r, v, kk, g, kd, lw, a = _rw_prepare(proj, p)
    yf, sf = _rw_scan(r, v, kk, kd, lw, a, state0, backward=False)
    finish = (yf, g, p['rw_r_k'].reshape(1, RW_W), p['rw_ln_w'].reshape(1, RW_W), p['rw_ln_b'].reshape(1, RW_W),
              _head_ones())
    out, sb = _rw_scan(r, v, kk, kd, lw, a, state0, backward=True, finish=finish)
    return out, jnp.stack([sf, sb], axis=1)


RET_W = RET_HEADS * RET_DV
RET_PAIRS = RET_HEADS // 2


RET_TB = 256
RET_NC = RET_TB // RET_CHUNK


def _ret_kernel(q_ref, k_ref, v_ref, lg_ref, s0_ref, o_ref, st_ref, s_scr, *, backward):
    step = pl.program_id(0)
    nb = pl.num_programs(0)
    bi = (nb - 1 - step) if backward else step
    pos, length = _seq_pos(bi * RET_TB)
    first = (pos + RET_TB == length) if backward else (pos == 0)
    in_ctx = bi * RET_TB < N_CTX

    @pl.when(jnp.logical_and(first, in_ctx))
    def _():
        s_scr[...] = jnp.zeros_like(s_scr)

    @pl.when(jnp.logical_and(first, jnp.logical_not(in_ctx)))
    def _():
        s_scr[...] = s0_ref[0, 0]

    c = RET_CHUNK
    d = 1 if backward else 0
    sgn = -1 if backward else 1
    lgs = -_softplus(-lg_ref[d:d + 1, :])
    ri = lax.broadcasted_iota(jnp.int32, (c, c), 0)
    cj = lax.broadcasted_iota(jnp.int32, (c, c), 1)
    dif = (ri - cj) * sgn
    valid = dif >= 0
    dist = jnp.maximum(dif, 0).astype(F32)
    pr = lax.broadcasted_iota(jnp.int32, (c, 2 * RET_DK), 0)
    tau = ((c - 1 - pr) if backward else pr).astype(F32)
    low = lax.broadcasted_iota(jnp.int32, (1, 2 * RET_DK), 1) < RET_DK
    rlow = lax.broadcasted_iota(jnp.int32, (2 * RET_DK, RET_DV), 0) < RET_DK
    heads = range(RET_HEADS)
    pairs = range(RET_PAIRS)
    mask = [low, jnp.logical_not(low)]
    lg_h = [lgs[:, h:h + 1] for h in heads]
    lg_row = [jnp.where(low, lg_h[2 * p], lg_h[2 * p + 1]) for p in pairs]
    dmat = [jnp.where(valid, jnp.exp(lg_h[h] * dist), 0.0) for h in heads]
    q_dec = [jnp.exp(lg_row[p] * (tau + 1.0)) for p in pairs]
    k_dec = [jnp.exp(lg_row[p] * (c - 1.0 - tau)) for p in pairs]
    c_dec = [jnp.where(rlow, jnp.exp(lg_h[2 * p] * c), jnp.exp(lg_h[2 * p + 1] * c)) for p in pairs]

    pre = []
    for j in range(RET_NC):
        rows = slice(j * c, (j + 1) * c)
        qp = [q_ref[rows, p * 128:(p + 1) * 128] for p in pairs]
        kp = [k_ref[rows, p * 128:(p + 1) * 128] * (RET_DK ** -0.5) for p in pairs]
        kpb = [t.astype(BF16) for t in kp]
        vb = [v_ref[rows, h * RET_DV:(h + 1) * RET_DV].astype(BF16) for h in heads]
        att = [_dg(jnp.where(mask[h % 2], qp[h // 2], 0.0).astype(BF16), kpb[h // 2], NT) * dmat[h]
               for h in heads]
        upd = [_dg(jnp.where(mask[h % 2], kp[h // 2] * k_dec[h // 2], 0.0).astype(BF16), vb[h], TN)
               for h in heads]
        inner = [_dg(att[h].astype(BF16), vb[h], NN) for h in heads]
        qd = [jnp.where(mask[h % 2], qp[h // 2] * q_dec[h // 2], 0.0).astype(BF16) for h in heads]
        pre.append((inner, qd, upd))

    s = [s_scr[p] for p in pairs]
    for j in (range(RET_NC - 1, -1, -1) if backward else range(RET_NC)):
        inner, qd, upd = pre[j]
        sb = [t.astype(BF16) for t in s]
        for h in heads:
            o_ref[j * c:(j + 1) * c, h * RET_DV:(h + 1) * RET_DV] = inner[h] + _dg(qd[h], sb[h // 2], NN)
        s = [s[p] * c_dec[p] + upd[2 * p] + upd[2 * p + 1] for p in pairs]
    for p in pairs:
        s_scr[p] = s[p]
        st_ref[0, p] = s[p]


def _ret_scan(proj, logit, s0, backward):
    n = proj.shape[0]
    nb = n // RET_TB
    d = 1 if backward else 0
    blk = (lambda i: nb - 1 - i) if backward else (lambda i: i)
    qk_w = RET_HEADS * RET_DK
    state = (RET_PAIRS, 2 * RET_DK, RET_DV)
    return pl.pallas_call(
        functools.partial(_ret_kernel, backward=backward),
        grid=(nb,),
        in_specs=[
            pl.BlockSpec((RET_TB, qk_w), lambda i: (blk(i), P_RET // qk_w)),
            pl.BlockSpec((RET_TB, qk_w), lambda i: (blk(i), P_RET // qk_w + 1)),
            pl.BlockSpec((RET_TB, RET_W), lambda i: (blk(i), (P_RET + 2 * qk_w) // RET_W)),
            pl.BlockSpec((2, RET_HEADS), lambda i: (0, 0)),
            pl.BlockSpec((1, 1) + state,
                         lambda i: (jnp.maximum(_seq_index(blk(i) * RET_TB) - BATCH, 0), d, 0, 0, 0)),
        ],
        out_specs=[pl.BlockSpec((RET_TB, RET_W), lambda i: (blk(i), 0)),
                   pl.BlockSpec((1,) + state, lambda i: (_seq_index(blk(i) * RET_TB), 0, 0, 0))],
        out_shape=[jax.ShapeDtypeStruct((n, RET_W), F32), jax.ShapeDtypeStruct((N_SEQS,) + state, F32)],
        scratch_shapes=[pltpu.VMEM(state, F32)],
        compiler_params=_cparams(("arbitrary",)),
        name="retention_scan_bwd" if backward else "retention_scan_fwd",
    )(proj, proj, proj, logit, s0)


def _standardize(x, eps):
    mu = jnp.mean(x, axis=-1, keepdims=True)
    xc = x - mu
    return xc * lax.rsqrt(jnp.mean(xc * xc, axis=-1, keepdims=True) + eps)


def _retpost_kernel(of_ref, ob_ref, g_ref, nw_ref, out_ref):
    o = of_ref[...] + ob_ref[...]
    g = g_ref[...]
    for h in range(RET_HEADS):
        sl = slice(h * RET_DV, (h + 1) * RET_DV)
        gh = g[:, sl]
        out_ref[:, sl] = (gh * _sigmoid(gh) * (_standardize(o[:, sl], EPS) * nw_ref[:, sl])).astype(BF16)


def _ret_post(of, ob, proj, norm_w):
    n = proj.shape[0]
    tm = 512
    return pl.pallas_call(
        _retpost_kernel,
        grid=(n // tm,),
        in_specs=[
            pl.BlockSpec((tm, RET_W), lambda i: (i, 0)),
            pl.BlockSpec((tm, RET_W), lambda i: (i, 0)),
            pl.BlockSpec((tm, RET_W), lambda i: (i, (P_RET + 1024) // RET_W)),
            pl.BlockSpec((1, RET_W), lambda i: (0, 0)),
        ],
        out_specs=pl.BlockSpec((tm, RET_W), lambda i: (i, 0)),
        out_shape=jax.ShapeDtypeStruct((n, RET_W), BF16),
        compiler_params=_cparams(("parallel",)),
        name="retention_post",
    )(of, ob, proj, norm_w.reshape(1, RET_W))


def _retention_branch(proj, state0, p):
    s0 = state0.reshape(DEC_BATCH, 2, RET_PAIRS, 2 * RET_DK, RET_DV)
    of, sf = _ret_scan(proj, p['ret_decay_logit'], s0, backward=False)
    ob, sb = _ret_scan(proj, p['ret_decay_logit'], s0, backward=True)
    st = jnp.stack([sf, sb], axis=1).reshape(N_SEQS, 2, RET_HEADS, RET_DK, RET_DV)
    return _ret_post(of, ob, proj, p['ret_norm_w']), st


CV_TM = 256
CV_HALO = 16


def _conv_kernel(a_ref, g_ref, ap_ref, gp_ref, an_ref, gn_ref, w_ref, b_ref, lnw_ref, lnb_ref, o_ref, buf, sbuf):
    row0 = pl.program_id(0) * CV_TM
    pos0, len0 = _seq_pos(row0)
    buf[CV_HALO:CV_HALO + CV_TM, :] = a_ref[...] * _sigmoid(g_ref[...])
    buf[0:CV_HALO, :] = jnp.where(pos0 == 0, 0.0, ap_ref[...] * _sigmoid(gp_ref[...]))
    buf[CV_HALO + CV_TM:, :] = jnp.where(pos0 + CV_TM == len0, 0.0, an_ref[...] * _sigmoid(gn_ref[...]))
    base = CV_HALO - CONV_K // 2
    acc = jnp.zeros((CV_TM, CV_W), F32)
    for r in range(8):
        taps = [m for m in range((base + CONV_K + 7) // 8) if 0 <= r + 8 * m - base < CONV_K]
        span = CV_TM + 8 * max(taps)
        if r:
            sbuf[0:span, :] = buf[r:r + span, :]
        src = sbuf if r else buf
        for m in taps:
            j = r + 8 * m - base
            acc = acc + w_ref[j:j + 1, :] * src[8 * m:8 * m + CV_TM, :]
    z = _standardize(acc + b_ref[...], EPS) * lnw_ref[...] + lnb_ref[...]
    o_ref[...] = (z * _sigmoid(z)).astype(BF16)


def _conv_branch(proj, p):
    n = proj.shape[0]
    nh = n // CV_HALO
    steps = CV_TM // CV_HALO
    ca = P_CV // CV_W
    prev = lambda i: jnp.maximum(i * steps - 1, 0)
    nxt = lambda i: jnp.minimum((i + 1) * steps, nh - 1)
    vec = pl.BlockSpec((1, CV_W), lambda i: (0, 0))
    return pl.pallas_call(
        _conv_kernel,
        grid=(n // CV_TM,),
        in_specs=[
            pl.BlockSpec((CV_TM, CV_W), lambda i: (i, ca)),
            pl.BlockSpec((CV_TM, CV_W), lambda i: (i, ca + 1)),
            pl.BlockSpec((CV_HALO, CV_W), lambda i: (prev(i), ca)),
            pl.BlockSpec((CV_HALO, CV_W), lambda i: (prev(i), ca + 1)),
            pl.BlockSpec((CV_HALO, CV_W), lambda i: (nxt(i), ca)),
            pl.BlockSpec((CV_HALO, CV_W), lambda i: (nxt(i), ca + 1)),
            pl.BlockSpec((CONV_K, CV_W), lambda i: (0, 0)),
            vec, vec, vec,
        ],
        out_specs=pl.BlockSpec((CV_TM, CV_W), lambda i: (i, 0)),
        out_shape=jax.ShapeDtypeStruct((n, CV_W), BF16),
        scratch_shapes=[pltpu.VMEM((CV_TM + 2 * CV_HALO, CV_W), F32)] * 2,
        compiler_params=_cparams(("parallel",)),
        name="conformer_conv",
    )(proj, proj, proj, proj, proj, proj, p['cv_dw_w'], p['cv_dw_b'].reshape(1, CV_W),
      p['cv_ln_w'].reshape(1, CV_W), p['cv_ln_b'].reshape(1, CV_W))


MG_TM = 512


def _merge_kernel(da_ref, rw_ref, ret_ref, cv_ref, g0_ref, g1_ref, g2_ref, g3_ref, x_ref, mod_ref, wb_ref, wo_ref,
                  nw_ref, x_o, h_o):
    m = None
    for n, (br, gt) in enumerate(((da_ref, g0_ref), (rw_ref, g1_ref), (ret_ref, g2_ref), (cv_ref, g3_ref))):
        t = _sigmoid(gt[...].astype(F32)) * _dg(br[...], wb_ref[n], NN)
        m = t if m is None else m + t
    out = _dg(m.astype(BF16), wo_ref[...], NN)
    mrow = _mod_row(mod_ref, pl.program_id(0) * MG_TM)
    gate1 = mrow[:, 2 * D_MODEL:3 * D_MODEL]
    sh2 = mrow[:, 3 * D_MODEL:4 * D_MODEL]
    sc2 = mrow[:, 4 * D_MODEL:5 * D_MODEL]
    x1 = x_ref[...] + gate1 * out
    x_o[...] = x1
    h_o[...] = _rms(x1, nw_ref[...]) * (1.0 + sc2) + sh2


def _merge(branches, gates, x, mod, w_branch, w_out, norm_w):
    n = x.shape[0]
    br = pl.BlockSpec((MG_TM, BR_W), lambda i: (i, 0))
    gspec = lambda j: pl.BlockSpec((MG_TM, D_MODEL), lambda i: (i, j))
    full = pl.BlockSpec((MG_TM, D_MODEL), lambda i: (i, 0))
    out = jax.ShapeDtypeStruct((n, D_MODEL), F32)
    return pl.pallas_call(
        _merge_kernel,
        grid=(n // MG_TM,),
        in_specs=[br, br, br, br, gspec(0), gspec(1), gspec(2), gspec(3), full,
                  pl.BlockSpec((MOD_ROWS, 6 * D_MODEL), lambda i: (0, 0)),
                  pl.BlockSpec((N_BRANCH, BR_W, D_MODEL), lambda i: (0, 0, 0)),
                  pl.BlockSpec((D_MODEL, D_MODEL), lambda i: (0, 0)),
                  pl.BlockSpec((1, D_MODEL), lambda i: (0, 0))],
        out_specs=[full, full],
        out_shape=[out, out],
        compiler_params=_cparams(("parallel",)),
        name="gated_merge",
    )(*branches, gates, gates, gates, gates, x, mod, w_branch.astype(BF16), w_out.astype(BF16),
      norm_w.reshape(1, D_MODEL))


RT_TM = 256
MOE_BM = 512
MOE_ROWS = N_TOK * TOP_K + N_EXPERTS * MOE_BM
DP_TM = 512
DP_GROUP = 64
CB_TM = 256
DMA_UNROLL = 8


def _router_kernel(h_ref, w_ref, b_ref, tri_ref, idx_o, gate_o, rank_o, cnt_o, carry):
    @pl.when(pl.program_id(0) == 0)
    def _():
        carry[...] = jnp.zeros_like(carry)

    logits = _dot3(w_ref[...], h_ref[...], NT) + b_ref[...]
    e_iota = lax.broadcasted_iota(jnp.int32, logits.shape, 0)
    work = logits
    vals, idxs, hots = [], [], []
    for _ in range(TOP_K):
        mx = jnp.max(work, axis=0, keepdims=True)
        ix = jnp.min(jnp.where(work == mx, e_iota, N_EXPERTS), axis=0, keepdims=True)
        hot = e_iota == ix
        vals.append(mx)
        idxs.append(ix)
        hots.append(hot.astype(F32))
        work = jnp.where(hot, -jnp.inf, work)
    es = [jnp.exp(v - vals[0]) for v in vals]
    inv = 1.0 / (es[0] + es[1] + es[2] + es[3])
    chosen = hots[0] + hots[1] + hots[2] + hots[3]
    ahead = carry[...][:, 0:1] + _dg(chosen.astype(BF16), tri_ref[...], NN)
    idx_o[...] = jnp.concatenate(idxs, axis=0)
    gate_o[...] = jnp.concatenate([e * inv for e in es], axis=0)
    rank_o[...] = jnp.concatenate(
        [jnp.sum(hot * ahead, axis=0, keepdims=True) for hot in hots], axis=0).astype(jnp.int32)
    carry[...] = carry[...] + jnp.sum(chosen, axis=1, keepdims=True)
    cnt_o[...] = carry[...]


def _router(h, router_w, router_b):
    n = h.shape[0]
    tri = (jnp.arange(RT_TM)[:, None] < jnp.arange(RT_TM)[None, :]).astype(BF16)
    col = pl.BlockSpec((TOP_K, RT_TM), lambda i: (0, i))
    return pl.pallas_call(
        _router_kernel,
        grid=(n // RT_TM,),
        in_specs=[
            pl.BlockSpec((RT_TM, D_MODEL), lambda i: (i, 0)),
            pl.BlockSpec((N_EXPERTS, D_MODEL), lambda i: (0, 0)),
            pl.BlockSpec((N_EXPERTS, 1), lambda i: (0, 0)),
            pl.BlockSpec((RT_TM, RT_TM), lambda i: (0, 0)),
        ],
        out_specs=[col, col, col, pl.BlockSpec((N_EXPERTS, 128), lambda i: (0, 0))],
        out_shape=[jax.ShapeDtypeStruct((TOP_K, n), jnp.int32), jax.ShapeDtypeStruct((TOP_K, n), F32),
                   jax.ShapeDtypeStruct((TOP_K, n), jnp.int32), jax.ShapeDtypeStruct((N_EXPERTS, 128), F32)],
        scratch_shapes=[pltpu.VMEM((N_EXPERTS, 128), F32)],
        compiler_params=_cparams(("arbitrary",)),
        name="router",
    )(h, router_w.T, router_b.reshape(N_EXPERTS, 1), tri)


def _tile_major(t, tm):
    k, n = t.shape
    return t.reshape(k, n // tm, tm).transpose(1, 0, 2).reshape(n // tm, 1, k * tm)


def _dispatch_kernel(dest_ref, pe_ref, h_ref, o_hbm, zbuf, sem, zsem):
    n_groups = DP_TM // DP_GROUP

    @pl.when(pl.program_id(0) == 0)
    def _():
        zbuf[...] = jnp.zeros_like(zbuf)

        def fill(e):
            end = pe_ref[e]
            begin = pe_ref[e - 1] if e else 0
            return end > begin, pltpu.make_async_copy(
                zbuf, o_hbm.at[pl.ds(pl.multiple_of(jnp.maximum(end - MOE_BM, 0), MOE_BM), MOE_BM)], zsem.at[0])

        for e in range(N_EXPERTS):
            nonempty, cp = fill(e)
            pl.when(nonempty)(cp.start)
        for e in range(N_EXPERTS):
            nonempty, cp = fill(e)
            pl.when(nonempty)(cp.wait)

        def tail(b):
            return pltpu.make_async_copy(zbuf, o_hbm.at[pl.ds(pl.multiple_of(b * MOE_BM, MOE_BM), MOE_BM)],
                                         zsem.at[0])

        first_unused = pe_ref[N_EXPERTS - 1] // MOE_BM
        lax.fori_loop(first_unused, MOE_ROWS // MOE_BM, lambda b, c: (tail(b).start(), c)[1], 0)
        lax.fori_loop(first_unused, MOE_ROWS // MOE_BM, lambda b, c: (tail(b).wait(), c)[1], 0)

    def wait_group(slot):
        pltpu.make_async_copy(h_ref.at[pl.ds(0, TOP_K * DP_GROUP)], o_hbm.at[pl.ds(0, TOP_K * DP_GROUP)],
                              sem.at[slot]).wait()

    def group(gi, carry):
        slot = gi % 2

        def issue(t, c):
            tok = gi * DP_GROUP + t
            for k in range(TOP_K):
                dst = dest_ref[0, 0, k * DP_TM + tok]
                pltpu.make_async_copy(h_ref.at[pl.ds(tok, 1)], o_hbm.at[pl.ds(dst, 1)], sem.at[slot]).start()
            return c

        lax.fori_loop(0, DP_GROUP, issue, 0, unroll=DMA_UNROLL)

        @pl.when(gi > 0)
        def _():
            wait_group(1 - slot)

        return carry

    lax.fori_loop(0, n_groups, group, 0)
    wait_group((n_groups - 1) % 2)


def _dispatch(h, dest, pad_end):
    n = h.shape[0]
    return pl.pallas_call(
        _dispatch_kernel,
        grid=(n // DP_TM,),
        in_specs=[
            pl.BlockSpec((1, 1, TOP_K * DP_TM), lambda i: (i, 0, 0), memory_space=pltpu.SMEM),
            pl.BlockSpec(memory_space=pltpu.SMEM),
            pl.BlockSpec((DP_TM, D_MODEL), lambda i: (i, 0)),
        ],
        out_specs=pl.BlockSpec(memory_space=pl.ANY),
        out_shape=jax.ShapeDtypeStruct((MOE_ROWS, D_MODEL), F32),
        scratch_shapes=[pltpu.VMEM((MOE_BM, D_MODEL), F32), pltpu.SemaphoreType.DMA((2,)),
                        pltpu.SemaphoreType.DMA((1,))],
        compiler_params=_cparams(("arbitrary",)),
        name="moe_dispatch",
    )(_tile_major(dest, DP_TM), pad_end, h)


def _expert_kernel(bx_ref, be_ref, nv_ref, x_ref, w1_ref, b1_ref, w2_ref, b2_ref, o_ref, w1b, w2b):
    i = pl.program_id(0)
    changed = jnp.logical_or(i == 0, be_ref[i] != be_ref[jnp.maximum(i - 1, 0)])

    @pl.when(changed)
    def _():
        w1b[...] = w1_ref[0, 0].astype(BF16)
        w2b[...] = w2_ref[0, 0].astype(BF16)

    def ffn(rows):
        hb = _dg(x_ref[0:rows, :].astype(BF16), w1b[...], NN) + b1_ref[0, 0]
        hg = jnp.minimum(hb[:, :D_FF], SWIGLU_LIMIT)
        hu = jnp.clip(hb[:, D_FF:], -SWIGLU_LIMIT, SWIGLU_LIMIT)
        act = hg * _sigmoid(SWIGLU_ALPHA * hg) * (hu + 1.0)
        o_ref[0:rows, :] = _dg(act.astype(BF16), w2b[...], NN) + b2_ref[0, 0]

    nv = nv_ref[i]
    half = MOE_BM // 2

    @pl.when(nv > half)
    def _():
        ffn(MOE_BM)

    @pl.when(jnp.logical_and(nv > 0, nv <= half))
    def _():
        ffn(half)
        o_ref[half:, :] = jnp.zeros((MOE_BM - half, D_MODEL), F32)

    @pl.when(nv == 0)
    def _():
        o_ref[...] = jnp.zeros_like(o_ref)


def _experts(x_rows, blk_x, blk_e, n_valid, layer, w1, b1, w2, b2):
    nb = MOE_ROWS // MOE_BM
    grid_spec = pltpu.PrefetchScalarGridSpec(
        num_scalar_prefetch=3,
        grid=(nb,),
        in_specs=[
            pl.BlockSpec((MOE_BM, D_MODEL), lambda i, bx, be, nv: (bx[i], 0)),
            pl.BlockSpec((1, 1, D_MODEL, 2 * D_FF), lambda i, bx, be, nv: (layer, be[i], 0, 0)),
            pl.BlockSpec((1, 1, 1, 2 * D_FF), lambda i, bx, be, nv: (layer, be[i], 0, 0)),
            pl.BlockSpec((1, 1, D_FF, D_MODEL), lambda i, bx, be, nv: (layer, be[i], 0, 0)),
            pl.BlockSpec((1, 1, 1, D_MODEL), lambda i, bx, be, nv: (layer, be[i], 0, 0)),
        ],
        out_specs=pl.BlockSpec((MOE_BM, D_MODEL), lambda i, bx, be, nv: (i, 0)),
        scratch_shapes=[pltpu.VMEM((D_MODEL, 2 * D_FF), BF16), pltpu.VMEM((D_FF, D_MODEL), BF16)],
    )
    return pl.pallas_call(
        _expert_kernel,
        grid_spec=grid_spec,
        out_shape=jax.ShapeDtypeStruct((MOE_ROWS, D_MODEL), F32),
        compiler_params=pltpu.CompilerParams(dimension_semantics=("arbitrary",),
                                             vmem_limit_bytes=56 * 1024 * 1024),
        name="moe_experts",
    )(blk_x, blk_e, n_valid, x_rows, w1, b1.reshape(DEPTH, N_EXPERTS, 1, 2 * D_FF), w2,
      b2.reshape(DEPTH, N_EXPERTS, 1, D_MODEL))


def _combine_kernel(dest_ref, gate_ref, x_ref, mod_ref, fw_ref, y_hbm, *rest, final):
    if final:
        o_ctx_ref, o_lat_ref, buf, sem = rest
    else:
        o_ref, buf, sem = rest

    def issue(t, c):
        for k in range(TOP_K):
            dst = dest_ref[0, 0, k * CB_TM + t]
            pltpu.make_async_copy(y_hbm.at[pl.ds(dst, 1)], buf.at[k, pl.ds(t, 1)], sem.at[0]).start()
        return c

    lax.fori_loop(0, CB_TM, issue, 0, unroll=DMA_UNROLL)
    for k in range(TOP_K):
        pltpu.make_async_copy(y_hbm.at[pl.ds(0, CB_TM)], buf.at[k], sem.at[0]).wait()
    g = gate_ref[...]
    acc = g[:, 0:1] * buf[0]
    for k in range(1, TOP_K):
        acc = acc + g[:, k:k + 1] * buf[k]
    gate2 = _mod_row(mod_ref, pl.program_id(0) * CB_TM)[:, 5 * D_MODEL:6 * D_MODEL]
    x2 = x_ref[...] + gate2 * acc
    if not final:
        o_ref[...] = x2
        return
    y = _rms(x2, fw_ref[...])
    in_ctx = pl.program_id(0) < N_CTX // CB_TM

    @pl.when(in_ctx)
    def _():
        o_ctx_ref[...] = y

    @pl.when(jnp.logical_not(in_ctx))
    def _():
        o_lat_ref[...] = y


def _combine(y_rows, dest, gates, x, mod, final_w, final):
    n = x.shape[0]
    full = pl.BlockSpec((CB_TM, D_MODEL), lambda i: (i, 0))
    nct = N_CTX // CB_TM
    if final:
        out_specs = [pl.BlockSpec((CB_TM, D_MODEL), lambda i: (jnp.minimum(i, nct - 1), 0)),
                     pl.BlockSpec((CB_TM, D_MODEL), lambda i: (jnp.maximum(i - nct, 0), 0))]
        out_shape = [jax.ShapeDtypeStruct((N_CTX, D_MODEL), F32), jax.ShapeDtypeStruct((N_LAT, D_MODEL), F32)]
    else:
        out_specs, out_shape = full, jax.ShapeDtypeStruct((n, D_MODEL), F32)
    return pl.pallas_call(
        functools.partial(_combine_kernel, final=final),
        grid=(n // CB_TM,),
        in_specs=[
            pl.BlockSpec((1, 1, TOP_K * CB_TM), lambda i: (i, 0, 0), memory_space=pltpu.SMEM),
            pl.BlockSpec((CB_TM, TOP_K), lambda i: (i, 0)),
            full,
            pl.BlockSpec((MOD_ROWS, 6 * D_MODEL), lambda i: (0, 0)),
            pl.BlockSpec((1, D_MODEL), lambda i: (0, 0)),
            pl.BlockSpec(memory_space=pl.ANY),
        ],
        out_specs=out_specs,
        out_shape=out_shape,
        scratch_shapes=[pltpu.VMEM((TOP_K, CB_TM, D_MODEL), F32), pltpu.SemaphoreType.DMA((1,))],
        compiler_params=_cparams(("arbitrary",)),
        name="moe_combine",
    )(_tile_major(dest, CB_TM), gates.T, x, mod, final_w.reshape(1, D_MODEL), y_rows)


def _routed_ffn(h, x, mod, p, moe, layer, final_w, final):
    idx, gates, rank, counts = _router(h, p['router_w'], p['router_b'])
    counts = counts[:, 0].astype(jnp.int32)
    padded = (counts + MOE_BM - 1) // MOE_BM * MOE_BM
    pad_end = jnp.cumsum(padded)
    pad_start = pad_end - padded
    experts = jnp.arange(N_EXPERTS, dtype=jnp.int32)
    start_of = jnp.sum(jnp.where(idx[:, :, None] == experts, pad_start, 0), axis=-1)
    dest = start_of + rank
    nb = MOE_ROWS // MOE_BM
    first_row = jnp.arange(nb, dtype=jnp.int32) * MOE_BM
    blk_e = jnp.minimum(jnp.sum((pad_end[None, :] <= first_row[:, None]).astype(jnp.int32), axis=1),
                        N_EXPERTS - 1)
    is_e = blk_e[:, None] == experts[None, :]
    end_of = jnp.sum(jnp.where(is_e, pad_start + counts, 0), axis=1)
    n_valid = jnp.clip(end_of - first_row, 0, MOE_BM).astype(jnp.int32)
    blk_x = jnp.minimum(jnp.arange(nb, dtype=jnp.int32), pad_end[-1] // MOE_BM - 1)
    x_rows = _dispatch(h, dest, pad_end)
    y_rows = _experts(x_rows, blk_x, blk_e, n_valid, layer, *moe)
    return _combine(y_rows, dest, gates, x, mod, final_w, final)


_LAYER_PARAMS = ('norm_mix_w', 'norm_ffn_w', 'da_lambda', 'da_norm_w', 'rw_shift', 'rw_w0', 'rw_w_up', 'rw_a0',
                 'rw_a_up', 'rw_g_up', 'rw_k_k', 'rw_k_a', 'rw_r_k', 'rw_ln_w', 'rw_ln_b', 'ret_decay_logit',
                 'ret_norm_w', 'cv_dw_w', 'cv_dw_b', 'cv_ln_w', 'cv_ln_b', 'w_branch', 'w_out', 'router_w',
                 'router_b')


def _layer(x, mod, w_p, p, moe, layer, lam_init, caches, tables, final_w, final):
    cache_k, cache_v, state_rw, state_ret = caches
    proj, gates = _input_projection(x, mod, p['norm_mix_w'], w_p, layer)
    o_da = _da_branch(proj, cache_k, cache_v, p['da_lambda'], p['da_norm_w'], lam_init, tables)
    o_rw, rw_state = _rwkv_branch(proj, state_rw, p)
    o_ret, ret_state = _retention_branch(proj, state_ret, p)
    o_cv = _conv_branch(proj, p)
    x1, h2 = _merge((o_da, o_rw, o_ret, o_cv), gates, x, mod, p['w_branch'], p['w_out'], p['norm_ffn_w'])
    x2 = _routed_ffn(h2, x1, mod, p, moe, layer, final_w, final)
    new_k = proj[:N_CTX, P_DA + DA_W:P_DA + 2 * DA_W].reshape(BATCH, SEQ, DA_HEADS, 2, DA_QK)
    new_v = proj[:N_CTX, P_DA + 2 * DA_W:P_DA + 3 * DA_W].reshape(BATCH, SEQ, DA_HEADS, DA_V)
    return x2, (new_k, new_v, rw_state[:BATCH], ret_state[:BATCH])


def kernel(x_prompt, x_sample, c, cache_da_k, cache_da_v, state_rwkv, state_ret, c_ctx, ada_w, ada_b, norm_mix_w,
           norm_ffn_w, w_in, da_lambda, da_norm_w, rw_shift, rw_w0, rw_w_up, rw_a0, rw_a_up, rw_g_up, rw_k_k,
           rw_k_a, rw_r_k, rw_ln_w, rw_ln_b, ret_decay_logit, ret_norm_w, cv_dw_w, cv_dw_b, cv_ln_w, cv_ln_b,
           w_branch, w_out, router_w, router_b, moe_w1, moe_b1, moe_w2, moe_b2, final_norm_w):
    weights = dict(norm_mix_w=norm_mix_w, norm_ffn_w=norm_ffn_w, da_lambda=da_lambda, da_norm_w=da_norm_w,
                   rw_shift=rw_shift, rw_w0=rw_w0, rw_w_up=rw_w_up, rw_a0=rw_a0, rw_a_up=rw_a_up, rw_g_up=rw_g_up,
                   rw_k_k=rw_k_k, rw_k_a=rw_k_a, rw_r_k=rw_r_k, rw_ln_w=rw_ln_w, rw_ln_b=rw_ln_b,
                   ret_decay_logit=ret_decay_logit, ret_norm_w=ret_norm_w, cv_dw_w=cv_dw_w, cv_dw_b=cv_dw_b,
                   cv_ln_w=cv_ln_w, cv_ln_b=cv_ln_b, w_branch=w_branch, w_out=w_out, router_w=router_w,
                   router_b=router_b)
    x = jnp.concatenate([x_prompt.reshape(N_CTX, D_MODEL), x_sample.reshape(N_LAT, D_MODEL)], axis=0)
    cvec = jnp.concatenate([c_ctx[None, :], c, jnp.zeros((MOD_ROWS - 1 - DEC_BATCH, D_MODEL), F32)], axis=0)
    mod = _modulation(cvec, ada_w, ada_b)
    w_p = _pad_w_in(w_in).astype(BF16)
    tables = _rope_tables()
    outs = []
    moe = (moe_w1, moe_b1, moe_w2, moe_b2)
    for i in range(DEPTH):
        p = {name: weights[name][i] for name in _LAYER_PARAMS}
        lam_init = 0.8 - 0.6 * math.exp(-0.3 * i)
        caches = (cache_da_k[:, i], cache_da_v[:, i], state_rwkv[:, i], state_ret[:, i])
        x, ctx_out = _layer(x, mod[i], w_p, p, moe, i, lam_init, caches, tables, final_norm_w, i == DEPTH - 1)
        outs.append(ctx_out)
    y_prompt = x[0].reshape(BATCH, SEQ, D_MODEL)
    y_sample = x[1].reshape(DEC_BATCH, DEC_SEQ, D_MODEL)
    stack = lambda j: jnp.stack([o[j] for o in outs], axis=1)
    return (y_prompt, y_sample, stack(0), stack(1), stack(2), stack(3))


def _pad_w_in(w_in):
    da, rw, ret, cv, gate = jnp.split(w_in, [1536, 3456, 4992, 6016], axis=-1)
    pad = jnp.zeros(w_in.shape[:-1] + (P_DA - RW_COLS,), w_in.dtype)
    return jnp.concatenate([rw, pad, da, ret, cv, gate], axis=-1)
```

```python
import functools
import math

import jax
import jax.numpy as jnp
from jax import lax
from jax.experimental import pallas as pl
from jax.experimental.pallas import tpu as pltpu

F32 = jnp.float32
BF16 = jnp.bfloat16

D_MODEL = 1024
BATCH = 16
SEQ = 256
DEPTH = 2
DEC_BATCH = 2
DEC_SEQ = 4096
PAST_LEN = 512
GRID_W = 64
EPS = 1e-6

DA_HEADS = 4
DA_QK = 64
DA_V = 128
ROPE_BASE = 10000.0

RW_HEADS = 8
RW_HD = 64
RW_W = 512
RW_LORA = 64
RW_G_LORA = 128
RW_GN_EPS = 64e-5

RET_HEADS = 4
RET_DK = 64
RET_DV = 128
RET_CHUNK = 128

CV_W = 512
CONV_K = 31
BR_W = 512
N_BRANCH = 4

N_EXPERTS = 32
TOP_K = 4
D_FF = 1024
SWIGLU_LIMIT = 7.0
SWIGLU_ALPHA = 1.702

N_CTX = BATCH * SEQ
N_LAT = DEC_BATCH * DEC_SEQ
N_TOK = N_CTX + N_LAT
N_SEQS = BATCH + DEC_BATCH
MOD_ROWS = 8
MOD_GROUP = 4096

RW_COLS = 1920
P_RW = 0
P_DA = 2048
P_RET = 3584
P_CV = 5120
P_GATE = 6144
P_COLS = 10240
B_DA, B_RET, B_CV, B_GATE = (p - P_DA for p in (P_DA, P_RET, P_CV, P_GATE))

VMEM_LIMIT = 48 * 1024 * 1024


def _cparams(sem):
    return pltpu.CompilerParams(dimension_semantics=sem, vmem_limit_bytes=VMEM_LIMIT)


def _dg(a, b, dims):
    return lax.dot_general(a, b, (dims, ((), ())), preferred_element_type=F32)


NN = ((1,), (0,))
NT = ((1,), (1,))
TN = ((0,), (0,))


def _dot(a, b, dims=NN):
    return _dg(a.astype(BF16), b.astype(BF16), dims)


def _split(x):
    hi = x.astype(BF16)
    lo = (x - hi.astype(F32)).astype(BF16)
    return hi, lo


def _dot3(a, b, dims=NN):
    ah, al = _split(a)
    bh, bl = _split(b)
    return _dg(ah, bh, dims) + (_dg(ah, bl, dims) + _dg(al, bh, dims))


def _dot2x(a, e, dims=NN):
    ah, al = _split(a)
    am = (a - ah.astype(F32) - al.astype(F32)).astype(BF16)
    eb = e.astype(BF16)
    return _dg(ah, eb, dims) + (_dg(al, eb, dims) + _dg(am, eb, dims))


def _sigmoid(x):
    return 1.0 / (1.0 + jnp.exp(-x))


def _softplus(x):
    return jnp.maximum(x, 0.0) + jnp.log(1.0 + jnp.exp(-jnp.abs(x)))


def _seq_pos(row):
    in_ctx = row < N_CTX
    pos = jnp.where(in_ctx, row & (SEQ - 1), (row - N_CTX) & (DEC_SEQ - 1))
    length = jnp.where(in_ctx, SEQ, DEC_SEQ)
    return pos, length


def _seq_index(row):
    return jnp.where(row < N_CTX, row // SEQ, BATCH + (row - N_CTX) // DEC_SEQ)


def _dotx(e, b, dims=NN):
    bh, bl = _split(b)
    bm = (b - bh.astype(F32) - bl.astype(F32)).astype(BF16)
    eb = e.astype(BF16)
    return _dg(eb, bh, dims) + (_dg(eb, bl, dims) + _dg(eb, bm, dims))


def _mod_kernel(c_ref, w_ref, b_ref, o_ref):
    c = c_ref[...]
    s = c * _sigmoid(c)
    o_ref[0] = _dot3(s, w_ref[0]) + b_ref[0]


def _modulation(cvec, ada_w, ada_b):
    tn = 1536
    return pl.pallas_call(
        _mod_kernel,
        grid=(DEPTH, 6 * D_MODEL // tn),
        in_specs=[
            pl.BlockSpec((MOD_ROWS, D_MODEL), lambda l, j: (0, 0)),
            pl.BlockSpec((1, D_MODEL, tn), lambda l, j: (l, 0, j)),
            pl.BlockSpec((1, 1, tn), lambda l, j: (l, 0, j)),
        ],
        out_specs=pl.BlockSpec((1, MOD_ROWS, tn), lambda l, j: (l, 0, j)),
        out_shape=jax.ShapeDtypeStruct((DEPTH, MOD_ROWS, 6 * D_MODEL), F32),
        compiler_params=_cparams(("parallel", "parallel")),
        name="modulation",
    )(cvec, ada_w, ada_b.reshape(DEPTH, 1, 6 * D_MODEL))


def _mod_row(mod_ref, first_row):
    g = first_row // MOD_GROUP
    return mod_ref[pl.ds(g, 1), :]


def _rms(x, w):
    return x * lax.rsqrt(jnp.mean(x * x, axis=-1, keepdims=True) + EPS) * w


IN_TM = 1024
IN_TN = 1024


IN_NA = P_DA // IN_TN


def _inproj_kernel(x_ref, mod_ref, nw_ref, w_ref, oa_ref, og_ref, h_ref):
    i = pl.program_id(0)
    j = pl.program_id(1)

    @pl.when(j == 0)
    def _():
        m = _mod_row(mod_ref, i * IN_TM)
        sh = m[:, 0:D_MODEL]
        sc = m[:, D_MODEL:2 * D_MODEL]
        h_ref[...] = (_rms(x_ref[...], nw_ref[...]) * (1.0 + sc) + sh).astype(BF16)

    acc = _dg(h_ref[...], w_ref[0].astype(BF16), NN)

    @pl.when(j < IN_NA)
    def _():
        oa_ref[...] = acc

    @pl.when(j >= IN_NA)
    def _():
        og_ref[...] = acc.astype(BF16)


def _input_projection(x, mod, norm_w, w_p, layer):
    n = x.shape[0]
    return pl.pallas_call(
        _inproj_kernel,
        grid=(n // IN_TM, P_COLS // IN_TN),
        in_specs=[
            pl.BlockSpec((IN_TM, D_MODEL), lambda i, j: (i, 0)),
            pl.BlockSpec((MOD_ROWS, 6 * D_MODEL), lambda i, j: (0, 0)),
            pl.BlockSpec((1, D_MODEL), lambda i, j: (0, 0)),
            pl.BlockSpec((1, D_MODEL, IN_TN), lambda i, j: (layer, 0, j)),
        ],
        out_specs=[pl.BlockSpec((IN_TM, IN_TN), lambda i, j: (i, jnp.minimum(j, IN_NA - 1))),
                   pl.BlockSpec((IN_TM, IN_TN), lambda i, j: (i, jnp.maximum(j - IN_NA, 0)))],
        out_shape=[jax.ShapeDtypeStruct((n, P_DA), F32), jax.ShapeDtypeStruct((n, P_COLS - P_DA), BF16)],
        scratch_shapes=[pltpu.VMEM((IN_TM, D_MODEL), BF16)],
        compiler_params=_cparams(("parallel", "arbitrary")),
        name="input_projection",
    )(x, mod, norm_w.reshape(1, D_MODEL), w_p)


QK_TM = 512
DA_W = DA_HEADS * 2 * DA_QK


def _qkprep_kernel(q_ref, k_ref, v_ref, c_ref, se_ref, so_ref, qo_ref, ko_ref, vo_ref):
    i = pl.program_id(0)
    for h in range(DA_HEADS):
        vo_ref[:, h * 2 * DA_V:h * 2 * DA_V + DA_V] = v_ref[:, h * DA_V:(h + 1) * DA_V].astype(BF16)
        vo_ref[:, h * 2 * DA_V + DA_V:(h + 1) * 2 * DA_V] = jnp.ones((QK_TM, DA_V), BF16)
    scale = DA_QK ** -0.5

    @pl.when(i * QK_TM < N_CTX)
    def _():
        qo_ref[...] = (q_ref[...].astype(F32) * scale).astype(BF16)
        ko_ref[...] = k_ref[...].astype(BF16)

    @pl.when(i * QK_TM >= N_CTX)
    def _():
        c = c_ref[...]
        se = se_ref[...]
        so = so_ref[...]

        def rope(x):
            nxt = pltpu.roll(x, DA_W - 1, axis=1)
            prv = pltpu.roll(x, 1, axis=1)
            return x * c + nxt * se + prv * so

        qo_ref[...] = (rope(q_ref[...].astype(F32)) * scale).astype(BF16)
        ko_ref[...] = rope(k_ref[...].astype(F32)).astype(BF16)


def _rope_tables():
    rows = DEC_SEQ // GRID_W
    row = jnp.repeat(jnp.arange(rows, dtype=F32), GRID_W)
    col = jnp.tile(jnp.arange(GRID_W, dtype=F32), rows)
    n_pairs = DA_QK // 4
    inv = ROPE_BASE ** (-jnp.arange(n_pairs, dtype=F32) / n_pairs)
    ang = jnp.concatenate([row[:, None] * inv, col[:, None] * inv], axis=-1)
    cos = jnp.repeat(jnp.cos(ang), 2, axis=-1)
    sin = jnp.repeat(jnp.sin(ang), 2, axis=-1)
    even = (jnp.arange(DA_QK) % 2 == 0)[None, :]
    s_even = jnp.where(even, -sin, 0.0)
    s_odd = jnp.where(even, 0.0, sin)
    rep = lambda t: jnp.tile(t, (1, DA_W // DA_QK))
    return rep(cos), rep(s_even), rep(s_odd)


def _qk_prepare(proj, tables):
    n = proj.shape[0]
    lat0 = N_CTX // QK_TM
    nlat = DEC_SEQ // QK_TM
    tab = pl.BlockSpec((QK_TM, DA_W), lambda i: (jnp.maximum(i - lat0, 0) % nlat, 0))
    c0 = B_DA // DA_W
    out = jax.ShapeDtypeStruct((n, DA_W), BF16)
    return pl.pallas_call(
        _qkprep_kernel,
        grid=(n // QK_TM,),
        in_specs=[
            pl.BlockSpec((QK_TM, DA_W), lambda i: (i, c0)),
            pl.BlockSpec((QK_TM, DA_W), lambda i: (i, c0 + 1)),
            pl.BlockSpec((QK_TM, DA_W), lambda i: (i, c0 + 2)),
            tab, tab, tab,
        ],
        out_specs=[pl.BlockSpec((QK_TM, DA_W), lambda i: (i, 0))] * 2
        + [pl.BlockSpec((QK_TM, 2 * DA_W), lambda i: (i, 0))],
        out_shape=[out, out, jax.ShapeDtypeStruct((n, 2 * DA_W), BF16)],
        compiler_params=_cparams(("parallel",)),
        name="qk_prepare",
    )(proj, proj, proj, *tables)


DA_TQ = 256


def _attend(q, segments, lam, nw, o_ref):
    for h in range(DA_HEADS):
        os = []
        for m in range(2):
            c0 = (2 * h + m) * DA_QK
            ss = [_dg(q[:, c0:c0 + DA_QK], k[:, c0:c0 + DA_QK], NT) for k, _ in segments]
            mx = jnp.max(ss[0], axis=-1, keepdims=True)
            for s in ss[1:]:
                mx = jnp.maximum(mx, jnp.max(s, axis=-1, keepdims=True))
            oe = None
            for s, (_, v) in zip(ss, segments):
                part = _dg(jnp.exp(s - mx).astype(BF16), v[:, h * 2 * DA_V:(h + 1) * 2 * DA_V], NN)
                oe = part if oe is None else oe + part
            os.append(oe[:, :DA_V] * (1.0 / oe[:, DA_V:DA_V + 1]))
        o_ref[:, h * DA_V:(h + 1) * DA_V] = _rms(os[0] - lam * os[1], nw).astype(BF16)


def _da_kernel(q_ref, ks_ref, vs_ref, kl_ref, vl_ref, dl_ref, nw_ref, o_ref, *, lam_init):
    dl = dl_ref[...]
    lam = (jnp.exp(jnp.sum(dl[0:1] * dl[1:2], axis=1, keepdims=True))
           - jnp.exp(jnp.sum(dl[2:3] * dl[3:4], axis=1, keepdims=True)) + lam_init)
    nw = nw_ref[...] * (1.0 - lam_init)
    q = q_ref[...]
    is_ctx = pl.program_id(0) < BATCH * (SEQ // DA_TQ)

    @pl.when(is_ctx)
    def _():
        _attend(q, [(ks_ref, vs_ref)], lam, nw, o_ref)

    @pl.when(jnp.logical_not(is_ctx))
    def _():
        _attend(q, [(kl_ref.at[0], vl_ref.at[0])], lam, nw, o_ref)


def _da_branch(proj, cache_k, cache_v, da_lambda, da_norm_w, lam_init, tables):
    qb, kb, vb = _qk_prepare(proj, tables)
    n = proj.shape[0]
    ck = cache_k.reshape(DEC_BATCH, PAST_LEN, DA_W).astype(BF16)
    cv = cache_v.astype(BF16)
    cv = jnp.concatenate([cv, jnp.ones_like(cv)], axis=-1).reshape(DEC_BATCH, PAST_LEN, 2 * DA_W)
    lat = lambda t: t[N_CTX:].reshape(DEC_BATCH, DEC_SEQ, t.shape[-1])
    k_all = jnp.concatenate([ck, lat(kb)], axis=1)
    v_all = jnp.concatenate([cv, lat(vb)], axis=1)
    tk = PAST_LEN + DEC_SEQ
    n_ctx_tiles = N_CTX // DA_TQ
    per_seq = DEC_SEQ // DA_TQ
    seq_c = lambda i: jnp.minimum(i * DA_TQ // SEQ, BATCH - 1)
    seq_l = lambda i: jnp.clip((i - n_ctx_tiles) // per_seq, 0, DEC_BATCH - 1)
    once = pl.Buffered(1)
    return pl.pallas_call(
        functools.partial(_da_kernel, lam_init=lam_init),
        grid=(n // DA_TQ,),
        in_specs=[
            pl.BlockSpec((DA_TQ, DA_W), lambda i: (i, 0)),
            pl.BlockSpec((SEQ, DA_W), lambda i: (seq_c(i), 0)),
            pl.BlockSpec((SEQ, 2 * DA_W), lambda i: (seq_c(i), 0)),
            pl.BlockSpec((1, tk, DA_W), lambda i: (seq_l(i), 0, 0), pipeline_mode=once),
            pl.BlockSpec((1, tk, 2 * DA_W), lambda i: (seq_l(i), 0, 0), pipeline_mode=once),
            pl.BlockSpec((4, DA_QK), lambda i: (0, 0)),
            pl.BlockSpec((1, DA_V), lambda i: (0, 0)),
        ],
        out_specs=pl.BlockSpec((DA_TQ, DA_W), lambda i: (i, 0)),
        out_shape=jax.ShapeDtypeStruct((n, DA_W), BF16),
        compiler_params=_cparams(("arbitrary",)),
        name="diff_attention",
    )(qb, kb, vb, k_all, v_all, da_lambda, da_norm_w.reshape(1, DA_V))


RW_TM = 256
RW_C = 64
RW_PAIRS = RW_HEADS // 2
HALO = 8


def _head_ones():
    idx = jnp.arange(2 * RW_HD) // RW_HD
    return (idx[:, None] == idx[None, :]).astype(BF16)


def _head_sum(x, ones_pair):
    w = 2 * RW_HD
    return jnp.concatenate([_dot2x(x[:, p * w:(p + 1) * w], ones_pair) for p in range(RW_PAIRS)], axis=1)


def _rwprep_kernel(x_ref, xp_ref, xn_ref, mu_ref, w0_ref, wup_ref, a0_ref, aup_ref, gup_ref, kk_ref, ka_ref,
                   ones_ref, r_o, v_o, kk_o, g_o, kd_o, lw_o, a_o, buf):
    row0 = pl.program_id(0) * RW_TM
    x = x_ref[:, 0:RW_COLS]
    buf[HALO:HALO + RW_TM, :] = x
    buf[HALO - 1:HALO, :] = xp_ref[HALO - 1:HALO, 0:RW_COLS]
    buf[HALO + RW_TM:HALO + RW_TM + 1, :] = xn_ref[0:1, 0:RW_COLS]
    rows = row0 + lax.broadcasted_iota(jnp.int32, (RW_TM, 1), 0)
    pos, length = _seq_pos(rows)
    prev = jnp.where(pos == 0, 0.0, buf[HALO - 1:HALO - 1 + RW_TM, :])
    nxt = jnp.where(pos == length - 1, 0.0, buf[HALO + 1:HALO + 1 + RW_TM, :])
    mu = mu_ref[...]
    u = x + mu[0:1] * (prev - x) + mu[1:2] * (nxt - x)

    r = u[:, 0:RW_W]
    k = u[:, RW_W:2 * RW_W]
    v = u[:, 2 * RW_W:3 * RW_W]
    wl = u[:, 3 * RW_W:3 * RW_W + 128]
    al = u[:, 3 * RW_W + 128:3 * RW_W + 256]
    gl = u[:, 3 * RW_W + 256:3 * RW_W + 384]
    w_raw = w0_ref[...] + _dot3(jnp.tanh(wl), wup_ref[...])
    lw = -jnp.exp(-_softplus(-w_raw) - 0.5)
    a = _sigmoid(a0_ref[...] + _dot3(al, aup_ref[...]))
    g = _dot3(_sigmoid(gl), gup_ref[...])
    kk = k * kk_ref[...]
    kk = kk * lax.rsqrt(jnp.maximum(_head_sum(kk * kk, ones_ref[...]), 1e-12))
    kd = jnp.concatenate([k, k], axis=1) * (1.0 + (a - 1.0) * ka_ref[...])

    r_o[...] = r
    v_o[...] = v
    kk_o[...] = kk
    g_o[...] = g
    for d in range(2):
        kd_o[d] = kd[:, d * RW_W:(d + 1) * RW_W]
        lw_o[d] = lw[:, d * RW_W:(d + 1) * RW_W]
        a_o[d] = a[:, d * RW_W:(d + 1) * RW_W]


def _rw_prepare(proj, p):
    n = proj.shape[0]
    nh = n // HALO
    steps = RW_TM // HALO
    wide = P_DA - P_RW
    cat2 = lambda t: t.reshape(1, 2 * RW_W)
    blockdiag = lambda t: jnp.concatenate(
        [jnp.concatenate([t[0], jnp.zeros_like(t[0])], axis=1),
         jnp.concatenate([jnp.zeros_like(t[1]), t[1]], axis=1)], axis=0)
    const = lambda shape: pl.BlockSpec(shape, lambda i: (0,) * len(shape))
    row = pl.BlockSpec((RW_TM, RW_W), lambda i: (i, 0))
    row2 = pl.BlockSpec((2, RW_TM, RW_W), lambda i: (0, i, 0))
    o1 = jax.ShapeDtypeStruct((n, RW_W), F32)
    o2 = jax.ShapeDtypeStruct((2, n, RW_W), F32)
    return pl.pallas_call(
        _rwprep_kernel,
        grid=(n // RW_TM,),
        in_specs=[
            pl.BlockSpec((RW_TM, wide), lambda i: (i, 0)),
            pl.BlockSpec((HALO, wide), lambda i: (jnp.maximum(i * steps - 1, 0), 0)),
            pl.BlockSpec((HALO, wide), lambda i: (jnp.minimum((i + 1) * steps, nh - 1), 0)),
            const((2, RW_COLS)), const((1, 2 * RW_W)), const((128, 2 * RW_W)), const((1, 2 * RW_W)),
            const((128, 2 * RW_W)), const((RW_G_LORA, RW_W)), const((1, RW_W)), const((1, 2 * RW_W)),
            const((2 * RW_HD, 2 * RW_HD)),
        ],
        out_specs=[row, row, row, row, row2, row2, row2],
        out_shape=[o1, o1, o1, o1, o2, o2, o2],
        scratch_shapes=[pltpu.VMEM((RW_TM + 2 * HALO, RW_COLS), F32)],
        compiler_params=_cparams(("parallel",)),
        name="rwkv_prepare",
    )(proj, proj, proj, p['rw_shift'], cat2(p['rw_w0']), blockdiag(p['rw_w_up']), cat2(p['rw_a0']),
      blockdiag(p['rw_a_up']), p['rw_g_up'], p['rw_k_k'].reshape(1, RW_W),
      jnp.tile(p['rw_k_a'].reshape(1, RW_W), (1, 2)), _head_ones())


RW_TB = 256
RW_NC = RW_TB // RW_C
RW_SIDE = 8


def _rw_finish(y, r, v, g, kd_sum, rk, lnw, lnb, ones):
    inv = 1.0 / RW_HD
    xc = y - _head_sum(y, ones) * inv
    var = _head_sum(xc * xc, ones) * inv
    yn = xc * lax.rsqrt(var + RW_GN_EPS) * lnw + lnb
    bonus = _head_sum(r * kd_sum * rk, ones) * v
    return ((yn + bonus) * g).astype(BF16)


def _rwscan_kernel(r_ref, v_ref, kk_ref, kd_ref, lw_ref, a_ref, tri_ref, inc_ref, str_ref, s0_ref, *rest, backward):
    if backward:
        yf_ref, g_ref, rk_ref, lnw_ref, lnb_ref, ones_ref, y_ref, st_ref, s_scr = rest
    else:
        y_ref, st_ref, s_scr = rest
    _rwscan_body(r_ref, v_ref, kk_ref, kd_ref, lw_ref, a_ref, tri_ref, inc_ref, str_ref, s0_ref, y_ref, st_ref,
                 s_scr, backward,
                 (lambda yb: _rw_finish(yf_ref[...] + yb, r_ref[...], v_ref[...], g_ref[...],
                                        kd_ref[0] + kd_ref[1], rk_ref[...], lnw_ref[...], lnb_ref[...],
                                        ones_ref[...])) if backward else None)


def _rwscan_body(r_ref, v_ref, kk_ref, kd_ref, lw_ref, a_ref, tri_ref, inc_ref, str_ref, s0_ref, y_ref, st_ref,
                 s_scr, backward, finish):
    step = pl.program_id(0)
    nb = pl.num_programs(0)
    bi = (nb - 1 - step) if backward else step
    pos, length = _seq_pos(bi * RW_TB)
    first = (pos + RW_TB == length) if backward else (pos == 0)
    last = (pos == 0) if backward else (pos + RW_TB == length)
    in_ctx = bi * RW_TB < N_CTX
    c = RW_C

    @pl.when(jnp.logical_and(first, in_ctx))
    def _():
        s_scr[...] = jnp.zeros_like(s_scr)

    @pl.when(jnp.logical_and(first, jnp.logical_not(in_ctx)))
    def _():
        z = jnp.zeros((c, c), F32)
        for p in range(RW_PAIRS):
            s_scr[p] = jnp.concatenate(
                [jnp.concatenate([s0_ref[0, 0, 2 * p], z], axis=1),
                 jnp.concatenate([z, s0_ref[0, 0, 2 * p + 1]], axis=1)], axis=0)

    incl2 = inc_ref[...] > 0.5
    strict2 = str_ref[...] > 0.5
    eye2 = (lax.broadcasted_iota(jnp.int32, (2 * c, 2 * c), 0)
            == lax.broadcasted_iota(jnp.int32, (2 * c, 2 * c), 1))
    m_e = lax.broadcasted_iota(jnp.int32, (1, 2 * c), 1) < c

    lw = lw_ref[0]
    cum = _dotx(tri_ref[...], lw)
    tots = [cum[(j * c if backward else j * c + c - 1):(j * c + 1 if backward else j * c + c), :]
            for j in range(RW_NC)]
    tot_b = jnp.concatenate([jnp.broadcast_to(t, (c, RW_W)) for t in tots], axis=0)
    kk = kk_ref[...]
    kd = kd_ref[1 if backward else 0]
    bp = kk * a_ref[0]
    g_inv = jnp.exp(-cum)
    g_rem = jnp.exp(tot_b - cum)
    ag = -kk * jnp.exp(cum - lw)
    rg = r_ref[...] * jnp.exp(cum)
    bdn = bp * g_inv
    kdn = kd * g_inv
    bc = bp * g_rem
    kc = kd * g_rem
    v = v_ref[...]

    def stack(x, j, p):
        xs = x[j * c:(j + 1) * c, p * 2 * c:(p + 1) * 2 * c]
        return jnp.concatenate([jnp.where(m_e, xs, 0.0), jnp.where(m_e, 0.0, xs)], axis=0).astype(BF16)

    pre = {}
    keys = [(j, p) for j in range(RW_NC) for p in range(RW_PAIRS)]
    for g0 in range(0, len(keys), RW_SIDE):
        grp = keys[g0:g0 + RW_SIDE]
        ops = {k: tuple(stack(t, *k) for t in (ag, rg, bdn, kdn, bc, kc, v)) for k in grp}
        gm = {k: _dg(jnp.concatenate([ops[k][0], ops[k][1]], axis=0),
                     jnp.concatenate([ops[k][2], ops[k][3]], axis=0), NT) for k in grp}
        lbb = {k: jnp.where(strict2, gm[k][:2 * c, :2 * c], 0.0) for k in grp}
        lkb = {k: jnp.where(strict2, gm[k][:2 * c, 2 * c:], 0.0).astype(BF16) for k in grp}
        lrk = {k: jnp.concatenate([jnp.where(incl2, gm[k][2 * c:, :2 * c], 0.0),
                                   jnp.where(incl2, gm[k][2 * c:, 2 * c:], 0.0)], axis=1).astype(BF16)
               for k in grp}
        lv = {k: _dg(lkb[k], ops[k][6], NN) for k in grp}
        x = {k: jnp.where(eye2, 1.0, lbb[k]) for k in grp}
        pw = lbb
        for _ in range(int(math.log2(c)) - 1):
            pwb = {k: pw[k].astype(BF16) for k in grp}
            pw = {k: _dg(pwb[k], pwb[k], NN) for k in grp}
            x = {k: x[k] + _dg(x[k].astype(BF16), pw[k].astype(BF16), NN) for k in grp}
        tw = {k: _dg(x[k].astype(BF16), jnp.concatenate([lv[k].astype(BF16), ops[k][0]], axis=1), NN)
              for k in grp}
        for k in grp:
            pre[k] = (tw[k][:, :2 * c], tw[k][:, 2 * c:].astype(BF16), ops[k][1], lrk[k], ops[k][6],
                      jnp.concatenate([ops[k][4], ops[k][5]], axis=0))

    order = range(RW_NC - 1, -1, -1) if backward else range(RW_NC)
    pairs = range(RW_PAIRS)
    s = [s_scr[p] for p in pairs]
    ys = {}
    for j in order:
        sb = [s[p].astype(BF16) for p in pairs]
        u = [_dg(pre[j, p][1], sb[p], NT) + pre[j, p][0] for p in pairs]
        uv = [jnp.concatenate([u[p].astype(BF16), pre[j, p][4]], axis=0) for p in pairs]
        y = [_dg(pre[j, p][2], sb[p], NT) + _dg(pre[j, p][3], uv[p], NN) for p in pairs]
        s = [s[p] * jnp.exp(tots[j][:, p * 2 * c:(p + 1) * 2 * c]) + _dg(uv[p], pre[j, p][5], TN) for p in pairs]
        ys[j] = [y[p][:c] + y[p][c:] for p in pairs]
    y_blk = jnp.concatenate([jnp.concatenate(ys[j], axis=1) for j in range(RW_NC)], axis=0)
    y_ref[...] = finish(y_blk) if finish else y_blk
    for p in pairs:
        s_scr[p] = s[p]

    @pl.when(last)
    def _():
        for p in pairs:
            st_ref[0, 2 * p] = s[p][:c, :c]
            st_ref[0, 2 * p + 1] = s[p][c:, c:]


def _scan_masks(backward):
    t = jnp.arange(RW_TB)
    sgn = -1 if backward else 1
    same_chunk = (t[:, None] // RW_C) == (t[None, :] // RW_C)
    tri = (same_chunk & ((t[:, None] - t[None, :]) * sgn >= 0)).astype(BF16)
    q = jnp.arange(2 * RW_C)
    same_head = (q[:, None] // RW_C) == (q[None, :] // RW_C)
    dif = ((q[:, None] % RW_C) - (q[None, :] % RW_C)) * sgn
    return tri, (same_head & (dif >= 0)).astype(F32), (same_head & (dif > 0)).astype(F32)


def _rw_scan(r, v, kk, kd, lw, a, s0, backward, finish=()):
    n = r.shape[0]
    nb = n // RW_TB
    d = 1 if backward else 0
    blk = (lambda i: nb - 1 - i) if backward else (lambda i: i)
    row = pl.BlockSpec((RW_TB, RW_W), lambda i: (blk(i), 0))
    row2 = pl.BlockSpec((1, RW_TB, RW_W), lambda i: (d, blk(i), 0))
    both = pl.BlockSpec((2, RW_TB, RW_W), lambda i: (0, blk(i), 0))
    const = lambda shape: pl.BlockSpec(shape, lambda i: (0,) * len(shape))
    vec = const((1, RW_W))
    pair = 2 * RW_C
    extra = [row, row, vec, vec, vec, const((pair, pair))] if backward else []
    return pl.pallas_call(
        functools.partial(_rwscan_kernel, backward=backward),
        grid=(nb,),
        in_specs=[row, row, row, both, row2, row2, const((RW_TB, RW_TB)), const((pair, pair)), const((pair, pair)),
                  pl.BlockSpec((1, 1, RW_HEADS, RW_HD, RW_HD),
                               lambda i: (jnp.maximum(_seq_index(blk(i) * RW_TB) - BATCH, 0), d, 0, 0, 0))] + extra,
        out_specs=[row, pl.BlockSpec((1, RW_HEADS, RW_HD, RW_HD), lambda i: (_seq_index(blk(i) * RW_TB), 0, 0, 0))],
        out_shape=[jax.ShapeDtypeStruct((n, RW_W), BF16 if backward else F32),
                   jax.ShapeDtypeStruct((N_SEQS, RW_HEADS, RW_HD, RW_HD), F32)],
        scratch_shapes=[pltpu.VMEM((RW_PAIRS, pair, pair), F32)],
        compiler_params=_cparams(("arbitrary",)),
        name="rwkv_scan_bwd" if backward else "rwkv_scan_fwd",
    )(r, v, kk, kd, lw, a, *_scan_masks(backward), s0, *finish)


def _rwkv_branch(proj, state0, p):
    r, v, kk, g, kd, lw, a = _rw_prepare(proj, p)
    yf, sf = _rw_scan(r, v, kk, kd, lw, a, state0, backward=False)
    finish = (yf, g, p['rw_r_k'].reshape(1, RW_W), p['rw_ln_w'].reshape(1, RW_W), p['rw_ln_b'].reshape(1, RW_W),
              _head_ones())
    out, sb = _rw_scan(r, v, kk, kd, lw, a, state0, backward=True, finish=finish)
    return out, jnp.stack([sf, sb], axis=1)


RET_W = RET_HEADS * RET_DV
RET_PAIRS = RET_HEADS // 2


RET_TB = 256
RET_NC = RET_TB // RET_CHUNK


def _ret_kernel(q_ref, k_ref, v_ref, lg_ref, s0_ref, o_ref, st_ref, s_scr, *, backward):
    step = pl.program_id(0)
    nb = pl.num_programs(0)
    bi = (nb - 1 - step) if backward else step
    pos, length = _seq_pos(bi * RET_TB)
    first = (pos + RET_TB == length) if backward else (pos == 0)
    in_ctx = bi * RET_TB < N_CTX

    @pl.when(jnp.logical_and(first, in_ctx))
    def _():
        s_scr[...] = jnp.zeros_like(s_scr)

    @pl.when(jnp.logical_and(first, jnp.logical_not(in_ctx)))
    def _():
        s_scr[...] = s0_ref[0, 0]

    c = RET_CHUNK
    d = 1 if backward else 0
    sgn = -1 if backward else 1
    lgs = -_softplus(-lg_ref[d:d + 1, :])
    ri = lax.broadcasted_iota(jnp.int32, (c, c), 0)
    cj = lax.broadcasted_iota(jnp.int32, (c, c), 1)
    dif = (ri - cj) * sgn
    valid = dif >= 0
    dist = jnp.maximum(dif, 0).astype(F32)
    pr = lax.broadcasted_iota(jnp.int32, (c, 2 * RET_DK), 0)
    tau = ((c - 1 - pr) if backward else pr).astype(F32)
    low = lax.broadcasted_iota(jnp.int32, (1, 2 * RET_DK), 1) < RET_DK
    rlow = lax.broadcasted_iota(jnp.int32, (2 * RET_DK, RET_DV), 0) < RET_DK
    heads = range(RET_HEADS)
    pairs = range(RET_PAIRS)
    mask = [low, jnp.logical_not(low)]
    lg_h = [lgs[:, h:h + 1] for h in heads]
    lg_row = [jnp.where(low, lg_h[2 * p], lg_h[2 * p + 1]) for p in pairs]
    dmat = [jnp.where(valid, jnp.exp(lg_h[h] * dist), 0.0) for h in heads]
    q_dec = [jnp.exp(lg_row[p] * (tau + 1.0)) for p in pairs]
    k_dec = [jnp.exp(lg_row[p] * (c - 1.0 - tau)) for p in pairs]
    c_dec = [jnp.where(rlow, jnp.exp(lg_h[2 * p] * c), jnp.exp(lg_h[2 * p + 1] * c)) for p in pairs]

    pre = []
    for j in range(RET_NC):
        rows = slice(j * c, (j + 1) * c)
        qp = [q_ref[rows, p * 128:(p + 1) * 128] for p in pairs]
        kp = [k_ref[rows, p * 128:(p + 1) * 128] * (RET_DK ** -0.5) for p in pairs]
        kpb = [t.astype(BF16) for t in kp]
        vb = [v_ref[rows, h * RET_DV:(h + 1) * RET_DV].astype(BF16) for h in heads]
        att = [_dg(jnp.where(mask[h % 2], qp[h // 2], 0.0).astype(BF16), kpb[h // 2], NT) * dmat[h]
               for h in heads]
        upd = [_dg(jnp.where(mask[h % 2], kp[h // 2] * k_dec[h // 2], 0.0).astype(BF16), vb[h], TN)
               for h in heads]
        inner = [_dg(att[h].astype(BF16), vb[h], NN) for h in heads]
        qd = [jnp.where(mask[h % 2], qp[h // 2] * q_dec[h // 2], 0.0).astype(BF16) for h in heads]
        pre.append((inner, qd, upd))

    s = [s_scr[p] for p in pairs]
    for j in (range(RET_NC - 1, -1, -1) if backward else range(RET_NC)):
        inner, qd, upd = pre[j]
        sb = [t.astype(BF16) for t in s]
        for h in heads:
            o_ref[j * c:(j + 1) * c, h * RET_DV:(h + 1) * RET_DV] = inner[h] + _dg(qd[h], sb[h // 2], NN)
        s = [s[p] * c_dec[p] + upd[2 * p] + upd[2 * p + 1] for p in pairs]
    for p in pairs:
        s_scr[p] = s[p]
        st_ref[0, p] = s[p]


def _ret_scan(proj, logit, s0, backward):
    n = proj.shape[0]
    nb = n // RET_TB
    d = 1 if backward else 0
    blk = (lambda i: nb - 1 - i) if backward else (lambda i: i)
    qk_w = RET_HEADS * RET_DK
    state = (RET_PAIRS, 2 * RET_DK, RET_DV)
    return pl.pallas_call(
        functools.partial(_ret_kernel, backward=backward),
        grid=(nb,),
        in_specs=[
            pl.BlockSpec((RET_TB, qk_w), lambda i: (blk(i), B_RET // qk_w)),
            pl.BlockSpec((RET_TB, qk_w), lambda i: (blk(i), B_RET // qk_w + 1)),
            pl.BlockSpec((RET_TB, RET_W), lambda i: (blk(i), (B_RET + 2 * qk_w) // RET_W)),
            pl.BlockSpec((2, RET_HEADS), lambda i: (0, 0)),
            pl.BlockSpec((1, 1) + state,
                         lambda i: (jnp.maximum(_seq_index(blk(i) * RET_TB) - BATCH, 0), d, 0, 0, 0)),
        ],
        out_specs=[pl.BlockSpec((RET_TB, RET_W), lambda i: (blk(i), 0)),
                   pl.BlockSpec((1,) + state, lambda i: (_seq_index(blk(i) * RET_TB), 0, 0, 0))],
        out_shape=[jax.ShapeDtypeStruct((n, RET_W), F32), jax.ShapeDtypeStruct((N_SEQS,) + state, F32)],
        scratch_shapes=[pltpu.VMEM(state, F32)],
        compiler_params=_cparams(("arbitrary",)),
        name="retention_scan_bwd" if backward else "retention_scan_fwd",
    )(proj, proj, proj, logit, s0)


def _standardize(x, eps):
    mu = jnp.mean(x, axis=-1, keepdims=True)
    xc = x - mu
    return xc * lax.rsqrt(jnp.mean(xc * xc, axis=-1, keepdims=True) + eps)


def _retpost_kernel(of_ref, ob_ref, g_ref, nw_ref, out_ref):
    o = of_ref[...] + ob_ref[...]
    g = g_ref[...].astype(F32)
    for h in range(RET_HEADS):
        sl = slice(h * RET_DV, (h + 1) * RET_DV)
        gh = g[:, sl]
        out_ref[:, sl] = (gh * _sigmoid(gh) * (_standardize(o[:, sl], EPS) * nw_ref[:, sl])).astype(BF16)


def _ret_post(of, ob, proj, norm_w):
    n = proj.shape[0]
    tm = 512
    return pl.pallas_call(
        _retpost_kernel,
        grid=(n // tm,),
        in_specs=[
            pl.BlockSpec((tm, RET_W), lambda i: (i, 0)),
            pl.BlockSpec((tm, RET_W), lambda i: (i, 0)),
            pl.BlockSpec((tm, RET_W), lambda i: (i, (B_RET + 1024) // RET_W)),
            pl.BlockSpec((1, RET_W), lambda i: (0, 0)),
        ],
        out_specs=pl.BlockSpec((tm, RET_W), lambda i: (i, 0)),
        out_shape=jax.ShapeDtypeStruct((n, RET_W), BF16),
        compiler_params=_cparams(("parallel",)),
        name="retention_post",
    )(of, ob, proj, norm_w.reshape(1, RET_W))


def _retention_branch(proj, state0, p):
    s0 = state0.reshape(DEC_BATCH, 2, RET_PAIRS, 2 * RET_DK, RET_DV)
    of, sf = _ret_scan(proj, p['ret_decay_logit'], s0, backward=False)
    ob, sb = _ret_scan(proj, p['ret_decay_logit'], s0, backward=True)
    st = jnp.stack([sf, sb], axis=1).reshape(N_SEQS, 2, RET_HEADS, RET_DK, RET_DV)
    return _ret_post(of, ob, proj, p['ret_norm_w']), st


CV_TM = 256
CV_HALO = 16


def _conv_kernel(a_ref, g_ref, ap_ref, gp_ref, an_ref, gn_ref, w_ref, b_ref, lnw_ref, lnb_ref, o_ref, buf, sbuf):
    row0 = pl.program_id(0) * CV_TM
    pos0, len0 = _seq_pos(row0)
    glu = lambda a, g: a[...].astype(F32) * _sigmoid(g[...].astype(F32))
    buf[CV_HALO:CV_HALO + CV_TM, :] = glu(a_ref, g_ref)
    buf[0:CV_HALO, :] = jnp.where(pos0 == 0, 0.0, glu(ap_ref, gp_ref))
    buf[CV_HALO + CV_TM:, :] = jnp.where(pos0 + CV_TM == len0, 0.0, glu(an_ref, gn_ref))
    base = CV_HALO - CONV_K // 2
    acc = jnp.zeros((CV_TM, CV_W), F32)
    for r in range(8):
        taps = [m for m in range((base + CONV_K + 7) // 8) if 0 <= r + 8 * m - base < CONV_K]
        span = CV_TM + 8 * max(taps)
        if r:
            sbuf[0:span, :] = buf[r:r + span, :]
        src = sbuf if r else buf
        for m in taps:
            j = r + 8 * m - base
            acc = acc + w_ref[j:j + 1, :] * src[8 * m:8 * m + CV_TM, :]
    z = _standardize(acc + b_ref[...], EPS) * lnw_ref[...] + lnb_ref[...]
    o_ref[...] = (z * _sigmoid(z)).astype(BF16)


def _conv_branch(proj, p):
    n = proj.shape[0]
    nh = n // CV_HALO
    steps = CV_TM // CV_HALO
    ca = B_CV // CV_W
    prev = lambda i: jnp.maximum(i * steps - 1, 0)
    nxt = lambda i: jnp.minimum((i + 1) * steps, nh - 1)
    vec = pl.BlockSpec((1, CV_W), lambda i: (0, 0))
    return pl.pallas_call(
        _conv_kernel,
        grid=(n // CV_TM,),
        in_specs=[
            pl.BlockSpec((CV_TM, CV_W), lambda i: (i, ca)),
            pl.BlockSpec((CV_TM, CV_W), lambda i: (i, ca + 1)),
            pl.BlockSpec((CV_HALO, CV_W), lambda i: (prev(i), ca)),
            pl.BlockSpec((CV_HALO, CV_W), lambda i: (prev(i), ca + 1)),
            pl.BlockSpec((CV_HALO, CV_W), lambda i: (nxt(i), ca)),
            pl.BlockSpec((CV_HALO, CV_W), lambda i: (nxt(i), ca + 1)),
            pl.BlockSpec((CONV_K, CV_W), lambda i: (0, 0)),
            vec, vec, vec,
        ],
        out_specs=pl.BlockSpec((CV_TM, CV_W), lambda i: (i, 0)),
        out_shape=jax.ShapeDtypeStruct((n, CV_W), BF16),
        scratch_shapes=[pltpu.VMEM((CV_TM + 2 * CV_HALO, CV_W), F32)] * 2,
        compiler_params=_cparams(("parallel",)),
        name="conformer_conv",
    )(proj, proj, proj, proj, proj, proj, p['cv_dw_w'], p['cv_dw_b'].reshape(1, CV_W),
      p['cv_ln_w'].reshape(1, CV_W), p['cv_ln_b'].reshape(1, CV_W))


MG_TM = 512


def _merge_kernel(da_ref, rw_ref, ret_ref, cv_ref, g0_ref, g1_ref, g2_ref, g3_ref, x_ref, mod_ref, wb_ref, wo_ref,
                  nw_ref, x_o, h_o):
    m = None
    for n, (br, gt) in enumerate(((da_ref, g0_ref), (rw_ref, g1_ref), (ret_ref, g2_ref), (cv_ref, g3_ref))):
        t = _sigmoid(gt[...].astype(F32)) * _dg(br[...], wb_ref[n], NN)
        m = t if m is None else m + t
    out = _dg(m.astype(BF16), wo_ref[...], NN)
    mrow = _mod_row(mod_ref, pl.program_id(0) * MG_TM)
    gate1 = mrow[:, 2 * D_MODEL:3 * D_MODEL]
    sh2 = mrow[:, 3 * D_MODEL:4 * D_MODEL]
    sc2 = mrow[:, 4 * D_MODEL:5 * D_MODEL]
    x1 = x_ref[...] + gate1 * out
    x_o[...] = x1
    h_o[...] = _rms(x1, nw_ref[...]) * (1.0 + sc2) + sh2


def _merge(branches, gates, x, mod, w_branch, w_out, norm_w):
    n = x.shape[0]
    br = pl.BlockSpec((MG_TM, BR_W), lambda i: (i, 0))
    gspec = lambda j: pl.BlockSpec((MG_TM, D_MODEL), lambda i: (i, B_GATE // D_MODEL + j))
    full = pl.BlockSpec((MG_TM, D_MODEL), lambda i: (i, 0))
    out = jax.ShapeDtypeStruct((n, D_MODEL), F32)
    return pl.pallas_call(
        _merge_kernel,
        grid=(n // MG_TM,),
        in_specs=[br, br, br, br, gspec(0), gspec(1), gspec(2), gspec(3), full,
                  pl.BlockSpec((MOD_ROWS, 6 * D_MODEL), lambda i: (0, 0)),
                  pl.BlockSpec((N_BRANCH, BR_W, D_MODEL), lambda i: (0, 0, 0)),
                  pl.BlockSpec((D_MODEL, D_MODEL), lambda i: (0, 0)),
                  pl.BlockSpec((1, D_MODEL), lambda i: (0, 0))],
        out_specs=[full, full],
        out_shape=[out, out],
        compiler_params=_cparams(("parallel",)),
        name="gated_merge",
    )(*branches, gates, gates, gates, gates, x, mod, w_branch.astype(BF16), w_out.astype(BF16),
      norm_w.reshape(1, D_MODEL))


RT_TM = 256
MOE_BM = 512
MOE_ROWS = N_TOK * TOP_K + N_EXPERTS * MOE_BM
DP_TM = 512
DP_GROUP = 64
CB_TM = 256
DMA_UNROLL = 8


def _router_kernel(h_ref, w_ref, b_ref, tri_ref, idx_o, gate_o, rank_o, cnt_o, carry):
    @pl.when(pl.program_id(0) == 0)
    def _():
        carry[...] = jnp.zeros_like(carry)

    logits = _dot3(w_ref[...], h_ref[...], NT) + b_ref[...]
    e_iota = lax.broadcasted_iota(jnp.int32, logits.shape, 0)
    work = logits
    vals, idxs, hots = [], [], []
    for _ in range(TOP_K):
        mx = jnp.max(work, axis=0, keepdims=True)
        ix = jnp.min(jnp.where(work == mx, e_iota, N_EXPERTS), axis=0, keepdims=True)
        hot = e_iota == ix
        vals.append(mx)
        idxs.append(ix)
        hots.append(hot.astype(F32))
        work = jnp.where(hot, -jnp.inf, work)
    es = [jnp.exp(v - vals[0]) for v in vals]
    inv = 1.0 / (es[0] + es[1] + es[2] + es[3])
    chosen = hots[0] + hots[1] + hots[2] + hots[3]
    ahead = carry[...][:, 0:1] + _dg(chosen.astype(BF16), tri_ref[...], NN)
    idx_o[...] = jnp.concatenate(idxs, axis=0)
    gate_o[...] = jnp.concatenate([e * inv for e in es], axis=0)
    rank_o[...] = jnp.concatenate(
        [jnp.sum(hot * ahead, axis=0, keepdims=True) for hot in hots], axis=0).astype(jnp.int32)
    carry[...] = carry[...] + jnp.sum(chosen, axis=1, keepdims=True)
    cnt_o[...] = carry[...]


def _router(h, router_w, router_b):
    n = h.shape[0]
    tri = (jnp.arange(RT_TM)[:, None] < jnp.arange(RT_TM)[None, :]).astype(BF16)
    col = pl.BlockSpec((TOP_K, RT_TM), lambda i: (0, i))
    return pl.pallas_call(
        _router_kernel,
        grid=(n // RT_TM,),
        in_specs=[
            pl.BlockSpec((RT_TM, D_MODEL), lambda i: (i, 0)),
            pl.BlockSpec((N_EXPERTS, D_MODEL), lambda i: (0, 0)),
            pl.BlockSpec((N_EXPERTS, 1), lambda i: (0, 0)),
            pl.BlockSpec((RT_TM, RT_TM), lambda i: (0, 0)),
        ],
        out_specs=[col, col, col, pl.BlockSpec((N_EXPERTS, 128), lambda i: (0, 0))],
        out_shape=[jax.ShapeDtypeStruct((TOP_K, n), jnp.int32), jax.ShapeDtypeStruct((TOP_K, n), F32),
                   jax.ShapeDtypeStruct((TOP_K, n), jnp.int32), jax.ShapeDtypeStruct((N_EXPERTS, 128), F32)],
        scratch_shapes=[pltpu.VMEM((N_EXPERTS, 128), F32)],
        compiler_params=_cparams(("arbitrary",)),
        name="router",
    )(h, router_w.T, router_b.reshape(N_EXPERTS, 1), tri)


def _tile_major(t, tm):
    k, n = t.shape
    return t.reshape(k, n // tm, tm).transpose(1, 0, 2).reshape(n // tm, 1, k * tm)


def _dispatch_kernel(dest_ref, pe_ref, h_ref, o_hbm, zbuf, sem, zsem):
    n_groups = DP_TM // DP_GROUP

    @pl.when(pl.program_id(0) == 0)
    def _():
        zbuf[...] = jnp.zeros_like(zbuf)

        def fill(e):
            end = pe_ref[e]
            begin = pe_ref[e - 1] if e else 0
            return end > begin, pltpu.make_async_copy(
                zbuf, o_hbm.at[pl.ds(pl.multiple_of(jnp.maximum(end - MOE_BM, 0), MOE_BM), MOE_BM)], zsem.at[0])

        for e in range(N_EXPERTS):
            nonempty, cp = fill(e)
            pl.when(nonempty)(cp.start)
        for e in range(N_EXPERTS):
            nonempty, cp = fill(e)
            pl.when(nonempty)(cp.wait)

        def tail(b):
            return pltpu.make_async_copy(zbuf, o_hbm.at[pl.ds(pl.multiple_of(b * MOE_BM, MOE_BM), MOE_BM)],
                                         zsem.at[0])

        first_unused = pe_ref[N_EXPERTS - 1] // MOE_BM
        lax.fori_loop(first_unused, MOE_ROWS // MOE_BM, lambda b, c: (tail(b).start(), c)[1], 0)
        lax.fori_loop(first_unused, MOE_ROWS // MOE_BM, lambda b, c: (tail(b).wait(), c)[1], 0)

    def wait_group(slot):
        pltpu.make_async_copy(h_ref.at[pl.ds(0, TOP_K * DP_GROUP)], o_hbm.at[pl.ds(0, TOP_K * DP_GROUP)],
                              sem.at[slot]).wait()

    def group(gi, carry):
        slot = gi % 2

        def issue(t, c):
            tok = gi * DP_GROUP + t
            for k in range(TOP_K):
                dst = dest_ref[0, 0, k * DP_TM + tok]
                pltpu.make_async_copy(h_ref.at[pl.ds(tok, 1)], o_hbm.at[pl.ds(dst, 1)], sem.at[slot]).start()
            return c

        lax.fori_loop(0, DP_GROUP, issue, 0, unroll=DMA_UNROLL)

        @pl.when(gi > 0)
        def _():
            wait_group(1 - slot)

        return carry

    lax.fori_loop(0, n_groups, group, 0)
    wait_group((n_groups - 1) % 2)


def _dispatch(h, dest, pad_end):
    n = h.shape[0]
    return pl.pallas_call(
        _dispatch_kernel,
        grid=(n // DP_TM,),
        in_specs=[
            pl.BlockSpec((1, 1, TOP_K * DP_TM), lambda i: (i, 0, 0), memory_space=pltpu.SMEM),
            pl.BlockSpec(memory_space=pltpu.SMEM),
            pl.BlockSpec((DP_TM, D_MODEL), lambda i: (i, 0)),
        ],
        out_specs=pl.BlockSpec(memory_space=pl.ANY),
        out_shape=jax.ShapeDtypeStruct((MOE_ROWS, D_MODEL), F32),
        scratch_shapes=[pltpu.VMEM((MOE_BM, D_MODEL), F32), pltpu.SemaphoreType.DMA((2,)),
                        pltpu.SemaphoreType.DMA((1,))],
        compiler_params=_cparams(("arbitrary",)),
        name="moe_dispatch",
    )(_tile_major(dest, DP_TM), pad_end, h)


def _expert_kernel(bx_ref, be_ref, nv_ref, x_ref, w1_ref, b1_ref, w2_ref, b2_ref, o_ref, w1b, w2b):
    i = pl.program_id(0)
    changed = jnp.logical_or(i == 0, be_ref[i] != be_ref[jnp.maximum(i - 1, 0)])

    @pl.when(changed)
    def _():
        w1b[...] = w1_ref[0, 0].astype(BF16)
        w2b[...] = w2_ref[0, 0].astype(BF16)

    def ffn(rows):
        hb = _dg(x_ref[0:rows, :].astype(BF16), w1b[...], NN) + b1_ref[0, 0]
        hg = jnp.minimum(hb[:, :D_FF], SWIGLU_LIMIT)
        hu = jnp.clip(hb[:, D_FF:], -SWIGLU_LIMIT, SWIGLU_LIMIT)
        act = hg * _sigmoid(SWIGLU_ALPHA * hg) * (hu + 1.0)
        o_ref[0:rows, :] = _dg(act.astype(BF16), w2b[...], NN) + b2_ref[0, 0]

    nv = nv_ref[i]
    half = MOE_BM // 2

    @pl.when(nv > half)
    def _():
        ffn(MOE_BM)

    @pl.when(jnp.logical_and(nv > 0, nv <= half))
    def _():
        ffn(half)
        o_ref[half:, :] = jnp.zeros((MOE_BM - half, D_MODEL), F32)

    @pl.when(nv == 0)
    def _():
        o_ref[...] = jnp.zeros_like(o_ref)


def _experts(x_rows, blk_x, blk_e, n_valid, layer, w1, b1, w2, b2):
    nb = MOE_ROWS // MOE_BM
    grid_spec = pltpu.PrefetchScalarGridSpec(
        num_scalar_prefetch=3,
        grid=(nb,),
        in_specs=[
            pl.BlockSpec((MOE_BM, D_MODEL), lambda i, bx, be, nv: (bx[i], 0)),
            pl.BlockSpec((1, 1, D_MODEL, 2 * D_FF), lambda i, bx, be, nv: (layer, be[i], 0, 0)),
            pl.BlockSpec((1, 1, 1, 2 * D_FF), lambda i, bx, be, nv: (layer, be[i], 0, 0)),
            pl.BlockSpec((1, 1, D_FF, D_MODEL), lambda i, bx, be, nv: (layer, be[i], 0, 0)),
            pl.BlockSpec((1, 1, 1, D_MODEL), lambda i, bx, be, nv: (layer, be[i], 0, 0)),
        ],
        out_specs=pl.BlockSpec((MOE_BM, D_MODEL), lambda i, bx, be, nv: (i, 0)),
        scratch_shapes=[pltpu.VMEM((D_MODEL, 2 * D_FF), BF16), pltpu.VMEM((D_FF, D_MODEL), BF16)],
    )
    return pl.pallas_call(
        _expert_kernel,
        grid_spec=grid_spec,
        out_shape=jax.ShapeDtypeStruct((MOE_ROWS, D_MODEL), F32),
        compiler_params=pltpu.CompilerParams(dimension_semantics=("arbitrary",),
                                             vmem_limit_bytes=56 * 1024 * 1024),
        name="moe_experts",
    )(blk_x, blk_e, n_valid, x_rows, w1, b1.reshape(DEPTH, N_EXPERTS, 1, 2 * D_FF), w2,
      b2.reshape(DEPTH, N_EXPERTS, 1, D_MODEL))


def _combine_kernel(dest_ref, gate_ref, x_ref, mod_ref, fw_ref, y_hbm, *rest, final):
    if final:
        o_ctx_ref, o_lat_ref, buf, sem = rest
    else:
        o_ref, buf, sem = rest

    def issue(t, c):
        for k in range(TOP_K):
            dst = dest_ref[0, 0, k * CB_TM + t]
            pltpu.make_async_copy(y_hbm.at[pl.ds(dst, 1)], buf.at[k, pl.ds(t, 1)], sem.at[0]).start()
        return c

    lax.fori_loop(0, CB_TM, issue, 0, unroll=DMA_UNROLL)
    for k in range(TOP_K):
        pltpu.make_async_copy(y_hbm.at[pl.ds(0, CB_TM)], buf.at[k], sem.at[0]).wait()
    g = gate_ref[...]
    acc = g[:, 0:1] * buf[0]
    for k in range(1, TOP_K):
        acc = acc + g[:, k:k + 1] * buf[k]
    gate2 = _mod_row(mod_ref, pl.program_id(0) * CB_TM)[:, 5 * D_MODEL:6 * D_MODEL]
    x2 = x_ref[...] + gate2 * acc
    if not final:
        o_ref[...] = x2
        return
    y = _rms(x2, fw_ref[...])
    in_ctx = pl.program_id(0) < N_CTX // CB_TM

    @pl.when(in_ctx)
    def _():
        o_ctx_ref[...] = y

    @pl.when(jnp.logical_not(in_ctx))
    def _():
        o_lat_ref[...] = y


def _combine(y_rows, dest, gates, x, mod, final_w, final):
    n = x.shape[0]
    full = pl.BlockSpec((CB_TM, D_MODEL), lambda i: (i, 0))
    nct = N_CTX // CB_TM
    if final:
        out_specs = [pl.BlockSpec((CB_TM, D_MODEL), lambda i: (jnp.minimum(i, nct - 1), 0)),
                     pl.BlockSpec((CB_TM, D_MODEL), lambda i: (jnp.maximum(i - nct, 0), 0))]
        out_shape = [jax.ShapeDtypeStruct((N_CTX, D_MODEL), F32), jax.ShapeDtypeStruct((N_LAT, D_MODEL), F32)]
    else:
        out_specs, out_shape = full, jax.ShapeDtypeStruct((n, D_MODEL), F32)
    return pl.pallas_call(
        functools.partial(_combine_kernel, final=final),
        grid=(n // CB_TM,),
        in_specs=[
            pl.BlockSpec((1, 1, TOP_K * CB_TM), lambda i: (i, 0, 0), memory_space=pltpu.SMEM),
            pl.BlockSpec((CB_TM, TOP_K), lambda i: (i, 0)),
            full,
            pl.BlockSpec((MOD_ROWS, 6 * D_MODEL), lambda i: (0, 0)),
            pl.BlockSpec((1, D_MODEL), lambda i: (0, 0)),
            pl.BlockSpec(memory_space=pl.ANY),
        ],
        out_specs=out_specs,
        out_shape=out_shape,
        scratch_shapes=[pltpu.VMEM((TOP_K, CB_TM, D_MODEL), F32), pltpu.SemaphoreType.DMA((1,))],
        compiler_params=_cparams(("arbitrary",)),
        name="moe_combine",
    )(_tile_major(dest, CB_TM), gates.T, x, mod, final_w.reshape(1, D_MODEL), y_rows)


def _routed_ffn(h, x, mod, p, moe, layer, final_w, final):
    idx, gates, rank, counts = _router(h, p['router_w'], p['router_b'])
    counts = counts[:, 0].astype(jnp.int32)
    padded = (counts + MOE_BM - 1) // MOE_BM * MOE_BM
    pad_end = jnp.cumsum(padded)
    pad_start = pad_end - padded
    experts = jnp.arange(N_EXPERTS, dtype=jnp.int32)
    start_of = jnp.sum(jnp.where(idx[:, :, None] == experts, pad_start, 0), axis=-1)
    dest = start_of + rank
    nb = MOE_ROWS // MOE_BM
    first_row = jnp.arange(nb, dtype=jnp.int32) * MOE_BM
    blk_e = jnp.minimum(jnp.sum((pad_end[None, :] <= first_row[:, None]).astype(jnp.int32), axis=1),
                        N_EXPERTS - 1)
    is_e = blk_e[:, None] == experts[None, :]
    end_of = jnp.sum(jnp.where(is_e, pad_start + counts, 0), axis=1)
    n_valid = jnp.clip(end_of - first_row, 0, MOE_BM).astype(jnp.int32)
    blk_x = jnp.minimum(jnp.arange(nb, dtype=jnp.int32), pad_end[-1] // MOE_BM - 1)
    x_rows = _dispatch(h, dest, pad_end)
    y_rows = _experts(x_rows, blk_x, blk_e, n_valid, layer, *moe)
    return _combine(y_rows, dest, gates, x, mod, final_w, final)


_LAYER_PARAMS = ('norm_mix_w', 'norm_ffn_w', 'da_lambda', 'da_norm_w', 'rw_shift', 'rw_w0', 'rw_w_up', 'rw_a0',
                 'rw_a_up', 'rw_g_up', 'rw_k_k', 'rw_k_a', 'rw_r_k', 'rw_ln_w', 'rw_ln_b', 'ret_decay_logit',
                 'ret_norm_w', 'cv_dw_w', 'cv_dw_b', 'cv_ln_w', 'cv_ln_b', 'w_branch', 'w_out', 'router_w',
                 'router_b')


def _layer(x, mod, w_p, p, moe, layer, lam_init, caches, tables, final_w, final):
    cache_k, cache_v, state_rw, state_ret = caches
    proj_rw, proj = _input_projection(x, mod, p['norm_mix_w'], w_p, layer)
    o_da = _da_branch(proj, cache_k, cache_v, p['da_lambda'], p['da_norm_w'], lam_init, tables)
    o_rw, rw_state = _rwkv_branch(proj_rw, state_rw, p)
    o_ret, ret_state = _retention_branch(proj, state_ret, p)
    o_cv = _conv_branch(proj, p)
    x1, h2 = _merge((o_da, o_rw, o_ret, o_cv), proj, x, mod, p['w_branch'], p['w_out'], p['norm_ffn_w'])
    x2 = _routed_ffn(h2, x1, mod, p, moe, layer, final_w, final)
    ctx_kv = proj[:N_CTX, B_DA + DA_W:B_DA + 3 * DA_W].astype(F32)
    new_k = ctx_kv[:, :DA_W].reshape(BATCH, SEQ, DA_HEADS, 2, DA_QK)
    new_v = ctx_kv[:, DA_W:].reshape(BATCH, SEQ, DA_HEADS, DA_V)
    return x2, (new_k, new_v, rw_state[:BATCH], ret_state[:BATCH])


def kernel(x_prompt, x_sample, c, cache_da_k, cache_da_v, state_rwkv, state_ret, c_ctx, ada_w, ada_b, norm_mix_w,
           norm_ffn_w, w_in, da_lambda, da_norm_w, rw_shift, rw_w0, rw_w_up, rw_a0, rw_a_up, rw_g_up, rw_k_k,
           rw_k_a, rw_r_k, rw_ln_w, rw_ln_b, ret_decay_logit, ret_norm_w, cv_dw_w, cv_dw_b, cv_ln_w, cv_ln_b,
           w_branch, w_out, router_w, router_b, moe_w1, moe_b1, moe_w2, moe_b2, final_norm_w):
    weights = dict(norm_mix_w=norm_mix_w, norm_ffn_w=norm_ffn_w, da_lambda=da_lambda, da_norm_w=da_norm_w,
                   rw_shift=rw_shift, rw_w0=rw_w0, rw_w_up=rw_w_up, rw_a0=rw_a0, rw_a_up=rw_a_up, rw_g_up=rw_g_up,
                   rw_k_k=rw_k_k, rw_k_a=rw_k_a, rw_r_k=rw_r_k, rw_ln_w=rw_ln_w, rw_ln_b=rw_ln_b,
                   ret_decay_logit=ret_decay_logit, ret_norm_w=ret_norm_w, cv_dw_w=cv_dw_w, cv_dw_b=cv_dw_b,
                   cv_ln_w=cv_ln_w, cv_ln_b=cv_ln_b, w_branch=w_branch, w_out=w_out, router_w=router_w,
                   router_b=router_b)
    x = jnp.concatenate([x_prompt.reshape(N_CTX, D_MODEL), x_sample.reshape(N_LAT, D_MODEL)], axis=0)
    cvec = jnp.concatenate([c_ctx[None, :], c, jnp.zeros((MOD_ROWS - 1 - DEC_BATCH, D_MODEL), F32)], axis=0)
    mod = _modulation(cvec, ada_w, ada_b)
    w_p = _pad_w_in(w_in).astype(BF16)
    tables = _rope_tables()
    outs = []
    moe = (moe_w1, moe_b1, moe_w2, moe_b2)
    for i in range(DEPTH):
        p = {name: weights[name][i] for name in _LAYER_PARAMS}
        lam_init = 0.8 - 0.6 * math.exp(-0.3 * i)
        caches = (cache_da_k[:, i], cache_da_v[:, i], state_rwkv[:, i], state_ret[:, i])
        x, ctx_out = _layer(x, mod[i], w_p, p, moe, i, lam_init, caches, tables, final_norm_w, i == DEPTH - 1)
        outs.append(ctx_out)
    y_prompt = x[0].reshape(BATCH, SEQ, D_MODEL)
    y_sample = x[1].reshape(DEC_BATCH, DEC_SEQ, D_MODEL)
    stack = lambda j: jnp.stack([o[j] for o in outs], axis=1)
    return (y_prompt, y_sample, stack(0), stack(1), stack(2), stack(3))


def _pad_w_in(w_in):
    da, rw, ret, cv, gate = jnp.split(w_in, [1536, 3456, 4992, 6016], axis=-1)
    pad = jnp.zeros(w_in.shape[:-1] + (P_DA - RW_COLS,), w_in.dtype)
    return jnp.concatenate([rw, pad, da, ret, cv, gate], axis=-1)
```

```python
import functools
import math

import jax
import jax.numpy as jnp
from jax import lax
from jax.experimental import pallas as pl
from jax.experimental.pallas import tpu as pltpu

F32 = jnp.float32
BF16 = jnp.bfloat16

D_MODEL = 1024
BATCH = 16
SEQ = 256
DEPTH = 2
DEC_BATCH = 2
DEC_SEQ = 4096
PAST_LEN = 512
GRID_W = 64
EPS = 1e-6

DA_HEADS = 4
DA_QK = 64
DA_V = 128
ROPE_BASE = 10000.0

RW_HEADS = 8
RW_HD = 64
RW_W = 512
RW_LORA = 64
RW_G_LORA = 128
RW_GN_EPS = 64e-5

RET_HEADS = 4
RET_DK = 64
RET_DV = 128
RET_CHUNK = 128

CV_W = 512
CONV_K = 31
BR_W = 512
N_BRANCH = 4

N_EXPERTS = 32
TOP_K = 4
D_FF = 1024
SWIGLU_LIMIT = 7.0
SWIGLU_ALPHA = 1.702

N_CTX = BATCH * SEQ
N_LAT = DEC_BATCH * DEC_SEQ
N_TOK = N_CTX + N_LAT
N_SEQS = BATCH + DEC_BATCH
MOD_ROWS = 8
MOD_GROUP = 4096

RW_COLS = 1920
P_RW = 0
P_DA = 2048
P_RET = 3584
P_CV = 5120
P_GATE = 6144
P_COLS = 10240
B_DA, B_RET, B_CV, B_GATE = (p - P_DA for p in (P_DA, P_RET, P_CV, P_GATE))

VMEM_LIMIT = 48 * 1024 * 1024


def _cparams(sem):
    return pltpu.CompilerParams(dimension_semantics=sem, vmem_limit_bytes=VMEM_LIMIT)


def _dg(a, b, dims):
    return lax.dot_general(a, b, (dims, ((), ())), preferred_element_type=F32)


NN = ((1,), (0,))
NT = ((1,), (1,))
TN = ((0,), (0,))


def _dot(a, b, dims=NN):
    return _dg(a.astype(BF16), b.astype(BF16), dims)


def _split(x):
    hi = x.astype(BF16)
    lo = (x - hi.astype(F32)).astype(BF16)
    return hi, lo


def _dot3(a, b, dims=NN):
    ah, al = _split(a)
    bh, bl = _split(b)
    return _dg(ah, bh, dims) + (_dg(ah, bl, dims) + _dg(al, bh, dims))


def _dot2x(a, e, dims=NN):
    ah, al = _split(a)
    am = (a - ah.astype(F32) - al.astype(F32)).astype(BF16)
    eb = e.astype(BF16)
    return _dg(ah, eb, dims) + (_dg(al, eb, dims) + _dg(am, eb, dims))


def _sigmoid(x):
    return 1.0 / (1.0 + jnp.exp(-x))


def _softplus(x):
    return jnp.maximum(x, 0.0) + jnp.log(1.0 + jnp.exp(-jnp.abs(x)))


def _seq_pos(row):
    in_ctx = row < N_CTX
    pos = jnp.where(in_ctx, row & (SEQ - 1), (row - N_CTX) & (DEC_SEQ - 1))
    length = jnp.where(in_ctx, SEQ, DEC_SEQ)
    return pos, length


def _seq_index(row):
    return jnp.where(row < N_CTX, row // SEQ, BATCH + (row - N_CTX) // DEC_SEQ)


def _dotx(e, b, dims=NN):
    bh, bl = _split(b)
    bm = (b - bh.astype(F32) - bl.astype(F32)).astype(BF16)
    eb = e.astype(BF16)
    return _dg(eb, bh, dims) + (_dg(eb, bl, dims) + _dg(eb, bm, dims))


def _mod_kernel(c_ref, w_ref, b_ref, o_ref):
    c = c_ref[...]
    s = c * _sigmoid(c)
    o_ref[0] = _dot3(s, w_ref[0]) + b_ref[0]


def _modulation(cvec, ada_w, ada_b):
    tn = 1536
    return pl.pallas_call(
        _mod_kernel,
        grid=(DEPTH, 6 * D_MODEL // tn),
        in_specs=[
            pl.BlockSpec((MOD_ROWS, D_MODEL), lambda l, j: (0, 0)),
            pl.BlockSpec((1, D_MODEL, tn), lambda l, j: (l, 0, j)),
            pl.BlockSpec((1, 1, tn), lambda l, j: (l, 0, j)),
        ],
        out_specs=pl.BlockSpec((1, MOD_ROWS, tn), lambda l, j: (l, 0, j)),
        out_shape=jax.ShapeDtypeStruct((DEPTH, MOD_ROWS, 6 * D_MODEL), F32),
        compiler_params=_cparams(("parallel", "parallel")),
        name="modulation",
    )(cvec, ada_w, ada_b.reshape(DEPTH, 1, 6 * D_MODEL))


def _mod_row(mod_ref, first_row):
    g = first_row // MOD_GROUP
    return mod_ref[pl.ds(g, 1), :]


def _rms(x, w):
    return x * lax.rsqrt(jnp.mean(x * x, axis=-1, keepdims=True) + EPS) * w


IN_TM = 1024
IN_TN = 1024


IN_NA = P_DA // IN_TN


def _inproj_kernel(x_ref, mod_ref, nw_ref, w_ref, oa_ref, og_ref, h_ref):
    i = pl.program_id(0)
    j = pl.program_id(1)

    @pl.when(j == 0)
    def _():
        m = _mod_row(mod_ref, i * IN_TM)
        sh = m[:, 0:D_MODEL]
        sc = m[:, D_MODEL:2 * D_MODEL]
        h_ref[...] = (_rms(x_ref[...], nw_ref[...]) * (1.0 + sc) + sh).astype(BF16)

    acc = _dg(h_ref[...], w_ref[0].astype(BF16), NN)

    @pl.when(j < IN_NA)
    def _():
        oa_ref[...] = acc

    @pl.when(j >= IN_NA)
    def _():
        og_ref[...] = acc.astype(BF16)


def _input_projection(x, mod, norm_w, w_p, layer):
    n = x.shape[0]
    return pl.pallas_call(
        _inproj_kernel,
        grid=(n // IN_TM, P_COLS // IN_TN),
        in_specs=[
            pl.BlockSpec((IN_TM, D_MODEL), lambda i, j: (i, 0)),
            pl.BlockSpec((MOD_ROWS, 6 * D_MODEL), lambda i, j: (0, 0)),
            pl.BlockSpec((1, D_MODEL), lambda i, j: (0, 0)),
            pl.BlockSpec((1, D_MODEL, IN_TN), lambda i, j: (layer, 0, j)),
        ],
        out_specs=[pl.BlockSpec((IN_TM, IN_TN), lambda i, j: (i, jnp.minimum(j, IN_NA - 1))),
                   pl.BlockSpec((IN_TM, IN_TN), lambda i, j: (i, jnp.maximum(j - IN_NA, 0)))],
        out_shape=[jax.ShapeDtypeStruct((n, P_DA), F32), jax.ShapeDtypeStruct((n, P_COLS - P_DA), BF16)],
        scratch_shapes=[pltpu.VMEM((IN_TM, D_MODEL), BF16)],
        compiler_params=_cparams(("parallel", "arbitrary")),
        name="input_projection",
    )(x, mod, norm_w.reshape(1, D_MODEL), w_p)


QK_TM = 512
DA_W = DA_HEADS * 2 * DA_QK


def _qkprep_kernel(q_ref, k_ref, v_ref, c_ref, se_ref, so_ref, qo_ref, ko_ref, vo_ref):
    i = pl.program_id(0)
    for h in range(DA_HEADS):
        vo_ref[:, h * 2 * DA_V:h * 2 * DA_V + DA_V] = v_ref[:, h * DA_V:(h + 1) * DA_V].astype(BF16)
        vo_ref[:, h * 2 * DA_V + DA_V:(h + 1) * 2 * DA_V] = jnp.ones((QK_TM, DA_V), BF16)
    scale = DA_QK ** -0.5

    @pl.when(i * QK_TM < N_CTX)
    def _():
        qo_ref[...] = (q_ref[...].astype(F32) * scale).astype(BF16)
        ko_ref[...] = k_ref[...].astype(BF16)

    @pl.when(i * QK_TM >= N_CTX)
    def _():
        c = c_ref[...]
        se = se_ref[...]
        so = so_ref[...]

        def rope(x):
            nxt = pltpu.roll(x, DA_W - 1, axis=1)
            prv = pltpu.roll(x, 1, axis=1)
            return x * c + nxt * se + prv * so

        qo_ref[...] = (rope(q_ref[...].astype(F32)) * scale).astype(BF16)
        ko_ref[...] = rope(k_ref[...].astype(F32)).astype(BF16)


def _rope_tables():
    rows = DEC_SEQ // GRID_W
    row = jnp.repeat(jnp.arange(rows, dtype=F32), GRID_W)
    col = jnp.tile(jnp.arange(GRID_W, dtype=F32), rows)
    n_pairs = DA_QK // 4
    inv = ROPE_BASE ** (-jnp.arange(n_pairs, dtype=F32) / n_pairs)
    ang = jnp.concatenate([row[:, None] * inv, col[:, None] * inv], axis=-1)
    cos = jnp.repeat(jnp.cos(ang), 2, axis=-1)
    sin = jnp.repeat(jnp.sin(ang), 2, axis=-1)
    even = (jnp.arange(DA_QK) % 2 == 0)[None, :]
    s_even = jnp.where(even, -sin, 0.0)
    s_odd = jnp.where(even, 0.0, sin)
    rep = lambda t: jnp.tile(t, (1, DA_W // DA_QK))
    return rep(cos), rep(s_even), rep(s_odd)


def _qk_prepare(proj, tables):
    n = proj.shape[0]
    lat0 = N_CTX // QK_TM
    nlat = DEC_SEQ // QK_TM
    tab = pl.BlockSpec((QK_TM, DA_W), lambda i: (jnp.maximum(i - lat0, 0) % nlat, 0))
    c0 = B_DA // DA_W
    out = jax.ShapeDtypeStruct((n, DA_W), BF16)
    return pl.pallas_call(
        _qkprep_kernel,
        grid=(n // QK_TM,),
        in_specs=[
            pl.BlockSpec((QK_TM, DA_W), lambda i: (i, c0)),
            pl.BlockSpec((QK_TM, DA_W), lambda i: (i, c0 + 1)),
            pl.BlockSpec((QK_TM, DA_W), lambda i: (i, c0 + 2)),
            tab, tab, tab,
        ],
        out_specs=[pl.BlockSpec((QK_TM, DA_W), lambda i: (i, 0))] * 2
        + [pl.BlockSpec((QK_TM, 2 * DA_W), lambda i: (i, 0))],
        out_shape=[out, out, jax.ShapeDtypeStruct((n, 2 * DA_W), BF16)],
        compiler_params=_cparams(("parallel",)),
        name="qk_prepare",
    )(proj, proj, proj, *tables)


DA_TQ = 256


def _attend(q, segments, lam, nw, o_ref):
    for h in range(DA_HEADS):
        os = []
        for m in range(2):
            c0 = (2 * h + m) * DA_QK
            ss = [_dg(q[:, c0:c0 + DA_QK], k[:, c0:c0 + DA_QK], NT) for k, _ in segments]
            mx = jnp.max(ss[0], axis=-1, keepdims=True)
            for s in ss[1:]:
                mx = jnp.maximum(mx, jnp.max(s, axis=-1, keepdims=True))
            oe = None
            for s, (_, v) in zip(ss, segments):
                part = _dg(jnp.exp(s - mx).astype(BF16), v[:, h * 2 * DA_V:(h + 1) * 2 * DA_V], NN)
                oe = part if oe is None else oe + part
            os.append(oe[:, :DA_V] * (1.0 / oe[:, DA_V:DA_V + 1]))
        o_ref[:, h * DA_V:(h + 1) * DA_V] = _rms(os[0] - lam * os[1], nw).astype(BF16)


def _da_kernel(q_ref, ks_ref, vs_ref, kl_ref, vl_ref, dl_ref, nw_ref, o_ref, *, lam_init):
    dl = dl_ref[...]
    lam = (jnp.exp(jnp.sum(dl[0:1] * dl[1:2], axis=1, keepdims=True))
           - jnp.exp(jnp.sum(dl[2:3] * dl[3:4], axis=1, keepdims=True)) + lam_init)
    nw = nw_ref[...] * (1.0 - lam_init)
    q = q_ref[...]
    is_ctx = pl.program_id(0) < BATCH * (SEQ // DA_TQ)

    @pl.when(is_ctx)
    def _():
        _attend(q, [(ks_ref, vs_ref)], lam, nw, o_ref)

    @pl.when(jnp.logical_not(is_ctx))
    def _():
        _attend(q, [(kl_ref.at[0], vl_ref.at[0])], lam, nw, o_ref)


def _da_branch(proj, cache_k, cache_v, da_lambda, da_norm_w, lam_init, tables):
    qb, kb, vb = _qk_prepare(proj, tables)
    n = proj.shape[0]
    ck = cache_k.reshape(DEC_BATCH, PAST_LEN, DA_W).astype(BF16)
    cv = cache_v.astype(BF16)
    cv = jnp.concatenate([cv, jnp.ones_like(cv)], axis=-1).reshape(DEC_BATCH, PAST_LEN, 2 * DA_W)
    lat = lambda t: t[N_CTX:].reshape(DEC_BATCH, DEC_SEQ, t.shape[-1])
    k_all = jnp.concatenate([ck, lat(kb)], axis=1)
    v_all = jnp.concatenate([cv, lat(vb)], axis=1)
    tk = PAST_LEN + DEC_SEQ
    n_ctx_tiles = N_CTX // DA_TQ
    per_seq = DEC_SEQ // DA_TQ
    seq_c = lambda i: jnp.minimum(i * DA_TQ // SEQ, BATCH - 1)
    seq_l = lambda i: jnp.clip((i - n_ctx_tiles) // per_seq, 0, DEC_BATCH - 1)
    once = pl.Buffered(1)
    return pl.pallas_call(
        functools.partial(_da_kernel, lam_init=lam_init),
        grid=(n // DA_TQ,),
        in_specs=[
            pl.BlockSpec((DA_TQ, DA_W), lambda i: (i, 0)),
            pl.BlockSpec((SEQ, DA_W), lambda i: (seq_c(i), 0)),
            pl.BlockSpec((SEQ, 2 * DA_W), lambda i: (seq_c(i), 0)),
            pl.BlockSpec((1, tk, DA_W), lambda i: (seq_l(i), 0, 0), pipeline_mode=once),
            pl.BlockSpec((1, tk, 2 * DA_W), lambda i: (seq_l(i), 0, 0), pipeline_mode=once),
            pl.BlockSpec((4, DA_QK), lambda i: (0, 0)),
            pl.BlockSpec((1, DA_V), lambda i: (0, 0)),
        ],
        out_specs=pl.BlockSpec((DA_TQ, DA_W), lambda i: (i, 0)),
        out_shape=jax.ShapeDtypeStruct((n, DA_W), BF16),
        compiler_params=_cparams(("arbitrary",)),
        name="diff_attention",
    )(qb, kb, vb, k_all, v_all, da_lambda, da_norm_w.reshape(1, DA_V))


RW_TM = 256
RW_C = 64
RW_PAIRS = RW_HEADS // 2
HALO = 8


def _head_ones():
    idx = jnp.arange(2 * RW_HD) // RW_HD
    return (idx[:, None] == idx[None, :]).astype(BF16)


def _head_sum(x, ones_pair):
    w = 2 * RW_HD
    return jnp.concatenate([_dot2x(x[:, p * w:(p + 1) * w], ones_pair) for p in range(RW_PAIRS)], axis=1)


def _rwprep_kernel(x_ref, xp_ref, xn_ref, mu_ref, w0_ref, wup_ref, a0_ref, aup_ref, gup_ref, kk_ref, ka_ref,
                   ones_ref, r_o, v_o, kk_o, g_o, kd_o, lw_o, a_o, buf):
    row0 = pl.program_id(0) * RW_TM
    x = x_ref[:, 0:RW_COLS]
    buf[HALO:HALO + RW_TM, :] = x
    buf[HALO - 1:HALO, :] = xp_ref[HALO - 1:HALO, 0:RW_COLS]
    buf[HALO + RW_TM:HALO + RW_TM + 1, :] = xn_ref[0:1, 0:RW_COLS]
    rows = row0 + lax.broadcasted_iota(jnp.int32, (RW_TM, 1), 0)
    pos, length = _seq_pos(rows)
    prev = jnp.where(pos == 0, 0.0, buf[HALO - 1:HALO - 1 + RW_TM, :])
    nxt = jnp.where(pos == length - 1, 0.0, buf[HALO + 1:HALO + 1 + RW_TM, :])
    mu = mu_ref[...]
    u = x + mu[0:1] * (prev - x) + mu[1:2] * (nxt - x)

    r = u[:, 0:RW_W]
    k = u[:, RW_W:2 * RW_W]
    v = u[:, 2 * RW_W:3 * RW_W]
    wl = u[:, 3 * RW_W:3 * RW_W + 128]
    al = u[:, 3 * RW_W + 128:3 * RW_W + 256]
    gl = u[:, 3 * RW_W + 256:3 * RW_W + 384]
    w_raw = w0_ref[...] + _dot3(jnp.tanh(wl), wup_ref[...])
    lw = -math.exp(-0.5) * _sigmoid(w_raw)
    a = _sigmoid(a0_ref[...] + _dot3(al, aup_ref[...]))
    g = _dot3(_sigmoid(gl), gup_ref[...])
    kk = k * kk_ref[...]
    kk = kk * lax.rsqrt(jnp.maximum(_head_sum(kk * kk, ones_ref[...]), 1e-12))
    kd = jnp.concatenate([k, k], axis=1) * (1.0 + (a - 1.0) * ka_ref[...])

    r_o[...] = r
    v_o[...] = v
    kk_o[...] = kk
    g_o[...] = g
    for d in range(2):
        kd_o[d] = kd[:, d * RW_W:(d + 1) * RW_W]
        lw_o[d] = lw[:, d * RW_W:(d + 1) * RW_W]
        a_o[d] = a[:, d * RW_W:(d + 1) * RW_W]


def _rw_prepare(proj, p):
    n = proj.shape[0]
    nh = n // HALO
    steps = RW_TM // HALO
    wide = P_DA - P_RW
    cat2 = lambda t: t.reshape(1, 2 * RW_W)
    blockdiag = lambda t: jnp.concatenate(
        [jnp.concatenate([t[0], jnp.zeros_like(t[0])], axis=1),
         jnp.concatenate([jnp.zeros_like(t[1]), t[1]], axis=1)], axis=0)
    const = lambda shape: pl.BlockSpec(shape, lambda i: (0,) * len(shape))
    row = pl.BlockSpec((RW_TM, RW_W), lambda i: (i, 0))
    row2 = pl.BlockSpec((2, RW_TM, RW_W), lambda i: (0, i, 0))
    o1 = jax.ShapeDtypeStruct((n, RW_W), F32)
    o2 = jax.ShapeDtypeStruct((2, n, RW_W), F32)
    return pl.pallas_call(
        _rwprep_kernel,
        grid=(n // RW_TM,),
        in_specs=[
            pl.BlockSpec((RW_TM, wide), lambda i: (i, 0)),
            pl.BlockSpec((HALO, wide), lambda i: (jnp.maximum(i * steps - 1, 0), 0)),
            pl.BlockSpec((HALO, wide), lambda i: (jnp.minimum((i + 1) * steps, nh - 1), 0)),
            const((2, RW_COLS)), const((1, 2 * RW_W)), const((128, 2 * RW_W)), const((1, 2 * RW_W)),
            const((128, 2 * RW_W)), const((RW_G_LORA, RW_W)), const((1, RW_W)), const((1, 2 * RW_W)),
            const((2 * RW_HD, 2 * RW_HD)),
        ],
        out_specs=[row, row, row, row, row2, row2, row2],
        out_shape=[o1, o1, o1, o1, o2, o2, o2],
        scratch_shapes=[pltpu.VMEM((RW_TM + 2 * HALO, RW_COLS), F32)],
        compiler_params=_cparams(("parallel",)),
        name="rwkv_prepare",
    )(proj, proj, proj, p['rw_shift'], cat2(p['rw_w0']), blockdiag(p['rw_w_up']), cat2(p['rw_a0']),
      blockdiag(p['rw_a_up']), p['rw_g_up'], p['rw_k_k'].reshape(1, RW_W),
      jnp.tile(p['rw_k_a'].reshape(1, RW_W), (1, 2)), _head_ones())


RW_TB = 256
RW_NC = RW_TB // RW_C
RW_SIDE = 16


def _rw_finish(y, r, v, g, kd_sum, rk, lnw, lnb, ones):
    inv = 1.0 / RW_HD
    xc = y - _head_sum(y, ones) * inv
    var = _head_sum(xc * xc, ones) * inv
    yn = xc * lax.rsqrt(var + RW_GN_EPS) * lnw + lnb
    bonus = _head_sum(r * kd_sum * rk, ones) * v
    return ((yn + bonus) * g).astype(BF16)


def _rwscan_kernel(r_ref, v_ref, kk_ref, kd_ref, lw_ref, a_ref, tri_ref, inc_ref, str_ref, s0_ref, *rest, backward):
    if backward:
        yf_ref, g_ref, rk_ref, lnw_ref, lnb_ref, ones_ref, y_ref, st_ref, s_scr = rest
    else:
        y_ref, st_ref, s_scr = rest
    _rwscan_body(r_ref, v_ref, kk_ref, kd_ref, lw_ref, a_ref, tri_ref, inc_ref, str_ref, s0_ref, y_ref, st_ref,
                 s_scr, backward,
                 (lambda yb: _rw_finish(yf_ref[...] + yb, r_ref[...], v_ref[...], g_ref[...],
                                        kd_ref[0] + kd_ref[1], rk_ref[...], lnw_ref[...], lnb_ref[...],
                                        ones_ref[...])) if backward else None)


def _rwscan_body(r_ref, v_ref, kk_ref, kd_ref, lw_ref, a_ref, tri_ref, inc_ref, str_ref, s0_ref, y_ref, st_ref,
                 s_scr, backward, finish):
    step = pl.program_id(0)
    nb = pl.num_programs(0)
    bi = (nb - 1 - step) if backward else step
    pos, length = _seq_pos(bi * RW_TB)
    first = (pos + RW_TB == length) if backward else (pos == 0)
    last = (pos == 0) if backward else (pos + RW_TB == length)
    in_ctx = bi * RW_TB < N_CTX
    c = RW_C

    @pl.when(jnp.logical_and(first, in_ctx))
    def _():
        s_scr[...] = jnp.zeros_like(s_scr)

    @pl.when(jnp.logical_and(first, jnp.logical_not(in_ctx)))
    def _():
        z = jnp.zeros((c, c), F32)
        for p in range(RW_PAIRS):
            s_scr[p] = jnp.concatenate(
                [jnp.concatenate([s0_ref[0, 0, 2 * p], z], axis=1),
                 jnp.concatenate([z, s0_ref[0, 0, 2 * p + 1]], axis=1)], axis=0)

    incl2 = inc_ref[...] > 0.5
    strict2 = str_ref[...] > 0.5
    eye2 = (lax.broadcasted_iota(jnp.int32, (2 * c, 2 * c), 0)
            == lax.broadcasted_iota(jnp.int32, (2 * c, 2 * c), 1))
    m_e = lax.broadcasted_iota(jnp.int32, (1, 2 * c), 1) < c

    lw = lw_ref[0]
    cum = _dotx(tri_ref[...], lw)
    tots = [cum[(j * c if backward else j * c + c - 1):(j * c + 1 if backward else j * c + c), :]
            for j in range(RW_NC)]
    tot_b = jnp.concatenate([jnp.broadcast_to(t, (c, RW_W)) for t in tots], axis=0)
    kk = kk_ref[...]
    kd = kd_ref[1 if backward else 0]
    bp = kk * a_ref[0]
    g_inv = jnp.exp(-cum)
    g_rem = jnp.exp(tot_b - cum)
    ag = -kk * jnp.exp(cum - lw)
    rg = r_ref[...] * jnp.exp(cum)
    bdn = bp * g_inv
    kdn = kd * g_inv
    bc = bp * g_rem
    kc = kd * g_rem
    v = v_ref[...]

    def stack(x, j, p):
        xs = x[j * c:(j + 1) * c, p * 2 * c:(p + 1) * 2 * c]
        return jnp.concatenate([jnp.where(m_e, xs, 0.0), jnp.where(m_e, 0.0, xs)], axis=0).astype(BF16)

    pre = {}
    keys = [(j, p) for j in range(RW_NC) for p in range(RW_PAIRS)]
    for g0 in range(0, len(keys), RW_SIDE):
        grp = keys[g0:g0 + RW_SIDE]
        ops = {k: tuple(stack(t, *k) for t in (ag, rg, bdn, kdn, bc, kc, v)) for k in grp}
        gm = {k: _dg(jnp.concatenate([ops[k][0], ops[k][1]], axis=0),
                     jnp.concatenate([ops[k][2], ops[k][3]], axis=0), NT) for k in grp}
        lbb = {k: jnp.where(strict2, gm[k][:2 * c, :2 * c], 0.0) for k in grp}
        lkb = {k: jnp.where(strict2, gm[k][:2 * c, 2 * c:], 0.0).astype(BF16) for k in grp}
        lrk = {k: jnp.concatenate([jnp.where(incl2, gm[k][2 * c:, :2 * c], 0.0),
                                   jnp.where(incl2, gm[k][2 * c:, 2 * c:], 0.0)], axis=1).astype(BF16)
               for k in grp}
        lv = {k: _dg(lkb[k], ops[k][6], NN) for k in grp}
        x = {k: jnp.where(eye2, 1.0, lbb[k]) for k in grp}
        pw = lbb
        for _ in range(int(math.log2(c)) - 1):
            pwb = {k: pw[k].astype(BF16) for k in grp}
            pw = {k: _dg(pwb[k], pwb[k], NN) for k in grp}
            x = {k: x[k] + _dg(x[k].astype(BF16), pw[k].astype(BF16), NN) for k in grp}
        tw = {k: _dg(x[k].astype(BF16), jnp.concatenate([lv[k].astype(BF16), ops[k][0]], axis=1), NN)
              for k in grp}
        for k in grp:
            pre[k] = (tw[k][:, :2 * c], tw[k][:, 2 * c:].astype(BF16), ops[k][1], lrk[k], ops[k][6],
                      jnp.concatenate([ops[k][4], ops[k][5]], axis=0))

    order = range(RW_NC - 1, -1, -1) if backward else range(RW_NC)
    pairs = range(RW_PAIRS)
    s = [s_scr[p] for p in pairs]
    ys = {}
    for j in order:
        sb = [s[p].astype(BF16) for p in pairs]
        u = [_dg(pre[j, p][1], sb[p], NT) + pre[j, p][0] for p in pairs]
        uv = [jnp.concatenate([u[p].astype(BF16), pre[j, p][4]], axis=0) for p in pairs]
        y = [_dg(pre[j, p][2], sb[p], NT) + _dg(pre[j, p][3], uv[p], NN) for p in pairs]
        s = [s[p] * jnp.exp(tots[j][:, p * 2 * c:(p + 1) * 2 * c]) + _dg(uv[p], pre[j, p][5], TN) for p in pairs]
        ys[j] = [y[p][:c] + y[p][c:] for p in pairs]
    y_blk = jnp.concatenate([jnp.concatenate(ys[j], axis=1) for j in range(RW_NC)], axis=0)
    y_ref[...] = finish(y_blk) if finish else y_blk
    for p in pairs:
        s_scr[p] = s[p]

    @pl.when(last)
    def _():
        for p in pairs:
            st_ref[0, 2 * p] = s[p][:c, :c]
            st_ref[0, 2 * p + 1] = s[p][c:, c:]


def _scan_masks(backward):
    t = jnp.arange(RW_TB)
    sgn = -1 if backward else 1
    same_chunk = (t[:, None] // RW_C) == (t[None, :] // RW_C)
    tri = (same_chunk & ((t[:, None] - t[None, :]) * sgn >= 0)).astype(BF16)
    q = jnp.arange(2 * RW_C)
    same_head = (q[:, None] // RW_C) == (q[None, :] // RW_C)
    dif = ((q[:, None] % RW_C) - (q[None, :] % RW_C)) * sgn
    return tri, (same_head & (dif >= 0)).astype(F32), (same_head & (dif > 0)).astype(F32)


def _rw_scan(r, v, kk, kd, lw, a, s0, backward, finish=()):
    n = r.shape[0]
    nb = n // RW_TB
    d = 1 if backward else 0
    blk = (lambda i: nb - 1 - i) if backward else (lambda i: i)
    row = pl.BlockSpec((RW_TB, RW_W), lambda i: (blk(i), 0))
    row2 = pl.BlockSpec((1, RW_TB, RW_W), lambda i: (d, blk(i), 0))
    both = pl.BlockSpec((2, RW_TB, RW_W), lambda i: (0, blk(i), 0))
    const = lambda shape: pl.BlockSpec(shape, lambda i: (0,) * len(shape))
    vec = const((1, RW_W))
    pair = 2 * RW_C
    extra = [row, row, vec, vec, vec, const((pair, pair))] if backward else []
    return pl.pallas_call(
        functools.partial(_rwscan_kernel, backward=backward),
        grid=(nb,),
        in_specs=[row, row, row, both, row2, row2, const((RW_TB, RW_TB)), const((pair, pair)), const((pair, pair)),
                  pl.BlockSpec((1, 1, RW_HEADS, RW_HD, RW_HD),
                               lambda i: (jnp.maximum(_seq_index(blk(i) * RW_TB) - BATCH, 0), d, 0, 0, 0))] + extra,
        out_specs=[row, pl.BlockSpec((1, RW_HEADS, RW_HD, RW_HD), lambda i: (_seq_index(blk(i) * RW_TB), 0, 0, 0))],
        out_shape=[jax.ShapeDtypeStruct((n, RW_W), BF16 if backward else F32),
                   jax.ShapeDtypeStruct((N_SEQS, RW_HEADS, RW_HD, RW_HD), F32)],
        scratch_shapes=[pltpu.VMEM((RW_PAIRS, pair, pair), F32)],
        compiler_params=_cparams(("arbitrary",)),
        name="rwkv_scan_bwd" if backward else "rwkv_scan_fwd",
    )(r, v, kk, kd, lw, a, *_scan_masks(backward), s0, *finish)


def _rwkv_branch(proj, state0, p):
    r, v, kk, g, kd, lw, a = _rw_prepare(proj, p)
    yf, sf = _rw_scan(r, v, kk, kd, lw, a, state0, backward=False)
    finish = (yf, g, p['rw_r_k'].reshape(1, RW_W), p['rw_ln_w'].reshape(1, RW_W), p['rw_ln_b'].reshape(1, RW_W),
              _head_ones())
    out, sb = _rw_scan(r, v, kk, kd, lw, a, state0, backward=True, finish=finish)
    return out, jnp.stack([sf, sb], axis=1)


RET_W = RET_HEADS * RET_DV
RET_PAIRS = RET_HEADS // 2


RET_TB = 256
RET_NC = RET_TB // RET_CHUNK


def _ret_kernel(q_ref, k_ref, v_ref, lg_ref, s0_ref, o_ref, st_ref, s_scr, *, backward):
    step = pl.program_id(0)
    nb = pl.num_programs(0)
    bi = (nb - 1 - step) if backward else step
    pos, length = _seq_pos(bi * RET_TB)
    first = (pos + RET_TB == length) if backward else (pos == 0)
    in_ctx = bi * RET_TB < N_CTX

    @pl.when(jnp.logical_and(first, in_ctx))
    def _():
        s_scr[...] = jnp.zeros_like(s_scr)

    @pl.when(jnp.logical_and(first, jnp.logical_not(in_ctx)))
    def _():
        s_scr[...] = s0_ref[0, 0]

    c = RET_CHUNK
    d = 1 if backward else 0
    sgn = -1 if backward else 1
    lgs = -_softplus(-lg_ref[d:d + 1, :])
    ri = lax.broadcasted_iota(jnp.int32, (c, c), 0)
    cj = lax.broadcasted_iota(jnp.int32, (c, c), 1)
    dif = (ri - cj) * sgn
    valid = dif >= 0
    dist = jnp.maximum(dif, 0).astype(F32)
    pr = lax.broadcasted_iota(jnp.int32, (c, 2 * RET_DK), 0)
    tau = ((c - 1 - pr) if backward else pr).astype(F32)
    low = lax.broadcasted_iota(jnp.int32, (1, 2 * RET_DK), 1) < RET_DK
    rlow = lax.broadcasted_iota(jnp.int32, (2 * RET_DK, RET_DV), 0) < RET_DK
    heads = range(RET_HEADS)
    pairs = range(RET_PAIRS)
    mask = [low, jnp.logical_not(low)]
    lg_h = [lgs[:, h:h + 1] for h in heads]
    lg_row = [jnp.where(low, lg_h[2 * p], lg_h[2 * p + 1]) for p in pairs]
    dmat = [jnp.where(valid, jnp.exp(lg_h[h] * dist), 0.0) for h in heads]
    q_dec = [jnp.exp(lg_row[p] * (tau + 1.0)) for p in pairs]
    k_dec = [jnp.exp(lg_row[p] * (c - 1.0 - tau)) for p in pairs]
    c_dec = [jnp.where(rlow, jnp.exp(lg_h[2 * p] * c), jnp.exp(lg_h[2 * p + 1] * c)) for p in pairs]

    pre = []
    for j in range(RET_NC):
        rows = slice(j * c, (j + 1) * c)
        qp = [q_ref[rows, p * 128:(p + 1) * 128] for p in pairs]
        kp = [k_ref[rows, p * 128:(p + 1) * 128] * (RET_DK ** -0.5) for p in pairs]
        kpb = [t.astype(BF16) for t in kp]
        vb = [v_ref[rows, h * RET_DV:(h + 1) * RET_DV].astype(BF16) for h in heads]
        att = [_dg(jnp.where(mask[h % 2], qp[h // 2], 0.0).astype(BF16), kpb[h // 2], NT) * dmat[h]
               for h in heads]
        upd = [_dg(jnp.where(mask[h % 2], kp[h // 2] * k_dec[h // 2], 0.0).astype(BF16), vb[h], TN)
               for h in heads]
        inner = [_dg(att[h].astype(BF16), vb[h], NN) for h in heads]
        qd = [jnp.where(mask[h % 2], qp[h // 2] * q_dec[h // 2], 0.0).astype(BF16) for h in heads]
        pre.append((inner, qd, upd))

    s = [s_scr[p] for p in pairs]
    for j in (range(RET_NC - 1, -1, -1) if backward else range(RET_NC)):
        inner, qd, upd = pre[j]
        sb = [t.astype(BF16) for t in s]
        for h in heads:
            o_ref[j * c:(j + 1) * c, h * RET_DV:(h + 1) * RET_DV] = inner[h] + _dg(qd[h], sb[h // 2], NN)
        s = [s[p] * c_dec[p] + upd[2 * p] + upd[2 * p + 1] for p in pairs]
    for p in pairs:
        s_scr[p] = s[p]
        st_ref[0, p] = s[p]


def _ret_scan(proj, logit, s0, backward):
    n = proj.shape[0]
    nb = n // RET_TB
    d = 1 if backward else 0
    blk = (lambda i: nb - 1 - i) if backward else (lambda i: i)
    qk_w = RET_HEADS * RET_DK
    state = (RET_PAIRS, 2 * RET_DK, RET_DV)
    return pl.pallas_call(
        functools.partial(_ret_kernel, backward=backward),
        grid=(nb,),
        in_specs=[
            pl.BlockSpec((RET_TB, qk_w), lambda i: (blk(i), B_RET // qk_w)),
            pl.BlockSpec((RET_TB, qk_w), lambda i: (blk(i), B_RET // qk_w + 1)),
            pl.BlockSpec((RET_TB, RET_W), lambda i: (blk(i), (B_RET + 2 * qk_w) // RET_W)),
            pl.BlockSpec((2, RET_HEADS), lambda i: (0, 0)),
            pl.BlockSpec((1, 1) + state,
                         lambda i: (jnp.maximum(_seq_index(blk(i) * RET_TB) - BATCH, 0), d, 0, 0, 0)),
        ],
        out_specs=[pl.BlockSpec((RET_TB, RET_W), lambda i: (blk(i), 0)),
                   pl.BlockSpec((1,) + state, lambda i: (_seq_index(blk(i) * RET_TB), 0, 0, 0))],
        out_shape=[jax.ShapeDtypeStruct((n, RET_W), F32), jax.ShapeDtypeStruct((N_SEQS,) + state, F32)],
        scratch_shapes=[pltpu.VMEM(state, F32)],
        compiler_params=_cparams(("arbitrary",)),
        name="retention_scan_bwd" if backward else "retention_scan_fwd",
    )(proj, proj, proj, logit, s0)


def _standardize(x, eps):
    mu = jnp.mean(x, axis=-1, keepdims=True)
    xc = x - mu
    return xc * lax.rsqrt(jnp.mean(xc * xc, axis=-1, keepdims=True) + eps)


def _retpost_kernel(of_ref, ob_ref, g_ref, nw_ref, out_ref):
    o = of_ref[...] + ob_ref[...]
    g = g_ref[...].astype(F32)
    for h in range(RET_HEADS):
        sl = slice(h * RET_DV, (h + 1) * RET_DV)
        gh = g[:, sl]
        out_ref[:, sl] = (gh * _sigmoid(gh) * (_standardize(o[:, sl], EPS) * nw_ref[:, sl])).astype(BF16)


def _ret_post(of, ob, proj, norm_w):
    n = proj.shape[0]
    tm = 512
    return pl.pallas_call(
        _retpost_kernel,
        grid=(n // tm,),
        in_specs=[
            pl.BlockSpec((tm, RET_W), lambda i: (i, 0)),
            pl.BlockSpec((tm, RET_W), lambda i: (i, 0)),
            pl.BlockSpec((tm, RET_W), lambda i: (i, (B_RET + 1024) // RET_W)),
            pl.BlockSpec((1, RET_W), lambda i: (0, 0)),
        ],
        out_specs=pl.BlockSpec((tm, RET_W), lambda i: (i, 0)),
        out_shape=jax.ShapeDtypeStruct((n, RET_W), BF16),
        compiler_params=_cparams(("parallel",)),
        name="retention_post",
    )(of, ob, proj, norm_w.reshape(1, RET_W))


def _retention_branch(proj, state0, p):
    s0 = state0.reshape(DEC_BATCH, 2, RET_PAIRS, 2 * RET_DK, RET_DV)
    of, sf = _ret_scan(proj, p['ret_decay_logit'], s0, backward=False)
    ob, sb = _ret_scan(proj, p['ret_decay_logit'], s0, backward=True)
    st = jnp.stack([sf, sb], axis=1).reshape(N_SEQS, 2, RET_HEADS, RET_DK, RET_DV)
    return _ret_post(of, ob, proj, p['ret_norm_w']), st


CV_TM = 256
CV_HALO = 16


def _conv_kernel(a_ref, g_ref, ap_ref, gp_ref, an_ref, gn_ref, w_ref, b_ref, lnw_ref, lnb_ref, o_ref, buf, sbuf):
    row0 = pl.program_id(0) * CV_TM
    pos0, len0 = _seq_pos(row0)
    glu = lambda a, g: a[...].astype(F32) * _sigmoid(g[...].astype(F32))
    buf[CV_HALO:CV_HALO + CV_TM, :] = glu(a_ref, g_ref)
    buf[0:CV_HALO, :] = jnp.where(pos0 == 0, 0.0, glu(ap_ref, gp_ref))
    buf[CV_HALO + CV_TM:, :] = jnp.where(pos0 + CV_TM == len0, 0.0, glu(an_ref, gn_ref))
    base = CV_HALO - CONV_K // 2
    acc = jnp.zeros((CV_TM, CV_W), F32)
    for r in range(8):
        taps = [m for m in range((base + CONV_K + 7) // 8) if 0 <= r + 8 * m - base < CONV_K]
        span = CV_TM + 8 * max(taps)
        if r:
            sbuf[0:span, :] = buf[r:r + span, :]
        src = sbuf if r else buf
        for m in taps:
            j = r + 8 * m - base
            acc = acc + w_ref[j:j + 1, :] * src[8 * m:8 * m + CV_TM, :]
    z = _standardize(acc + b_ref[...], EPS) * lnw_ref[...] + lnb_ref[...]
    o_ref[...] = (z * _sigmoid(z)).astype(BF16)


def _conv_branch(proj, p):
    n = proj.shape[0]
    nh = n // CV_HALO
    steps = CV_TM // CV_HALO
    ca = B_CV // CV_W
    prev = lambda i: jnp.maximum(i * steps - 1, 0)
    nxt = lambda i: jnp.minimum((i + 1) * steps, nh - 1)
    vec = pl.BlockSpec((1, CV_W), lambda i: (0, 0))
    return pl.pallas_call(
        _conv_kernel,
        grid=(n // CV_TM,),
        in_specs=[
            pl.BlockSpec((CV_TM, CV_W), lambda i: (i, ca)),
            pl.BlockSpec((CV_TM, CV_W), lambda i: (i, ca + 1)),
            pl.BlockSpec((CV_HALO, CV_W), lambda i: (prev(i), ca)),
            pl.BlockSpec((CV_HALO, CV_W), lambda i: (prev(i), ca + 1)),
            pl.BlockSpec((CV_HALO, CV_W), lambda i: (nxt(i), ca)),
            pl.BlockSpec((CV_HALO, CV_W), lambda i: (nxt(i), ca + 1)),
            pl.BlockSpec((CONV_K, CV_W), lambda i: (0, 0)),
            vec, vec, vec,
        ],
        out_specs=pl.BlockSpec((CV_TM, CV_W), lambda i: (i, 0)),
        out_shape=jax.ShapeDtypeStruct((n, CV_W), BF16),
        scratch_shapes=[pltpu.VMEM((CV_TM + 2 * CV_HALO, CV_W), F32)] * 2,
        compiler_params=_cparams(("parallel",)),
        name="conformer_conv",
    )(proj, proj, proj, proj, proj, proj, p['cv_dw_w'], p['cv_dw_b'].reshape(1, CV_W),
      p['cv_ln_w'].reshape(1, CV_W), p['cv_ln_b'].reshape(1, CV_W))


MG_TM = 512


def _merge_kernel(da_ref, rw_ref, ret_ref, cv_ref, g0_ref, g1_ref, g2_ref, g3_ref, x_ref, mod_ref, wb_ref, wo_ref,
                  nw_ref, x_o, h_o):
    m = None
    for n, (br, gt) in enumerate(((da_ref, g0_ref), (rw_ref, g1_ref), (ret_ref, g2_ref), (cv_ref, g3_ref))):
        t = _sigmoid(gt[...].astype(F32)) * _dg(br[...], wb_ref[n], NN)
        m = t if m is None else m + t
    out = _dg(m.astype(BF16), wo_ref[...], NN)
    mrow = _mod_row(mod_ref, pl.program_id(0) * MG_TM)
    gate1 = mrow[:, 2 * D_MODEL:3 * D_MODEL]
    sh2 = mrow[:, 3 * D_MODEL:4 * D_MODEL]
    sc2 = mrow[:, 4 * D_MODEL:5 * D_MODEL]
    x1 = x_ref[...] + gate1 * out
    x_o[...] = x1
    h_o[...] = _rms(x1, nw_ref[...]) * (1.0 + sc2) + sh2


def _merge(branches, gates, x, mod, w_branch, w_out, norm_w):
    n = x.shape[0]
    br = pl.BlockSpec((MG_TM, BR_W), lambda i: (i, 0))
    gspec = lambda j: pl.BlockSpec((MG_TM, D_MODEL), lambda i: (i, B_GATE // D_MODEL + j))
    full = pl.BlockSpec((MG_TM, D_MODEL), lambda i: (i, 0))
    out = jax.ShapeDtypeStruct((n, D_MODEL), F32)
    return pl.pallas_call(
        _merge_kernel,
        grid=(n // MG_TM,),
        in_specs=[br, br, br, br, gspec(0), gspec(1), gspec(2), gspec(3), full,
                  pl.BlockSpec((MOD_ROWS, 6 * D_MODEL), lambda i: (0, 0)),
                  pl.BlockSpec((N_BRANCH, BR_W, D_MODEL), lambda i: (0, 0, 0)),
                  pl.BlockSpec((D_MODEL, D_MODEL), lambda i: (0, 0)),
                  pl.BlockSpec((1, D_MODEL), lambda i: (0, 0))],
        out_specs=[full, full],
        out_shape=[out, out],
        compiler_params=_cparams(("parallel",)),
        name="gated_merge",
    )(*branches, gates, gates, gates, gates, x, mod, w_branch.astype(BF16), w_out.astype(BF16),
      norm_w.reshape(1, D_MODEL))


RT_TM = 256
MOE_BM = 512
MOE_ROWS = N_TOK * TOP_K + N_EXPERTS * MOE_BM
DP_TM = 512
DP_GROUP = 64
CB_TM = 256
DMA_UNROLL = 8


def _router_kernel(h_ref, w_ref, b_ref, tri_ref, idx_o, gate_o, rank_o, cnt_o, carry):
    @pl.when(pl.program_id(0) == 0)
    def _():
        carry[...] = jnp.zeros_like(carry)

    logits = _dot3(w_ref[...], h_ref[...], NT) + b_ref[...]
    e_iota = lax.broadcasted_iota(jnp.int32, logits.shape, 0)
    work = logits
    vals, idxs, hots = [], [], []
    for _ in range(TOP_K):
        mx = jnp.max(work, axis=0, keepdims=True)
        ix = jnp.min(jnp.where(work == mx, e_iota, N_EXPERTS), axis=0, keepdims=True)
        hot = e_iota == ix
        vals.append(mx)
        idxs.append(ix)
        hots.append(hot.astype(F32))
        work = jnp.where(hot, -jnp.inf, work)
    es = [jnp.exp(v - vals[0]) for v in vals]
    inv = 1.0 / (es[0] + es[1] + es[2] + es[3])
    chosen = hots[0] + hots[1] + hots[2] + hots[3]
    ahead = carry[...][:, 0:1] + _dg(chosen.astype(BF16), tri_ref[...], NN)
    idx_o[...] = jnp.concatenate(idxs, axis=0)
    gate_o[...] = jnp.concatenate([e * inv for e in es], axis=0)
    rank_o[...] = jnp.concatenate(
        [jnp.sum(hot * ahead, axis=0, keepdims=True) for hot in hots], axis=0).astype(jnp.int32)
    carry[...] = carry[...] + jnp.sum(chosen, axis=1, keepdims=True)
    cnt_o[...] = carry[...]


def _router(h, router_w, router_b):
    n = h.shape[0]
    tri = (jnp.arange(RT_TM)[:, None] < jnp.arange(RT_TM)[None, :]).astype(BF16)
    col = pl.BlockSpec((TOP_K, RT_TM), lambda i: (0, i))
    return pl.pallas_call(
        _router_kernel,
        grid=(n // RT_TM,),
        in_specs=[
            pl.BlockSpec((RT_TM, D_MODEL), lambda i: (i, 0)),
            pl.BlockSpec((N_EXPERTS, D_MODEL), lambda i: (0, 0)),
            pl.BlockSpec((N_EXPERTS, 1), lambda i: (0, 0)),
            pl.BlockSpec((RT_TM, RT_TM), lambda i: (0, 0)),
        ],
        out_specs=[col, col, col, pl.BlockSpec((N_EXPERTS, 128), lambda i: (0, 0))],
        out_shape=[jax.ShapeDtypeStruct((TOP_K, n), jnp.int32), jax.ShapeDtypeStruct((TOP_K, n), F32),
                   jax.ShapeDtypeStruct((TOP_K, n), jnp.int32), jax.ShapeDtypeStruct((N_EXPERTS, 128), F32)],
        scratch_shapes=[pltpu.VMEM((N_EXPERTS, 128), F32)],
        compiler_params=_cparams(("arbitrary",)),
        name="router",
    )(h, router_w.T, router_b.reshape(N_EXPERTS, 1), tri)


def _tile_major(t, tm):
    k, n = t.shape
    return t.reshape(k, n // tm, tm).transpose(1, 0, 2).reshape(n // tm, 1, k * tm)


def _dispatch_kernel(dest_ref, pe_ref, h_ref, o_hbm, zbuf, sem, zsem):
    n_groups = DP_TM // DP_GROUP

    @pl.when(pl.program_id(0) == 0)
    def _():
        zbuf[...] = jnp.zeros_like(zbuf)

        def fill(e):
            end = pe_ref[e]
            begin = pe_ref[e - 1] if e else 0
            return end > begin, pltpu.make_async_copy(
                zbuf, o_hbm.at[pl.ds(pl.multiple_of(jnp.maximum(end - MOE_BM, 0), MOE_BM), MOE_BM)], zsem.at[0])

        for e in range(N_EXPERTS):
            nonempty, cp = fill(e)
            pl.when(nonempty)(cp.start)
        for e in range(N_EXPERTS):
            nonempty, cp = fill(e)
            pl.when(nonempty)(cp.wait)

        def tail(b):
            return pltpu.make_async_copy(zbuf, o_hbm.at[pl.ds(pl.multiple_of(b * MOE_BM, MOE_BM), MOE_BM)],
                                         zsem.at[0])

        first_unused = pe_ref[N_EXPERTS - 1] // MOE_BM
        lax.fori_loop(first_unused, MOE_ROWS // MOE_BM, lambda b, c: (tail(b).start(), c)[1], 0)
        lax.fori_loop(first_unused, MOE_ROWS // MOE_BM, lambda b, c: (tail(b).wait(), c)[1], 0)

    def wait_group(slot):
        pltpu.make_async_copy(h_ref.at[pl.ds(0, TOP_K * DP_GROUP)], o_hbm.at[pl.ds(0, TOP_K * DP_GROUP)],
                              sem.at[slot]).wait()

    def group(gi, carry):
        slot = gi % 2

        def issue(t, c):
            tok = gi * DP_GROUP + t
            for k in range(TOP_K):
                dst = dest_ref[0, 0, k * DP_TM + tok]
                pltpu.make_async_copy(h_ref.at[pl.ds(tok, 1)], o_hbm.at[pl.ds(dst, 1)], sem.at[slot]).start()
            return c

        lax.fori_loop(0, DP_GROUP, issue, 0, unroll=DMA_UNROLL)

        @pl.when(gi > 0)
        def _():
            wait_group(1 - slot)

        return carry

    lax.fori_loop(0, n_groups, group, 0)
    wait_group((n_groups - 1) % 2)


def _dispatch(h, dest, pad_end):
    n = h.shape[0]
    return pl.pallas_call(
        _dispatch_kernel,
        grid=(n // DP_TM,),
        in_specs=[
            pl.BlockSpec((1, 1, TOP_K * DP_TM), lambda i: (i, 0, 0), memory_space=pltpu.SMEM),
            pl.BlockSpec(memory_space=pltpu.SMEM),
            pl.BlockSpec((DP_TM, D_MODEL), lambda i: (i, 0)),
        ],
        out_specs=pl.BlockSpec(memory_space=pl.ANY),
        out_shape=jax.ShapeDtypeStruct((MOE_ROWS, D_MODEL), F32),
        scratch_shapes=[pltpu.VMEM((MOE_BM, D_MODEL), F32), pltpu.SemaphoreType.DMA((2,)),
                        pltpu.SemaphoreType.DMA((1,))],
        compiler_params=_cparams(("arbitrary",)),
        name="moe_dispatch",
    )(_tile_major(dest, DP_TM), pad_end, h)


def _expert_kernel(bx_ref, be_ref, nv_ref, x_ref, w1_ref, b1_ref, w2_ref, b2_ref, o_ref, w1b, w2b):
    i = pl.program_id(0)
    changed = jnp.logical_or(i == 0, be_ref[i] != be_ref[jnp.maximum(i - 1, 0)])

    @pl.when(changed)
    def _():
        w1b[...] = w1_ref[0, 0].astype(BF16)
        w2b[...] = w2_ref[0, 0].astype(BF16)

    def ffn(rows):
        hb = _dg(x_ref[0:rows, :].astype(BF16), w1b[...], NN) + b1_ref[0, 0]
        hg = jnp.minimum(hb[:, :D_FF], SWIGLU_LIMIT)
        hu = jnp.clip(hb[:, D_FF:], -SWIGLU_LIMIT, SWIGLU_LIMIT)
        act = hg * _sigmoid(SWIGLU_ALPHA * hg) * (hu + 1.0)
        o_ref[0:rows, :] = _dg(act.astype(BF16), w2b[...], NN) + b2_ref[0, 0]

    nv = nv_ref[i]
    half = MOE_BM // 2

    @pl.when(nv > half)
    def _():
        ffn(MOE_BM)

    @pl.when(jnp.logical_and(nv > 0, nv <= half))
    def _():
        ffn(half)
        o_ref[half:, :] = jnp.zeros((MOE_BM - half, D_MODEL), F32)

    @pl.when(nv == 0)
    def _():
        o_ref[...] = jnp.zeros_like(o_ref)


def _experts(x_rows, blk_x, blk_e, n_valid, layer, w1, b1, w2, b2):
    nb = MOE_ROWS // MOE_BM
    grid_spec = pltpu.PrefetchScalarGridSpec(
        num_scalar_prefetch=3,
        grid=(nb,),
        in_specs=[
            pl.BlockSpec((MOE_BM, D_MODEL), lambda i, bx, be, nv: (bx[i], 0)),
            pl.BlockSpec((1, 1, D_MODEL, 2 * D_FF), lambda i, bx, be, nv: (layer, be[i], 0, 0)),
            pl.BlockSpec((1, 1, 1, 2 * D_FF), lambda i, bx, be, nv: (layer, be[i], 0, 0)),
            pl.BlockSpec((1, 1, D_FF, D_MODEL), lambda i, bx, be, nv: (layer, be[i], 0, 0)),
            pl.BlockSpec((1, 1, 1, D_MODEL), lambda i, bx, be, nv: (layer, be[i], 0, 0)),
        ],
        out_specs=pl.BlockSpec((MOE_BM, D_MODEL), lambda i, bx, be, nv: (i, 0)),
        scratch_shapes=[pltpu.VMEM((D_MODEL, 2 * D_FF), BF16), pltpu.VMEM((D_FF, D_MODEL), BF16)],
    )
    return pl.pallas_call(
        _expert_kernel,
        grid_spec=grid_spec,
        out_shape=jax.ShapeDtypeStruct((MOE_ROWS, D_MODEL), F32),
        compiler_params=pltpu.CompilerParams(dimension_semantics=("arbitrary",),
                                             vmem_limit_bytes=56 * 1024 * 1024),
        name="moe_experts",
    )(blk_x, blk_e, n_valid, x_rows, w1, b1.reshape(DEPTH, N_EXPERTS, 1, 2 * D_FF), w2,
      b2.reshape(DEPTH, N_EXPERTS, 1, D_MODEL))


def _combine_kernel(dest_ref, gate_ref, x_ref, mod_ref, fw_ref, y_hbm, *rest, final):
    if final:
        o_ctx_ref, o_lat_ref, buf, sem = rest
    else:
        o_ref, buf, sem = rest

    def issue(t, c):
        for k in range(TOP_K):
            dst = dest_ref[0, 0, k * CB_TM + t]
            pltpu.make_async_copy(y_hbm.at[pl.ds(dst, 1)], buf.at[k, pl.ds(t, 1)], sem.at[0]).start()
        return c

    lax.fori_loop(0, CB_TM, issue, 0, unroll=DMA_UNROLL)
    for k in range(TOP_K):
        pltpu.make_async_copy(y_hbm.at[pl.ds(0, CB_TM)], buf.at[k], sem.at[0]).wait()
    g = gate_ref[...]
    acc = g[:, 0:1] * buf[0]
    for k in range(1, TOP_K):
        acc = acc + g[:, k:k + 1] * buf[k]
    gate2 = _mod_row(mod_ref, pl.program_id(0) * CB_TM)[:, 5 * D_MODEL:6 * D_MODEL]
    x2 = x_ref[...] + gate2 * acc
    if not final:
        o_ref[...] = x2
        return
    y = _rms(x2, fw_ref[...])
    in_ctx = pl.program_id(0) < N_CTX // CB_TM

    @pl.when(in_ctx)
    def _():
        o_ctx_ref[...] = y

    @pl.when(jnp.logical_not(in_ctx))
    def _():
        o_lat_ref[...] = y


def _combine(y_rows, dest, gates, x, mod, final_w, final):
    n = x.shape[0]
    full = pl.BlockSpec((CB_TM, D_MODEL), lambda i: (i, 0))
    nct = N_CTX // CB_TM
    if final:
        out_specs = [pl.BlockSpec((CB_TM, D_MODEL), lambda i: (jnp.minimum(i, nct - 1), 0)),
                     pl.BlockSpec((CB_TM, D_MODEL), lambda i: (jnp.maximum(i - nct, 0), 0))]
        out_shape = [jax.ShapeDtypeStruct((N_CTX, D_MODEL), F32), jax.ShapeDtypeStruct((N_LAT, D_MODEL), F32)]
    else:
        out_specs, out_shape = full, jax.ShapeDtypeStruct((n, D_MODEL), F32)
    return pl.pallas_call(
        functools.partial(_combine_kernel, final=final),
        grid=(n // CB_TM,),
        in_specs=[
            pl.BlockSpec((1, 1, TOP_K * CB_TM), lambda i: (i, 0, 0), memory_space=pltpu.SMEM),
            pl.BlockSpec((CB_TM, TOP_K), lambda i: (i, 0)),
            full,
            pl.BlockSpec((MOD_ROWS, 6 * D_MODEL), lambda i: (0, 0)),
            pl.BlockSpec((1, D_MODEL), lambda i: (0, 0)),
            pl.BlockSpec(memory_space=pl.ANY),
        ],
        out_specs=out_specs,
        out_shape=out_shape,
        scratch_shapes=[pltpu.VMEM((TOP_K, CB_TM, D_MODEL), F32), pltpu.SemaphoreType.DMA((1,))],
        compiler_params=_cparams(("arbitrary",)),
        name="moe_combine",
    )(_tile_major(dest, CB_TM), gates.T, x, mod, final_w.reshape(1, D_MODEL), y_rows)


def _routed_ffn(h, x, mod, p, moe, layer, final_w, final):
    idx, gates, rank, counts = _router(h, p['router_w'], p['router_b'])
    counts = counts[:, 0].astype(jnp.int32)
    padded = (counts + MOE_BM - 1) // MOE_BM * MOE_BM
    pad_end = jnp.cumsum(padded)
    pad_start = pad_end - padded
    experts = jnp.arange(N_EXPERTS, dtype=jnp.int32)
    start_of = jnp.sum(jnp.where(idx[:, :, None] == experts, pad_start, 0), axis=-1)
    dest = start_of + rank
    nb = MOE_ROWS // MOE_BM
    first_row = jnp.arange(nb, dtype=jnp.int32) * MOE_BM
    blk_e = jnp.minimum(jnp.sum((pad_end[None, :] <= first_row[:, None]).astype(jnp.int32), axis=1),
                        N_EXPERTS - 1)
    is_e = blk_e[:, None] == experts[None, :]
    end_of = jnp.sum(jnp.where(is_e, pad_start + counts, 0), axis=1)
    n_valid = jnp.clip(end_of - first_row, 0, MOE_BM).astype(jnp.int32)
    blk_x = jnp.minimum(jnp.arange(nb, dtype=jnp.int32), pad_end[-1] // MOE_BM - 1)
    x_rows = _dispatch(h, dest, pad_end)
    y_rows = _experts(x_rows, blk_x, blk_e, n_valid, layer, *moe)
    return _combine(y_rows, dest, gates, x, mod, final_w, final)


_LAYER_PARAMS = ('norm_mix_w', 'norm_ffn_w', 'da_lambda', 'da_norm_w', 'rw_shift', 'rw_w0', 'rw_w_up', 'rw_a0',
                 'rw_a_up', 'rw_g_up', 'rw_k_k', 'rw_k_a', 'rw_r_k', 'rw_ln_w', 'rw_ln_b', 'ret_decay_logit',
                 'ret_norm_w', 'cv_dw_w', 'cv_dw_b', 'cv_ln_w', 'cv_ln_b', 'w_branch', 'w_out', 'router_w',
                 'router_b')


def _layer(x, mod, w_p, p, moe, layer, lam_init, caches, tables, final_w, final):
    cache_k, cache_v, state_rw, state_ret = caches
    proj_rw, proj = _input_projection(x, mod, p['norm_mix_w'], w_p, layer)
    o_da = _da_branch(proj, cache_k, cache_v, p['da_lambda'], p['da_norm_w'], lam_init, tables)
    o_rw, rw_state = _rwkv_branch(proj_rw, state_rw, p)
    o_ret, ret_state = _retention_branch(proj, state_ret, p)
    o_cv = _conv_branch(proj, p)
    x1, h2 = _merge((o_da, o_rw, o_ret, o_cv), proj, x, mod, p['w_branch'], p['w_out'], p['norm_ffn_w'])
    x2 = _routed_ffn(h2, x1, mod, p, moe, layer, final_w, final)
    ctx_kv = proj[:N_CTX, B_DA + DA_W:B_DA + 3 * DA_W].astype(F32)
    new_k = ctx_kv[:, :DA_W].reshape(BATCH, SEQ, DA_HEADS, 2, DA_QK)
    new_v = ctx_kv[:, DA_W:].reshape(BATCH, SEQ, DA_HEADS, DA_V)
    return x2, (new_k, new_v, rw_state[:BATCH], ret_state[:BATCH])


def kernel(x_prompt, x_sample, c, cache_da_k, cache_da_v, state_rwkv, state_ret, c_ctx, ada_w, ada_b, norm_mix_w,
           norm_ffn_w, w_in, da_lambda, da_norm_w, rw_shift, rw_w0, rw_w_up, rw_a0, rw_a_up, rw_g_up, rw_k_k,
           rw_k_a, rw_r_k, rw_ln_w, rw_ln_b, ret_decay_logit, ret_norm_w, cv_dw_w, cv_dw_b, cv_ln_w, cv_ln_b,
           w_branch, w_out, router_w, router_b, moe_w1, moe_b1, moe_w2, moe_b2, final_norm_w):
    weights = dict(norm_mix_w=norm_mix_w, norm_ffn_w=norm_ffn_w, da_lambda=da_lambda, da_norm_w=da_norm_w,
                   rw_shift=rw_shift, rw_w0=rw_w0, rw_w_up=rw_w_up, rw_a0=rw_a0, rw_a_up=rw_a_up, rw_g_up=rw_g_up,
                   rw_k_k=rw_k_k, rw_k_a=rw_k_a, rw_r_k=rw_r_k, rw_ln_w=rw_ln_w, rw_ln_b=rw_ln_b,
                   ret_decay_logit=ret_decay_logit, ret_norm_w=ret_norm_w, cv_dw_w=cv_dw_w, cv_dw_b=cv_dw_b,
                   cv_ln_w=cv_ln_w, cv_ln_b=cv_ln_b, w_branch=w_branch, w_out=w_out, router_w=router_w,
                   router_b=router_b)
    x = jnp.concatenate([x_prompt.reshape(N_CTX, D_MODEL), x_sample.reshape(N_LAT, D_MODEL)], axis=0)
    cvec = jnp.concatenate([c_ctx[None, :], c, jnp.zeros((MOD_ROWS - 1 - DEC_BATCH, D_MODEL), F32)], axis=0)
    mod = _modulation(cvec, ada_w, ada_b)
    w_p = _pad_w_in(w_in).astype(BF16)
    tables = _rope_tables()
    outs = []
    moe = (moe_w1, moe_b1, moe_w2, moe_b2)
    for i in range(DEPTH):
        p = {name: weights[name][i] for name in _LAYER_PARAMS}
        lam_init = 0.8 - 0.6 * math.exp(-0.3 * i)
        caches = (cache_da_k[:, i], cache_da_v[:, i], state_rwkv[:, i], state_ret[:, i])
        x, ctx_out = _layer(x, mod[i], w_p, p, moe, i, lam_init, caches, tables, final_norm_w, i == DEPTH - 1)
        outs.append(ctx_out)
    y_prompt = x[0].reshape(BATCH, SEQ, D_MODEL)
    y_sample = x[1].reshape(DEC_BATCH, DEC_SEQ, D_MODEL)
    stack = lambda j: jnp.stack([o[j] for o in outs], axis=1)
    return (y_prompt, y_sample, stack(0), stack(1), stack(2), stack(3))


def _pad_w_in(w_in):
    da, rw, ret, cv, gate = jnp.split(w_in, [1536, 3456, 4992, 6016], axis=-1)
    pad = jnp.zeros(w_in.shape[:-1] + (P_DA - RW_COLS,), w_in.dtype)
    return jnp.concatenate([rw, pad, da, ret, cv, gate], axis=-1)
```

```python
import functools
import math

import jax
import jax.numpy as jnp
from jax import lax
from jax.experimental import pallas as pl
from jax.experimental.pallas import tpu as pltpu

F32 = jnp.float32
BF16 = jnp.bfloat16

D_MODEL = 1024
BATCH = 16
SEQ = 256
DEPTH = 2
DEC_BATCH = 2
DEC_SEQ = 4096
PAST_LEN = 512
GRID_W = 64
EPS = 1e-6

DA_HEADS = 4
DA_QK = 64
DA_V = 128
ROPE_BASE = 10000.0

RW_HEADS = 8
RW_HD = 64
RW_W = 512
RW_G_LORA = 128
RW_GN_EPS = 64e-5

RET_HEADS = 4
RET_DK = 64
RET_DV = 128
RET_CHUNK = 128

CV_W = 512
CONV_K = 31
BR_W = 512
N_BRANCH = 4

N_EXPERTS = 32
TOP_K = 4
D_FF = 1024
SWIGLU_LIMIT = 7.0
SWIGLU_ALPHA = 1.702

N_CTX = BATCH * SEQ
N_LAT = DEC_BATCH * DEC_SEQ
N_TOK = N_CTX + N_LAT
N_SEQS = BATCH + DEC_BATCH
MOD_ROWS = 8
MOD_GROUP = 4096

RW_COLS = 1920
P_RW = 0
P_DA = 2048
P_RET = 3584
P_CV = 5120
P_GATE = 6144
P_COLS = 10240
B_DA, B_RET, B_CV, B_GATE = (p - P_DA for p in (P_DA, P_RET, P_CV, P_GATE))

VMEM_LIMIT = 48 * 1024 * 1024
EXPERT_VMEM_LIMIT = 56 * 1024 * 1024


def _cparams(sem):
    return pltpu.CompilerParams(dimension_semantics=sem, vmem_limit_bytes=VMEM_LIMIT)


def _dg(a, b, dims):
    return lax.dot_general(a, b, (dims, ((), ())), preferred_element_type=F32)


NN = ((1,), (0,))
NT = ((1,), (1,))
TN = ((0,), (0,))


def _split(x):
    hi = x.astype(BF16)
    lo = (x - hi.astype(F32)).astype(BF16)
    return hi, lo


def _dot3(a, b, dims=NN):
    ah, al = _split(a)
    bh, bl = _split(b)
    return _dg(ah, bh, dims) + (_dg(ah, bl, dims) + _dg(al, bh, dims))


def _dot2x(a, e, dims=NN):
    ah, al = _split(a)
    am = (a - ah.astype(F32) - al.astype(F32)).astype(BF16)
    eb = e.astype(BF16)
    return _dg(ah, eb, dims) + (_dg(al, eb, dims) + _dg(am, eb, dims))


def _sigmoid(x):
    return 1.0 / (1.0 + jnp.exp(-x))


def _softplus(x):
    return jnp.maximum(x, 0.0) + jnp.log(1.0 + jnp.exp(-jnp.abs(x)))


def _seq_pos(row):
    in_ctx = row < N_CTX
    pos = jnp.where(in_ctx, row & (SEQ - 1), (row - N_CTX) & (DEC_SEQ - 1))
    length = jnp.where(in_ctx, SEQ, DEC_SEQ)
    return pos, length


def _seq_index(row):
    return jnp.where(row < N_CTX, row // SEQ, BATCH + (row - N_CTX) // DEC_SEQ)


def _dotx(e, b, dims=NN):
    bh, bl = _split(b)
    bm = (b - bh.astype(F32) - bl.astype(F32)).astype(BF16)
    eb = e.astype(BF16)
    return _dg(eb, bh, dims) + (_dg(eb, bl, dims) + _dg(eb, bm, dims))


def _mod_kernel(c_ref, w_ref, b_ref, o_ref):
    c = c_ref[...]
    s = c * _sigmoid(c)
    o_ref[0] = _dot3(s, w_ref[0]) + b_ref[0]


def _modulation(cvec, ada_w, ada_b):
    tn = 1536
    return pl.pallas_call(
        _mod_kernel,
        grid=(DEPTH, 6 * D_MODEL // tn),
        in_specs=[
            pl.BlockSpec((MOD_ROWS, D_MODEL), lambda l, j: (0, 0)),
            pl.BlockSpec((1, D_MODEL, tn), lambda l, j: (l, 0, j)),
            pl.BlockSpec((1, 1, tn), lambda l, j: (l, 0, j)),
        ],
        out_specs=pl.BlockSpec((1, MOD_ROWS, tn), lambda l, j: (l, 0, j)),
        out_shape=jax.ShapeDtypeStruct((DEPTH, MOD_ROWS, 6 * D_MODEL), F32),
        compiler_params=_cparams(("parallel", "parallel")),
        name="modulation",
    )(cvec, ada_w, ada_b.reshape(DEPTH, 1, 6 * D_MODEL))


def _mod_row(mod_ref, first_row):
    g = first_row // MOD_GROUP
    return mod_ref[pl.ds(g, 1), :]


def _rms(x, w):
    return x * lax.rsqrt(jnp.mean(x * x, axis=-1, keepdims=True) + EPS) * w


IN_TM = 1024
IN_TN = 1024


IN_NA = P_DA // IN_TN


def _inproj_kernel(x_ref, mod_ref, nw_ref, w_ref, oa_ref, og_ref, h_ref):
    i = pl.program_id(0)
    j = pl.program_id(1)

    @pl.when(j == 0)
    def _():
        m = _mod_row(mod_ref, i * IN_TM)
        sh = m[:, 0:D_MODEL]
        sc = m[:, D_MODEL:2 * D_MODEL]
        h_ref[...] = (_rms(x_ref[...], nw_ref[...]) * (1.0 + sc) + sh).astype(BF16)

    acc = _dg(h_ref[...], w_ref[0].astype(BF16), NN)

    @pl.when(j < IN_NA)
    def _():
        oa_ref[...] = acc

    @pl.when(j >= IN_NA)
    def _():
        og_ref[...] = acc.astype(BF16)


def _input_projection(x, mod, norm_w, w_p, layer):
    n = x.shape[0]
    return pl.pallas_call(
        _inproj_kernel,
        grid=(n // IN_TM, P_COLS // IN_TN),
        in_specs=[
            pl.BlockSpec((IN_TM, D_MODEL), lambda i, j: (i, 0)),
            pl.BlockSpec((MOD_ROWS, 6 * D_MODEL), lambda i, j: (0, 0)),
            pl.BlockSpec((1, D_MODEL), lambda i, j: (0, 0)),
            pl.BlockSpec((1, D_MODEL, IN_TN), lambda i, j: (layer, 0, j)),
        ],
        out_specs=[pl.BlockSpec((IN_TM, IN_TN), lambda i, j: (i, jnp.minimum(j, IN_NA - 1))),
                   pl.BlockSpec((IN_TM, IN_TN), lambda i, j: (i, jnp.maximum(j - IN_NA, 0)))],
        out_shape=[jax.ShapeDtypeStruct((n, P_DA), F32), jax.ShapeDtypeStruct((n, P_COLS - P_DA), BF16)],
        scratch_shapes=[pltpu.VMEM((IN_TM, D_MODEL), BF16)],
        compiler_params=_cparams(("parallel", "arbitrary")),
        name="input_projection",
    )(x, mod, norm_w.reshape(1, D_MODEL), w_p)


QK_TM = 512
DA_W = DA_HEADS * 2 * DA_QK


def _qkprep_kernel(q_ref, k_ref, v_ref, c_ref, se_ref, so_ref, qo_ref, ko_ref, vo_ref):
    i = pl.program_id(0)
    for h in range(DA_HEADS):
        vo_ref[:, h * 2 * DA_V:h * 2 * DA_V + DA_V] = v_ref[:, h * DA_V:(h + 1) * DA_V].astype(BF16)
        vo_ref[:, h * 2 * DA_V + DA_V:(h + 1) * 2 * DA_V] = jnp.ones((QK_TM, DA_V), BF16)
    scale = DA_QK ** -0.5

    @pl.when(i * QK_TM < N_CTX)
    def _():
        qo_ref[...] = (q_ref[...].astype(F32) * scale).astype(BF16)
        ko_ref[...] = k_ref[...].astype(BF16)

    @pl.when(i * QK_TM >= N_CTX)
    def _():
        c = c_ref[...]
        se = se_ref[...]
        so = so_ref[...]

        def rope(x):
            nxt = pltpu.roll(x, DA_W - 1, axis=1)
            prv = pltpu.roll(x, 1, axis=1)
            return x * c + nxt * se + prv * so

        qo_ref[...] = (rope(q_ref[...].astype(F32)) * scale).astype(BF16)
        ko_ref[...] = rope(k_ref[...].astype(F32)).astype(BF16)


def _rope_tables():
    rows = DEC_SEQ // GRID_W
    row = jnp.repeat(jnp.arange(rows, dtype=F32), GRID_W)
    col = jnp.tile(jnp.arange(GRID_W, dtype=F32), rows)
    n_pairs = DA_QK // 4
    inv = ROPE_BASE ** (-jnp.arange(n_pairs, dtype=F32) / n_pairs)
    ang = jnp.concatenate([row[:, None] * inv, col[:, None] * inv], axis=-1)
    cos = jnp.repeat(jnp.cos(ang), 2, axis=-1)
    sin = jnp.repeat(jnp.sin(ang), 2, axis=-1)
    even = (jnp.arange(DA_QK) % 2 == 0)[None, :]
    s_even = jnp.where(even, -sin, 0.0)
    s_odd = jnp.where(even, 0.0, sin)
    rep = lambda t: jnp.tile(t, (1, DA_W // DA_QK))
    return rep(cos), rep(s_even), rep(s_odd)


def _qk_prepare(proj, tables):
    n = proj.shape[0]
    lat0 = N_CTX // QK_TM
    nlat = DEC_SEQ // QK_TM
    tab = pl.BlockSpec((QK_TM, DA_W), lambda i: (jnp.maximum(i - lat0, 0) % nlat, 0))
    c0 = B_DA // DA_W
    out = jax.ShapeDtypeStruct((n, DA_W), BF16)
    return pl.pallas_call(
        _qkprep_kernel,
        grid=(n // QK_TM,),
        in_specs=[
            pl.BlockSpec((QK_TM, DA_W), lambda i: (i, c0)),
            pl.BlockSpec((QK_TM, DA_W), lambda i: (i, c0 + 1)),
            pl.BlockSpec((QK_TM, DA_W), lambda i: (i, c0 + 2)),
            tab, tab, tab,
        ],
        out_specs=[pl.BlockSpec((QK_TM, DA_W), lambda i: (i, 0))] * 2
        + [pl.BlockSpec((QK_TM, 2 * DA_W), lambda i: (i, 0))],
        out_shape=[out, out, jax.ShapeDtypeStruct((n, 2 * DA_W), BF16)],
        compiler_params=_cparams(("parallel",)),
        name="qk_prepare",
    )(proj, proj, proj, *tables)


DA_TQ = 256


def _attend(q, segments, lam, nw, o_ref):
    for h in range(DA_HEADS):
        os = []
        for m in range(2):
            c0 = (2 * h + m) * DA_QK
            ss = [_dg(q[:, c0:c0 + DA_QK], k[:, c0:c0 + DA_QK], NT) for k, _ in segments]
            mx = jnp.max(ss[0], axis=-1, keepdims=True)
            for s in ss[1:]:
                mx = jnp.maximum(mx, jnp.max(s, axis=-1, keepdims=True))
            oe = None
            for s, (_, v) in zip(ss, segments):
                part = _dg(jnp.exp(s - mx).astype(BF16), v[:, h * 2 * DA_V:(h + 1) * 2 * DA_V], NN)
                oe = part if oe is None else oe + part
            os.append(oe[:, :DA_V] * (1.0 / oe[:, DA_V:DA_V + 1]))
        o_ref[:, h * DA_V:(h + 1) * DA_V] = _rms(os[0] - lam * os[1], nw).astype(BF16)


def _da_kernel(q_ref, ks_ref, vs_ref, kl_ref, vl_ref, dl_ref, nw_ref, o_ref, *, lam_init):
    dl = dl_ref[...]
    lam = (jnp.exp(jnp.sum(dl[0:1] * dl[1:2], axis=1, keepdims=True))
           - jnp.exp(jnp.sum(dl[2:3] * dl[3:4], axis=1, keepdims=True)) + lam_init)
    nw = nw_ref[...] * (1.0 - lam_init)
    q = q_ref[...]
    is_ctx = pl.program_id(0) < BATCH * (SEQ // DA_TQ)

    @pl.when(is_ctx)
    def _():
        _attend(q, [(ks_ref, vs_ref)], lam, nw, o_ref)

    @pl.when(jnp.logical_not(is_ctx))
    def _():
        _attend(q, [(kl_ref.at[0], vl_ref.at[0])], lam, nw, o_ref)


def _da_branch(proj, cache_k, cache_v, da_lambda, da_norm_w, lam_init, tables):
    qb, kb, vb = _qk_prepare(proj, tables)
    n = proj.shape[0]
    ck = cache_k.reshape(DEC_BATCH, PAST_LEN, DA_W).astype(BF16)
    cv = cache_v.astype(BF16)
    cv = jnp.concatenate([cv, jnp.ones_like(cv)], axis=-1).reshape(DEC_BATCH, PAST_LEN, 2 * DA_W)
    lat = lambda t: t[N_CTX:].reshape(DEC_BATCH, DEC_SEQ, t.shape[-1])
    k_all = jnp.concatenate([ck, lat(kb)], axis=1)
    v_all = jnp.concatenate([cv, lat(vb)], axis=1)
    tk = PAST_LEN + DEC_SEQ
    n_ctx_tiles = N_CTX // DA_TQ
    per_seq = DEC_SEQ // DA_TQ
    seq_c = lambda i: jnp.minimum(i * DA_TQ // SEQ, BATCH - 1)
    seq_l = lambda i: jnp.clip((i - n_ctx_tiles) // per_seq, 0, DEC_BATCH - 1)
    once = pl.Buffered(1)
    return pl.pallas_call(
        functools.partial(_da_kernel, lam_init=lam_init),
        grid=(n // DA_TQ,),
        in_specs=[
            pl.BlockSpec((DA_TQ, DA_W), lambda i: (i, 0)),
            pl.BlockSpec((SEQ, DA_W), lambda i: (seq_c(i), 0)),
            pl.BlockSpec((SEQ, 2 * DA_W), lambda i: (seq_c(i), 0)),
            pl.BlockSpec((1, tk, DA_W), lambda i: (seq_l(i), 0, 0), pipeline_mode=once),
            pl.BlockSpec((1, tk, 2 * DA_W), lambda i: (seq_l(i), 0, 0), pipeline_mode=once),
            pl.BlockSpec((4, DA_QK), lambda i: (0, 0)),
            pl.BlockSpec((1, DA_V), lambda i: (0, 0)),
        ],
        out_specs=pl.BlockSpec((DA_TQ, DA_W), lambda i: (i, 0)),
        out_shape=jax.ShapeDtypeStruct((n, DA_W), BF16),
        compiler_params=_cparams(("arbitrary",)),
        name="diff_attention",
    )(qb, kb, vb, k_all, v_all, da_lambda, da_norm_w.reshape(1, DA_V))


RW_TM = 256
RW_C = 64
RW_PAIRS = RW_HEADS // 2
HALO = 8


def _head_ones():
    idx = jnp.arange(2 * RW_HD) // RW_HD
    return (idx[:, None] == idx[None, :]).astype(BF16)


def _head_sum(x, ones_pair):
    w = 2 * RW_HD
    return jnp.concatenate([_dot2x(x[:, p * w:(p + 1) * w], ones_pair) for p in range(RW_PAIRS)], axis=1)


def _rwprep_kernel(x_ref, xp_ref, xn_ref, mu_ref, w0_ref, wup_ref, a0_ref, aup_ref, gup_ref, kk_ref, ka_ref,
                   ones_ref, r_o, v_o, kk_o, g_o, kd_o, lw_o, a_o, buf):
    row0 = pl.program_id(0) * RW_TM
    x = x_ref[:, 0:RW_COLS]
    buf[HALO:HALO + RW_TM, :] = x
    buf[HALO - 1:HALO, :] = xp_ref[HALO - 1:HALO, 0:RW_COLS]
    buf[HALO + RW_TM:HALO + RW_TM + 1, :] = xn_ref[0:1, 0:RW_COLS]
    rows = row0 + lax.broadcasted_iota(jnp.int32, (RW_TM, 1), 0)
    pos, length = _seq_pos(rows)
    prev = jnp.where(pos == 0, 0.0, buf[HALO - 1:HALO - 1 + RW_TM, :])
    nxt = jnp.where(pos == length - 1, 0.0, buf[HALO + 1:HALO + 1 + RW_TM, :])
    mu = mu_ref[...]
    u = x + mu[0:1] * (prev - x) + mu[1:2] * (nxt - x)

    r = u[:, 0:RW_W]
    k = u[:, RW_W:2 * RW_W]
    v = u[:, 2 * RW_W:3 * RW_W]
    wl = u[:, 3 * RW_W:3 * RW_W + 128]
    al = u[:, 3 * RW_W + 128:3 * RW_W + 256]
    gl = u[:, 3 * RW_W + 256:3 * RW_W + 384]
    w_raw = w0_ref[...] + _dot3(jnp.tanh(wl), wup_ref[...])
    lw = -math.exp(-0.5) * _sigmoid(w_raw)
    a = _sigmoid(a0_ref[...] + _dot3(al, aup_ref[...]))
    g = _dot3(_sigmoid(gl), gup_ref[...])
    kk = k * kk_ref[...]
    kk = kk * lax.rsqrt(jnp.maximum(_head_sum(kk * kk, ones_ref[...]), 1e-12))
    kd = jnp.concatenate([k, k], axis=1) * (1.0 + (a - 1.0) * ka_ref[...])

    r_o[...] = r
    v_o[...] = v
    kk_o[...] = kk
    g_o[...] = g
    for d in range(2):
        kd_o[d] = kd[:, d * RW_W:(d + 1) * RW_W]
        lw_o[d] = lw[:, d * RW_W:(d + 1) * RW_W]
        a_o[d] = a[:, d * RW_W:(d + 1) * RW_W]


def _rw_prepare(proj, p):
    n = proj.shape[0]
    nh = n // HALO
    steps = RW_TM // HALO
    wide = P_DA - P_RW
    cat2 = lambda t: t.reshape(1, 2 * RW_W)
    blockdiag = lambda t: jnp.concatenate(
        [jnp.concatenate([t[0], jnp.zeros_like(t[0])], axis=1),
         jnp.concatenate([jnp.zeros_like(t[1]), t[1]], axis=1)], axis=0)
    const = lambda shape: pl.BlockSpec(shape, lambda i: (0,) * len(shape))
    row = pl.BlockSpec((RW_TM, RW_W), lambda i: (i, 0))
    row2 = pl.BlockSpec((2, RW_TM, RW_W), lambda i: (0, i, 0))
    o1 = jax.ShapeDtypeStruct((n, RW_W), F32)
    o2 = jax.ShapeDtypeStruct((2, n, RW_W), F32)
    return pl.pallas_call(
        _rwprep_kernel,
        grid=(n // RW_TM,),
        in_specs=[
            pl.BlockSpec((RW_TM, wide), lambda i: (i, 0)),
            pl.BlockSpec((HALO, wide), lambda i: (jnp.maximum(i * steps - 1, 0), 0)),
            pl.BlockSpec((HALO, wide), lambda i: (jnp.minimum((i + 1) * steps, nh - 1), 0)),
            const((2, RW_COLS)), const((1, 2 * RW_W)), const((128, 2 * RW_W)), const((1, 2 * RW_W)),
            const((128, 2 * RW_W)), const((RW_G_LORA, RW_W)), const((1, RW_W)), const((1, 2 * RW_W)),
            const((2 * RW_HD, 2 * RW_HD)),
        ],
        out_specs=[row, row, row, row, row2, row2, row2],
        out_shape=[o1, o1, o1, o1, o2, o2, o2],
        scratch_shapes=[pltpu.VMEM((RW_TM + 2 * HALO, RW_COLS), F32)],
        compiler_params=_cparams(("parallel",)),
        name="rwkv_prepare",
    )(proj, proj, proj, p['rw_shift'], cat2(p['rw_w0']), blockdiag(p['rw_w_up']), cat2(p['rw_a0']),
      blockdiag(p['rw_a_up']), p['rw_g_up'], p['rw_k_k'].reshape(1, RW_W),
      jnp.tile(p['rw_k_a'].reshape(1, RW_W), (1, 2)), _head_ones())


RW_TB = 256
RW_NC = RW_TB // RW_C
RW_SIDE = 16


def _rw_finish(y, r, v, g, kd_sum, rk, lnw, lnb, ones):
    inv = 1.0 / RW_HD
    xc = y - _head_sum(y, ones) * inv
    var = _head_sum(xc * xc, ones) * inv
    yn = xc * lax.rsqrt(var + RW_GN_EPS) * lnw + lnb
    bonus = _head_sum(r * kd_sum * rk, ones) * v
    return ((yn + bonus) * g).astype(BF16)


def _rwscan_kernel(r_ref, v_ref, kk_ref, kd_ref, lw_ref, a_ref, tri_ref, inc_ref, str_ref, s0_ref, *rest, backward):
    if backward:
        yf_ref, g_ref, rk_ref, lnw_ref, lnb_ref, ones_ref, y_ref, st_ref, s_scr = rest
    else:
        y_ref, st_ref, s_scr = rest
    _rwscan_body(r_ref, v_ref, kk_ref, kd_ref, lw_ref, a_ref, tri_ref, inc_ref, str_ref, s0_ref, y_ref, st_ref,
                 s_scr, backward,
                 (lambda yb: _rw_finish(yf_ref[...] + yb, r_ref[...], v_ref[...], g_ref[...],
                                        kd_ref[0] + kd_ref[1], rk_ref[...], lnw_ref[...], lnb_ref[...],
                                        ones_ref[...])) if backward else None)


def _rwscan_body(r_ref, v_ref, kk_ref, kd_ref, lw_ref, a_ref, tri_ref, inc_ref, str_ref, s0_ref, y_ref, st_ref,
                 s_scr, backward, finish):
    step = pl.program_id(0)
    nb = pl.num_programs(0)
    bi = (nb - 1 - step) if backward else step
    pos, length = _seq_pos(bi * RW_TB)
    first = (pos + RW_TB == length) if backward else (pos == 0)
    last = (pos == 0) if backward else (pos + RW_TB == length)
    in_ctx = bi * RW_TB < N_CTX
    c = RW_C

    @pl.when(jnp.logical_and(first, in_ctx))
    def _():
        s_scr[...] = jnp.zeros_like(s_scr)

    @pl.when(jnp.logical_and(first, jnp.logical_not(in_ctx)))
    def _():
        z = jnp.zeros((c, c), F32)
        for p in range(RW_PAIRS):
            s_scr[p] = jnp.concatenate(
                [jnp.concatenate([s0_ref[0, 0, 2 * p], z], axis=1),
                 jnp.concatenate([z, s0_ref[0, 0, 2 * p + 1]], axis=1)], axis=0)

    incl2 = inc_ref[...] > 0.5
    strict2 = str_ref[...] > 0.5
    eye2 = (lax.broadcasted_iota(jnp.int32, (2 * c, 2 * c), 0)
            == lax.broadcasted_iota(jnp.int32, (2 * c, 2 * c), 1))
    m_e = lax.broadcasted_iota(jnp.int32, (1, 2 * c), 1) < c

    lw = lw_ref[0]
    cum = _dotx(tri_ref[...], lw)
    tots = [cum[(j * c if backward else j * c + c - 1):(j * c + 1 if backward else j * c + c), :]
            for j in range(RW_NC)]
    tot_b = jnp.concatenate([jnp.broadcast_to(t, (c, RW_W)) for t in tots], axis=0)
    kk = kk_ref[...]
    kd = kd_ref[1 if backward else 0]
    bp = kk * a_ref[0]
    g_inv = jnp.exp(-cum)
    g_rem = jnp.exp(tot_b - cum)
    ag = -kk * jnp.exp(cum - lw)
    rg = r_ref[...] * jnp.exp(cum)
    bdn = bp * g_inv
    kdn = kd * g_inv
    bc = bp * g_rem
    kc = kd * g_rem
    v = v_ref[...]

    def stack(x, j, p):
        xs = x[j * c:(j + 1) * c, p * 2 * c:(p + 1) * 2 * c]
        return jnp.concatenate([jnp.where(m_e, xs, 0.0), jnp.where(m_e, 0.0, xs)], axis=0).astype(BF16)

    pre = {}
    keys = [(j, p) for j in range(RW_NC) for p in range(RW_PAIRS)]
    for g0 in range(0, len(keys), RW_SIDE):
        grp = keys[g0:g0 + RW_SIDE]
        ops = {k: tuple(stack(t, *k) for t in (ag, rg, bdn, kdn, bc, kc, v)) for k in grp}
        gm = {k: _dg(jnp.concatenate([ops[k][0], ops[k][1]], axis=0),
                     jnp.concatenate([ops[k][2], ops[k][3]], axis=0), NT) for k in grp}
        lbb = {k: jnp.where(strict2, gm[k][:2 * c, :2 * c], 0.0) for k in grp}
        lkb = {k: jnp.where(strict2, gm[k][:2 * c, 2 * c:], 0.0).astype(BF16) for k in grp}
        lrk = {k: jnp.concatenate([jnp.where(incl2, gm[k][2 * c:, :2 * c], 0.0),
                                   jnp.where(incl2, gm[k][2 * c:, 2 * c:], 0.0)], axis=1).astype(BF16)
               for k in grp}
        lv = {k: _dg(lkb[k], ops[k][6], NN) for k in grp}
        x = {k: jnp.where(eye2, 1.0, lbb[k]) for k in grp}
        pw = lbb
        for _ in range(int(math.log2(c)) - 1):
            pwb = {k: pw[k].astype(BF16) for k in grp}
            pw = {k: _dg(pwb[k], pwb[k], NN) for k in grp}
            x = {k: x[k] + _dg(x[k].astype(BF16), pw[k].astype(BF16), NN) for k in grp}
        tw = {k: _dg(x[k].astype(BF16), jnp.concatenate([lv[k].astype(BF16), ops[k][0]], axis=1), NN)
              for k in grp}
        for k in grp:
            pre[k] = (tw[k][:, :2 * c], tw[k][:, 2 * c:].astype(BF16), ops[k][1], lrk[k], ops[k][6],
                      jnp.concatenate([ops[k][4], ops[k][5]], axis=0))

    order = range(RW_NC - 1, -1, -1) if backward else range(RW_NC)
    pairs = range(RW_PAIRS)
    s = [s_scr[p] for p in pairs]
    ys = {}
    for j in order:
        sb = [s[p].astype(BF16) for p in pairs]
        u = [_dg(pre[j, p][1], sb[p], NT) + pre[j, p][0] for p in pairs]
        uv = [jnp.concatenate([u[p].astype(BF16), pre[j, p][4]], axis=0) for p in pairs]
        y = [_dg(pre[j, p][2], sb[p], NT) + _dg(pre[j, p][3], uv[p], NN) for p in pairs]
        s = [s[p] * jnp.exp(tots[j][:, p * 2 * c:(p + 1) * 2 * c]) + _dg(uv[p], pre[j, p][5], TN) for p in pairs]
        ys[j] = [y[p][:c] + y[p][c:] for p in pairs]
    y_blk = jnp.concatenate([jnp.concatenate(ys[j], axis=1) for j in range(RW_NC)], axis=0)
    y_ref[...] = finish(y_blk) if finish else y_blk
    for p in pairs:
        s_scr[p] = s[p]

    @pl.when(last)
    def _():
        for p in pairs:
            st_ref[0, 2 * p] = s[p][:c, :c]
            st_ref[0, 2 * p + 1] = s[p][c:, c:]


def _scan_masks(backward):
    t = jnp.arange(RW_TB)
    sgn = -1 if backward else 1
    same_chunk = (t[:, None] // RW_C) == (t[None, :] // RW_C)
    tri = (same_chunk & ((t[:, None] - t[None, :]) * sgn >= 0)).astype(BF16)
    q = jnp.arange(2 * RW_C)
    same_head = (q[:, None] // RW_C) == (q[None, :] // RW_C)
    dif = ((q[:, None] % RW_C) - (q[None, :] % RW_C)) * sgn
    return tri, (same_head & (dif >= 0)).astype(F32), (same_head & (dif > 0)).astype(F32)


def _rw_scan(r, v, kk, kd, lw, a, s0, backward, finish=()):
    n = r.shape[0]
    nb = n // RW_TB
    d = 1 if backward else 0
    blk = (lambda i: nb - 1 - i) if backward else (lambda i: i)
    row = pl.BlockSpec((RW_TB, RW_W), lambda i: (blk(i), 0))
    row2 = pl.BlockSpec((1, RW_TB, RW_W), lambda i: (d, blk(i), 0))
    both = pl.BlockSpec((2, RW_TB, RW_W), lambda i: (0, blk(i), 0))
    const = lambda shape: pl.BlockSpec(shape, lambda i: (0,) * len(shape))
    vec = const((1, RW_W))
    pair = 2 * RW_C
    extra = [row, row, vec, vec, vec, const((pair, pair))] if backward else []
    return pl.pallas_call(
        functools.partial(_rwscan_kernel, backward=backward),
        grid=(nb,),
        in_specs=[row, row, row, both, row2, row2, const((RW_TB, RW_TB)), const((pair, pair)), const((pair, pair)),
                  pl.BlockSpec((1, 1, RW_HEADS, RW_HD, RW_HD),
                               lambda i: (jnp.maximum(_seq_index(blk(i) * RW_TB) - BATCH, 0), d, 0, 0, 0))] + extra,
        out_specs=[row, pl.BlockSpec((1, RW_HEADS, RW_HD, RW_HD), lambda i: (_seq_index(blk(i) * RW_TB), 0, 0, 0))],
        out_shape=[jax.ShapeDtypeStruct((n, RW_W), BF16 if backward else F32),
                   jax.ShapeDtypeStruct((N_SEQS, RW_HEADS, RW_HD, RW_HD), F32)],
        scratch_shapes=[pltpu.VMEM((RW_PAIRS, pair, pair), F32)],
        compiler_params=_cparams(("arbitrary",)),
        name="rwkv_scan_bwd" if backward else "rwkv_scan_fwd",
    )(r, v, kk, kd, lw, a, *_scan_masks(backward), s0, *finish)


def _rwkv_branch(proj, state0, p):
    r, v, kk, g, kd, lw, a = _rw_prepare(proj, p)
    yf, sf = _rw_scan(r, v, kk, kd, lw, a, state0, backward=False)
    finish = (yf, g, p['rw_r_k'].reshape(1, RW_W), p['rw_ln_w'].reshape(1, RW_W), p['rw_ln_b'].reshape(1, RW_W),
              _head_ones())
    out, sb = _rw_scan(r, v, kk, kd, lw, a, state0, backward=True, finish=finish)
    return out, jnp.stack([sf, sb], axis=1)


RET_W = RET_HEADS * RET_DV
RET_PAIRS = RET_HEADS // 2


RET_TB = 256
RET_NC = RET_TB // RET_CHUNK


def _ret_kernel(q_ref, k_ref, v_ref, lg_ref, s0_ref, *rest, backward):
    if backward:
        of_ref, g_ref, nw_ref, o_ref, st_ref, s_scr = rest
    else:
        o_ref, st_ref, s_scr = rest
    step = pl.program_id(0)
    nb = pl.num_programs(0)
    bi = (nb - 1 - step) if backward else step
    pos, length = _seq_pos(bi * RET_TB)
    first = (pos + RET_TB == length) if backward else (pos == 0)
    in_ctx = bi * RET_TB < N_CTX

    @pl.when(jnp.logical_and(first, in_ctx))
    def _():
        s_scr[...] = jnp.zeros_like(s_scr)

    @pl.when(jnp.logical_and(first, jnp.logical_not(in_ctx)))
    def _():
        s_scr[...] = s0_ref[0, 0]

    c = RET_CHUNK
    d = 1 if backward else 0
    sgn = -1 if backward else 1
    lgs = -_softplus(-lg_ref[d:d + 1, :])
    ri = lax.broadcasted_iota(jnp.int32, (c, c), 0)
    cj = lax.broadcasted_iota(jnp.int32, (c, c), 1)
    dif = (ri - cj) * sgn
    valid = dif >= 0
    dist = jnp.maximum(dif, 0).astype(F32)
    pr = lax.broadcasted_iota(jnp.int32, (c, 2 * RET_DK), 0)
    tau = ((c - 1 - pr) if backward else pr).astype(F32)
    low = lax.broadcasted_iota(jnp.int32, (1, 2 * RET_DK), 1) < RET_DK
    rlow = lax.broadcasted_iota(jnp.int32, (2 * RET_DK, RET_DV), 0) < RET_DK
    heads = range(RET_HEADS)
    pairs = range(RET_PAIRS)
    mask = [low, jnp.logical_not(low)]
    lg_h = [lgs[:, h:h + 1] for h in heads]
    lg_row = [jnp.where(low, lg_h[2 * p], lg_h[2 * p + 1]) for p in pairs]
    dmat = [jnp.where(valid, jnp.exp(lg_h[h] * dist), 0.0) for h in heads]
    q_dec = [jnp.exp(lg_row[p] * (tau + 1.0)) for p in pairs]
    k_dec = [jnp.exp(lg_row[p] * (c - 1.0 - tau)) for p in pairs]
    c_dec = [jnp.where(rlow, jnp.exp(lg_h[2 * p] * c), jnp.exp(lg_h[2 * p + 1] * c)) for p in pairs]

    pre = []
    for j in range(RET_NC):
        rows = slice(j * c, (j + 1) * c)
        pw = 2 * RET_DK
        qp = [q_ref[rows, p * pw:(p + 1) * pw] for p in pairs]
        kp = [k_ref[rows, p * pw:(p + 1) * pw] * (RET_DK ** -0.5) for p in pairs]
        kpb = [t.astype(BF16) for t in kp]
        vb = [v_ref[rows, h * RET_DV:(h + 1) * RET_DV].astype(BF16) for h in heads]
        att = [_dg(jnp.where(mask[h % 2], qp[h // 2], 0.0).astype(BF16), kpb[h // 2], NT) * dmat[h]
               for h in heads]
        upd = [_dg(jnp.where(mask[h % 2], kp[h // 2] * k_dec[h // 2], 0.0).astype(BF16), vb[h], TN)
               for h in heads]
        inner = [_dg(att[h].astype(BF16), vb[h], NN) for h in heads]
        qd = [jnp.where(mask[h % 2], qp[h // 2] * q_dec[h // 2], 0.0).astype(BF16) for h in heads]
        pre.append((inner, qd, upd))

    s = [s_scr[p] for p in pairs]
    for j in (range(RET_NC - 1, -1, -1) if backward else range(RET_NC)):
        inner, qd, upd = pre[j]
        sb = [t.astype(BF16) for t in s]
        for h in heads:
            rows, cols = slice(j * c, (j + 1) * c), slice(h * RET_DV, (h + 1) * RET_DV)
            val = inner[h] + _dg(qd[h], sb[h // 2], NN)
            if backward:
                gh = g_ref[rows, cols].astype(F32)
                val = (gh * _sigmoid(gh) * (_standardize(of_ref[rows, cols] + val, EPS) * nw_ref[:, cols])
                       ).astype(BF16)
            o_ref[rows, cols] = val
        s = [s[p] * c_dec[p] + upd[2 * p] + upd[2 * p + 1] for p in pairs]
    for p in pairs:
        s_scr[p] = s[p]
        st_ref[0, p] = s[p]


def _ret_scan(proj, logit, s0, backward, finish=()):
    n = proj.shape[0]
    nb = n // RET_TB
    d = 1 if backward else 0
    blk = (lambda i: nb - 1 - i) if backward else (lambda i: i)
    qk_w = RET_HEADS * RET_DK
    state = (RET_PAIRS, 2 * RET_DK, RET_DV)
    extra, extra_args = [], ()
    if backward:
        extra = [pl.BlockSpec((RET_TB, RET_W), lambda i: (blk(i), 0)),
                 pl.BlockSpec((RET_TB, RET_W), lambda i: (blk(i), (B_RET + 2 * qk_w + RET_W) // RET_W)),
                 pl.BlockSpec((1, RET_W), lambda i: (0, 0))]
        extra_args = (finish[0], proj, finish[1])
    return pl.pallas_call(
        functools.partial(_ret_kernel, backward=backward),
        grid=(nb,),
        in_specs=[
            pl.BlockSpec((RET_TB, qk_w), lambda i: (blk(i), B_RET // qk_w)),
            pl.BlockSpec((RET_TB, qk_w), lambda i: (blk(i), B_RET // qk_w + 1)),
            pl.BlockSpec((RET_TB, RET_W), lambda i: (blk(i), (B_RET + 2 * qk_w) // RET_W)),
            pl.BlockSpec((2, RET_HEADS), lambda i: (0, 0)),
            pl.BlockSpec((1, 1) + state,
                         lambda i: (jnp.maximum(_seq_index(blk(i) * RET_TB) - BATCH, 0), d, 0, 0, 0)),
        ] + extra,
        out_specs=[pl.BlockSpec((RET_TB, RET_W), lambda i: (blk(i), 0)),
                   pl.BlockSpec((1,) + state, lambda i: (_seq_index(blk(i) * RET_TB), 0, 0, 0))],
        out_shape=[jax.ShapeDtypeStruct((n, RET_W), BF16 if backward else F32),
                   jax.ShapeDtypeStruct((N_SEQS,) + state, F32)],
        scratch_shapes=[pltpu.VMEM(state, F32)],
        compiler_params=_cparams(("arbitrary",)),
        name="retention_scan_bwd" if backward else "retention_scan_fwd",
    )(proj, proj, proj, logit, s0, *extra_args)


def _standardize(x, eps):
    mu = jnp.mean(x, axis=-1, keepdims=True)
    xc = x - mu
    return xc * lax.rsqrt(jnp.mean(xc * xc, axis=-1, keepdims=True) + eps)


def _retention_branch(proj, state0, p):
    s0 = state0.reshape(DEC_BATCH, 2, RET_PAIRS, 2 * RET_DK, RET_DV)
    of, sf = _ret_scan(proj, p['ret_decay_logit'], s0, backward=False)
    out, sb = _ret_scan(proj, p['ret_decay_logit'], s0, backward=True,
                        finish=(of, p['ret_norm_w'].reshape(1, RET_W)))
    st = jnp.stack([sf, sb], axis=1).reshape(N_SEQS, 2, RET_HEADS, RET_DK, RET_DV)
    return out, st


CV_TM = 256
CV_HALO = 16


def _conv_kernel(a_ref, g_ref, ap_ref, gp_ref, an_ref, gn_ref, w_ref, b_ref, lnw_ref, lnb_ref, o_ref, buf, sbuf):
    row0 = pl.program_id(0) * CV_TM
    pos0, len0 = _seq_pos(row0)
    glu = lambda a, g: a[...].astype(F32) * _sigmoid(g[...].astype(F32))
    buf[CV_HALO:CV_HALO + CV_TM, :] = glu(a_ref, g_ref)
    buf[0:CV_HALO, :] = jnp.where(pos0 == 0, 0.0, glu(ap_ref, gp_ref))
    buf[CV_HALO + CV_TM:, :] = jnp.where(pos0 + CV_TM == len0, 0.0, glu(an_ref, gn_ref))
    base = CV_HALO - CONV_K // 2
    acc = jnp.zeros((CV_TM, CV_W), F32)
    for r in range(8):
        taps = [m for m in range((base + CONV_K + 7) // 8) if 0 <= r + 8 * m - base < CONV_K]
        span = CV_TM + 8 * max(taps)
        if r:
            sbuf[0:span, :] = buf[r:r + span, :]
        src = sbuf if r else buf
        for m in taps:
            j = r + 8 * m - base
            acc = acc + w_ref[j:j + 1, :] * src[8 * m:8 * m + CV_TM, :]
    z = _standardize(acc + b_ref[...], EPS) * lnw_ref[...] + lnb_ref[...]
    o_ref[...] = (z * _sigmoid(z)).astype(BF16)


def _conv_branch(proj, p):
    n = proj.shape[0]
    nh = n // CV_HALO
    steps = CV_TM // CV_HALO
    ca = B_CV // CV_W
    prev = lambda i: jnp.maximum(i * steps - 1, 0)
    nxt = lambda i: jnp.minimum((i + 1) * steps, nh - 1)
    vec = pl.BlockSpec((1, CV_W), lambda i: (0, 0))
    return pl.pallas_call(
        _conv_kernel,
        grid=(n // CV_TM,),
        in_specs=[
            pl.BlockSpec((CV_TM, CV_W), lambda i: (i, ca)),
            pl.BlockSpec((CV_TM, CV_W), lambda i: (i, ca + 1)),
            pl.BlockSpec((CV_HALO, CV_W), lambda i: (prev(i), ca)),
            pl.BlockSpec((CV_HALO, CV_W), lambda i: (prev(i), ca + 1)),
            pl.BlockSpec((CV_HALO, CV_W), lambda i: (nxt(i), ca)),
            pl.BlockSpec((CV_HALO, CV_W), lambda i: (nxt(i), ca + 1)),
            pl.BlockSpec((CONV_K, CV_W), lambda i: (0, 0)),
            vec, vec, vec,
        ],
        out_specs=pl.BlockSpec((CV_TM, CV_W), lambda i: (i, 0)),
        out_shape=jax.ShapeDtypeStruct((n, CV_W), BF16),
        scratch_shapes=[pltpu.VMEM((CV_TM + 2 * CV_HALO, CV_W), F32)] * 2,
        compiler_params=_cparams(("parallel",)),
        name="conformer_conv",
    )(proj, proj, proj, proj, proj, proj, p['cv_dw_w'], p['cv_dw_b'].reshape(1, CV_W),
      p['cv_ln_w'].reshape(1, CV_W), p['cv_ln_b'].reshape(1, CV_W))


MG_TM = 512


def _merge_kernel(da_ref, rw_ref, ret_ref, cv_ref, g0_ref, g1_ref, g2_ref, g3_ref, x_ref, mod_ref, wb_ref, wo_ref,
                  nw_ref, x_o, h_o):
    m = None
    for n, (br, gt) in enumerate(((da_ref, g0_ref), (rw_ref, g1_ref), (ret_ref, g2_ref), (cv_ref, g3_ref))):
        t = _sigmoid(gt[...].astype(F32)) * _dg(br[...], wb_ref[n], NN)
        m = t if m is None else m + t
    out = _dg(m.astype(BF16), wo_ref[...], NN)
    mrow = _mod_row(mod_ref, pl.program_id(0) * MG_TM)
    gate1 = mrow[:, 2 * D_MODEL:3 * D_MODEL]
    sh2 = mrow[:, 3 * D_MODEL:4 * D_MODEL]
    sc2 = mrow[:, 4 * D_MODEL:5 * D_MODEL]
    x1 = x_ref[...] + gate1 * out
    x_o[...] = x1
    h_o[...] = _rms(x1, nw_ref[...]) * (1.0 + sc2) + sh2


def _merge(branches, gates, x, mod, w_branch, w_out, norm_w):
    n = x.shape[0]
    br = pl.BlockSpec((MG_TM, BR_W), lambda i: (i, 0))
    gspec = lambda j: pl.BlockSpec((MG_TM, D_MODEL), lambda i: (i, B_GATE // D_MODEL + j))
    full = pl.BlockSpec((MG_TM, D_MODEL), lambda i: (i, 0))
    out = jax.ShapeDtypeStruct((n, D_MODEL), F32)
    return pl.pallas_call(
        _merge_kernel,
        grid=(n // MG_TM,),
        in_specs=[br, br, br, br, gspec(0), gspec(1), gspec(2), gspec(3), full,
                  pl.BlockSpec((MOD_ROWS, 6 * D_MODEL), lambda i: (0, 0)),
                  pl.BlockSpec((N_BRANCH, BR_W, D_MODEL), lambda i: (0, 0, 0)),
                  pl.BlockSpec((D_MODEL, D_MODEL), lambda i: (0, 0)),
                  pl.BlockSpec((1, D_MODEL), lambda i: (0, 0))],
        out_specs=[full, full],
        out_shape=[out, out],
        compiler_params=_cparams(("parallel",)),
        name="gated_merge",
    )(*branches, gates, gates, gates, gates, x, mod, w_branch.astype(BF16), w_out.astype(BF16),
      norm_w.reshape(1, D_MODEL))


RT_TM = 256
MOE_BM = 512
MOE_ROWS = N_TOK * TOP_K + N_EXPERTS * MOE_BM
DP_TM = 512
DP_GROUP = 64
CB_TM = 256
DMA_UNROLL = 8


def _router_kernel(h_ref, w_ref, b_ref, tri_ref, idx_o, gate_o, rank_o, cnt_o, carry):
    @pl.when(pl.program_id(0) == 0)
    def _():
        carry[...] = jnp.zeros_like(carry)

    logits = _dot3(w_ref[...], h_ref[...], NT) + b_ref[...]
    e_iota = lax.broadcasted_iota(jnp.int32, logits.shape, 0)
    work = logits
    vals, idxs, hots = [], [], []
    for _ in range(TOP_K):
        mx = jnp.max(work, axis=0, keepdims=True)
        ix = jnp.min(jnp.where(work == mx, e_iota, N_EXPERTS), axis=0, keepdims=True)
        hot = e_iota == ix
        vals.append(mx)
        idxs.append(ix)
        hots.append(hot.astype(F32))
        work = jnp.where(hot, -jnp.inf, work)
    es = [jnp.exp(v - vals[0]) for v in vals]
    inv = 1.0 / (es[0] + es[1] + es[2] + es[3])
    chosen = hots[0] + hots[1] + hots[2] + hots[3]
    ahead = carry[...][:, 0:1] + _dg(chosen.astype(BF16), tri_ref[...], NN)
    idx_o[...] = jnp.concatenate(idxs, axis=0)
    gate_o[...] = jnp.concatenate([e * inv for e in es], axis=0)
    rank_o[...] = jnp.concatenate(
        [jnp.sum(hot * ahead, axis=0, keepdims=True) for hot in hots], axis=0).astype(jnp.int32)
    carry[...] = carry[...] + jnp.sum(chosen, axis=1, keepdims=True)
    cnt_o[...] = carry[...]


def _router(h, router_w, router_b):
    n = h.shape[0]
    tri = (jnp.arange(RT_TM)[:, None] < jnp.arange(RT_TM)[None, :]).astype(BF16)
    col = pl.BlockSpec((TOP_K, RT_TM), lambda i: (0, i))
    return pl.pallas_call(
        _router_kernel,
        grid=(n // RT_TM,),
        in_specs=[
            pl.BlockSpec((RT_TM, D_MODEL), lambda i: (i, 0)),
            pl.BlockSpec((N_EXPERTS, D_MODEL), lambda i: (0, 0)),
            pl.BlockSpec((N_EXPERTS, 1), lambda i: (0, 0)),
            pl.BlockSpec((RT_TM, RT_TM), lambda i: (0, 0)),
        ],
        out_specs=[col, col, col, pl.BlockSpec((N_EXPERTS, 128), lambda i: (0, 0))],
        out_shape=[jax.ShapeDtypeStruct((TOP_K, n), jnp.int32), jax.ShapeDtypeStruct((TOP_K, n), F32),
                   jax.ShapeDtypeStruct((TOP_K, n), jnp.int32), jax.ShapeDtypeStruct((N_EXPERTS, 128), F32)],
        scratch_shapes=[pltpu.VMEM((N_EXPERTS, 128), F32)],
        compiler_params=_cparams(("arbitrary",)),
        name="router",
    )(h, router_w.T, router_b.reshape(N_EXPERTS, 1), tri)


def _tile_major(t, tm):
    k, n = t.shape
    return t.reshape(k, n // tm, tm).transpose(1, 0, 2).reshape(n // tm, 1, k * tm)


def _dispatch_kernel(dest_ref, pe_ref, h_ref, o_hbm, zbuf, sem, zsem):
    n_groups = DP_TM // DP_GROUP

    @pl.when(pl.program_id(0) == 0)
    def _():
        zbuf[...] = jnp.zeros_like(zbuf)

        def fill(e):
            end = pe_ref[e]
            begin = pe_ref[e - 1] if e else 0
            return end > begin, pltpu.make_async_copy(
                zbuf, o_hbm.at[pl.ds(pl.multiple_of(jnp.maximum(end - MOE_BM, 0), MOE_BM), MOE_BM)], zsem.at[0])

        for e in range(N_EXPERTS):
            nonempty, cp = fill(e)
            pl.when(nonempty)(cp.start)
        for e in range(N_EXPERTS):
            nonempty, cp = fill(e)
            pl.when(nonempty)(cp.wait)

        def tail(b):
            return pltpu.make_async_copy(zbuf, o_hbm.at[pl.ds(pl.multiple_of(b * MOE_BM, MOE_BM), MOE_BM)],
                                         zsem.at[0])

        first_unused = pe_ref[N_EXPERTS - 1] // MOE_BM
        lax.fori_loop(first_unused, MOE_ROWS // MOE_BM, lambda b, c: (tail(b).start(), c)[1], 0)
        lax.fori_loop(first_unused, MOE_ROWS // MOE_BM, lambda b, c: (tail(b).wait(), c)[1], 0)

    def wait_group(slot):
        pltpu.make_async_copy(h_ref.at[pl.ds(0, TOP_K * DP_GROUP)], o_hbm.at[pl.ds(0, TOP_K * DP_GROUP)],
                              sem.at[slot]).wait()

    def group(gi, carry):
        slot = gi % 2

        def issue(t, c):
            tok = gi * DP_GROUP + t
            for k in range(TOP_K):
                dst = dest_ref[0, 0, k * DP_TM + tok]
                pltpu.make_async_copy(h_ref.at[pl.ds(tok, 1)], o_hbm.at[pl.ds(dst, 1)], sem.at[slot]).start()
            return c

        lax.fori_loop(0, DP_GROUP, issue, 0, unroll=DMA_UNROLL)

        @pl.when(gi > 0)
        def _():
            wait_group(1 - slot)

        return carry

    lax.fori_loop(0, n_groups, group, 0)
    wait_group((n_groups - 1) % 2)


def _dispatch(h, dest, pad_end):
    n = h.shape[0]
    return pl.pallas_call(
        _dispatch_kernel,
        grid=(n // DP_TM,),
        in_specs=[
            pl.BlockSpec((1, 1, TOP_K * DP_TM), lambda i: (i, 0, 0), memory_space=pltpu.SMEM),
            pl.BlockSpec(memory_space=pltpu.SMEM),
            pl.BlockSpec((DP_TM, D_MODEL), lambda i: (i, 0)),
        ],
        out_specs=pl.BlockSpec(memory_space=pl.ANY),
        out_shape=jax.ShapeDtypeStruct((MOE_ROWS, D_MODEL), F32),
        scratch_shapes=[pltpu.VMEM((MOE_BM, D_MODEL), F32), pltpu.SemaphoreType.DMA((2,)),
                        pltpu.SemaphoreType.DMA((1,))],
        compiler_params=_cparams(("arbitrary",)),
        name="moe_dispatch",
    )(_tile_major(dest, DP_TM), pad_end, h)


def _expert_kernel(bx_ref, be_ref, nv_ref, x_ref, w1_ref, b1_ref, w2_ref, b2_ref, o_ref, w1b, w2b):
    i = pl.program_id(0)
    changed = jnp.logical_or(i == 0, be_ref[i] != be_ref[jnp.maximum(i - 1, 0)])

    @pl.when(changed)
    def _():
        w1b[...] = w1_ref[0, 0].astype(BF16)
        w2b[...] = w2_ref[0, 0].astype(BF16)

    def ffn(rows):
        hb = _dg(x_ref[0:rows, :].astype(BF16), w1b[...], NN) + b1_ref[0, 0]
        hg = jnp.minimum(hb[:, :D_FF], SWIGLU_LIMIT)
        hu = jnp.clip(hb[:, D_FF:], -SWIGLU_LIMIT, SWIGLU_LIMIT)
        act = hg * _sigmoid(SWIGLU_ALPHA * hg) * (hu + 1.0)
        o_ref[0:rows, :] = _dg(act.astype(BF16), w2b[...], NN) + b2_ref[0, 0]

    nv = nv_ref[i]
    half = MOE_BM // 2

    @pl.when(nv > half)
    def _():
        ffn(MOE_BM)

    @pl.when(jnp.logical_and(nv > 0, nv <= half))
    def _():
        ffn(half)
        o_ref[half:, :] = jnp.zeros((MOE_BM - half, D_MODEL), F32)

    @pl.when(nv == 0)
    def _():
        o_ref[...] = jnp.zeros_like(o_ref)


def _experts(x_rows, blk_x, blk_e, n_valid, layer, w1, b1, w2, b2):
    nb = MOE_ROWS // MOE_BM
    grid_spec = pltpu.PrefetchScalarGridSpec(
        num_scalar_prefetch=3,
        grid=(nb,),
        in_specs=[
            pl.BlockSpec((MOE_BM, D_MODEL), lambda i, bx, be, nv: (bx[i], 0)),
            pl.BlockSpec((1, 1, D_MODEL, 2 * D_FF), lambda i, bx, be, nv: (layer, be[i], 0, 0)),
            pl.BlockSpec((1, 1, 1, 2 * D_FF), lambda i, bx, be, nv: (layer, be[i], 0, 0)),
            pl.BlockSpec((1, 1, D_FF, D_MODEL), lambda i, bx, be, nv: (layer, be[i], 0, 0)),
            pl.BlockSpec((1, 1, 1, D_MODEL), lambda i, bx, be, nv: (layer, be[i], 0, 0)),
        ],
        out_specs=pl.BlockSpec((MOE_BM, D_MODEL), lambda i, bx, be, nv: (i, 0)),
        scratch_shapes=[pltpu.VMEM((D_MODEL, 2 * D_FF), BF16), pltpu.VMEM((D_FF, D_MODEL), BF16)],
    )
    return pl.pallas_call(
        _expert_kernel,
        grid_spec=grid_spec,
        out_shape=jax.ShapeDtypeStruct((MOE_ROWS, D_MODEL), F32),
        compiler_params=pltpu.CompilerParams(dimension_semantics=("arbitrary",),
                                             vmem_limit_bytes=EXPERT_VMEM_LIMIT),
        name="moe_experts",
    )(blk_x, blk_e, n_valid, x_rows, w1, b1.reshape(DEPTH, N_EXPERTS, 1, 2 * D_FF), w2,
      b2.reshape(DEPTH, N_EXPERTS, 1, D_MODEL))


def _combine_kernel(dest_ref, gate_ref, x_ref, mod_ref, fw_ref, y_hbm, *rest, final):
    if final:
        o_ctx_ref, o_lat_ref, buf, sem = rest
    else:
        o_ref, buf, sem = rest

    def issue(t, c):
        for k in range(TOP_K):
            dst = dest_ref[0, 0, k * CB_TM + t]
            pltpu.make_async_copy(y_hbm.at[pl.ds(dst, 1)], buf.at[k, pl.ds(t, 1)], sem.at[0]).start()
        return c

    lax.fori_loop(0, CB_TM, issue, 0, unroll=DMA_UNROLL)
    for k in range(TOP_K):
        pltpu.make_async_copy(y_hbm.at[pl.ds(0, CB_TM)], buf.at[k], sem.at[0]).wait()
    g = gate_ref[...]
    acc = g[:, 0:1] * buf[0]
    for k in range(1, TOP_K):
        acc = acc + g[:, k:k + 1] * buf[k]
    gate2 = _mod_row(mod_ref, pl.program_id(0) * CB_TM)[:, 5 * D_MODEL:6 * D_MODEL]
    x2 = x_ref[...] + gate2 * acc
    if not final:
        o_ref[...] = x2
        return
    y = _rms(x2, fw_ref[...])
    in_ctx = pl.program_id(0) < N_CTX // CB_TM

    @pl.when(in_ctx)
    def _():
        o_ctx_ref[...] = y

    @pl.when(jnp.logical_not(in_ctx))
    def _():
        o_lat_ref[...] = y


def _combine(y_rows, dest, gates, x, mod, final_w, final):
    n = x.shape[0]
    full = pl.BlockSpec((CB_TM, D_MODEL), lambda i: (i, 0))
    nct = N_CTX // CB_TM
    if final:
        out_specs = [pl.BlockSpec((CB_TM, D_MODEL), lambda i: (jnp.minimum(i, nct - 1), 0)),
                     pl.BlockSpec((CB_TM, D_MODEL), lambda i: (jnp.maximum(i - nct, 0), 0))]
        out_shape = [jax.ShapeDtypeStruct((N_CTX, D_MODEL), F32), jax.ShapeDtypeStruct((N_LAT, D_MODEL), F32)]
    else:
        out_specs, out_shape = full, jax.ShapeDtypeStruct((n, D_MODEL), F32)
    return pl.pallas_call(
        functools.partial(_combine_kernel, final=final),
        grid=(n // CB_TM,),
        in_specs=[
            pl.BlockSpec((1, 1, TOP_K * CB_TM), lambda i: (i, 0, 0), memory_space=pltpu.SMEM),
            pl.BlockSpec((CB_TM, TOP_K), lambda i: (i, 0)),
            full,
            pl.BlockSpec((MOD_ROWS, 6 * D_MODEL), lambda i: (0, 0)),
            pl.BlockSpec((1, D_MODEL), lambda i: (0, 0)),
            pl.BlockSpec(memory_space=pl.ANY),
        ],
        out_specs=out_specs,
        out_shape=out_shape,
        scratch_shapes=[pltpu.VMEM((TOP_K, CB_TM, D_MODEL), F32), pltpu.SemaphoreType.DMA((1,))],
        compiler_params=_cparams(("arbitrary",)),
        name="moe_combine",
    )(_tile_major(dest, CB_TM), gates.T, x, mod, final_w.reshape(1, D_MODEL), y_rows)


def _routed_ffn(h, x, mod, p, moe, layer, final_w, final):
    idx, gates, rank, counts = _router(h, p['router_w'], p['router_b'])
    counts = counts[:, 0].astype(jnp.int32)
    padded = (counts + MOE_BM - 1) // MOE_BM * MOE_BM
    pad_end = jnp.cumsum(padded)
    pad_start = pad_end - padded
    experts = jnp.arange(N_EXPERTS, dtype=jnp.int32)
    start_of = jnp.sum(jnp.where(idx[:, :, None] == experts, pad_start, 0), axis=-1)
    dest = start_of + rank
    nb = MOE_ROWS // MOE_BM
    first_row = jnp.arange(nb, dtype=jnp.int32) * MOE_BM
    blk_e = jnp.minimum(jnp.sum((pad_end[None, :] <= first_row[:, None]).astype(jnp.int32), axis=1),
                        N_EXPERTS - 1)
    is_e = blk_e[:, None] == experts[None, :]
    end_of = jnp.sum(jnp.where(is_e, pad_start + counts, 0), axis=1)
    n_valid = jnp.clip(end_of - first_row, 0, MOE_BM).astype(jnp.int32)
    blk_x = jnp.minimum(jnp.arange(nb, dtype=jnp.int32), pad_end[-1] // MOE_BM - 1)
    x_rows = _dispatch(h, dest, pad_end)
    y_rows = _experts(x_rows, blk_x, blk_e, n_valid, layer, *moe)
    return _combine(y_rows, dest, gates, x, mod, final_w, final)


_LAYER_PARAMS = ('norm_mix_w', 'norm_ffn_w', 'da_lambda', 'da_norm_w', 'rw_shift', 'rw_w0', 'rw_w_up', 'rw_a0',
                 'rw_a_up', 'rw_g_up', 'rw_k_k', 'rw_k_a', 'rw_r_k', 'rw_ln_w', 'rw_ln_b', 'ret_decay_logit',
                 'ret_norm_w', 'cv_dw_w', 'cv_dw_b', 'cv_ln_w', 'cv_ln_b', 'w_branch', 'w_out', 'router_w',
                 'router_b')


def _check_layout():
    assert SEQ & (SEQ - 1) == 0 and DEC_SEQ & (DEC_SEQ - 1) == 0, "sequence positions are bit masks"
    assert N_CTX % DEC_SEQ == 0, "latent sequences start on a DEC_SEQ boundary of the stream"
    assert N_CTX % MOD_GROUP == 0 and DEC_SEQ % MOD_GROUP == 0 and N_CTX + DEC_BATCH * DEC_SEQ == N_TOK
    for tile in (RW_TM, RW_TB, RET_TB, CV_TM, DA_TQ):
        assert SEQ % tile == 0 and DEC_SEQ % tile == 0
    for tile in (IN_TM, MG_TM, CB_TM):
        assert MOD_GROUP % tile == 0
    assert N_TOK % QK_TM == 0 and N_CTX % QK_TM == 0 and N_TOK % RT_TM == 0 and N_TOK % DP_TM == 0
    assert DP_TM % DP_GROUP == 0 and DP_GROUP % DMA_UNROLL == 0 and CB_TM % DMA_UNROLL == 0


def _layer(x, mod, w_p, p, moe, layer, lam_init, caches, tables, final_w, final):
    cache_k, cache_v, state_rw, state_ret = caches
    proj_rw, proj = _input_projection(x, mod, p['norm_mix_w'], w_p, layer)
    o_da = _da_branch(proj, cache_k, cache_v, p['da_lambda'], p['da_norm_w'], lam_init, tables)
    o_rw, rw_state = _rwkv_branch(proj_rw, state_rw, p)
    o_ret, ret_state = _retention_branch(proj, state_ret, p)
    o_cv = _conv_branch(proj, p)
    x1, h2 = _merge((o_da, o_rw, o_ret, o_cv), proj, x, mod, p['w_branch'], p['w_out'], p['norm_ffn_w'])
    x2 = _routed_ffn(h2, x1, mod, p, moe, layer, final_w, final)
    ctx_kv = proj[:N_CTX, B_DA + DA_W:B_DA + 3 * DA_W].astype(F32)
    new_k = ctx_kv[:, :DA_W].reshape(BATCH, SEQ, DA_HEADS, 2, DA_QK)
    new_v = ctx_kv[:, DA_W:].reshape(BATCH, SEQ, DA_HEADS, DA_V)
    return x2, (new_k, new_v, rw_state[:BATCH], ret_state[:BATCH])


def kernel(x_prompt, x_sample, c, cache_da_k, cache_da_v, state_rwkv, state_ret, c_ctx, ada_w, ada_b, norm_mix_w,
           norm_ffn_w, w_in, da_lambda, da_norm_w, rw_shift, rw_w0, rw_w_up, rw_a0, rw_a_up, rw_g_up, rw_k_k,
           rw_k_a, rw_r_k, rw_ln_w, rw_ln_b, ret_decay_logit, ret_norm_w, cv_dw_w, cv_dw_b, cv_ln_w, cv_ln_b,
           w_branch, w_out, router_w, router_b, moe_w1, moe_b1, moe_w2, moe_b2, final_norm_w):
    _check_layout()
    weights = dict(norm_mix_w=norm_mix_w, norm_ffn_w=norm_ffn_w, da_lambda=da_lambda, da_norm_w=da_norm_w,
                   rw_shift=rw_shift, rw_w0=rw_w0, rw_w_up=rw_w_up, rw_a0=rw_a0, rw_a_up=rw_a_up, rw_g_up=rw_g_up,
                   rw_k_k=rw_k_k, rw_k_a=rw_k_a, rw_r_k=rw_r_k, rw_ln_w=rw_ln_w, rw_ln_b=rw_ln_b,
                   ret_decay_logit=ret_decay_logit, ret_norm_w=ret_norm_w, cv_dw_w=cv_dw_w, cv_dw_b=cv_dw_b,
                   cv_ln_w=cv_ln_w, cv_ln_b=cv_ln_b, w_branch=w_branch, w_out=w_out, router_w=router_w,
                   router_b=router_b)
    x = jnp.concatenate([x_prompt.reshape(N_CTX, D_MODEL), x_sample.reshape(N_LAT, D_MODEL)], axis=0)
    cvec = jnp.concatenate([c_ctx[None, :], c, jnp.zeros((MOD_ROWS - 1 - DEC_BATCH, D_MODEL), F32)], axis=0)
    mod = _modulation(cvec, ada_w, ada_b)
    w_p = _pad_w_in(w_in).astype(BF16)
    tables = _rope_tables()
    outs = []
    moe = (moe_w1, moe_b1, moe_w2, moe_b2)
    for i in range(DEPTH):
        p = {name: weights[name][i] for name in _LAYER_PARAMS}
        lam_init = 0.8 - 0.6 * math.exp(-0.3 * i)
        caches = (cache_da_k[:, i], cache_da_v[:, i], state_rwkv[:, i], state_ret[:, i])
        x, ctx_out = _layer(x, mod[i], w_p, p, moe, i, lam_init, caches, tables, final_norm_w, i == DEPTH - 1)
        outs.append(ctx_out)
    y_prompt = x[0].reshape(BATCH, SEQ, D_MODEL)
    y_sample = x[1].reshape(DEC_BATCH, DEC_SEQ, D_MODEL)
    stack = lambda j: jnp.stack([o[j] for o in outs], axis=1)
    return (y_prompt, y_sample, stack(0), stack(1), stack(2), stack(3))


def _pad_w_in(w_in):
    da, rw, ret, cv, gate = jnp.split(w_in, [1536, 3456, 4992, 6016], axis=-1)
    pad = jnp.zeros(w_in.shape[:-1] + (P_DA - RW_COLS,), w_in.dtype)
    return jnp.concatenate([rw, pad, da, ret, cv, gate], axis=-1)
```

```python
import functools
import math

import jax
import jax.numpy as jnp
from jax import lax
from jax.experimental import pallas as pl
from jax.experimental.pallas import tpu as pltpu

F32 = jnp.float32
BF16 = jnp.bfloat16

D_MODEL = 1024
BATCH = 16
SEQ = 256
DEPTH = 2
DEC_BATCH = 2
DEC_SEQ = 4096
PAST_LEN = 512
GRID_W = 64
EPS = 1e-6

DA_HEADS = 4
DA_QK = 64
DA_V = 128
ROPE_BASE = 10000.0

RW_HEADS = 8
RW_HD = 64
RW_W = 512
RW_G_LORA = 128
RW_GN_EPS = 64e-5

RET_HEADS = 4
RET_DK = 64
RET_DV = 128
RET_CHUNK = 128

CV_W = 512
CONV_K = 31
BR_W = 512
N_BRANCH = 4

N_EXPERTS = 32
TOP_K = 4
D_FF = 1024
SWIGLU_LIMIT = 7.0
SWIGLU_ALPHA = 1.702

N_CTX = BATCH * SEQ
N_LAT = DEC_BATCH * DEC_SEQ
N_TOK = N_CTX + N_LAT
N_SEQS = BATCH + DEC_BATCH
MOD_ROWS = 8
MOD_GROUP = 4096

RW_COLS = 1920
P_RW = 0
P_DA = 2048
P_RET = 3584
P_CV = 5120
P_GATE = 6144
P_COLS = 10240
B_DA, B_RET, B_CV, B_GATE = (p - P_DA for p in (P_DA, P_RET, P_CV, P_GATE))

VMEM_LIMIT = 48 * 1024 * 1024
EXPERT_VMEM_LIMIT = 56 * 1024 * 1024


def _cparams(sem):
    return pltpu.CompilerParams(dimension_semantics=sem, vmem_limit_bytes=VMEM_LIMIT)


def _dg(a, b, dims):
    return lax.dot_general(a, b, (dims, ((), ())), preferred_element_type=F32)


NN = ((1,), (0,))
NT = ((1,), (1,))
TN = ((0,), (0,))


def _dot1(a, b, dims=NN):
    return _dg(a.astype(BF16), b.astype(BF16), dims)


def _split(x):
    hi = x.astype(BF16)
    lo = (x - hi.astype(F32)).astype(BF16)
    return hi, lo


def _dot3(a, b, dims=NN):
    ah, al = _split(a)
    bh, bl = _split(b)
    return _dg(ah, bh, dims) + (_dg(ah, bl, dims) + _dg(al, bh, dims))


def _dot2x(a, e, dims=NN):
    ah, al = _split(a)
    am = (a - ah.astype(F32) - al.astype(F32)).astype(BF16)
    eb = e.astype(BF16)
    return _dg(ah, eb, dims) + (_dg(al, eb, dims) + _dg(am, eb, dims))


def _sigmoid(x):
    return 1.0 / (1.0 + jnp.exp(-x))


def _softplus(x):
    return jnp.maximum(x, 0.0) + jnp.log(1.0 + jnp.exp(-jnp.abs(x)))


def _seq_pos(row):
    in_ctx = row < N_CTX
    pos = jnp.where(in_ctx, row & (SEQ - 1), (row - N_CTX) & (DEC_SEQ - 1))
    length = jnp.where(in_ctx, SEQ, DEC_SEQ)
    return pos, length


def _seq_index(row):
    return jnp.where(row < N_CTX, row // SEQ, BATCH + (row - N_CTX) // DEC_SEQ)


def _dotx(e, b, dims=NN):
    bh, bl = _split(b)
    bm = (b - bh.astype(F32) - bl.astype(F32)).astype(BF16)
    eb = e.astype(BF16)
    return _dg(eb, bh, dims) + (_dg(eb, bl, dims) + _dg(eb, bm, dims))


def _mod_kernel(c_ref, w_ref, b_ref, o_ref):
    c = c_ref[...]
    s = c * _sigmoid(c)
    o_ref[0] = _dot3(s, w_ref[0]) + b_ref[0]


def _modulation(cvec, ada_w, ada_b):
    tn = 1536
    return pl.pallas_call(
        _mod_kernel,
        grid=(DEPTH, 6 * D_MODEL // tn),
        in_specs=[
            pl.BlockSpec((MOD_ROWS, D_MODEL), lambda l, j: (0, 0)),
            pl.BlockSpec((1, D_MODEL, tn), lambda l, j: (l, 0, j)),
            pl.BlockSpec((1, 1, tn), lambda l, j: (l, 0, j)),
        ],
        out_specs=pl.BlockSpec((1, MOD_ROWS, tn), lambda l, j: (l, 0, j)),
        out_shape=jax.ShapeDtypeStruct((DEPTH, MOD_ROWS, 6 * D_MODEL), F32),
        compiler_params=_cparams(("parallel", "parallel")),
        name="modulation",
    )(cvec, ada_w, ada_b.reshape(DEPTH, 1, 6 * D_MODEL))


def _mod_row(mod_ref, first_row):
    g = first_row // MOD_GROUP
    return mod_ref[pl.ds(g, 1), :]


def _rms(x, w):
    return x * lax.rsqrt(jnp.mean(x * x, axis=-1, keepdims=True) + EPS) * w


IN_TM = 1024
IN_TN = 1024


IN_NA = P_DA // IN_TN


def _inproj_kernel(x_ref, mod_ref, nw_ref, w_ref, oa_ref, og_ref, h_ref):
    i = pl.program_id(0)
    j = pl.program_id(1)

    @pl.when(j == 0)
    def _():
        m = _mod_row(mod_ref, i * IN_TM)
        sh = m[:, 0:D_MODEL]
        sc = m[:, D_MODEL:2 * D_MODEL]
        h_ref[...] = (_rms(x_ref[...], nw_ref[...]) * (1.0 + sc) + sh).astype(BF16)

    acc = _dg(h_ref[...], w_ref[0].astype(BF16), NN)

    @pl.when(j < IN_NA)
    def _():
        oa_ref[...] = acc

    @pl.when(j >= IN_NA)
    def _():
        og_ref[...] = acc.astype(BF16)


def _input_projection(x, mod, norm_w, w_p, layer):
    n = x.shape[0]
    return pl.pallas_call(
        _inproj_kernel,
        grid=(n // IN_TM, P_COLS // IN_TN),
        in_specs=[
            pl.BlockSpec((IN_TM, D_MODEL), lambda i, j: (i, 0)),
            pl.BlockSpec((MOD_ROWS, 6 * D_MODEL), lambda i, j: (0, 0)),
            pl.BlockSpec((1, D_MODEL), lambda i, j: (0, 0)),
            pl.BlockSpec((1, D_MODEL, IN_TN), lambda i, j: (layer, 0, j)),
        ],
        out_specs=[pl.BlockSpec((IN_TM, IN_TN), lambda i, j: (i, jnp.minimum(j, IN_NA - 1))),
                   pl.BlockSpec((IN_TM, IN_TN), lambda i, j: (i, jnp.maximum(j - IN_NA, 0)))],
        out_shape=[jax.ShapeDtypeStruct((n, P_DA), F32), jax.ShapeDtypeStruct((n, P_COLS - P_DA), BF16)],
        scratch_shapes=[pltpu.VMEM((IN_TM, D_MODEL), BF16)],
        compiler_params=_cparams(("parallel", "arbitrary")),
        name="input_projection",
    )(x, mod, norm_w.reshape(1, D_MODEL), w_p)


QK_TM = 512
DA_W = DA_HEADS * 2 * DA_QK


def _qkprep_kernel(q_ref, k_ref, v_ref, c_ref, se_ref, so_ref, qo_ref, ko_ref, vo_ref):
    i = pl.program_id(0)
    for h in range(DA_HEADS):
        vo_ref[:, h * 2 * DA_V:h * 2 * DA_V + DA_V] = v_ref[:, h * DA_V:(h + 1) * DA_V].astype(BF16)
        vo_ref[:, h * 2 * DA_V + DA_V:(h + 1) * 2 * DA_V] = jnp.ones((QK_TM, DA_V), BF16)
    scale = DA_QK ** -0.5

    @pl.when(i * QK_TM < N_CTX)
    def _():
        qo_ref[...] = (q_ref[...].astype(F32) * scale).astype(BF16)
        ko_ref[...] = k_ref[...].astype(BF16)

    @pl.when(i * QK_TM >= N_CTX)
    def _():
        c = c_ref[...]
        se = se_ref[...]
        so = so_ref[...]

        def rope(x):
            nxt = pltpu.roll(x, DA_W - 1, axis=1)
            prv = pltpu.roll(x, 1, axis=1)
            return x * c + nxt * se + prv * so

        qo_ref[...] = (rope(q_ref[...].astype(F32)) * scale).astype(BF16)
        ko_ref[...] = rope(k_ref[...].astype(F32)).astype(BF16)


def _rope_tables():
    rows = DEC_SEQ // GRID_W
    row = jnp.repeat(jnp.arange(rows, dtype=F32), GRID_W)
    col = jnp.tile(jnp.arange(GRID_W, dtype=F32), rows)
    n_pairs = DA_QK // 4
    inv = ROPE_BASE ** (-jnp.arange(n_pairs, dtype=F32) / n_pairs)
    ang = jnp.concatenate([row[:, None] * inv, col[:, None] * inv], axis=-1)
    cos = jnp.repeat(jnp.cos(ang), 2, axis=-1)
    sin = jnp.repeat(jnp.sin(ang), 2, axis=-1)
    even = (jnp.arange(DA_QK) % 2 == 0)[None, :]
    s_even = jnp.where(even, -sin, 0.0)
    s_odd = jnp.where(even, 0.0, sin)
    rep = lambda t: jnp.tile(t, (1, DA_W // DA_QK))
    return rep(cos), rep(s_even), rep(s_odd)


def _qk_prepare(proj, tables):
    n = proj.shape[0]
    lat0 = N_CTX // QK_TM
    nlat = DEC_SEQ // QK_TM
    tab = pl.BlockSpec((QK_TM, DA_W), lambda i: (jnp.maximum(i - lat0, 0) % nlat, 0))
    c0 = B_DA // DA_W
    out = jax.ShapeDtypeStruct((n, DA_W), BF16)
    return pl.pallas_call(
        _qkprep_kernel,
        grid=(n // QK_TM,),
        in_specs=[
            pl.BlockSpec((QK_TM, DA_W), lambda i: (i, c0)),
            pl.BlockSpec((QK_TM, DA_W), lambda i: (i, c0 + 1)),
            pl.BlockSpec((QK_TM, DA_W), lambda i: (i, c0 + 2)),
            tab, tab, tab,
        ],
        out_specs=[pl.BlockSpec((QK_TM, DA_W), lambda i: (i, 0))] * 2
        + [pl.BlockSpec((QK_TM, 2 * DA_W), lambda i: (i, 0))],
        out_shape=[out, out, jax.ShapeDtypeStruct((n, 2 * DA_W), BF16)],
        compiler_params=_cparams(("parallel",)),
        name="qk_prepare",
    )(proj, proj, proj, *tables)


DA_TQ = 256


def _attend(q, segments, lam, nw, o_ref):
    for h in range(DA_HEADS):
        os = []
        for m in range(2):
            c0 = (2 * h + m) * DA_QK
            ss = [_dg(q[:, c0:c0 + DA_QK], k[:, c0:c0 + DA_QK], NT) for k, _ in segments]
            mx = jnp.max(ss[0], axis=-1, keepdims=True)
            for s in ss[1:]:
                mx = jnp.maximum(mx, jnp.max(s, axis=-1, keepdims=True))
            oe = None
            for s, (_, v) in zip(ss, segments):
                part = _dg(jnp.exp(s - mx).astype(BF16), v[:, h * 2 * DA_V:(h + 1) * 2 * DA_V], NN)
                oe = part if oe is None else oe + part
            os.append(oe[:, :DA_V] * (1.0 / oe[:, DA_V:DA_V + 1]))
        o_ref[:, h * DA_V:(h + 1) * DA_V] = _rms(os[0] - lam * os[1], nw).astype(BF16)


def _da_kernel(q_ref, ks_ref, vs_ref, kl_ref, vl_ref, dl_ref, nw_ref, o_ref, *, lam_init):
    dl = dl_ref[...]
    lam = (jnp.exp(jnp.sum(dl[0:1] * dl[1:2], axis=1, keepdims=True))
           - jnp.exp(jnp.sum(dl[2:3] * dl[3:4], axis=1, keepdims=True)) + lam_init)
    nw = nw_ref[...] * (1.0 - lam_init)
    q = q_ref[...]
    is_ctx = pl.program_id(0) < BATCH * (SEQ // DA_TQ)

    @pl.when(is_ctx)
    def _():
        _attend(q, [(ks_ref, vs_ref)], lam, nw, o_ref)

    @pl.when(jnp.logical_not(is_ctx))
    def _():
        _attend(q, [(kl_ref.at[0], vl_ref.at[0])], lam, nw, o_ref)


def _da_branch(proj, cache_k, cache_v, da_lambda, da_norm_w, lam_init, tables):
    qb, kb, vb = _qk_prepare(proj, tables)
    n = proj.shape[0]
    ck = cache_k.reshape(DEC_BATCH, PAST_LEN, DA_W).astype(BF16)
    cv = cache_v.astype(BF16)
    cv = jnp.concatenate([cv, jnp.ones_like(cv)], axis=-1).reshape(DEC_BATCH, PAST_LEN, 2 * DA_W)
    lat = lambda t: t[N_CTX:].reshape(DEC_BATCH, DEC_SEQ, t.shape[-1])
    k_all = jnp.concatenate([ck, lat(kb)], axis=1)
    v_all = jnp.concatenate([cv, lat(vb)], axis=1)
    tk = PAST_LEN + DEC_SEQ
    n_ctx_tiles = N_CTX // DA_TQ
    per_seq = DEC_SEQ // DA_TQ
    seq_c = lambda i: jnp.minimum(i * DA_TQ // SEQ, BATCH - 1)
    seq_l = lambda i: jnp.clip((i - n_ctx_tiles) // per_seq, 0, DEC_BATCH - 1)
    once = pl.Buffered(1)
    return pl.pallas_call(
        functools.partial(_da_kernel, lam_init=lam_init),
        grid=(n // DA_TQ,),
        in_specs=[
            pl.BlockSpec((DA_TQ, DA_W), lambda i: (i, 0)),
            pl.BlockSpec((SEQ, DA_W), lambda i: (seq_c(i), 0)),
            pl.BlockSpec((SEQ, 2 * DA_W), lambda i: (seq_c(i), 0)),
            pl.BlockSpec((1, tk, DA_W), lambda i: (seq_l(i), 0, 0), pipeline_mode=once),
            pl.BlockSpec((1, tk, 2 * DA_W), lambda i: (seq_l(i), 0, 0), pipeline_mode=once),
            pl.BlockSpec((4, DA_QK), lambda i: (0, 0)),
            pl.BlockSpec((1, DA_V), lambda i: (0, 0)),
        ],
        out_specs=pl.BlockSpec((DA_TQ, DA_W), lambda i: (i, 0)),
        out_shape=jax.ShapeDtypeStruct((n, DA_W), BF16),
        compiler_params=_cparams(("arbitrary",)),
        name="diff_attention",
    )(qb, kb, vb, k_all, v_all, da_lambda, da_norm_w.reshape(1, DA_V))


RW_TM = 256
RW_C = 64
RW_PAIRS = RW_HEADS // 2
HALO = 8


def _head_ones():
    idx = jnp.arange(2 * RW_HD) // RW_HD
    return (idx[:, None] == idx[None, :]).astype(BF16)


def _head_sum(x, ones_pair):
    w = 2 * RW_HD
    return jnp.concatenate([_dot2x(x[:, p * w:(p + 1) * w], ones_pair) for p in range(RW_PAIRS)], axis=1)


def _rwprep_kernel(x_ref, xp_ref, xn_ref, mu_ref, w0_ref, wup_ref, a0_ref, aup_ref, gup_ref, kk_ref, ka_ref,
                   ones_ref, r_o, v_o, kk_o, g_o, kd_o, lw_o, a_o, buf):
    row0 = pl.program_id(0) * RW_TM
    x = x_ref[:, 0:RW_COLS]
    buf[HALO:HALO + RW_TM, :] = x
    buf[HALO - 1:HALO, :] = xp_ref[HALO - 1:HALO, 0:RW_COLS]
    buf[HALO + RW_TM:HALO + RW_TM + 1, :] = xn_ref[0:1, 0:RW_COLS]
    rows = row0 + lax.broadcasted_iota(jnp.int32, (RW_TM, 1), 0)
    pos, length = _seq_pos(rows)
    prev = jnp.where(pos == 0, 0.0, buf[HALO - 1:HALO - 1 + RW_TM, :])
    nxt = jnp.where(pos == length - 1, 0.0, buf[HALO + 1:HALO + 1 + RW_TM, :])
    mu = mu_ref[...]
    u = x + mu[0:1] * (prev - x) + mu[1:2] * (nxt - x)

    r = u[:, 0:RW_W]
    k = u[:, RW_W:2 * RW_W]
    v = u[:, 2 * RW_W:3 * RW_W]
    wl = u[:, 3 * RW_W:3 * RW_W + 128]
    al = u[:, 3 * RW_W + 128:3 * RW_W + 256]
    gl = u[:, 3 * RW_W + 256:3 * RW_W + 384]
    w_raw = w0_ref[...] + _dot1(jnp.tanh(wl), wup_ref[...])
    lw = -math.exp(-0.5) * _sigmoid(w_raw)
    a = _sigmoid(a0_ref[...] + _dot1(al, aup_ref[...]))
    g = _dot1(_sigmoid(gl), gup_ref[...])
    kk = k * kk_ref[...]
    kk = kk * lax.rsqrt(jnp.maximum(_head_sum(kk * kk, ones_ref[...]), 1e-12))
    kd = jnp.concatenate([k, k], axis=1) * (1.0 + (a - 1.0) * ka_ref[...])

    r_o[...] = r
    v_o[...] = v
    kk_o[...] = kk
    g_o[...] = g
    for d in range(2):
        kd_o[d] = kd[:, d * RW_W:(d + 1) * RW_W]
        lw_o[d] = lw[:, d * RW_W:(d + 1) * RW_W]
        a_o[d] = a[:, d * RW_W:(d + 1) * RW_W]


def _rw_prepare(proj, p):
    n = proj.shape[0]
    nh = n // HALO
    steps = RW_TM // HALO
    wide = P_DA - P_RW
    cat2 = lambda t: t.reshape(1, 2 * RW_W)
    blockdiag = lambda t: jnp.concatenate(
        [jnp.concatenate([t[0], jnp.zeros_like(t[0])], axis=1),
         jnp.concatenate([jnp.zeros_like(t[1]), t[1]], axis=1)], axis=0)
    const = lambda shape: pl.BlockSpec(shape, lambda i: (0,) * len(shape))
    row = pl.BlockSpec((RW_TM, RW_W), lambda i: (i, 0))
    row2 = pl.BlockSpec((2, RW_TM, RW_W), lambda i: (0, i, 0))
    o1 = jax.ShapeDtypeStruct((n, RW_W), F32)
    o2 = jax.ShapeDtypeStruct((2, n, RW_W), F32)
    return pl.pallas_call(
        _rwprep_kernel,
        grid=(n // RW_TM,),
        in_specs=[
            pl.BlockSpec((RW_TM, wide), lambda i: (i, 0)),
            pl.BlockSpec((HALO, wide), lambda i: (jnp.maximum(i * steps - 1, 0), 0)),
            pl.BlockSpec((HALO, wide), lambda i: (jnp.minimum((i + 1) * steps, nh - 1), 0)),
            const((2, RW_COLS)), const((1, 2 * RW_W)), const((128, 2 * RW_W)), const((1, 2 * RW_W)),
            const((128, 2 * RW_W)), const((RW_G_LORA, RW_W)), const((1, RW_W)), const((1, 2 * RW_W)),
            const((2 * RW_HD, 2 * RW_HD)),
        ],
        out_specs=[row, row, row, row, row2, row2, row2],
        out_shape=[o1, o1, o1, o1, o2, o2, o2],
        scratch_shapes=[pltpu.VMEM((RW_TM + 2 * HALO, RW_COLS), F32)],
        compiler_params=_cparams(("parallel",)),
        name="rwkv_prepare",
    )(proj, proj, proj, p['rw_shift'], cat2(p['rw_w0']), blockdiag(p['rw_w_up']), cat2(p['rw_a0']),
      blockdiag(p['rw_a_up']), p['rw_g_up'], p['rw_k_k'].reshape(1, RW_W),
      jnp.tile(p['rw_k_a'].reshape(1, RW_W), (1, 2)), _head_ones())


RW_TB = 256
RW_NC = RW_TB // RW_C
RW_SIDE = 16


def _rw_finish(y, r, v, g, kd_sum, rk, lnw, lnb, ones):
    inv = 1.0 / RW_HD
    xc = y - _head_sum(y, ones) * inv
    var = _head_sum(xc * xc, ones) * inv
    yn = xc * lax.rsqrt(var + RW_GN_EPS) * lnw + lnb
    bonus = _head_sum(r * kd_sum * rk, ones) * v
    return ((yn + bonus) * g).astype(BF16)


def _rwscan_kernel(r_ref, v_ref, kk_ref, kd_ref, lw_ref, a_ref, tri_ref, inc_ref, str_ref, s0_ref, *rest, backward):
    if backward:
        yf_ref, g_ref, rk_ref, lnw_ref, lnb_ref, ones_ref, y_ref, st_ref, s_scr = rest
    else:
        y_ref, st_ref, s_scr = rest
    _rwscan_body(r_ref, v_ref, kk_ref, kd_ref, lw_ref, a_ref, tri_ref, inc_ref, str_ref, s0_ref, y_ref, st_ref,
                 s_scr, backward,
                 (lambda yb: _rw_finish(yf_ref[...] + yb, r_ref[...], v_ref[...], g_ref[...],
                                        kd_ref[0] + kd_ref[1], rk_ref[...], lnw_ref[...], lnb_ref[...],
                                        ones_ref[...])) if backward else None)


def _rwscan_body(r_ref, v_ref, kk_ref, kd_ref, lw_ref, a_ref, tri_ref, inc_ref, str_ref, s0_ref, y_ref, st_ref,
                 s_scr, backward, finish):
    step = pl.program_id(0)
    nb = pl.num_programs(0)
    bi = (nb - 1 - step) if backward else step
    pos, length = _seq_pos(bi * RW_TB)
    first = (pos + RW_TB == length) if backward else (pos == 0)
    last = (pos == 0) if backward else (pos + RW_TB == length)
    in_ctx = bi * RW_TB < N_CTX
    c = RW_C

    @pl.when(jnp.logical_and(first, in_ctx))
    def _():
        s_scr[...] = jnp.zeros_like(s_scr)

    @pl.when(jnp.logical_and(first, jnp.logical_not(in_ctx)))
    def _():
        z = jnp.zeros((c, c), F32)
        for p in range(RW_PAIRS):
            s_scr[p] = jnp.concatenate(
                [jnp.concatenate([s0_ref[0, 0, 2 * p], z], axis=1),
                 jnp.concatenate([z, s0_ref[0, 0, 2 * p + 1]], axis=1)], axis=0)

    incl2 = inc_ref[...] > 0.5
    strict2 = str_ref[...] > 0.5
    eye2 = (lax.broadcasted_iota(jnp.int32, (2 * c, 2 * c), 0)
            == lax.broadcasted_iota(jnp.int32, (2 * c, 2 * c), 1))
    m_e = lax.broadcasted_iota(jnp.int32, (1, 2 * c), 1) < c

    lw = lw_ref[0]
    cum = _dotx(tri_ref[...], lw)
    tots = [cum[(j * c if backward else j * c + c - 1):(j * c + 1 if backward else j * c + c), :]
            for j in range(RW_NC)]
    tot_b = jnp.concatenate([jnp.broadcast_to(t, (c, RW_W)) for t in tots], axis=0)
    kk = kk_ref[...]
    kd = kd_ref[1 if backward else 0]
    bp = kk * a_ref[0]
    g_inv = jnp.exp(-cum)
    g_rem = jnp.exp(tot_b - cum)
    ag = -kk * jnp.exp(cum - lw)
    rg = r_ref[...] * jnp.exp(cum)
    bdn = bp * g_inv
    kdn = kd * g_inv
    bc = bp * g_rem
    kc = kd * g_rem
    v = v_ref[...]

    def stack(x, j, p):
        xs = x[j * c:(j + 1) * c, p * 2 * c:(p + 1) * 2 * c]
        return jnp.concatenate([jnp.where(m_e, xs, 0.0), jnp.where(m_e, 0.0, xs)], axis=0).astype(BF16)

    pre = {}
    keys = [(j, p) for j in range(RW_NC) for p in range(RW_PAIRS)]
    for g0 in range(0, len(keys), RW_SIDE):
        grp = keys[g0:g0 + RW_SIDE]
        ops = {k: tuple(stack(t, *k) for t in (ag, rg, bdn, kdn, bc, kc, v)) for k in grp}
        gm = {k: _dg(jnp.concatenate([ops[k][0], ops[k][1]], axis=0),
                     jnp.concatenate([ops[k][2], ops[k][3]], axis=0), NT) for k in grp}
        lbb = {k: jnp.where(strict2, gm[k][:2 * c, :2 * c], 0.0) for k in grp}
        lkb = {k: jnp.where(strict2, gm[k][:2 * c, 2 * c:], 0.0).astype(BF16) for k in grp}
        lrk = {k: jnp.concatenate([jnp.where(incl2, gm[k][2 * c:, :2 * c], 0.0),
                                   jnp.where(incl2, gm[k][2 * c:, 2 * c:], 0.0)], axis=1).astype(BF16)
               for k in grp}
        lv = {k: _dg(lkb[k], ops[k][6], NN) for k in grp}
        x = {k: jnp.where(eye2, 1.0, lbb[k]) for k in grp}
        pw = lbb
        for _ in range(int(math.log2(c)) - 1):
            pwb = {k: pw[k].astype(BF16) for k in grp}
            pw = {k: _dg(pwb[k], pwb[k], NN) for k in grp}
            x = {k: x[k] + _dg(x[k].astype(BF16), pw[k].astype(BF16), NN) for k in grp}
        tw = {k: _dg(x[k].astype(BF16), jnp.concatenate([lv[k].astype(BF16), ops[k][0]], axis=1), NN)
              for k in grp}
        for k in grp:
            pre[k] = (tw[k][:, :2 * c], tw[k][:, 2 * c:].astype(BF16), ops[k][1], lrk[k], ops[k][6],
                      jnp.concatenate([ops[k][4], ops[k][5]], axis=0))

    order = range(RW_NC - 1, -1, -1) if backward else range(RW_NC)
    pairs = range(RW_PAIRS)
    s = [s_scr[p] for p in pairs]
    ys = {}
    for j in order:
        sb = [s[p].astype(BF16) for p in pairs]
        u = [_dg(pre[j, p][1], sb[p], NT) + pre[j, p][0] for p in pairs]
        uv = [jnp.concatenate([u[p].astype(BF16), pre[j, p][4]], axis=0) for p in pairs]
        y = [_dg(pre[j, p][2], sb[p], NT) + _dg(pre[j, p][3], uv[p], NN) for p in pairs]
        s = [s[p] * jnp.exp(tots[j][:, p * 2 * c:(p + 1) * 2 * c]) + _dg(uv[p], pre[j, p][5], TN) for p in pairs]
        ys[j] = [y[p][:c] + y[p][c:] for p in pairs]
    y_blk = jnp.concatenate([jnp.concatenate(ys[j], axis=1) for j in range(RW_NC)], axis=0)
    y_ref[...] = finish(y_blk) if finish else y_blk
    for p in pairs:
        s_scr[p] = s[p]

    @pl.when(last)
    def _():
        for p in pairs:
            st_ref[0, 2 * p] = s[p][:c, :c]
            st_ref[0, 2 * p + 1] = s[p][c:, c:]


def _scan_masks(backward):
    t = jnp.arange(RW_TB)
    sgn = -1 if backward else 1
    same_chunk = (t[:, None] // RW_C) == (t[None, :] // RW_C)
    tri = (same_chunk & ((t[:, None] - t[None, :]) * sgn >= 0)).astype(BF16)
    q = jnp.arange(2 * RW_C)
    same_head = (q[:, None] // RW_C) == (q[None, :] // RW_C)
    dif = ((q[:, None] % RW_C) - (q[None, :] % RW_C)) * sgn
    return tri, (same_head & (dif >= 0)).astype(F32), (same_head & (dif > 0)).astype(F32)


def _rw_scan(r, v, kk, kd, lw, a, s0, backward, finish=()):
    n = r.shape[0]
    nb = n // RW_TB
    d = 1 if backward else 0
    blk = (lambda i: nb - 1 - i) if backward else (lambda i: i)
    row = pl.BlockSpec((RW_TB, RW_W), lambda i: (blk(i), 0))
    row2 = pl.BlockSpec((1, RW_TB, RW_W), lambda i: (d, blk(i), 0))
    both = pl.BlockSpec((2, RW_TB, RW_W), lambda i: (0, blk(i), 0))
    const = lambda shape: pl.BlockSpec(shape, lambda i: (0,) * len(shape))
    vec = const((1, RW_W))
    pair = 2 * RW_C
    extra = [row, row, vec, vec, vec, const((pair, pair))] if backward else []
    return pl.pallas_call(
        functools.partial(_rwscan_kernel, backward=backward),
        grid=(nb,),
        in_specs=[row, row, row, both, row2, row2, const((RW_TB, RW_TB)), const((pair, pair)), const((pair, pair)),
                  pl.BlockSpec((1, 1, RW_HEADS, RW_HD, RW_HD),
                               lambda i: (jnp.maximum(_seq_index(blk(i) * RW_TB) - BATCH, 0), d, 0, 0, 0))] + extra,
        out_specs=[row, pl.BlockSpec((1, RW_HEADS, RW_HD, RW_HD), lambda i: (_seq_index(blk(i) * RW_TB), 0, 0, 0))],
        out_shape=[jax.ShapeDtypeStruct((n, RW_W), BF16 if backward else F32),
                   jax.ShapeDtypeStruct((N_SEQS, RW_HEADS, RW_HD, RW_HD), F32)],
        scratch_shapes=[pltpu.VMEM((RW_PAIRS, pair, pair), F32)],
        compiler_params=_cparams(("arbitrary",)),
        name="rwkv_scan_bwd" if backward else "rwkv_scan_fwd",
    )(r, v, kk, kd, lw, a, *_scan_masks(backward), s0, *finish)


def _rwkv_branch(proj, state0, p):
    r, v, kk, g, kd, lw, a = _rw_prepare(proj, p)
    yf, sf = _rw_scan(r, v, kk, kd, lw, a, state0, backward=False)
    finish = (yf, g, p['rw_r_k'].reshape(1, RW_W), p['rw_ln_w'].reshape(1, RW_W), p['rw_ln_b'].reshape(1, RW_W),
              _head_ones())
    out, sb = _rw_scan(r, v, kk, kd, lw, a, state0, backward=True, finish=finish)
    return out, jnp.stack([sf, sb], axis=1)


RET_W = RET_HEADS * RET_DV
RET_PAIRS = RET_HEADS // 2


RET_TB = 256
RET_NC = RET_TB // RET_CHUNK


def _ret_kernel(q_ref, k_ref, v_ref, lg_ref, s0_ref, *rest, backward):
    if backward:
        of_ref, g_ref, nw_ref, o_ref, st_ref, s_scr = rest
    else:
        o_ref, st_ref, s_scr = rest
    step = pl.program_id(0)
    nb = pl.num_programs(0)
    bi = (nb - 1 - step) if backward else step
    pos, length = _seq_pos(bi * RET_TB)
    first = (pos + RET_TB == length) if backward else (pos == 0)
    in_ctx = bi * RET_TB < N_CTX

    @pl.when(jnp.logical_and(first, in_ctx))
    def _():
        s_scr[...] = jnp.zeros_like(s_scr)

    @pl.when(jnp.logical_and(first, jnp.logical_not(in_ctx)))
    def _():
        s_scr[...] = s0_ref[0, 0]

    c = RET_CHUNK
    d = 1 if backward else 0
    sgn = -1 if backward else 1
    lgs = -_softplus(-lg_ref[d:d + 1, :])
    ri = lax.broadcasted_iota(jnp.int32, (c, c), 0)
    cj = lax.broadcasted_iota(jnp.int32, (c, c), 1)
    dif = (ri - cj) * sgn
    valid = dif >= 0
    dist = jnp.maximum(dif, 0).astype(F32)
    pr = lax.broadcasted_iota(jnp.int32, (c, 2 * RET_DK), 0)
    tau = ((c - 1 - pr) if backward else pr).astype(F32)
    low = lax.broadcasted_iota(jnp.int32, (1, 2 * RET_DK), 1) < RET_DK
    rlow = lax.broadcasted_iota(jnp.int32, (2 * RET_DK, RET_DV), 0) < RET_DK
    heads = range(RET_HEADS)
    pairs = range(RET_PAIRS)
    mask = [low, jnp.logical_not(low)]
    lg_h = [lgs[:, h:h + 1] for h in heads]
    lg_row = [jnp.where(low, lg_h[2 * p], lg_h[2 * p + 1]) for p in pairs]
    dmat = [jnp.where(valid, jnp.exp(lg_h[h] * dist), 0.0) for h in heads]
    q_dec = [jnp.exp(lg_row[p] * (tau + 1.0)) for p in pairs]
    k_dec = [jnp.exp(lg_row[p] * (c - 1.0 - tau)) for p in pairs]
    c_dec = [jnp.where(rlow, jnp.exp(lg_h[2 * p] * c), jnp.exp(lg_h[2 * p + 1] * c)) for p in pairs]

    pre = []
    for j in range(RET_NC):
        rows = slice(j * c, (j + 1) * c)
        pw = 2 * RET_DK
        qp = [q_ref[rows, p * pw:(p + 1) * pw] for p in pairs]
        kp = [k_ref[rows, p * pw:(p + 1) * pw] * (RET_DK ** -0.5) for p in pairs]
        kpb = [t.astype(BF16) for t in kp]
        vb = [v_ref[rows, h * RET_DV:(h + 1) * RET_DV].astype(BF16) for h in heads]
        att = [_dg(jnp.where(mask[h % 2], qp[h // 2], 0.0).astype(BF16), kpb[h // 2], NT) * dmat[h]
               for h in heads]
        upd = [_dg(jnp.where(mask[h % 2], kp[h // 2] * k_dec[h // 2], 0.0).astype(BF16), vb[h], TN)
               for h in heads]
        inner = [_dg(att[h].astype(BF16), vb[h], NN) for h in heads]
        qd = [jnp.where(mask[h % 2], qp[h // 2] * q_dec[h // 2], 0.0).astype(BF16) for h in heads]
        pre.append((inner, qd, upd))

    s = [s_scr[p] for p in pairs]
    for j in (range(RET_NC - 1, -1, -1) if backward else range(RET_NC)):
        inner, qd, upd = pre[j]
        sb = [t.astype(BF16) for t in s]
        for h in heads:
            rows, cols = slice(j * c, (j + 1) * c), slice(h * RET_DV, (h + 1) * RET_DV)
            val = inner[h] + _dg(qd[h], sb[h // 2], NN)
            if backward:
                gh = g_ref[rows, cols].astype(F32)
                val = (gh * _sigmoid(gh) * (_standardize(of_ref[rows, cols] + val, EPS) * nw_ref[:, cols])
                       ).astype(BF16)
            o_ref[rows, cols] = val
        s = [s[p] * c_dec[p] + upd[2 * p] + upd[2 * p + 1] for p in pairs]
    for p in pairs:
        s_scr[p] = s[p]
        st_ref[0, p] = s[p]


def _ret_scan(proj, logit, s0, backward, finish=()):
    n = proj.shape[0]
    nb = n // RET_TB
    d = 1 if backward else 0
    blk = (lambda i: nb - 1 - i) if backward else (lambda i: i)
    qk_w = RET_HEADS * RET_DK
    state = (RET_PAIRS, 2 * RET_DK, RET_DV)
    extra, extra_args = [], ()
    if backward:
        extra = [pl.BlockSpec((RET_TB, RET_W), lambda i: (blk(i), 0)),
                 pl.BlockSpec((RET_TB, RET_W), lambda i: (blk(i), (B_RET + 2 * qk_w + RET_W) // RET_W)),
                 pl.BlockSpec((1, RET_W), lambda i: (0, 0))]
        extra_args = (finish[0], proj, finish[1])
    return pl.pallas_call(
        functools.partial(_ret_kernel, backward=backward),
        grid=(nb,),
        in_specs=[
            pl.BlockSpec((RET_TB, qk_w), lambda i: (blk(i), B_RET // qk_w)),
            pl.BlockSpec((RET_TB, qk_w), lambda i: (blk(i), B_RET // qk_w + 1)),
            pl.BlockSpec((RET_TB, RET_W), lambda i: (blk(i), (B_RET + 2 * qk_w) // RET_W)),
            pl.BlockSpec((2, RET_HEADS), lambda i: (0, 0)),
            pl.BlockSpec((1, 1) + state,
                         lambda i: (jnp.maximum(_seq_index(blk(i) * RET_TB) - BATCH, 0), d, 0, 0, 0)),
        ] + extra,
        out_specs=[pl.BlockSpec((RET_TB, RET_W), lambda i: (blk(i), 0)),
                   pl.BlockSpec((1,) + state, lambda i: (_seq_index(blk(i) * RET_TB), 0, 0, 0))],
        out_shape=[jax.ShapeDtypeStruct((n, RET_W), BF16 if backward else F32),
                   jax.ShapeDtypeStruct((N_SEQS,) + state, F32)],
        scratch_shapes=[pltpu.VMEM(state, F32)],
        compiler_params=_cparams(("arbitrary",)),
        name="retention_scan_bwd" if backward else "retention_scan_fwd",
    )(proj, proj, proj, logit, s0, *extra_args)


def _standardize(x, eps):
    mu = jnp.mean(x, axis=-1, keepdims=True)
    xc = x - mu
    return xc * lax.rsqrt(jnp.mean(xc * xc, axis=-1, keepdims=True) + eps)


def _retention_branch(proj, state0, p):
    s0 = state0.reshape(DEC_BATCH, 2, RET_PAIRS, 2 * RET_DK, RET_DV)
    of, sf = _ret_scan(proj, p['ret_decay_logit'], s0, backward=False)
    out, sb = _ret_scan(proj, p['ret_decay_logit'], s0, backward=True,
                        finish=(of, p['ret_norm_w'].reshape(1, RET_W)))
    st = jnp.stack([sf, sb], axis=1).reshape(N_SEQS, 2, RET_HEADS, RET_DK, RET_DV)
    return out, st


CV_TM = 256
CV_HALO = 16


def _conv_kernel(a_ref, g_ref, ap_ref, gp_ref, an_ref, gn_ref, w_ref, b_ref, lnw_ref, lnb_ref, o_ref, buf, sbuf):
    row0 = pl.program_id(0) * CV_TM
    pos0, len0 = _seq_pos(row0)
    glu = lambda a, g: a[...].astype(F32) * _sigmoid(g[...].astype(F32))
    buf[CV_HALO:CV_HALO + CV_TM, :] = glu(a_ref, g_ref)
    buf[0:CV_HALO, :] = jnp.where(pos0 == 0, 0.0, glu(ap_ref, gp_ref))
    buf[CV_HALO + CV_TM:, :] = jnp.where(pos0 + CV_TM == len0, 0.0, glu(an_ref, gn_ref))
    base = CV_HALO - CONV_K // 2
    acc = jnp.zeros((CV_TM, CV_W), F32)
    for r in range(8):
        taps = [m for m in range((base + CONV_K + 7) // 8) if 0 <= r + 8 * m - base < CONV_K]
        span = CV_TM + 8 * max(taps)
        if r:
            sbuf[0:span, :] = buf[r:r + span, :]
        src = sbuf if r else buf
        for m in taps:
            j = r + 8 * m - base
            acc = acc + w_ref[j:j + 1, :] * src[8 * m:8 * m + CV_TM, :]
    z = _standardize(acc + b_ref[...], EPS) * lnw_ref[...] + lnb_ref[...]
    o_ref[...] = (z * _sigmoid(z)).astype(BF16)


def _conv_branch(proj, p):
    n = proj.shape[0]
    nh = n // CV_HALO
    steps = CV_TM // CV_HALO
    ca = B_CV // CV_W
    prev = lambda i: jnp.maximum(i * steps - 1, 0)
    nxt = lambda i: jnp.minimum((i + 1) * steps, nh - 1)
    vec = pl.BlockSpec((1, CV_W), lambda i: (0, 0))
    return pl.pallas_call(
        _conv_kernel,
        grid=(n // CV_TM,),
        in_specs=[
            pl.BlockSpec((CV_TM, CV_W), lambda i: (i, ca)),
            pl.BlockSpec((CV_TM, CV_W), lambda i: (i, ca + 1)),
            pl.BlockSpec((CV_HALO, CV_W), lambda i: (prev(i), ca)),
            pl.BlockSpec((CV_HALO, CV_W), lambda i: (prev(i), ca + 1)),
            pl.BlockSpec((CV_HALO, CV_W), lambda i: (nxt(i), ca)),
            pl.BlockSpec((CV_HALO, CV_W), lambda i: (nxt(i), ca + 1)),
            pl.BlockSpec((CONV_K, CV_W), lambda i: (0, 0)),
            vec, vec, vec,
        ],
        out_specs=pl.BlockSpec((CV_TM, CV_W), lambda i: (i, 0)),
        out_shape=jax.ShapeDtypeStruct((n, CV_W), BF16),
        scratch_shapes=[pltpu.VMEM((CV_TM + 2 * CV_HALO, CV_W), F32)] * 2,
        compiler_params=_cparams(("parallel",)),
        name="conformer_conv",
    )(proj, proj, proj, proj, proj, proj, p['cv_dw_w'], p['cv_dw_b'].reshape(1, CV_W),
      p['cv_ln_w'].reshape(1, CV_W), p['cv_ln_b'].reshape(1, CV_W))


MG_TM = 512


def _merge_kernel(da_ref, rw_ref, ret_ref, cv_ref, g0_ref, g1_ref, g2_ref, g3_ref, x_ref, mod_ref, wb_ref, wo_ref,
                  nw_ref, x_o, h_o):
    m = None
    for n, (br, gt) in enumerate(((da_ref, g0_ref), (rw_ref, g1_ref), (ret_ref, g2_ref), (cv_ref, g3_ref))):
        t = _sigmoid(gt[...].astype(F32)) * _dg(br[...], wb_ref[n], NN)
        m = t if m is None else m + t
    out = _dg(m.astype(BF16), wo_ref[...], NN)
    mrow = _mod_row(mod_ref, pl.program_id(0) * MG_TM)
    gate1 = mrow[:, 2 * D_MODEL:3 * D_MODEL]
    sh2 = mrow[:, 3 * D_MODEL:4 * D_MODEL]
    sc2 = mrow[:, 4 * D_MODEL:5 * D_MODEL]
    x1 = x_ref[...] + gate1 * out
    x_o[...] = x1
    h_o[...] = _rms(x1, nw_ref[...]) * (1.0 + sc2) + sh2


def _merge(branches, gates, x, mod, w_branch, w_out, norm_w):
    n = x.shape[0]
    br = pl.BlockSpec((MG_TM, BR_W), lambda i: (i, 0))
    gspec = lambda j: pl.BlockSpec((MG_TM, D_MODEL), lambda i: (i, B_GATE // D_MODEL + j))
    full = pl.BlockSpec((MG_TM, D_MODEL), lambda i: (i, 0))
    out = jax.ShapeDtypeStruct((n, D_MODEL), F32)
    return pl.pallas_call(
        _merge_kernel,
        grid=(n // MG_TM,),
        in_specs=[br, br, br, br, gspec(0), gspec(1), gspec(2), gspec(3), full,
                  pl.BlockSpec((MOD_ROWS, 6 * D_MODEL), lambda i: (0, 0)),
                  pl.BlockSpec((N_BRANCH, BR_W, D_MODEL), lambda i: (0, 0, 0)),
                  pl.BlockSpec((D_MODEL, D_MODEL), lambda i: (0, 0)),
                  pl.BlockSpec((1, D_MODEL), lambda i: (0, 0))],
        out_specs=[full, full],
        out_shape=[out, out],
        compiler_params=_cparams(("parallel",)),
        name="gated_merge",
    )(*branches, gates, gates, gates, gates, x, mod, w_branch.astype(BF16), w_out.astype(BF16),
      norm_w.reshape(1, D_MODEL))


RT_TM = 256
MOE_BM = 512
MOE_ROWS = N_TOK * TOP_K + N_EXPERTS * MOE_BM
DP_TM = 512
DP_GROUP = 64
CB_TM = 256
DMA_UNROLL = 8


def _router_kernel(h_ref, w_ref, b_ref, tri_ref, idx_o, gate_o, rank_o, cnt_o, carry):
    @pl.when(pl.program_id(0) == 0)
    def _():
        carry[...] = jnp.zeros_like(carry)

    logits = _dot3(w_ref[...], h_ref[...], NT) + b_ref[...]
    e_iota = lax.broadcasted_iota(jnp.int32, logits.shape, 0)
    work = logits
    vals, idxs, hots = [], [], []
    for _ in range(TOP_K):
        mx = jnp.max(work, axis=0, keepdims=True)
        ix = jnp.min(jnp.where(work == mx, e_iota, N_EXPERTS), axis=0, keepdims=True)
        hot = e_iota == ix
        vals.append(mx)
        idxs.append(ix)
        hots.append(hot.astype(F32))
        work = jnp.where(hot, -jnp.inf, work)
    es = [jnp.exp(v - vals[0]) for v in vals]
    inv = 1.0 / (es[0] + es[1] + es[2] + es[3])
    chosen = hots[0] + hots[1] + hots[2] + hots[3]
    ahead = carry[...][:, 0:1] + _dg(chosen.astype(BF16), tri_ref[...], NN)
    idx_o[...] = jnp.concatenate(idxs, axis=0)
    gate_o[...] = jnp.concatenate([e * inv for e in es], axis=0)
    rank_o[...] = jnp.concatenate(
        [jnp.sum(hot * ahead, axis=0, keepdims=True) for hot in hots], axis=0).astype(jnp.int32)
    carry[...] = carry[...] + jnp.sum(chosen, axis=1, keepdims=True)
    cnt_o[...] = carry[...]


def _router(h, router_w, router_b):
    n = h.shape[0]
    tri = (jnp.arange(RT_TM)[:, None] < jnp.arange(RT_TM)[None, :]).astype(BF16)
    col = pl.BlockSpec((TOP_K, RT_TM), lambda i: (0, i))
    return pl.pallas_call(
        _router_kernel,
        grid=(n // RT_TM,),
        in_specs=[
            pl.BlockSpec((RT_TM, D_MODEL), lambda i: (i, 0)),
            pl.BlockSpec((N_EXPERTS, D_MODEL), lambda i: (0, 0)),
            pl.BlockSpec((N_EXPERTS, 1), lambda i: (0, 0)),
            pl.BlockSpec((RT_TM, RT_TM), lambda i: (0, 0)),
        ],
        out_specs=[col, col, col, pl.BlockSpec((N_EXPERTS, 128), lambda i: (0, 0))],
        out_shape=[jax.ShapeDtypeStruct((TOP_K, n), jnp.int32), jax.ShapeDtypeStruct((TOP_K, n), F32),
                   jax.ShapeDtypeStruct((TOP_K, n), jnp.int32), jax.ShapeDtypeStruct((N_EXPERTS, 128), F32)],
        scratch_shapes=[pltpu.VMEM((N_EXPERTS, 128), F32)],
        compiler_params=_cparams(("arbitrary",)),
        name="router",
    )(h, router_w.T, router_b.reshape(N_EXPERTS, 1), tri)


def _tile_major(t, tm):
    k, n = t.shape
    return t.reshape(k, n // tm, tm).transpose(1, 0, 2).reshape(n // tm, 1, k * tm)


def _dispatch_kernel(dest_ref, pe_ref, h_ref, o_hbm, zbuf, sem, zsem):
    n_groups = DP_TM // DP_GROUP

    @pl.when(pl.program_id(0) == 0)
    def _():
        zbuf[...] = jnp.zeros_like(zbuf)

        def fill(e):
            end = pe_ref[e]
            begin = pe_ref[e - 1] if e else 0
            return end > begin, pltpu.make_async_copy(
                zbuf, o_hbm.at[pl.ds(pl.multiple_of(jnp.maximum(end - MOE_BM, 0), MOE_BM), MOE_BM)], zsem.at[0])

        for e in range(N_EXPERTS):
            nonempty, cp = fill(e)
            pl.when(nonempty)(cp.start)
        for e in range(N_EXPERTS):
            nonempty, cp = fill(e)
            pl.when(nonempty)(cp.wait)

        def tail(b):
            return pltpu.make_async_copy(zbuf, o_hbm.at[pl.ds(pl.multiple_of(b * MOE_BM, MOE_BM), MOE_BM)],
                                         zsem.at[0])

        first_unused = pe_ref[N_EXPERTS - 1] // MOE_BM
        lax.fori_loop(first_unused, MOE_ROWS // MOE_BM, lambda b, c: (tail(b).start(), c)[1], 0)
        lax.fori_loop(first_unused, MOE_ROWS // MOE_BM, lambda b, c: (tail(b).wait(), c)[1], 0)

    def wait_group(slot):
        pltpu.make_async_copy(h_ref.at[pl.ds(0, TOP_K * DP_GROUP)], o_hbm.at[pl.ds(0, TOP_K * DP_GROUP)],
                              sem.at[slot]).wait()

    def group(gi, carry):
        slot = gi % 2

        def issue(t, c):
            tok = gi * DP_GROUP + t
            for k in range(TOP_K):
                dst = dest_ref[0, 0, k * DP_TM + tok]
                pltpu.make_async_copy(h_ref.at[pl.ds(tok, 1)], o_hbm.at[pl.ds(dst, 1)], sem.at[slot]).start()
            return c

        lax.fori_loop(0, DP_GROUP, issue, 0, unroll=DMA_UNROLL)

        @pl.when(gi > 0)
        def _():
            wait_group(1 - slot)

        return carry

    lax.fori_loop(0, n_groups, group, 0)
    wait_group((n_groups - 1) % 2)


def _dispatch(h, dest, pad_end):
    n = h.shape[0]
    return pl.pallas_call(
        _dispatch_kernel,
        grid=(n // DP_TM,),
        in_specs=[
            pl.BlockSpec((1, 1, TOP_K * DP_TM), lambda i: (i, 0, 0), memory_space=pltpu.SMEM),
            pl.BlockSpec(memory_space=pltpu.SMEM),
            pl.BlockSpec((DP_TM, D_MODEL), lambda i: (i, 0)),
        ],
        out_specs=pl.BlockSpec(memory_space=pl.ANY),
        out_shape=jax.ShapeDtypeStruct((MOE_ROWS, D_MODEL), F32),
        scratch_shapes=[pltpu.VMEM((MOE_BM, D_MODEL), F32), pltpu.SemaphoreType.DMA((2,)),
                        pltpu.SemaphoreType.DMA((1,))],
        compiler_params=_cparams(("arbitrary",)),
        name="moe_dispatch",
    )(_tile_major(dest, DP_TM), pad_end, h)


def _expert_kernel(bx_ref, be_ref, nv_ref, x_ref, w1_ref, b1_ref, w2_ref, b2_ref, o_ref, w1b, w2b):
    i = pl.program_id(0)
    changed = jnp.logical_or(i == 0, be_ref[i] != be_ref[jnp.maximum(i - 1, 0)])

    @pl.when(changed)
    def _():
        w1b[...] = w1_ref[0, 0].astype(BF16)
        w2b[...] = w2_ref[0, 0].astype(BF16)

    def ffn(rows):
        hb = _dg(x_ref[0:rows, :].astype(BF16), w1b[...], NN) + b1_ref[0, 0]
        hg = jnp.minimum(hb[:, :D_FF], SWIGLU_LIMIT)
        hu = jnp.clip(hb[:, D_FF:], -SWIGLU_LIMIT, SWIGLU_LIMIT)
        act = hg * _sigmoid(SWIGLU_ALPHA * hg) * (hu + 1.0)
        o_ref[0:rows, :] = _dg(act.astype(BF16), w2b[...], NN) + b2_ref[0, 0]

    nv = nv_ref[i]
    half = MOE_BM // 2

    @pl.when(nv > half)
    def _():
        ffn(MOE_BM)

    @pl.when(jnp.logical_and(nv > 0, nv <= half))
    def _():
        ffn(half)
        o_ref[half:, :] = jnp.zeros((MOE_BM - half, D_MODEL), F32)

    @pl.when(nv == 0)
    def _():
        o_ref[...] = jnp.zeros_like(o_ref)


def _experts(x_rows, blk_x, blk_e, n_valid, layer, w1, b1, w2, b2):
    nb = MOE_ROWS // MOE_BM
    grid_spec = pltpu.PrefetchScalarGridSpec(
        num_scalar_prefetch=3,
        grid=(nb,),
        in_specs=[
            pl.BlockSpec((MOE_BM, D_MODEL), lambda i, bx, be, nv: (bx[i], 0)),
            pl.BlockSpec((1, 1, D_MODEL, 2 * D_FF), lambda i, bx, be, nv: (layer, be[i], 0, 0)),
            pl.BlockSpec((1, 1, 1, 2 * D_FF), lambda i, bx, be, nv: (layer, be[i], 0, 0)),
            pl.BlockSpec((1, 1, D_FF, D_MODEL), lambda i, bx, be, nv: (layer, be[i], 0, 0)),
            pl.BlockSpec((1, 1, 1, D_MODEL), lambda i, bx, be, nv: (layer, be[i], 0, 0)),
        ],
        out_specs=pl.BlockSpec((MOE_BM, D_MODEL), lambda i, bx, be, nv: (i, 0)),
        scratch_shapes=[pltpu.VMEM((D_MODEL, 2 * D_FF), BF16), pltpu.VMEM((D_FF, D_MODEL), BF16)],
    )
    return pl.pallas_call(
        _expert_kernel,
        grid_spec=grid_spec,
        out_shape=jax.ShapeDtypeStruct((MOE_ROWS, D_MODEL), F32),
        compiler_params=pltpu.CompilerParams(dimension_semantics=("arbitrary",),
                                             vmem_limit_bytes=EXPERT_VMEM_LIMIT),
        name="moe_experts",
    )(blk_x, blk_e, n_valid, x_rows, w1, b1.reshape(DEPTH, N_EXPERTS, 1, 2 * D_FF), w2,
      b2.reshape(DEPTH, N_EXPERTS, 1, D_MODEL))


def _combine_kernel(dest_ref, gate_ref, x_ref, mod_ref, fw_ref, y_hbm, *rest, final):
    if final:
        o_ctx_ref, o_lat_ref, buf, sem = rest
    else:
        o_ref, buf, sem = rest

    def issue(t, c):
        for k in range(TOP_K):
            dst = dest_ref[0, 0, k * CB_TM + t]
            pltpu.make_async_copy(y_hbm.at[pl.ds(dst, 1)], buf.at[k, pl.ds(t, 1)], sem.at[0]).start()
        return c

    lax.fori_loop(0, CB_TM, issue, 0, unroll=DMA_UNROLL)
    for k in range(TOP_K):
        pltpu.make_async_copy(y_hbm.at[pl.ds(0, CB_TM)], buf.at[k], sem.at[0]).wait()
    g = gate_ref[...]
    acc = g[:, 0:1] * buf[0]
    for k in range(1, TOP_K):
        acc = acc + g[:, k:k + 1] * buf[k]
    gate2 = _mod_row(mod_ref, pl.program_id(0) * CB_TM)[:, 5 * D_MODEL:6 * D_MODEL]
    x2 = x_ref[...] + gate2 * acc
    if not final:
        o_ref[...] = x2
        return
    y = _rms(x2, fw_ref[...])
    in_ctx = pl.program_id(0) < N_CTX // CB_TM

    @pl.when(in_ctx)
    def _():
        o_ctx_ref[...] = y

    @pl.when(jnp.logical_not(in_ctx))
    def _():
        o_lat_ref[...] = y


def _combine(y_rows, dest, gates, x, mod, final_w, final):
    n = x.shape[0]
    full = pl.BlockSpec((CB_TM, D_MODEL), lambda i: (i, 0))
    nct = N_CTX // CB_TM
    if final:
        out_specs = [pl.BlockSpec((CB_TM, D_MODEL), lambda i: (jnp.minimum(i, nct - 1), 0)),
                     pl.BlockSpec((CB_TM, D_MODEL), lambda i: (jnp.maximum(i - nct, 0), 0))]
        out_shape = [jax.ShapeDtypeStruct((N_CTX, D_MODEL), F32), jax.ShapeDtypeStruct((N_LAT, D_MODEL), F32)]
    else:
        out_specs, out_shape = full, jax.ShapeDtypeStruct((n, D_MODEL), F32)
    return pl.pallas_call(
        functools.partial(_combine_kernel, final=final),
        grid=(n // CB_TM,),
        in_specs=[
            pl.BlockSpec((1, 1, TOP_K * CB_TM), lambda i: (i, 0, 0), memory_space=pltpu.SMEM),
            pl.BlockSpec((CB_TM, TOP_K), lambda i: (i, 0)),
            full,
            pl.BlockSpec((MOD_ROWS, 6 * D_MODEL), lambda i: (0, 0)),
            pl.BlockSpec((1, D_MODEL), lambda i: (0, 0)),
            pl.BlockSpec(memory_space=pl.ANY),
        ],
        out_specs=out_specs,
        out_shape=out_shape,
        scratch_shapes=[pltpu.VMEM((TOP_K, CB_TM, D_MODEL), F32), pltpu.SemaphoreType.DMA((1,))],
        compiler_params=_cparams(("arbitrary",)),
        name="moe_combine",
    )(_tile_major(dest, CB_TM), gates.T, x, mod, final_w.reshape(1, D_MODEL), y_rows)


def _routed_ffn(h, x, mod, p, moe, layer, final_w, final):
    idx, gates, rank, counts = _router(h, p['router_w'], p['router_b'])
    counts = counts[:, 0].astype(jnp.int32)
    padded = (counts + MOE_BM - 1) // MOE_BM * MOE_BM
    pad_end = jnp.cumsum(padded)
    pad_start = pad_end - padded
    experts = jnp.arange(N_EXPERTS, dtype=jnp.int32)
    start_of = jnp.sum(jnp.where(idx[:, :, None] == experts, pad_start, 0), axis=-1)
    dest = start_of + rank
    nb = MOE_ROWS // MOE_BM
    first_row = jnp.arange(nb, dtype=jnp.int32) * MOE_BM
    blk_e = jnp.minimum(jnp.sum((pad_end[None, :] <= first_row[:, None]).astype(jnp.int32), axis=1),
                        N_EXPERTS - 1)
    is_e = blk_e[:, None] == experts[None, :]
    end_of = jnp.sum(jnp.where(is_e, pad_start + counts, 0), axis=1)
    n_valid = jnp.clip(end_of - first_row, 0, MOE_BM).astype(jnp.int32)
    blk_x = jnp.minimum(jnp.arange(nb, dtype=jnp.int32), pad_end[-1] // MOE_BM - 1)
    x_rows = _dispatch(h, dest, pad_end)
    y_rows = _experts(x_rows, blk_x, blk_e, n_valid, layer, *moe)
    return _combine(y_rows, dest, gates, x, mod, final_w, final)


_LAYER_PARAMS = ('norm_mix_w', 'norm_ffn_w', 'da_lambda', 'da_norm_w', 'rw_shift', 'rw_w0', 'rw_w_up', 'rw_a0',
                 'rw_a_up', 'rw_g_up', 'rw_k_k', 'rw_k_a', 'rw_r_k', 'rw_ln_w', 'rw_ln_b', 'ret_decay_logit',
                 'ret_norm_w', 'cv_dw_w', 'cv_dw_b', 'cv_ln_w', 'cv_ln_b', 'w_branch', 'w_out', 'router_w',
                 'router_b')


def _check_layout():
    assert SEQ & (SEQ - 1) == 0 and DEC_SEQ & (DEC_SEQ - 1) == 0, "sequence positions are bit masks"
    assert N_CTX % DEC_SEQ == 0, "latent sequences start on a DEC_SEQ boundary of the stream"
    assert N_CTX % MOD_GROUP == 0 and DEC_SEQ % MOD_GROUP == 0 and N_CTX + DEC_BATCH * DEC_SEQ == N_TOK
    for tile in (RW_TM, RW_TB, RET_TB, CV_TM, DA_TQ):
        assert SEQ % tile == 0 and DEC_SEQ % tile == 0
    for tile in (IN_TM, MG_TM, CB_TM):
        assert MOD_GROUP % tile == 0
    assert N_TOK % QK_TM == 0 and N_CTX % QK_TM == 0 and N_TOK % RT_TM == 0 and N_TOK % DP_TM == 0
    assert DP_TM % DP_GROUP == 0 and DP_GROUP % DMA_UNROLL == 0 and CB_TM % DMA_UNROLL == 0


def _layer(x, mod, w_p, p, moe, layer, lam_init, caches, tables, final_w, final):
    cache_k, cache_v, state_rw, state_ret = caches
    proj_rw, proj = _input_projection(x, mod, p['norm_mix_w'], w_p, layer)
    o_da = _da_branch(proj, cache_k, cache_v, p['da_lambda'], p['da_norm_w'], lam_init, tables)
    o_rw, rw_state = _rwkv_branch(proj_rw, state_rw, p)
    o_ret, ret_state = _retention_branch(proj, state_ret, p)
    o_cv = _conv_branch(proj, p)
    x1, h2 = _merge((o_da, o_rw, o_ret, o_cv), proj, x, mod, p['w_branch'], p['w_out'], p['norm_ffn_w'])
    x2 = _routed_ffn(h2, x1, mod, p, moe, layer, final_w, final)
    ctx_kv = proj[:N_CTX, B_DA + DA_W:B_DA + 3 * DA_W].astype(F32)
    new_k = ctx_kv[:, :DA_W].reshape(BATCH, SEQ, DA_HEADS, 2, DA_QK)
    new_v = ctx_kv[:, DA_W:].reshape(BATCH, SEQ, DA_HEADS, DA_V)
    return x2, (new_k, new_v, rw_state[:BATCH], ret_state[:BATCH])


def kernel(x_prompt, x_sample, c, cache_da_k, cache_da_v, state_rwkv, state_ret, c_ctx, ada_w, ada_b, norm_mix_w,
           norm_ffn_w, w_in, da_lambda, da_norm_w, rw_shift, rw_w0, rw_w_up, rw_a0, rw_a_up, rw_g_up, rw_k_k,
           rw_k_a, rw_r_k, rw_ln_w, rw_ln_b, ret_decay_logit, ret_norm_w, cv_dw_w, cv_dw_b, cv_ln_w, cv_ln_b,
           w_branch, w_out, router_w, router_b, moe_w1, moe_b1, moe_w2, moe_b2, final_norm_w):
    _check_layout()
    weights = dict(norm_mix_w=norm_mix_w, norm_ffn_w=norm_ffn_w, da_lambda=da_lambda, da_norm_w=da_norm_w,
                   rw_shift=rw_shift, rw_w0=rw_w0, rw_w_up=rw_w_up, rw_a0=rw_a0, rw_a_up=rw_a_up, rw_g_up=rw_g_up,
                   rw_k_k=rw_k_k, rw_k_a=rw_k_a, rw_r_k=rw_r_k, rw_ln_w=rw_ln_w, rw_ln_b=rw_ln_b,
                   ret_decay_logit=ret_decay_logit, ret_norm_w=ret_norm_w, cv_dw_w=cv_dw_w, cv_dw_b=cv_dw_b,
                   cv_ln_w=cv_ln_w, cv_ln_b=cv_ln_b, w_branch=w_branch, w_out=w_out, router_w=router_w,
                   router_b=router_b)
    x = jnp.concatenate([x_prompt.reshape(N_CTX, D_MODEL), x_sample.reshape(N_LAT, D_MODEL)], axis=0)
    cvec = jnp.concatenate([c_ctx[None, :], c, jnp.zeros((MOD_ROWS - 1 - DEC_BATCH, D_MODEL), F32)], axis=0)
    mod = _modulation(cvec, ada_w, ada_b)
    w_p = _pad_w_in(w_in).astype(BF16)
    tables = _rope_tables()
    outs = []
    moe = (moe_w1, moe_b1, moe_w2, moe_b2)
    for i in range(DEPTH):
        p = {name: weights[name][i] for name in _LAYER_PARAMS}
        lam_init = 0.8 - 0.6 * math.exp(-0.3 * i)
        caches = (cache_da_k[:, i], cache_da_v[:, i], state_rwkv[:, i], state_ret[:, i])
        x, ctx_out = _layer(x, mod[i], w_p, p, moe, i, lam_init, caches, tables, final_norm_w, i == DEPTH - 1)
        outs.append(ctx_out)
    y_prompt = x[0].reshape(BATCH, SEQ, D_MODEL)
    y_sample = x[1].reshape(DEC_BATCH, DEC_SEQ, D_MODEL)
    stack = lambda j: jnp.stack([o[j] for o in outs], axis=1)
    return (y_prompt, y_sample, stack(0), stack(1), stack(2), stack(3))


def _pad_w_in(w_in):
    da, rw, ret, cv, gate = jnp.split(w_in, [1536, 3456, 4992, 6016], axis=-1)
    pad = jnp.zeros(w_in.shape[:-1] + (P_DA - RW_COLS,), w_in.dtype)
    return jnp.concatenate([rw, pad, da, ret, cv, gate], axis=-1)
```

```python
import functools
import math

import jax
import jax.numpy as jnp
from jax import lax
from jax.experimental import pallas as pl
from jax.experimental.pallas import tpu as pltpu

F32 = jnp.float32
BF16 = jnp.bfloat16

D_MODEL = 1024
BATCH = 16
SEQ = 256
DEPTH = 2
DEC_BATCH = 2
DEC_SEQ = 4096
PAST_LEN = 512
GRID_W = 64
EPS = 1e-6

DA_HEADS = 4
DA_QK = 64
DA_V = 128
ROPE_BASE = 10000.0

RW_HEADS = 8
RW_HD = 64
RW_W = 512
RW_G_LORA = 128
RW_GN_EPS = 64e-5

RET_HEADS = 4
RET_DK = 64
RET_DV = 128
RET_CHUNK = 128

CV_W = 512
CONV_K = 31
BR_W = 512
N_BRANCH = 4

N_EXPERTS = 32
TOP_K = 4
D_FF = 1024
SWIGLU_LIMIT = 7.0
SWIGLU_ALPHA = 1.702

N_CTX = BATCH * SEQ
N_LAT = DEC_BATCH * DEC_SEQ
N_TOK = N_CTX + N_LAT
N_SEQS = BATCH + DEC_BATCH
MOD_ROWS = 8
MOD_GROUP = 4096

RW_COLS = 1920
P_RW = 0
P_DA = 2048
P_RET = 3584
P_CV = 5120
P_GATE = 6144
P_COLS = 10240
B_DA, B_RET, B_CV, B_GATE = (p - P_DA for p in (P_DA, P_RET, P_CV, P_GATE))

VMEM_LIMIT = 48 * 1024 * 1024
EXPERT_VMEM_LIMIT = 56 * 1024 * 1024


def _cparams(sem):
    return pltpu.CompilerParams(dimension_semantics=sem, vmem_limit_bytes=VMEM_LIMIT)


def _dg(a, b, dims):
    return lax.dot_general(a, b, (dims, ((), ())), preferred_element_type=F32)


NN = ((1,), (0,))
NT = ((1,), (1,))
TN = ((0,), (0,))


def _dot1(a, b, dims=NN):
    return _dg(a.astype(BF16), b.astype(BF16), dims)


def _split(x):
    hi = x.astype(BF16)
    lo = (x - hi.astype(F32)).astype(BF16)
    return hi, lo


def _dot3(a, b, dims=NN):
    ah, al = _split(a)
    bh, bl = _split(b)
    return _dg(ah, bh, dims) + (_dg(ah, bl, dims) + _dg(al, bh, dims))


def _dot2x(a, e, dims=NN):
    ah, al = _split(a)
    am = (a - ah.astype(F32) - al.astype(F32)).astype(BF16)
    eb = e.astype(BF16)
    return _dg(ah, eb, dims) + (_dg(al, eb, dims) + _dg(am, eb, dims))


def _sigmoid(x):
    return 1.0 / (1.0 + jnp.exp(-x))


def _softplus(x):
    return jnp.maximum(x, 0.0) + jnp.log(1.0 + jnp.exp(-jnp.abs(x)))


def _seq_pos(row):
    in_ctx = row < N_CTX
    pos = jnp.where(in_ctx, row & (SEQ - 1), (row - N_CTX) & (DEC_SEQ - 1))
    length = jnp.where(in_ctx, SEQ, DEC_SEQ)
    return pos, length


def _seq_index(row):
    return jnp.where(row < N_CTX, row // SEQ, BATCH + (row - N_CTX) // DEC_SEQ)


def _dotx(e, b, dims=NN):
    bh, bl = _split(b)
    bm = (b - bh.astype(F32) - bl.astype(F32)).astype(BF16)
    eb = e.astype(BF16)
    return _dg(eb, bh, dims) + (_dg(eb, bl, dims) + _dg(eb, bm, dims))


def _mod_kernel(c_ref, w_ref, b_ref, o_ref):
    c = c_ref[...]
    s = c * _sigmoid(c)
    o_ref[0] = _dot3(s, w_ref[0]) + b_ref[0]


def _modulation(cvec, ada_w, ada_b):
    tn = 1536
    return pl.pallas_call(
        _mod_kernel,
        grid=(DEPTH, 6 * D_MODEL // tn),
        in_specs=[
            pl.BlockSpec((MOD_ROWS, D_MODEL), lambda l, j: (0, 0)),
            pl.BlockSpec((1, D_MODEL, tn), lambda l, j: (l, 0, j)),
            pl.BlockSpec((1, 1, tn), lambda l, j: (l, 0, j)),
        ],
        out_specs=pl.BlockSpec((1, MOD_ROWS, tn), lambda l, j: (l, 0, j)),
        out_shape=jax.ShapeDtypeStruct((DEPTH, MOD_ROWS, 6 * D_MODEL), F32),
        compiler_params=_cparams(("parallel", "parallel")),
        name="modulation",
    )(cvec, ada_w, ada_b.reshape(DEPTH, 1, 6 * D_MODEL))


def _mod_row(mod_ref, first_row):
    g = first_row // MOD_GROUP
    return mod_ref[pl.ds(g, 1), :]


def _rms(x, w):
    return x * lax.rsqrt(jnp.mean(x * x, axis=-1, keepdims=True) + EPS) * w


IN_TM = 1024
IN_TN = 1024


IN_NA = P_DA // IN_TN


def _inproj_kernel(x_ref, mod_ref, nw_ref, w_ref, oa_ref, og_ref, h_ref):
    i = pl.program_id(0)
    j = pl.program_id(1)

    @pl.when(j == 0)
    def _():
        m = _mod_row(mod_ref, i * IN_TM)
        sh = m[:, 0:D_MODEL]
        sc = m[:, D_MODEL:2 * D_MODEL]
        h_ref[...] = (_rms(x_ref[...], nw_ref[...]) * (1.0 + sc) + sh).astype(BF16)

    acc = _dg(h_ref[...], w_ref[0].astype(BF16), NN)

    @pl.when(j < IN_NA)
    def _():
        oa_ref[...] = acc

    @pl.when(j >= IN_NA)
    def _():
        og_ref[...] = acc.astype(BF16)


def _input_projection(x, mod, norm_w, w_p, layer):
    n = x.shape[0]
    return pl.pallas_call(
        _inproj_kernel,
        grid=(n // IN_TM, P_COLS // IN_TN),
        in_specs=[
            pl.BlockSpec((IN_TM, D_MODEL), lambda i, j: (i, 0)),
            pl.BlockSpec((MOD_ROWS, 6 * D_MODEL), lambda i, j: (0, 0)),
            pl.BlockSpec((1, D_MODEL), lambda i, j: (0, 0)),
            pl.BlockSpec((1, D_MODEL, IN_TN), lambda i, j: (layer, 0, j)),
        ],
        out_specs=[pl.BlockSpec((IN_TM, IN_TN), lambda i, j: (i, jnp.minimum(j, IN_NA - 1))),
                   pl.BlockSpec((IN_TM, IN_TN), lambda i, j: (i, jnp.maximum(j - IN_NA, 0)))],
        out_shape=[jax.ShapeDtypeStruct((n, P_DA), F32), jax.ShapeDtypeStruct((n, P_COLS - P_DA), BF16)],
        scratch_shapes=[pltpu.VMEM((IN_TM, D_MODEL), BF16)],
        compiler_params=_cparams(("parallel", "arbitrary")),
        name="input_projection",
    )(x, mod, norm_w.reshape(1, D_MODEL), w_p)


QK_TM = 512
DA_W = DA_HEADS * 2 * DA_QK


def _qkprep_kernel(q_ref, k_ref, v_ref, c_ref, se_ref, so_ref, qo_ref, ko_ref, vo_ref):
    i = pl.program_id(0)
    for h in range(DA_HEADS):
        vo_ref[:, h * 2 * DA_V:h * 2 * DA_V + DA_V] = v_ref[:, h * DA_V:(h + 1) * DA_V].astype(BF16)
        vo_ref[:, h * 2 * DA_V + DA_V:(h + 1) * 2 * DA_V] = jnp.ones((QK_TM, DA_V), BF16)
    scale = DA_QK ** -0.5

    @pl.when(i * QK_TM < N_CTX)
    def _():
        qo_ref[...] = (q_ref[...].astype(F32) * scale).astype(BF16)
        ko_ref[...] = k_ref[...].astype(BF16)

    @pl.when(i * QK_TM >= N_CTX)
    def _():
        c = c_ref[...]
        se = se_ref[...]
        so = so_ref[...]

        def rope(x):
            nxt = pltpu.roll(x, DA_W - 1, axis=1)
            prv = pltpu.roll(x, 1, axis=1)
            return x * c + nxt * se + prv * so

        qo_ref[...] = (rope(q_ref[...].astype(F32)) * scale).astype(BF16)
        ko_ref[...] = rope(k_ref[...].astype(F32)).astype(BF16)


def _rope_tables():
    rows = DEC_SEQ // GRID_W
    row = jnp.repeat(jnp.arange(rows, dtype=F32), GRID_W)
    col = jnp.tile(jnp.arange(GRID_W, dtype=F32), rows)
    n_pairs = DA_QK // 4
    inv = ROPE_BASE ** (-jnp.arange(n_pairs, dtype=F32) / n_pairs)
    ang = jnp.concatenate([row[:, None] * inv, col[:, None] * inv], axis=-1)
    cos = jnp.repeat(jnp.cos(ang), 2, axis=-1)
    sin = jnp.repeat(jnp.sin(ang), 2, axis=-1)
    even = (jnp.arange(DA_QK) % 2 == 0)[None, :]
    s_even = jnp.where(even, -sin, 0.0)
    s_odd = jnp.where(even, 0.0, sin)
    rep = lambda t: jnp.tile(t, (1, DA_W // DA_QK))
    return rep(cos), rep(s_even), rep(s_odd)


def _qk_prepare(proj, tables):
    n = proj.shape[0]
    lat0 = N_CTX // QK_TM
    nlat = DEC_SEQ // QK_TM
    tab = pl.BlockSpec((QK_TM, DA_W), lambda i: (jnp.maximum(i - lat0, 0) % nlat, 0))
    c0 = B_DA // DA_W
    out = jax.ShapeDtypeStruct((n, DA_W), BF16)
    return pl.pallas_call(
        _qkprep_kernel,
        grid=(n // QK_TM,),
        in_specs=[
            pl.BlockSpec((QK_TM, DA_W), lambda i: (i, c0)),
            pl.BlockSpec((QK_TM, DA_W), lambda i: (i, c0 + 1)),
            pl.BlockSpec((QK_TM, DA_W), lambda i: (i, c0 + 2)),
            tab, tab, tab,
        ],
        out_specs=[pl.BlockSpec((QK_TM, DA_W), lambda i: (i, 0))] * 2
        + [pl.BlockSpec((QK_TM, 2 * DA_W), lambda i: (i, 0))],
        out_shape=[out, out, jax.ShapeDtypeStruct((n, 2 * DA_W), BF16)],
        compiler_params=_cparams(("parallel",)),
        name="qk_prepare",
    )(proj, proj, proj, *tables)


DA_TQ = 256


def _attend(q, segments, lam, nw, o_ref):
    for h in range(DA_HEADS):
        os = []
        for m in range(2):
            c0 = (2 * h + m) * DA_QK
            ss = [_dg(q[:, c0:c0 + DA_QK], k[:, c0:c0 + DA_QK], NT) for k, _ in segments]
            mx = jnp.max(ss[0], axis=-1, keepdims=True)
            for s in ss[1:]:
                mx = jnp.maximum(mx, jnp.max(s, axis=-1, keepdims=True))
            oe = None
            for s, (_, v) in zip(ss, segments):
                part = _dg(jnp.exp(s - mx).astype(BF16), v[:, h * 2 * DA_V:(h + 1) * 2 * DA_V], NN)
                oe = part if oe is None else oe + part
            os.append(oe[:, :DA_V] * (1.0 / oe[:, DA_V:DA_V + 1]))
        o_ref[:, h * DA_V:(h + 1) * DA_V] = _rms(os[0] - lam * os[1], nw).astype(BF16)


def _da_kernel(q_ref, ks_ref, vs_ref, kl_ref, vl_ref, dl_ref, nw_ref, o_ref, *, lam_init):
    dl = dl_ref[...]
    lam = (jnp.exp(jnp.sum(dl[0:1] * dl[1:2], axis=1, keepdims=True))
           - jnp.exp(jnp.sum(dl[2:3] * dl[3:4], axis=1, keepdims=True)) + lam_init)
    nw = nw_ref[...] * (1.0 - lam_init)
    q = q_ref[...]
    is_ctx = pl.program_id(0) < BATCH * (SEQ // DA_TQ)

    @pl.when(is_ctx)
    def _():
        _attend(q, [(ks_ref, vs_ref)], lam, nw, o_ref)

    @pl.when(jnp.logical_not(is_ctx))
    def _():
        _attend(q, [(kl_ref.at[0], vl_ref.at[0])], lam, nw, o_ref)


def _da_branch(proj, cache_k, cache_v, da_lambda, da_norm_w, lam_init, tables):
    qb, kb, vb = _qk_prepare(proj, tables)
    n = proj.shape[0]
    ck = cache_k.reshape(DEC_BATCH, PAST_LEN, DA_W).astype(BF16)
    cv = cache_v.astype(BF16)
    cv = jnp.concatenate([cv, jnp.ones_like(cv)], axis=-1).reshape(DEC_BATCH, PAST_LEN, 2 * DA_W)
    lat = lambda t: t[N_CTX:].reshape(DEC_BATCH, DEC_SEQ, t.shape[-1])
    k_all = jnp.concatenate([ck, lat(kb)], axis=1)
    v_all = jnp.concatenate([cv, lat(vb)], axis=1)
    tk = PAST_LEN + DEC_SEQ
    n_ctx_tiles = N_CTX // DA_TQ
    per_seq = DEC_SEQ // DA_TQ
    seq_c = lambda i: jnp.minimum(i * DA_TQ // SEQ, BATCH - 1)
    seq_l = lambda i: jnp.clip((i - n_ctx_tiles) // per_seq, 0, DEC_BATCH - 1)
    once = pl.Buffered(1)
    return pl.pallas_call(
        functools.partial(_da_kernel, lam_init=lam_init),
        grid=(n // DA_TQ,),
        in_specs=[
            pl.BlockSpec((DA_TQ, DA_W), lambda i: (i, 0)),
            pl.BlockSpec((SEQ, DA_W), lambda i: (seq_c(i), 0)),
            pl.BlockSpec((SEQ, 2 * DA_W), lambda i: (seq_c(i), 0)),
            pl.BlockSpec((1, tk, DA_W), lambda i: (seq_l(i), 0, 0), pipeline_mode=once),
            pl.BlockSpec((1, tk, 2 * DA_W), lambda i: (seq_l(i), 0, 0), pipeline_mode=once),
            pl.BlockSpec((4, DA_QK), lambda i: (0, 0)),
            pl.BlockSpec((1, DA_V), lambda i: (0, 0)),
        ],
        out_specs=pl.BlockSpec((DA_TQ, DA_W), lambda i: (i, 0)),
        out_shape=jax.ShapeDtypeStruct((n, DA_W), BF16),
        compiler_params=_cparams(("arbitrary",)),
        name="diff_attention",
    )(qb, kb, vb, k_all, v_all, da_lambda, da_norm_w.reshape(1, DA_V))


RW_TM = 256
RW_C = 64
RW_PAIRS = RW_HEADS // 2
HALO = 8


def _head_ones():
    idx = jnp.arange(2 * RW_HD) // RW_HD
    return (idx[:, None] == idx[None, :]).astype(BF16)


def _head_sum(x, ones_pair):
    w = 2 * RW_HD
    return jnp.concatenate([_dot2x(x[:, p * w:(p + 1) * w], ones_pair) for p in range(RW_PAIRS)], axis=1)


def _rwprep_kernel(x_ref, xp_ref, xn_ref, mu_ref, w0_ref, wup_ref, a0_ref, aup_ref, gup_ref, kk_ref, ka_ref,
                   ones_ref, r_o, v_o, kk_o, g_o, kd_o, lw_o, a_o, buf):
    row0 = pl.program_id(0) * RW_TM
    x = x_ref[:, 0:RW_COLS]
    buf[HALO:HALO + RW_TM, :] = x
    buf[HALO - 1:HALO, :] = xp_ref[HALO - 1:HALO, 0:RW_COLS]
    buf[HALO + RW_TM:HALO + RW_TM + 1, :] = xn_ref[0:1, 0:RW_COLS]
    rows = row0 + lax.broadcasted_iota(jnp.int32, (RW_TM, 1), 0)
    pos, length = _seq_pos(rows)
    prev = jnp.where(pos == 0, 0.0, buf[HALO - 1:HALO - 1 + RW_TM, :])
    nxt = jnp.where(pos == length - 1, 0.0, buf[HALO + 1:HALO + 1 + RW_TM, :])
    mu = mu_ref[...]
    u = x + mu[0:1] * (prev - x) + mu[1:2] * (nxt - x)

    r = u[:, 0:RW_W]
    k = u[:, RW_W:2 * RW_W]
    v = u[:, 2 * RW_W:3 * RW_W]
    wl = u[:, 3 * RW_W:3 * RW_W + 128]
    al = u[:, 3 * RW_W + 128:3 * RW_W + 256]
    gl = u[:, 3 * RW_W + 256:3 * RW_W + 384]
    w_raw = w0_ref[...] + _dot1(jnp.tanh(wl), wup_ref[...])
    lw = -math.exp(-0.5) * _sigmoid(w_raw)
    a = _sigmoid(a0_ref[...] + _dot1(al, aup_ref[...]))
    g = _dot1(_sigmoid(gl), gup_ref[...])
    kk = k * kk_ref[...]
    kk = kk * lax.rsqrt(jnp.maximum(_head_sum(kk * kk, ones_ref[...]), 1e-12))
    kd = jnp.concatenate([k, k], axis=1) * (1.0 + (a - 1.0) * ka_ref[...])

    r_o[...] = r
    v_o[...] = v
    kk_o[...] = kk
    g_o[...] = g
    for d in range(2):
        kd_o[d] = kd[:, d * RW_W:(d + 1) * RW_W]
        lw_o[d] = lw[:, d * RW_W:(d + 1) * RW_W]
        a_o[d] = a[:, d * RW_W:(d + 1) * RW_W]


def _rw_prepare(proj, p):
    n = proj.shape[0]
    nh = n // HALO
    steps = RW_TM // HALO
    wide = P_DA - P_RW
    cat2 = lambda t: t.reshape(1, 2 * RW_W)
    blockdiag = lambda t: jnp.concatenate(
        [jnp.concatenate([t[0], jnp.zeros_like(t[0])], axis=1),
         jnp.concatenate([jnp.zeros_like(t[1]), t[1]], axis=1)], axis=0)
    const = lambda shape: pl.BlockSpec(shape, lambda i: (0,) * len(shape))
    row = pl.BlockSpec((RW_TM, RW_W), lambda i: (i, 0))
    row2 = pl.BlockSpec((2, RW_TM, RW_W), lambda i: (0, i, 0))
    o1 = jax.ShapeDtypeStruct((n, RW_W), F32)
    o2 = jax.ShapeDtypeStruct((2, n, RW_W), F32)
    return pl.pallas_call(
        _rwprep_kernel,
        grid=(n // RW_TM,),
        in_specs=[
            pl.BlockSpec((RW_TM, wide), lambda i: (i, 0)),
            pl.BlockSpec((HALO, wide), lambda i: (jnp.maximum(i * steps - 1, 0), 0)),
            pl.BlockSpec((HALO, wide), lambda i: (jnp.minimum((i + 1) * steps, nh - 1), 0)),
            const((2, RW_COLS)), const((1, 2 * RW_W)), const((128, 2 * RW_W)), const((1, 2 * RW_W)),
            const((128, 2 * RW_W)), const((RW_G_LORA, RW_W)), const((1, RW_W)), const((1, 2 * RW_W)),
            const((2 * RW_HD, 2 * RW_HD)),
        ],
        out_specs=[row, row, row, row, row2, row2, row2],
        out_shape=[o1, o1, o1, o1, o2, o2, o2],
        scratch_shapes=[pltpu.VMEM((RW_TM + 2 * HALO, RW_COLS), F32)],
        compiler_params=_cparams(("parallel",)),
        name="rwkv_prepare",
    )(proj, proj, proj, p['rw_shift'], cat2(p['rw_w0']), blockdiag(p['rw_w_up']), cat2(p['rw_a0']),
      blockdiag(p['rw_a_up']), p['rw_g_up'], p['rw_k_k'].reshape(1, RW_W),
      jnp.tile(p['rw_k_a'].reshape(1, RW_W), (1, 2)), _head_ones())


RW_TB = 256
RW_NC = RW_TB // RW_C
RW_SIDE = 16


def _rw_finish(y, r, v, g, kd_sum, rk, lnw, lnb, ones):
    inv = 1.0 / RW_HD
    xc = y - _head_sum(y, ones) * inv
    var = _head_sum(xc * xc, ones) * inv
    yn = xc * lax.rsqrt(var + RW_GN_EPS) * lnw + lnb
    bonus = _head_sum(r * kd_sum * rk, ones) * v
    return ((yn + bonus) * g).astype(BF16)


def _rwscan_kernel(r_ref, v_ref, kk_ref, kd_ref, lw_ref, a_ref, tri_ref, inc_ref, str_ref, s0_ref, *rest, backward):
    if backward:
        yf_ref, g_ref, rk_ref, lnw_ref, lnb_ref, ones_ref, y_ref, st_ref, s_scr = rest
    else:
        y_ref, st_ref, s_scr = rest
    _rwscan_body(r_ref, v_ref, kk_ref, kd_ref, lw_ref, a_ref, tri_ref, inc_ref, str_ref, s0_ref, y_ref, st_ref,
                 s_scr, backward,
                 (lambda yb: _rw_finish(yf_ref[...] + yb, r_ref[...], v_ref[...], g_ref[...],
                                        kd_ref[0] + kd_ref[1], rk_ref[...], lnw_ref[...], lnb_ref[...],
                                        ones_ref[...])) if backward else None)


def _rwscan_body(r_ref, v_ref, kk_ref, kd_ref, lw_ref, a_ref, tri_ref, inc_ref, str_ref, s0_ref, y_ref, st_ref,
                 s_scr, backward, finish):
    step = pl.program_id(0)
    nb = pl.num_programs(0)
    bi = (nb - 1 - step) if backward else step
    pos, length = _seq_pos(bi * RW_TB)
    first = (pos + RW_TB == length) if backward else (pos == 0)
    last = (pos == 0) if backward else (pos + RW_TB == length)
    in_ctx = bi * RW_TB < N_CTX
    c = RW_C

    @pl.when(jnp.logical_and(first, in_ctx))
    def _():
        s_scr[...] = jnp.zeros_like(s_scr)

    @pl.when(jnp.logical_and(first, jnp.logical_not(in_ctx)))
    def _():
        z = jnp.zeros((c, c), F32)
        for p in range(RW_PAIRS):
            s_scr[p] = jnp.concatenate(
                [jnp.concatenate([s0_ref[0, 0, 2 * p], z], axis=1),
                 jnp.concatenate([z, s0_ref[0, 0, 2 * p + 1]], axis=1)], axis=0)

    incl2 = inc_ref[...] > 0.5
    strict2 = str_ref[...] > 0.5
    eye2 = (lax.broadcasted_iota(jnp.int32, (2 * c, 2 * c), 0)
            == lax.broadcasted_iota(jnp.int32, (2 * c, 2 * c), 1))
    m_e = lax.broadcasted_iota(jnp.int32, (1, 2 * c), 1) < c

    lw = lw_ref[0]
    cum = _dotx(tri_ref[...], lw)
    tots = [cum[(j * c if backward else j * c + c - 1):(j * c + 1 if backward else j * c + c), :]
            for j in range(RW_NC)]
    tot_b = jnp.concatenate([jnp.broadcast_to(t, (c, RW_W)) for t in tots], axis=0)
    kk = kk_ref[...]
    kd = kd_ref[1 if backward else 0]
    bp = kk * a_ref[0]
    g_inv = jnp.exp(-cum)
    g_rem = jnp.exp(tot_b - cum)
    ag = -kk * jnp.exp(cum - lw)
    rg = r_ref[...] * jnp.exp(cum)
    bdn = bp * g_inv
    kdn = kd * g_inv
    bc = bp * g_rem
    kc = kd * g_rem
    v = v_ref[...]

    def stack(x, j, p):
        xs = x[j * c:(j + 1) * c, p * 2 * c:(p + 1) * 2 * c]
        return jnp.concatenate([jnp.where(m_e, xs, 0.0), jnp.where(m_e, 0.0, xs)], axis=0).astype(BF16)

    pre = {}
    keys = [(j, p) for j in range(RW_NC) for p in range(RW_PAIRS)]
    for g0 in range(0, len(keys), RW_SIDE):
        grp = keys[g0:g0 + RW_SIDE]
        ops = {k: tuple(stack(t, *k) for t in (ag, rg, bdn, kdn, bc, kc, v)) for k in grp}
        gm = {k: _dg(jnp.concatenate([ops[k][0], ops[k][1]], axis=0),
                     jnp.concatenate([ops[k][2], ops[k][3]], axis=0), NT) for k in grp}
        lbb = {k: jnp.where(strict2, gm[k][:2 * c, :2 * c], 0.0) for k in grp}
        lkb = {k: jnp.where(strict2, gm[k][:2 * c, 2 * c:], 0.0).astype(BF16) for k in grp}
        lrk = {k: jnp.concatenate([jnp.where(incl2, gm[k][2 * c:, :2 * c], 0.0),
                                   jnp.where(incl2, gm[k][2 * c:, 2 * c:], 0.0)], axis=1).astype(BF16)
               for k in grp}
        lv = {k: _dg(lkb[k], ops[k][6], NN) for k in grp}
        x = {k: jnp.where(eye2, 1.0, lbb[k]) for k in grp}
        pw = lbb
        for _ in range(int(math.log2(c)) - 1):
            pwb = {k: pw[k].astype(BF16) for k in grp}
            pw = {k: _dg(pwb[k], pwb[k], NN) for k in grp}
            x = {k: x[k] + _dg(x[k].astype(BF16), pw[k].astype(BF16), NN) for k in grp}
        tw = {k: _dg(x[k].astype(BF16), jnp.concatenate([lv[k].astype(BF16), ops[k][0]], axis=1), NN)
              for k in grp}
        for k in grp:
            pre[k] = (tw[k][:, :2 * c], tw[k][:, 2 * c:].astype(BF16), ops[k][1], lrk[k], ops[k][6],
                      jnp.concatenate([ops[k][4], ops[k][5]], axis=0))

    order = range(RW_NC - 1, -1, -1) if backward else range(RW_NC)
    pairs = range(RW_PAIRS)
    s = [s_scr[p] for p in pairs]
    ys = {}
    for j in order:
        sb = [s[p].astype(BF16) for p in pairs]
        u = [_dg(pre[j, p][1], sb[p], NT) + pre[j, p][0] for p in pairs]
        uv = [jnp.concatenate([u[p].astype(BF16), pre[j, p][4]], axis=0) for p in pairs]
        y = [_dg(pre[j, p][2], sb[p], NT) + _dg(pre[j, p][3], uv[p], NN) for p in pairs]
        s = [s[p] * jnp.exp(tots[j][:, p * 2 * c:(p + 1) * 2 * c]) + _dg(uv[p], pre[j, p][5], TN) for p in pairs]
        ys[j] = [y[p][:c] + y[p][c:] for p in pairs]
    y_blk = jnp.concatenate([jnp.concatenate(ys[j], axis=1) for j in range(RW_NC)], axis=0)
    y_ref[...] = finish(y_blk) if finish else y_blk
    for p in pairs:
        s_scr[p] = s[p]

    @pl.when(last)
    def _():
        for p in pairs:
            st_ref[0, 2 * p] = s[p][:c, :c]
            st_ref[0, 2 * p + 1] = s[p][c:, c:]


def _scan_masks(backward):
    t = jnp.arange(RW_TB)
    sgn = -1 if backward else 1
    same_chunk = (t[:, None] // RW_C) == (t[None, :] // RW_C)
    tri = (same_chunk & ((t[:, None] - t[None, :]) * sgn >= 0)).astype(BF16)
    q = jnp.arange(2 * RW_C)
    same_head = (q[:, None] // RW_C) == (q[None, :] // RW_C)
    dif = ((q[:, None] % RW_C) - (q[None, :] % RW_C)) * sgn
    return tri, (same_head & (dif >= 0)).astype(F32), (same_head & (dif > 0)).astype(F32)


def _rw_scan(r, v, kk, kd, lw, a, s0, backward, finish=()):
    n = r.shape[0]
    nb = n // RW_TB
    d = 1 if backward else 0
    blk = (lambda i: nb - 1 - i) if backward else (lambda i: i)
    row = pl.BlockSpec((RW_TB, RW_W), lambda i: (blk(i), 0))
    row2 = pl.BlockSpec((1, RW_TB, RW_W), lambda i: (d, blk(i), 0))
    both = pl.BlockSpec((2, RW_TB, RW_W), lambda i: (0, blk(i), 0))
    const = lambda shape: pl.BlockSpec(shape, lambda i: (0,) * len(shape))
    vec = const((1, RW_W))
    pair = 2 * RW_C
    extra = [row, row, vec, vec, vec, const((pair, pair))] if backward else []
    return pl.pallas_call(
        functools.partial(_rwscan_kernel, backward=backward),
        grid=(nb,),
        in_specs=[row, row, row, both, row2, row2, const((RW_TB, RW_TB)), const((pair, pair)), const((pair, pair)),
                  pl.BlockSpec((1, 1, RW_HEADS, RW_HD, RW_HD),
                               lambda i: (jnp.maximum(_seq_index(blk(i) * RW_TB) - BATCH, 0), d, 0, 0, 0))] + extra,
        out_specs=[row, pl.BlockSpec((1, RW_HEADS, RW_HD, RW_HD), lambda i: (_seq_index(blk(i) * RW_TB), 0, 0, 0))],
        out_shape=[jax.ShapeDtypeStruct((n, RW_W), BF16 if backward else F32),
                   jax.ShapeDtypeStruct((N_SEQS, RW_HEADS, RW_HD, RW_HD), F32)],
        scratch_shapes=[pltpu.VMEM((RW_PAIRS, pair, pair), F32)],
        compiler_params=_cparams(("arbitrary",)),
        name="rwkv_scan_bwd" if backward else "rwkv_scan_fwd",
    )(r, v, kk, kd, lw, a, *_scan_masks(backward), s0, *finish)


def _rwkv_branch(proj, state0, p):
    r, v, kk, g, kd, lw, a = _rw_prepare(proj, p)
    yf, sf = _rw_scan(r, v, kk, kd, lw, a, state0, backward=False)
    finish = (yf, g, p['rw_r_k'].reshape(1, RW_W), p['rw_ln_w'].reshape(1, RW_W), p['rw_ln_b'].reshape(1, RW_W),
              _head_ones())
    out, sb = _rw_scan(r, v, kk, kd, lw, a, state0, backward=True, finish=finish)
    return out, jnp.stack([sf, sb], axis=1)


RET_W = RET_HEADS * RET_DV
RET_PAIRS = RET_HEADS // 2


RET_TB = 256
RET_NC = RET_TB // RET_CHUNK


def _ret_kernel(q_ref, k_ref, v_ref, lg_ref, s0_ref, *rest, backward):
    if backward:
        of_ref, g_ref, nw_ref, o_ref, st_ref, s_scr = rest
    else:
        o_ref, st_ref, s_scr = rest
    step = pl.program_id(0)
    nb = pl.num_programs(0)
    bi = (nb - 1 - step) if backward else step
    pos, length = _seq_pos(bi * RET_TB)
    first = (pos + RET_TB == length) if backward else (pos == 0)
    in_ctx = bi * RET_TB < N_CTX

    @pl.when(jnp.logical_and(first, in_ctx))
    def _():
        s_scr[...] = jnp.zeros_like(s_scr)

    @pl.when(jnp.logical_and(first, jnp.logical_not(in_ctx)))
    def _():
        s_scr[...] = s0_ref[0, 0]

    c = RET_CHUNK
    d = 1 if backward else 0
    sgn = -1 if backward else 1
    lgs = -_softplus(-lg_ref[d:d + 1, :])
    ri = lax.broadcasted_iota(jnp.int32, (c, c), 0)
    cj = lax.broadcasted_iota(jnp.int32, (c, c), 1)
    dif = (ri - cj) * sgn
    valid = dif >= 0
    dist = jnp.maximum(dif, 0).astype(F32)
    pr = lax.broadcasted_iota(jnp.int32, (c, 2 * RET_DK), 0)
    tau = ((c - 1 - pr) if backward else pr).astype(F32)
    low = lax.broadcasted_iota(jnp.int32, (1, 2 * RET_DK), 1) < RET_DK
    rlow = lax.broadcasted_iota(jnp.int32, (2 * RET_DK, RET_DV), 0) < RET_DK
    heads = range(RET_HEADS)
    pairs = range(RET_PAIRS)
    mask = [low, jnp.logical_not(low)]
    lg_h = [lgs[:, h:h + 1] for h in heads]
    lg_row = [jnp.where(low, lg_h[2 * p], lg_h[2 * p + 1]) for p in pairs]
    dmat = [jnp.where(valid, jnp.exp(lg_h[h] * dist), 0.0) for h in heads]
    q_dec = [jnp.exp(lg_row[p] * (tau + 1.0)) for p in pairs]
    k_dec = [jnp.exp(lg_row[p] * (c - 1.0 - tau)) for p in pairs]
    c_dec = [jnp.where(rlow, jnp.exp(lg_h[2 * p] * c), jnp.exp(lg_h[2 * p + 1] * c)) for p in pairs]

    pre = []
    for j in range(RET_NC):
        rows = slice(j * c, (j + 1) * c)
        pw = 2 * RET_DK
        qp = [q_ref[rows, p * pw:(p + 1) * pw] for p in pairs]
        kp = [k_ref[rows, p * pw:(p + 1) * pw] * (RET_DK ** -0.5) for p in pairs]
        kpb = [t.astype(BF16) for t in kp]
        vb = [v_ref[rows, h * RET_DV:(h + 1) * RET_DV].astype(BF16) for h in heads]
        att = [_dg(jnp.where(mask[h % 2], qp[h // 2], 0.0).astype(BF16), kpb[h // 2], NT) * dmat[h]
               for h in heads]
        upd = [_dg(jnp.where(mask[h % 2], kp[h // 2] * k_dec[h // 2], 0.0).astype(BF16), vb[h], TN)
               for h in heads]
        inner = [_dg(att[h].astype(BF16), vb[h], NN) for h in heads]
        qd = [jnp.where(mask[h % 2], qp[h // 2] * q_dec[h // 2], 0.0).astype(BF16) for h in heads]
        pre.append((inner, qd, upd))

    s = [s_scr[p] for p in pairs]
    for j in (range(RET_NC - 1, -1, -1) if backward else range(RET_NC)):
        inner, qd, upd = pre[j]
        sb = [t.astype(BF16) for t in s]
        for h in heads:
            rows, cols = slice(j * c, (j + 1) * c), slice(h * RET_DV, (h + 1) * RET_DV)
            val = inner[h] + _dg(qd[h], sb[h // 2], NN)
            if backward:
                gh = g_ref[rows, cols].astype(F32)
                val = (gh * _sigmoid(gh) * (_standardize(of_ref[rows, cols] + val, EPS) * nw_ref[:, cols])
                       ).astype(BF16)
            o_ref[rows, cols] = val
        s = [s[p] * c_dec[p] + upd[2 * p] + upd[2 * p + 1] for p in pairs]
    for p in pairs:
        s_scr[p] = s[p]
        st_ref[0, p] = s[p]


def _ret_scan(proj, logit, s0, backward, finish=()):
    n = proj.shape[0]
    nb = n // RET_TB
    d = 1 if backward else 0
    blk = (lambda i: nb - 1 - i) if backward else (lambda i: i)
    qk_w = RET_HEADS * RET_DK
    state = (RET_PAIRS, 2 * RET_DK, RET_DV)
    extra, extra_args = [], ()
    if backward:
        extra = [pl.BlockSpec((RET_TB, RET_W), lambda i: (blk(i), 0)),
                 pl.BlockSpec((RET_TB, RET_W), lambda i: (blk(i), (B_RET + 2 * qk_w + RET_W) // RET_W)),
                 pl.BlockSpec((1, RET_W), lambda i: (0, 0))]
        extra_args = (finish[0], proj, finish[1])
    return pl.pallas_call(
        functools.partial(_ret_kernel, backward=backward),
        grid=(nb,),
        in_specs=[
            pl.BlockSpec((RET_TB, qk_w), lambda i: (blk(i), B_RET // qk_w)),
            pl.BlockSpec((RET_TB, qk_w), lambda i: (blk(i), B_RET // qk_w + 1)),
            pl.BlockSpec((RET_TB, RET_W), lambda i: (blk(i), (B_RET + 2 * qk_w) // RET_W)),
            pl.BlockSpec((2, RET_HEADS), lambda i: (0, 0)),
            pl.BlockSpec((1, 1) + state,
                         lambda i: (jnp.maximum(_seq_index(blk(i) * RET_TB) - BATCH, 0), d, 0, 0, 0)),
        ] + extra,
        out_specs=[pl.BlockSpec((RET_TB, RET_W), lambda i: (blk(i), 0)),
                   pl.BlockSpec((1,) + state, lambda i: (_seq_index(blk(i) * RET_TB), 0, 0, 0))],
        out_shape=[jax.ShapeDtypeStruct((n, RET_W), BF16 if backward else F32),
                   jax.ShapeDtypeStruct((N_SEQS,) + state, F32)],
        scratch_shapes=[pltpu.VMEM(state, F32)],
        compiler_params=_cparams(("arbitrary",)),
        name="retention_scan_bwd" if backward else "retention_scan_fwd",
    )(proj, proj, proj, logit, s0, *extra_args)


def _standardize(x, eps):
    mu = jnp.mean(x, axis=-1, keepdims=True)
    xc = x - mu
    return xc * lax.rsqrt(jnp.mean(xc * xc, axis=-1, keepdims=True) + eps)


def _retention_branch(proj, state0, p):
    s0 = state0.reshape(DEC_BATCH, 2, RET_PAIRS, 2 * RET_DK, RET_DV)
    of, sf = _ret_scan(proj, p['ret_decay_logit'], s0, backward=False)
    out, sb = _ret_scan(proj, p['ret_decay_logit'], s0, backward=True,
                        finish=(of, p['ret_norm_w'].reshape(1, RET_W)))
    st = jnp.stack([sf, sb], axis=1).reshape(N_SEQS, 2, RET_HEADS, RET_DK, RET_DV)
    return out, st


CV_TM = 256
CV_HALO = 16


def _conv_kernel(a_ref, g_ref, ap_ref, gp_ref, an_ref, gn_ref, w_ref, b_ref, lnw_ref, lnb_ref, o_ref, buf, sbuf):
    row0 = pl.program_id(0) * CV_TM
    pos0, len0 = _seq_pos(row0)
    glu = lambda a, g: a[...].astype(F32) * _sigmoid(g[...].astype(F32))
    buf[CV_HALO:CV_HALO + CV_TM, :] = glu(a_ref, g_ref)
    buf[0:CV_HALO, :] = jnp.where(pos0 == 0, 0.0, glu(ap_ref, gp_ref))
    buf[CV_HALO + CV_TM:, :] = jnp.where(pos0 + CV_TM == len0, 0.0, glu(an_ref, gn_ref))
    base = CV_HALO - CONV_K // 2
    acc = jnp.zeros((CV_TM, CV_W), F32)
    for r in range(8):
        taps = [m for m in range((base + CONV_K + 7) // 8) if 0 <= r + 8 * m - base < CONV_K]
        span = CV_TM + 8 * max(taps)
        if r:
            sbuf[0:span, :] = buf[r:r + span, :]
        src = sbuf if r else buf
        for m in taps:
            j = r + 8 * m - base
            acc = acc + w_ref[j:j + 1, :] * src[8 * m:8 * m + CV_TM, :]
    z = _standardize(acc + b_ref[...], EPS) * lnw_ref[...] + lnb_ref[...]
    o_ref[...] = (z * _sigmoid(z)).astype(BF16)


def _conv_branch(proj, p):
    n = proj.shape[0]
    nh = n // CV_HALO
    steps = CV_TM // CV_HALO
    ca = B_CV // CV_W
    prev = lambda i: jnp.maximum(i * steps - 1, 0)
    nxt = lambda i: jnp.minimum((i + 1) * steps, nh - 1)
    vec = pl.BlockSpec((1, CV_W), lambda i: (0, 0))
    return pl.pallas_call(
        _conv_kernel,
        grid=(n // CV_TM,),
        in_specs=[
            pl.BlockSpec((CV_TM, CV_W), lambda i: (i, ca)),
            pl.BlockSpec((CV_TM, CV_W), lambda i: (i, ca + 1)),
            pl.BlockSpec((CV_HALO, CV_W), lambda i: (prev(i), ca)),
            pl.BlockSpec((CV_HALO, CV_W), lambda i: (prev(i), ca + 1)),
            pl.BlockSpec((CV_HALO, CV_W), lambda i: (nxt(i), ca)),
            pl.BlockSpec((CV_HALO, CV_W), lambda i: (nxt(i), ca + 1)),
            pl.BlockSpec((CONV_K, CV_W), lambda i: (0, 0)),
            vec, vec, vec,
        ],
        out_specs=pl.BlockSpec((CV_TM, CV_W), lambda i: (i, 0)),
        out_shape=jax.ShapeDtypeStruct((n, CV_W), BF16),
        scratch_shapes=[pltpu.VMEM((CV_TM + 2 * CV_HALO, CV_W), F32)] * 2,
        compiler_params=_cparams(("parallel",)),
        name="conformer_conv",
    )(proj, proj, proj, proj, proj, proj, p['cv_dw_w'], p['cv_dw_b'].reshape(1, CV_W),
      p['cv_ln_w'].reshape(1, CV_W), p['cv_ln_b'].reshape(1, CV_W))


MG_TM = 512


def _merge_kernel(da_ref, rw_ref, ret_ref, cv_ref, g0_ref, g1_ref, g2_ref, g3_ref, x_ref, mod_ref, wb_ref, wo_ref,
                  nw_ref, x_o, h_o):
    m = None
    for n, (br, gt) in enumerate(((da_ref, g0_ref), (rw_ref, g1_ref), (ret_ref, g2_ref), (cv_ref, g3_ref))):
        t = _sigmoid(gt[...].astype(F32)) * _dg(br[...], wb_ref[n], NN)
        m = t if m is None else m + t
    out = _dg(m.astype(BF16), wo_ref[...], NN)
    mrow = _mod_row(mod_ref, pl.program_id(0) * MG_TM)
    gate1 = mrow[:, 2 * D_MODEL:3 * D_MODEL]
    sh2 = mrow[:, 3 * D_MODEL:4 * D_MODEL]
    sc2 = mrow[:, 4 * D_MODEL:5 * D_MODEL]
    x1 = x_ref[...] + gate1 * out
    x_o[...] = x1
    h_o[...] = _rms(x1, nw_ref[...]) * (1.0 + sc2) + sh2


def _merge(branches, gates, x, mod, w_branch, w_out, norm_w):
    n = x.shape[0]
    br = pl.BlockSpec((MG_TM, BR_W), lambda i: (i, 0))
    gspec = lambda j: pl.BlockSpec((MG_TM, D_MODEL), lambda i: (i, B_GATE // D_MODEL + j))
    full = pl.BlockSpec((MG_TM, D_MODEL), lambda i: (i, 0))
    out = jax.ShapeDtypeStruct((n, D_MODEL), F32)
    return pl.pallas_call(
        _merge_kernel,
        grid=(n // MG_TM,),
        in_specs=[br, br, br, br, gspec(0), gspec(1), gspec(2), gspec(3), full,
                  pl.BlockSpec((MOD_ROWS, 6 * D_MODEL), lambda i: (0, 0)),
                  pl.BlockSpec((N_BRANCH, BR_W, D_MODEL), lambda i: (0, 0, 0)),
                  pl.BlockSpec((D_MODEL, D_MODEL), lambda i: (0, 0)),
                  pl.BlockSpec((1, D_MODEL), lambda i: (0, 0))],
        out_specs=[full, full],
        out_shape=[out, out],
        compiler_params=_cparams(("parallel",)),
        name="gated_merge",
    )(*branches, gates, gates, gates, gates, x, mod, w_branch.astype(BF16), w_out.astype(BF16),
      norm_w.reshape(1, D_MODEL))


RT_TM = 256
MOE_BM = 512
MOE_ROWS = N_TOK * TOP_K + N_EXPERTS * MOE_BM
DP_TM = 512
DP_GROUP = 64
CB_TM = 256
DMA_UNROLL = 8


def _router_kernel(h_ref, w_ref, b_ref, tri_ref, idx_o, gate_o, rank_o, cnt_o, carry):
    @pl.when(pl.program_id(0) == 0)
    def _():
        carry[...] = jnp.zeros_like(carry)

    logits = _dot3(w_ref[...], h_ref[...], NT) + b_ref[...]
    e_iota = lax.broadcasted_iota(jnp.int32, logits.shape, 0)
    work = logits
    vals, idxs, hots = [], [], []
    for _ in range(TOP_K):
        mx = jnp.max(work, axis=0, keepdims=True)
        ix = jnp.min(jnp.where(work == mx, e_iota, N_EXPERTS), axis=0, keepdims=True)
        hot = e_iota == ix
        vals.append(mx)
        idxs.append(ix)
        hots.append(hot.astype(F32))
        work = jnp.where(hot, -jnp.inf, work)
    es = [jnp.exp(v - vals[0]) for v in vals]
    inv = 1.0 / (es[0] + es[1] + es[2] + es[3])
    chosen = hots[0] + hots[1] + hots[2] + hots[3]
    ahead = carry[...][:, 0:1] + _dg(chosen.astype(BF16), tri_ref[...], NN)
    idx_o[...] = jnp.concatenate(idxs, axis=0)
    gate_o[...] = jnp.concatenate([e * inv for e in es], axis=0)
    rank_o[...] = jnp.concatenate(
        [jnp.sum(hot * ahead, axis=0, keepdims=True) for hot in hots], axis=0).astype(jnp.int32)
    carry[...] = carry[...] + jnp.sum(chosen, axis=1, keepdims=True)
    cnt_o[...] = carry[...]


def _router(h, router_w, router_b):
    n = h.shape[0]
    tri = (jnp.arange(RT_TM)[:, None] < jnp.arange(RT_TM)[None, :]).astype(BF16)
    col = pl.BlockSpec((TOP_K, RT_TM), lambda i: (0, i))
    return pl.pallas_call(
        _router_kernel,
        grid=(n // RT_TM,),
        in_specs=[
            pl.BlockSpec((RT_TM, D_MODEL), lambda i: (i, 0)),
            pl.BlockSpec((N_EXPERTS, D_MODEL), lambda i: (0, 0)),
            pl.BlockSpec((N_EXPERTS, 1), lambda i: (0, 0)),
            pl.BlockSpec((RT_TM, RT_TM), lambda i: (0, 0)),
        ],
        out_specs=[col, col, col, pl.BlockSpec((N_EXPERTS, 128), lambda i: (0, 0))],
        out_shape=[jax.ShapeDtypeStruct((TOP_K, n), jnp.int32), jax.ShapeDtypeStruct((TOP_K, n), F32),
                   jax.ShapeDtypeStruct((TOP_K, n), jnp.int32), jax.ShapeDtypeStruct((N_EXPERTS, 128), F32)],
        scratch_shapes=[pltpu.VMEM((N_EXPERTS, 128), F32)],
        compiler_params=_cparams(("arbitrary",)),
        name="router",
    )(h, router_w.T, router_b.reshape(N_EXPERTS, 1), tri)


def _tile_major(t, tm):
    k, n = t.shape
    return t.reshape(k, n // tm, tm).transpose(1, 0, 2).reshape(n // tm, 1, k * tm)


def _dispatch_kernel(dest_ref, pe_ref, h_ref, o_hbm, zbuf, sem, zsem):
    n_groups = DP_TM // DP_GROUP

    @pl.when(pl.program_id(0) == 0)
    def _():
        zbuf[...] = jnp.zeros_like(zbuf)

        def fill(e):
            end = pe_ref[e]
            begin = pe_ref[e - 1] if e else 0
            return end > begin, pltpu.make_async_copy(
                zbuf, o_hbm.at[pl.ds(pl.multiple_of(jnp.maximum(end - MOE_BM, 0), MOE_BM), MOE_BM)], zsem.at[0])

        for e in range(N_EXPERTS):
            nonempty, cp = fill(e)
            pl.when(nonempty)(cp.start)
        for e in range(N_EXPERTS):
            nonempty, cp = fill(e)
            pl.when(nonempty)(cp.wait)

        def tail(b):
            return pltpu.make_async_copy(zbuf, o_hbm.at[pl.ds(pl.multiple_of(b * MOE_BM, MOE_BM), MOE_BM)],
                                         zsem.at[0])

        first_unused = pe_ref[N_EXPERTS - 1] // MOE_BM
        lax.fori_loop(first_unused, MOE_ROWS // MOE_BM, lambda b, c: (tail(b).start(), c)[1], 0)
        lax.fori_loop(first_unused, MOE_ROWS // MOE_BM, lambda b, c: (tail(b).wait(), c)[1], 0)

    def wait_group(slot):
        pltpu.make_async_copy(h_ref.at[pl.ds(0, TOP_K * DP_GROUP)], o_hbm.at[pl.ds(0, TOP_K * DP_GROUP)],
                              sem.at[slot]).wait()

    def group(gi, carry):
        slot = gi % 2

        def issue(t, c):
            tok = gi * DP_GROUP + t
            for k in range(TOP_K):
                dst = dest_ref[0, 0, k * DP_TM + tok]
                pltpu.make_async_copy(h_ref.at[pl.ds(tok, 1)], o_hbm.at[pl.ds(dst, 1)],
                                      sem.at[slot]).start(priority=k % 2)
            return c

        lax.fori_loop(0, DP_GROUP, issue, 0, unroll=DMA_UNROLL)

        @pl.when(gi > 0)
        def _():
            wait_group(1 - slot)

        return carry

    lax.fori_loop(0, n_groups, group, 0)
    wait_group((n_groups - 1) % 2)


def _dispatch(h, dest, pad_end):
    n = h.shape[0]
    return pl.pallas_call(
        _dispatch_kernel,
        grid=(n // DP_TM,),
        in_specs=[
            pl.BlockSpec((1, 1, TOP_K * DP_TM), lambda i: (i, 0, 0), memory_space=pltpu.SMEM),
            pl.BlockSpec(memory_space=pltpu.SMEM),
            pl.BlockSpec((DP_TM, D_MODEL), lambda i: (i, 0)),
        ],
        out_specs=pl.BlockSpec(memory_space=pl.ANY),
        out_shape=jax.ShapeDtypeStruct((MOE_ROWS, D_MODEL), F32),
        scratch_shapes=[pltpu.VMEM((MOE_BM, D_MODEL), F32), pltpu.SemaphoreType.DMA((2,)),
                        pltpu.SemaphoreType.DMA((1,))],
        compiler_params=_cparams(("arbitrary",)),
        name="moe_dispatch",
    )(_tile_major(dest, DP_TM), pad_end, h)


def _expert_kernel(bx_ref, be_ref, nv_ref, x_ref, w1_ref, b1_ref, w2_ref, b2_ref, o_ref, w1b, w2b):
    i = pl.program_id(0)
    changed = jnp.logical_or(i == 0, be_ref[i] != be_ref[jnp.maximum(i - 1, 0)])

    @pl.when(changed)
    def _():
        w1b[...] = w1_ref[0, 0].astype(BF16)
        w2b[...] = w2_ref[0, 0].astype(BF16)

    def ffn(rows):
        hb = _dg(x_ref[0:rows, :].astype(BF16), w1b[...], NN) + b1_ref[0, 0]
        hg = jnp.minimum(hb[:, :D_FF], SWIGLU_LIMIT)
        hu = jnp.clip(hb[:, D_FF:], -SWIGLU_LIMIT, SWIGLU_LIMIT)
        act = hg * _sigmoid(SWIGLU_ALPHA * hg) * (hu + 1.0)
        o_ref[0:rows, :] = _dg(act.astype(BF16), w2b[...], NN) + b2_ref[0, 0]

    nv = nv_ref[i]
    half = MOE_BM // 2

    @pl.when(nv > half)
    def _():
        ffn(MOE_BM)

    @pl.when(jnp.logical_and(nv > 0, nv <= half))
    def _():
        ffn(half)
        o_ref[half:, :] = jnp.zeros((MOE_BM - half, D_MODEL), F32)

    @pl.when(nv == 0)
    def _():
        o_ref[...] = jnp.zeros_like(o_ref)


def _experts(x_rows, blk_x, blk_e, n_valid, layer, w1, b1, w2, b2):
    nb = MOE_ROWS // MOE_BM
    grid_spec = pltpu.PrefetchScalarGridSpec(
        num_scalar_prefetch=3,
        grid=(nb,),
        in_specs=[
            pl.BlockSpec((MOE_BM, D_MODEL), lambda i, bx, be, nv: (bx[i], 0)),
            pl.BlockSpec((1, 1, D_MODEL, 2 * D_FF), lambda i, bx, be, nv: (layer, be[i], 0, 0)),
            pl.BlockSpec((1, 1, 1, 2 * D_FF), lambda i, bx, be, nv: (layer, be[i], 0, 0)),
            pl.BlockSpec((1, 1, D_FF, D_MODEL), lambda i, bx, be, nv: (layer, be[i], 0, 0)),
            pl.BlockSpec((1, 1, 1, D_MODEL), lambda i, bx, be, nv: (layer, be[i], 0, 0)),
        ],
        out_specs=pl.BlockSpec((MOE_BM, D_MODEL), lambda i, bx, be, nv: (i, 0)),
        scratch_shapes=[pltpu.VMEM((D_MODEL, 2 * D_FF), BF16), pltpu.VMEM((D_FF, D_MODEL), BF16)],
    )
    return pl.pallas_call(
        _expert_kernel,
        grid_spec=grid_spec,
        out_shape=jax.ShapeDtypeStruct((MOE_ROWS, D_MODEL), F32),
        compiler_params=pltpu.CompilerParams(dimension_semantics=("arbitrary",),
                                             vmem_limit_bytes=EXPERT_VMEM_LIMIT),
        name="moe_experts",
    )(blk_x, blk_e, n_valid, x_rows, w1, b1.reshape(DEPTH, N_EXPERTS, 1, 2 * D_FF), w2,
      b2.reshape(DEPTH, N_EXPERTS, 1, D_MODEL))


def _combine_kernel(dest_ref, gate_ref, x_ref, mod_ref, fw_ref, y_hbm, *rest, final):
    if final:
        o_ctx_ref, o_lat_ref, buf, sem = rest
    else:
        o_ref, buf, sem = rest

    def issue(t, c):
        for k in range(TOP_K):
            dst = dest_ref[0, 0, k * CB_TM + t]
            pltpu.make_async_copy(y_hbm.at[pl.ds(dst, 1)], buf.at[k, pl.ds(t, 1)],
                                  sem.at[0]).start(priority=k % 2)
        return c

    lax.fori_loop(0, CB_TM, issue, 0, unroll=DMA_UNROLL)
    for k in range(TOP_K):
        pltpu.make_async_copy(y_hbm.at[pl.ds(0, CB_TM)], buf.at[k], sem.at[0]).wait()
    g = gate_ref[...]
    acc = g[:, 0:1] * buf[0]
    for k in range(1, TOP_K):
        acc = acc + g[:, k:k + 1] * buf[k]
    gate2 = _mod_row(mod_ref, pl.program_id(0) * CB_TM)[:, 5 * D_MODEL:6 * D_MODEL]
    x2 = x_ref[...] + gate2 * acc
    if not final:
        o_ref[...] = x2
        return
    y = _rms(x2, fw_ref[...])
    in_ctx = pl.program_id(0) < N_CTX // CB_TM

    @pl.when(in_ctx)
    def _():
        o_ctx_ref[...] = y

    @pl.when(jnp.logical_not(in_ctx))
    def _():
        o_lat_ref[...] = y


def _combine(y_rows, dest, gates, x, mod, final_w, final):
    n = x.shape[0]
    full = pl.BlockSpec((CB_TM, D_MODEL), lambda i: (i, 0))
    nct = N_CTX // CB_TM
    if final:
        out_specs = [pl.BlockSpec((CB_TM, D_MODEL), lambda i: (jnp.minimum(i, nct - 1), 0)),
                     pl.BlockSpec((CB_TM, D_MODEL), lambda i: (jnp.maximum(i - nct, 0), 0))]
        out_shape = [jax.ShapeDtypeStruct((N_CTX, D_MODEL), F32), jax.ShapeDtypeStruct((N_LAT, D_MODEL), F32)]
    else:
        out_specs, out_shape = full, jax.ShapeDtypeStruct((n, D_MODEL), F32)
    return pl.pallas_call(
        functools.partial(_combine_kernel, final=final),
        grid=(n // CB_TM,),
        in_specs=[
            pl.BlockSpec((1, 1, TOP_K * CB_TM), lambda i: (i, 0, 0), memory_space=pltpu.SMEM),
            pl.BlockSpec((CB_TM, TOP_K), lambda i: (i, 0)),
            full,
            pl.BlockSpec((MOD_ROWS, 6 * D_MODEL), lambda i: (0, 0)),
            pl.BlockSpec((1, D_MODEL), lambda i: (0, 0)),
            pl.BlockSpec(memory_space=pl.ANY),
        ],
        out_specs=out_specs,
        out_shape=out_shape,
        scratch_shapes=[pltpu.VMEM((TOP_K, CB_TM, D_MODEL), F32), pltpu.SemaphoreType.DMA((1,))],
        compiler_params=_cparams(("arbitrary",)),
        name="moe_combine",
    )(_tile_major(dest, CB_TM), gates.T, x, mod, final_w.reshape(1, D_MODEL), y_rows)


def _routed_ffn(h, x, mod, p, moe, layer, final_w, final):
    idx, gates, rank, counts = _router(h, p['router_w'], p['router_b'])
    counts = counts[:, 0].astype(jnp.int32)
    padded = (counts + MOE_BM - 1) // MOE_BM * MOE_BM
    pad_end = jnp.cumsum(padded)
    pad_start = pad_end - padded
    experts = jnp.arange(N_EXPERTS, dtype=jnp.int32)
    start_of = jnp.sum(jnp.where(idx[:, :, None] == experts, pad_start, 0), axis=-1)
    dest = start_of + rank
    nb = MOE_ROWS // MOE_BM
    first_row = jnp.arange(nb, dtype=jnp.int32) * MOE_BM
    blk_e = jnp.minimum(jnp.sum((pad_end[None, :] <= first_row[:, None]).astype(jnp.int32), axis=1),
                        N_EXPERTS - 1)
    is_e = blk_e[:, None] == experts[None, :]
    end_of = jnp.sum(jnp.where(is_e, pad_start + counts, 0), axis=1)
    n_valid = jnp.clip(end_of - first_row, 0, MOE_BM).astype(jnp.int32)
    blk_x = jnp.minimum(jnp.arange(nb, dtype=jnp.int32), pad_end[-1] // MOE_BM - 1)
    x_rows = _dispatch(h, dest, pad_end)
    y_rows = _experts(x_rows, blk_x, blk_e, n_valid, layer, *moe)
    return _combine(y_rows, dest, gates, x, mod, final_w, final)


_LAYER_PARAMS = ('norm_mix_w', 'norm_ffn_w', 'da_lambda', 'da_norm_w', 'rw_shift', 'rw_w0', 'rw_w_up', 'rw_a0',
                 'rw_a_up', 'rw_g_up', 'rw_k_k', 'rw_k_a', 'rw_r_k', 'rw_ln_w', 'rw_ln_b', 'ret_decay_logit',
                 'ret_norm_w', 'cv_dw_w', 'cv_dw_b', 'cv_ln_w', 'cv_ln_b', 'w_branch', 'w_out', 'router_w',
                 'router_b')


def _check_layout():
    assert SEQ & (SEQ - 1) == 0 and DEC_SEQ & (DEC_SEQ - 1) == 0, "sequence positions are bit masks"
    assert N_CTX % DEC_SEQ == 0, "latent sequences start on a DEC_SEQ boundary of the stream"
    assert N_CTX % MOD_GROUP == 0 and DEC_SEQ % MOD_GROUP == 0 and N_CTX + DEC_BATCH * DEC_SEQ == N_TOK
    for tile in (RW_TM, RW_TB, RET_TB, CV_TM, DA_TQ):
        assert SEQ % tile == 0 and DEC_SEQ % tile == 0
    for tile in (IN_TM, MG_TM, CB_TM):
        assert MOD_GROUP % tile == 0
    assert N_TOK % QK_TM == 0 and N_CTX % QK_TM == 0 and N_TOK % RT_TM == 0 and N_TOK % DP_TM == 0
    assert DP_TM % DP_GROUP == 0 and DP_GROUP % DMA_UNROLL == 0 and CB_TM % DMA_UNROLL == 0


def _layer(x, mod, w_p, p, moe, layer, lam_init, caches, tables, final_w, final):
    cache_k, cache_v, state_rw, state_ret = caches
    proj_rw, proj = _input_projection(x, mod, p['norm_mix_w'], w_p, layer)
    o_da = _da_branch(proj, cache_k, cache_v, p['da_lambda'], p['da_norm_w'], lam_init, tables)
    o_rw, rw_state = _rwkv_branch(proj_rw, state_rw, p)
    o_ret, ret_state = _retention_branch(proj, state_ret, p)
    o_cv = _conv_branch(proj, p)
    x1, h2 = _merge((o_da, o_rw, o_ret, o_cv), proj, x, mod, p['w_branch'], p['w_out'], p['norm_ffn_w'])
    x2 = _routed_ffn(h2, x1, mod, p, moe, layer, final_w, final)
    ctx_kv = proj[:N_CTX, B_DA + DA_W:B_DA + 3 * DA_W].astype(F32)
    new_k = ctx_kv[:, :DA_W].reshape(BATCH, SEQ, DA_HEADS, 2, DA_QK)
    new_v = ctx_kv[:, DA_W:].reshape(BATCH, SEQ, DA_HEADS, DA_V)
    return x2, (new_k, new_v, rw_state[:BATCH], ret_state[:BATCH])


def kernel(x_prompt, x_sample, c, cache_da_k, cache_da_v, state_rwkv, state_ret, c_ctx, ada_w, ada_b, norm_mix_w,
           norm_ffn_w, w_in, da_lambda, da_norm_w, rw_shift, rw_w0, rw_w_up, rw_a0, rw_a_up, rw_g_up, rw_k_k,
           rw_k_a, rw_r_k, rw_ln_w, rw_ln_b, ret_decay_logit, ret_norm_w, cv_dw_w, cv_dw_b, cv_ln_w, cv_ln_b,
           w_branch, w_out, router_w, router_b, moe_w1, moe_b1, moe_w2, moe_b2, final_norm_w):
    _check_layout()
    weights = dict(norm_mix_w=norm_mix_w, norm_ffn_w=norm_ffn_w, da_lambda=da_lambda, da_norm_w=da_norm_w,
                   rw_shift=rw_shift, rw_w0=rw_w0, rw_w_up=rw_w_up, rw_a0=rw_a0, rw_a_up=rw_a_up, rw_g_up=rw_g_up,
                   rw_k_k=rw_k_k, rw_k_a=rw_k_a, rw_r_k=rw_r_k, rw_ln_w=rw_ln_w, rw_ln_b=rw_ln_b,
                   ret_decay_logit=ret_decay_logit, ret_norm_w=ret_norm_w, cv_dw_w=cv_dw_w, cv_dw_b=cv_dw_b,
                   cv_ln_w=cv_ln_w, cv_ln_b=cv_ln_b, w_branch=w_branch, w_out=w_out, router_w=router_w,
                   router_b=router_b)
    x = jnp.concatenate([x_prompt.reshape(N_CTX, D_MODEL), x_sample.reshape(N_LAT, D_MODEL)], axis=0)
    cvec = jnp.concatenate([c_ctx[None, :], c, jnp.zeros((MOD_ROWS - 1 - DEC_BATCH, D_MODEL), F32)], axis=0)
    mod = _modulation(cvec, ada_w, ada_b)
    w_p = _pad_w_in(w_in).astype(BF16)
    tables = _rope_tables()
    outs = []
    moe = (moe_w1, moe_b1, moe_w2, moe_b2)
    for i in range(DEPTH):
        p = {name: weights[name][i] for name in _LAYER_PARAMS}
        lam_init = 0.8 - 0.6 * math.exp(-0.3 * i)
        caches = (cache_da_k[:, i], cache_da_v[:, i], state_rwkv[:, i], state_ret[:, i])
        x, ctx_out = _layer(x, mod[i], w_p, p, moe, i, lam_init, caches, tables, final_norm_w, i == DEPTH - 1)
        outs.append(ctx_out)
    y_prompt = x[0].reshape(BATCH, SEQ, D_MODEL)
    y_sample = x[1].reshape(DEC_BATCH, DEC_SEQ, D_MODEL)
    stack = lambda j: jnp.stack([o[j] for o in outs], axis=1)
    return (y_prompt, y_sample, stack(0), stack(1), stack(2), stack(3))


def _pad_w_in(w_in):
    da, rw, ret, cv, gate = jnp.split(w_in, [1536, 3456, 4992, 6016], axis=-1)
    pad = jnp.zeros(w_in.shape[:-1] + (P_DA - RW_COLS,), w_in.dtype)
    return jnp.concatenate([rw, pad, da, ret, cv, gate], axis=-1)
```
